```python
import math
import jax, jax.numpy as jnp
from jax import lax
import numpy as np

D_MODEL = 2048
BATCH = 2
SEQ = 8192
DEPTH = 2

N_MEM = 256
A_HEADS = 8
A_KV_HEADS = 2
A_HEAD_DIM = 128
IDX_HEADS = 4
IDX_DIM = 64
TOPK_MAX = 256
Q_BLOCK = 128
LRU_WIDTH = 1024
LRU_BLOCKS = 8
CONV_WIDTH = 4
LRU_C = 8.0
X_HEADS = 4
X_HEAD_DIM = 256
N_BRANCH = 3
BRANCH_WIDTH = 1024
D_FF = 5504
N_BUCKETS = 32
MAX_DISTANCE = 128
EPS = 1e-6

A_Q = A_HEADS * A_HEAD_DIM
A_KV = A_KV_HEADS * A_HEAD_DIM
SPLIT_SIZES = (A_Q, A_KV, A_KV, IDX_HEADS * IDX_DIM, IDX_DIM, IDX_HEADS,
               LRU_WIDTH, LRU_WIDTH, X_HEADS * X_HEAD_DIM, N_BRANCH * D_MODEL)
D_IN = sum(SPLIT_SIZES)

kernel_name = "hybrid_dsa_rglru_memory_macaron"


def rmsnorm(x, g):
    xf = x.astype(jnp.float32)
    y = xf * lax.rsqrt(jnp.mean(xf * xf, axis=-1, keepdims=True) + EPS)
    return (y * g.astype(jnp.float32)).astype(x.dtype)


def layernorm(x, g, b):
    xf = x.astype(jnp.float32)
    mu = jnp.mean(xf, axis=-1, keepdims=True)
    var = jnp.mean(jnp.square(xf - mu), axis=-1, keepdims=True)
    y = (xf - mu) * lax.rsqrt(var + EPS)
    return (y * g.astype(jnp.float32) + b.astype(jnp.float32)).astype(x.dtype)


def swiglu(x, w_gu, w_down):
    g, u = jnp.split(x @ w_gu, 2, axis=-1)
    return (jax.nn.silu(g) * u) @ w_down


def split_points():
    pts, acc = [], 0
    for n in SPLIT_SIZES[:-1]:
        acc += n
        pts.append(acc)
    return pts


def t5_bucket(dist):
    max_exact = N_BUCKETS // 2
    d = jnp.maximum(dist, 0)
    df = jnp.maximum(d, 1).astype(jnp.float32)
    large = max_exact + (jnp.log(df / max_exact) / math.log(MAX_DISTANCE / max_exact)
                         * (N_BUCKETS - max_exact)).astype(jnp.int32)
    large = jnp.minimum(large, N_BUCKETS - 1)
    return jnp.where(d < max_exact, d, large)


def dsa_attention(q, k, v, q_idx, k_idx, w_idx, rel_bias):
    b, s = q.shape[0], q.shape[1]
    topk = min(TOPK_MAX, s // 4)
    n_blocks = s // Q_BLOCK
    group = A_HEADS // A_KV_HEADS
    key_pos = jnp.arange(s, dtype=jnp.int32)
    gather = jax.vmap(lambda t, i: t[i])
    idx_scale = IDX_DIM ** -0.5
    w_scale = IDX_HEADS ** -0.5
    att_scale = A_HEAD_DIM ** -0.5

    def block(i):
        start = i * Q_BLOCK
        q_pos = start + jnp.arange(Q_BLOCK, dtype=jnp.int32)
        qb = lax.dynamic_slice_in_dim(q, start, Q_BLOCK, axis=1)
        qib = lax.dynamic_slice_in_dim(q_idx, start, Q_BLOCK, axis=1)
        wb = lax.dynamic_slice_in_dim(w_idx, start, Q_BLOCK, axis=1)
        dots = jnp.einsum('bqhd,bsd->bqhs', qib, k_idx).astype(jnp.float32) * idx_scale
        score = jnp.einsum('bqh,bqhs->bqs', wb.astype(jnp.float32) * w_scale, jax.nn.relu(dots))
        causal = key_pos[None, None, :] <= q_pos[None, :, None]
        score = jnp.where(causal, score, -jnp.inf)
        _, sel = lax.top_k(score, topk)
        kg = gather(k, sel)
        vg = gather(v, sel)
        qg = qb.reshape(b, Q_BLOCK, A_KV_HEADS, group, A_HEAD_DIM)
        logits = jnp.einsum('bqngd,bqknd->bqngk', qg, kg).astype(jnp.float32) * att_scale
        dist = q_pos[None, :, None] - sel
        bias = rel_bias[t5_bucket(dist)].astype(jnp.float32)
        bias = bias.reshape(b, Q_BLOCK, topk, A_KV_HEADS, group).transpose(0, 1, 3, 4, 2)
        valid = (dist >= 0)[:, :, None, None, :]
        logits = jnp.where(valid, logits + bias, -jnp.inf)
        p = jax.nn.softmax(logits, axis=-1).astype(v.dtype)
        out = jnp.einsum('bqngk,bqknd->bqngd', p, vg)
        return out.reshape(b, Q_BLOCK, A_Q)

    outs = lax.map(block, jnp.arange(n_blocks, dtype=jnp.int32))
    return outs.transpose(1, 0, 2, 3).reshape(b, s, A_Q)


def rglru_branch(xb, gate, conv_w, conv_b, w_a, b_a, w_i, b_i, lam):
    b, s, _ = xb.shape
    xp = jnp.pad(xb, ((0, 0), (CONV_WIDTH - 1, 0), (0, 0)))
    xc = conv_b + sum(xp[:, j:j + s] * conv_w[j] for j in range(CONV_WIDTH))
    xr = xc.reshape(b, s, LRU_BLOCKS, LRU_WIDTH // LRU_BLOCKS)
    r = jax.nn.sigmoid(jnp.einsum('bsnc,ncd->bsnd', xr, w_a).reshape(b, s, LRU_WIDTH) + b_a)
    i = jax.nn.sigmoid(jnp.einsum('bsnc,ncd->bsnd', xr, w_i).reshape(b, s, LRU_WIDTH) + b_i)
    log_a = -LRU_C * r.astype(jnp.float32) * jax.nn.softplus(-lam.astype(jnp.float32))
    a = jnp.exp(log_a)
    mult = jnp.sqrt(jnp.maximum(-jnp.expm1(2.0 * log_a), 0.0))
    u = mult * (i * xc).astype(jnp.float32)

    def combine(left, right):
        a1, b1 = left
        a2, b2 = right
        return a1 * a2, a2 * b1 + b2

    _, h = lax.associative_scan(combine, (a, u), axis=1)
    return h.astype(xb.dtype) * jax.nn.gelu(gate)


def memory_attention(q, mem_k, mem_v):
    b, s = q.shape[0], q.shape[1]
    logits = jnp.einsum('bshd,bmhd->bhsm', q, mem_k).astype(jnp.float32) * (X_HEAD_DIM ** -0.5)
    p = jax.nn.softmax(logits, axis=-1).astype(q.dtype)
    return jnp.einsum('bhsm,bmhd->bshd', p, mem_v).reshape(b, s, X_HEADS * X_HEAD_DIM)


def hybrid_layer(x, mem, rel_bias, norm_ff1, w_ff1_gu, w_ff1_down, norm_mix, w_in,
                 conv_w, conv_b, w_a, b_a, w_i, b_i, lam, idx_ln_g, idx_ln_b,
                 mem_norm, w_mem_kv, w_branch, w_out, norm_ff2, w_ff2_gu, w_ff2_down):
    b, s, _ = x.shape
    x = x + 0.5 * swiglu(rmsnorm(x, norm_ff1), w_ff1_gu, w_ff1_down)
    h = rmsnorm(x, norm_mix)
    proj = h @ w_in
    (q_a, k_a, v_a, q_idx, k_idx, w_idx, x_lru, g_lru, q_mem, gates) = jnp.split(proj, split_points(), axis=-1)
    y_a = dsa_attention(q_a.reshape(b, s, A_HEADS, A_HEAD_DIM),
                        k_a.reshape(b, s, A_KV_HEADS, A_HEAD_DIM),
                        v_a.reshape(b, s, A_KV_HEADS, A_HEAD_DIM),
                        q_idx.reshape(b, s, IDX_HEADS, IDX_DIM),
                        layernorm(k_idx, idx_ln_g, idx_ln_b), w_idx, rel_bias)
    y_b = rglru_branch(x_lru, g_lru, conv_w, conv_b, w_a, b_a, w_i, b_i, lam)
    mk, mv = jnp.split(rmsnorm(mem, mem_norm) @ w_mem_kv, 2, axis=-1)
    m = mem.shape[1]
    y_c = memory_attention(q_mem.reshape(b, s, X_HEADS, X_HEAD_DIM),
                           mk.reshape(b, m, X_HEADS, X_HEAD_DIM),
                           mv.reshape(b, m, X_HEADS, X_HEAD_DIM))
    branches = jnp.stack([y_a, y_b, y_c], axis=2)
    branch_d = jnp.einsum('bsjc,jcd->bsjd', branches, w_branch)
    g = jax.nn.sigmoid(gates.reshape(b, s, N_BRANCH, D_MODEL))
    merged = jnp.einsum('bsjd,bsjd->bsd', g, branch_d)
    x = x + merged @ w_out
    x = x + 0.5 * swiglu(rmsnorm(x, norm_ff2), w_ff2_gu, w_ff2_down)
    return x


def setup_inputs(seed: int = 0) -> dict:
    key = jax.random.key(seed)
    ks = jax.random.split(key, 32)
    f32 = jnp.float32
    nrm = lambda k, shape, scale: jax.random.normal(k, shape, f32) * scale
    gain = lambda k, shape: 1.0 + 0.02 * jax.random.normal(k, shape, f32)
    L = DEPTH
    base = jnp.sqrt(jax.random.uniform(ks[12], (L, LRU_WIDTH), f32, 0.81, 0.998))
    lam = jnp.log(base) - jnp.log1p(-base)
    bw = LRU_WIDTH // LRU_BLOCKS
    return {
        "x": nrm(ks[0], (BATCH, SEQ, D_MODEL), 1.0),
        "mem": nrm(ks[1], (BATCH, N_MEM, D_MODEL), 1.0),
        "rel_bias": nrm(ks[2], (N_BUCKETS, A_HEADS), 0.5),
        "final_norm": gain(ks[3], (D_MODEL,)),
        "norm_ff1": gain(ks[4], (L, D_MODEL)),
        "w_ff1_gu": nrm(ks[5], (L, D_MODEL, 2 * D_FF), D_MODEL ** -0.5),
        "w_ff1_down": nrm(ks[6], (L, D_FF, D_MODEL), D_FF ** -0.5),
        "norm_mix": gain(ks[7], (L, D_MODEL)),
        "w_in": nrm(ks[8], (L, D_MODEL, D_IN), D_MODEL ** -0.5),
        "conv_w": nrm(ks[9], (L, CONV_WIDTH, LRU_WIDTH), CONV_WIDTH ** -0.5),
        "conv_b": nrm(ks[10], (L, LRU_WIDTH), 0.01),
        "w_a": nrm(ks[11], (L, LRU_BLOCKS, bw, bw), bw ** -0.5),
        "b_a": nrm(ks[13], (L, LRU_WIDTH), 0.01),
        "w_i": nrm(ks[14], (L, LRU_BLOCKS, bw, bw), bw ** -0.5),
        "b_i": nrm(ks[15], (L, LRU_WIDTH), 0.01),
        "lam": lam,
        "idx_ln_g": gain(ks[16], (L, IDX_DIM)),
        "idx_ln_b": nrm(ks[17], (L, IDX_DIM), 0.01),
        "mem_norm": gain(ks[18], (L, D_MODEL)),
        "w_mem_kv": nrm(ks[19], (L, D_MODEL, 2 * X_HEADS * X_HEAD_DIM), D_MODEL ** -0.5),
        "w_branch": nrm(ks[20], (L, N_BRANCH, BRANCH_WIDTH, D_MODEL), BRANCH_WIDTH ** -0.5),
        "w_out": nrm(ks[21], (L, D_MODEL, D_MODEL), D_MODEL ** -0.5),
        "norm_ff2": gain(ks[22], (L, D_MODEL)),
        "w_ff2_gu": nrm(ks[23], (L, D_MODEL, 2 * D_FF), D_MODEL ** -0.5),
        "w_ff2_down": nrm(ks[24], (L, D_FF, D_MODEL), D_FF ** -0.5),
    }


def reference(x, mem, rel_bias, final_norm, norm_ff1, w_ff1_gu, w_ff1_down, norm_mix, w_in,
              conv_w, conv_b, w_a, b_a, w_i, b_i, lam, idx_ln_g, idx_ln_b, mem_norm,
              w_mem_kv, w_branch, w_out, norm_ff2, w_ff2_gu, w_ff2_down):
    for l in range(DEPTH):
        x = hybrid_layer(x, mem, rel_bias,
                         norm_ff1[l], w_ff1_gu[l], w_ff1_down[l], norm_mix[l], w_in[l],
                         conv_w[l], conv_b[l], w_a[l], b_a[l], w_i[l], b_i[l], lam[l],
                         idx_ln_g[l], idx_ln_b[l], mem_norm[l], w_mem_kv[l],
                         w_branch[l], w_out[l], norm_ff2[l], w_ff2_gu[l], w_ff2_down[l])
    return rmsnorm(x, final_norm)
```

```python
import functools
import math

import jax
import jax.numpy as jnp
from jax import lax
from jax.experimental import pallas as pl
from jax.experimental.pallas import tpu as pltpu

F32 = jnp.float32
BF16 = jnp.bfloat16
I32 = jnp.int32

EPS = 1e-6
A_HEADS = 8
A_KV_HEADS = 2
A_HEAD_DIM = 128
A_GROUP = A_HEADS // A_KV_HEADS
IDX_HEADS = 4
IDX_DIM = 64
TOPK_MAX = 256
LRU_WIDTH = 1024
LRU_BLOCKS = 8
LRU_BLOCK_W = LRU_WIDTH // LRU_BLOCKS
CONV_WIDTH = 4
LRU_C = 8.0
X_HEADS = 4
X_HEAD_DIM = 256
N_BRANCH = 3
BRANCH_WIDTH = 1024
N_BUCKETS = 32
MAX_DISTANCE = 128

LANES = 128
SUBLANES = 8
VMEM_LIMIT = 56 * 1024 * 1024

QB = 128
NEG = -1e30
INT_MIN = -2 ** 31


def _params(*sem):
    return pltpu.CompilerParams(dimension_semantics=sem, vmem_limit_bytes=VMEM_LIMIT)


def _rms(x, g):
    ms = jnp.mean(x * x, axis=-1, keepdims=True)
    return x * lax.rsqrt(ms + EPS) * g


def _sigmoid(x):
    return 1.0 / (1.0 + jnp.exp(-x))


def _dot(a, b):
    return jnp.dot(a, b, preferred_element_type=F32)


def _dot_nt(a, b):
    return lax.dot_general(a, b, (((1,), (1,)), ((), ())), preferred_element_type=F32)


def _ffn_kernel(x_ref, g_ref, wg_ref, wu_ref, wd_ref, fg_ref, o_ref, h_ref, *, final_norm):
    f = pl.program_id(1)

    @pl.when(f == 0)
    def _():
        h_ref[...] = _rms(x_ref[...], g_ref[...]).astype(BF16)
        o_ref[...] = jnp.zeros_like(o_ref)

    h = h_ref[...]
    g = _dot(h, wg_ref[...])
    u = _dot(h, wu_ref[...])
    a = (g * _sigmoid(g) * u).astype(BF16)
    o_ref[...] += _dot(a, wd_ref[...])

    @pl.when(f == pl.num_programs(1) - 1)
    def _():
        y = x_ref[...] + 0.5 * o_ref[...]
        if final_norm:
            y = _rms(y, fg_ref[...])
        o_ref[...] = y


def _ffn(x2, g, wg, wu, wd, fg, *, final_norm, tm, tf):
    m, d = x2.shape
    fp = wg.shape[1]
    return pl.pallas_call(
        functools.partial(_ffn_kernel, final_norm=final_norm),
        out_shape=jax.ShapeDtypeStruct((m, d), F32),
        grid=(m // tm, fp // tf),
        in_specs=[
            pl.BlockSpec((tm, d), lambda i, f: (i, 0)),
            pl.BlockSpec((1, d), lambda i, f: (0, 0)),
            pl.BlockSpec((d, tf), lambda i, f: (0, f)),
            pl.BlockSpec((d, tf), lambda i, f: (0, f)),
            pl.BlockSpec((tf, d), lambda i, f: (f, 0)),
            pl.BlockSpec((1, d), lambda i, f: (0, 0)),
        ],
        out_specs=pl.BlockSpec((tm, d), lambda i, f: (i, 0)),
        scratch_shapes=[pltpu.VMEM((tm, d), BF16)],
        compiler_params=_params("parallel", "arbitrary"),
        name="ffn",
    )(x2, g, wg, wu, wd, fg)


def _norm_matmul_kernel(x_ref, g_ref, w_ref, o_ref, h_ref):
    @pl.when(pl.program_id(1) == 0)
    def _():
        h_ref[...] = _rms(x_ref[...], g_ref[...]).astype(BF16)

    o_ref[...] = _dot(h_ref[...], w_ref[...]).astype(o_ref.dtype)


def _norm_matmul(x2, g, w, out_dtype, *, tm, tn, name):
    m, d = x2.shape
    n = w.shape[1]
    return pl.pallas_call(
        _norm_matmul_kernel,
        out_shape=jax.ShapeDtypeStruct((m, n), out_dtype),
        grid=(m // tm, n // tn),
        in_specs=[
            pl.BlockSpec((tm, d), lambda i, j: (i, 0)),
            pl.BlockSpec((1, d), lambda i, j: (0, 0)),
            pl.BlockSpec((d, tn), lambda i, j: (0, j)),
        ],
        out_specs=pl.BlockSpec((tm, tn), lambda i, j: (i, j)),
        scratch_shapes=[pltpu.VMEM((tm, d), BF16)],
        compiler_params=_params("parallel", "arbitrary"),
        name=name,
    )(x2, g, w)


def _lru_kernel(x_ref, gate_ref, cw_ref, cb_ref, wa_ref, wi_ref, ba_ref, bi_ref, lam_ref,
                o_ref, xbuf_ref, hc_ref, *, tc):
    halo = SUBLANES

    @pl.when(pl.program_id(1) == 0)
    def _():
        xbuf_ref[0:halo, :] = jnp.zeros((halo, LRU_WIDTH), F32)
        hc_ref[...] = jnp.zeros_like(hc_ref)

    xbuf_ref[halo:halo + tc, :] = x_ref[0]
    base = halo - (CONV_WIDTH - 1)
    xc = xbuf_ref[base:base + tc, :] * cw_ref[0:1, :]
    for j in range(1, CONV_WIDTH):
        xc = xc + xbuf_ref[base + j:base + j + tc, :] * cw_ref[j:j + 1, :]
    xc = cb_ref[...] + xc
    xbuf_ref[0:halo, :] = xbuf_ref[tc:tc + halo, :]

    xcb = xc.astype(BF16)
    r_parts, i_parts = [], []
    for n in range(LRU_BLOCKS):
        blk = xcb[:, n * LRU_BLOCK_W:(n + 1) * LRU_BLOCK_W]
        r_parts.append(_dot(blk, wa_ref[n]))
        i_parts.append(_dot(blk, wi_ref[n]))
    r = _sigmoid(jnp.concatenate(r_parts, axis=1) + ba_ref[...])
    gi = _sigmoid(jnp.concatenate(i_parts, axis=1) + bi_ref[...])

    softplus_neg_lam = jnp.log1p(jnp.exp(-lam_ref[...]))
    log_a = (-LRU_C) * r * softplus_neg_lam
    a = jnp.exp(log_a)
    th = jnp.tanh(log_a)
    mult = jnp.sqrt(jnp.maximum(-2.0 * th / (1.0 - th), 0.0))
    u = mult * (gi * xc)

    row = lax.broadcasted_iota(I32, (tc, LRU_WIDTH), 0)
    d = 1
    while d < tc:
        keep = row >= d
        a_sh = jnp.where(keep, pltpu.roll(a, d, 0), 1.0)
        u_sh = jnp.where(keep, pltpu.roll(u, d, 0), 0.0)
        u = a * u_sh + u
        a = a * a_sh
        d *= 2
    h = a * hc_ref[...] + u
    hc_ref[...] = h[tc - 1:tc, :]

    gate = gate_ref[0]
    gelu = 0.5 * gate * (1.0 + jnp.tanh(math.sqrt(2.0 / math.pi) * (gate + 0.044715 * (gate * gate * gate))))
    o_ref[0] = (h * gelu).astype(o_ref.dtype)


def _lru(p32, conv_w, conv_b, w_a, w_i, b_a, b_i, lam, *, tc):
    b, s, _ = p32.shape
    vec = pl.BlockSpec((1, LRU_WIDTH), lambda bi, c: (0, 0))
    wblk = pl.BlockSpec((LRU_BLOCKS, LRU_BLOCK_W, LRU_BLOCK_W), lambda bi, c: (0, 0, 0))
    return pl.pallas_call(
        functools.partial(_lru_kernel, tc=tc),
        out_shape=jax.ShapeDtypeStruct((b, s, LRU_WIDTH), BF16),
        grid=(b, s // tc),
        in_specs=[
            pl.BlockSpec((1, tc, LRU_WIDTH), lambda bi, c: (bi, c, 0)),
            pl.BlockSpec((1, tc, LRU_WIDTH), lambda bi, c: (bi, c, 1)),
            pl.BlockSpec((CONV_WIDTH, LRU_WIDTH), lambda bi, c: (0, 0)),
            vec, wblk, wblk, vec, vec, vec,
        ],
        out_specs=pl.BlockSpec((1, tc, LRU_WIDTH), lambda bi, c: (bi, c, 0)),
        scratch_shapes=[pltpu.VMEM((tc + SUBLANES, LRU_WIDTH), F32), pltpu.VMEM((1, LRU_WIDTH), F32)],
        compiler_params=_params("arbitrary", "arbitrary"),
        name="rglru",
    )(p32, p32, conv_w, conv_b, w_a, w_i, b_a, b_i, lam)


def _memattn_kernel(q_ref, kv_ref, o_ref):
    kw = X_HEADS * X_HEAD_DIM
    for h in range(X_HEADS):
        sl = slice(h * X_HEAD_DIM, (h + 1) * X_HEAD_DIM)
        q = q_ref[0, :, sl]
        k = kv_ref[0, :, sl]
        v = kv_ref[0, :, kw + h * X_HEAD_DIM:kw + (h + 1) * X_HEAD_DIM]
        s = _dot_nt(q, k)
        m = jnp.max(s, axis=-1, keepdims=True)
        p = jnp.exp(s - m)
        l = jnp.sum(p, axis=-1, keepdims=True)
        o = _dot(p.astype(BF16), v)
        o_ref[0, :, sl] = (o / l).astype(o_ref.dtype)


def _memattn(p16, mkv, *, tm):
    b, s, _ = p16.shape
    nm = mkv.shape[1]
    kw = X_HEADS * X_HEAD_DIM
    return pl.pallas_call(
        _memattn_kernel,
        out_shape=jax.ShapeDtypeStruct((b, s, kw), BF16),
        grid=(b, s // tm),
        in_specs=[
            pl.BlockSpec((1, tm, kw), lambda bi, i: (bi, i, 1)),
            pl.BlockSpec((1, nm, 2 * kw), lambda bi, i: (bi, 0, 0)),
        ],
        out_specs=pl.BlockSpec((1, tm, kw), lambda bi, i: (bi, i, 0)),
        compiler_params=_params("parallel", "parallel"),
        name="memattn",
    )(p16, mkv)


def _merge_kernel(x_ref, g_ref, ya_ref, yb_ref, yc_ref, wg0_ref, wg1_ref, wg2_ref, wb_ref, wo_ref,
                  o_ref, h_ref):
    n = pl.program_id(1)

    @pl.when(n == 0)
    def _():
        h_ref[...] = _rms(x_ref[...], g_ref[...]).astype(BF16)
        o_ref[...] = jnp.zeros_like(o_ref)

    h = h_ref[...]
    merged = None
    for j, (y_ref, wg_ref) in enumerate(((ya_ref, wg0_ref), (yb_ref, wg1_ref), (yc_ref, wg2_ref))):
        term = _sigmoid(_dot(h, wg_ref[...])) * _dot(y_ref[...], wb_ref[j])
        merged = term if merged is None else merged + term
    o_ref[...] += _dot(merged.astype(BF16), wo_ref[...])

    @pl.when(n == pl.num_programs(1) - 1)
    def _():
        o_ref[...] = x_ref[...] + o_ref[...]


def _merge(x2, g, ya, yb, yc, wgates, wbranch, wout, *, tm, tn):
    m, d = x2.shape
    nblk = d // tn
    yspec = pl.BlockSpec((tm, BRANCH_WIDTH), lambda i, n: (i, 0))
    return pl.pallas_call(
        _merge_kernel,
        out_shape=jax.ShapeDtypeStruct((m, d), F32),
        grid=(m // tm, nblk),
        in_specs=[
            pl.BlockSpec((tm, d), lambda i, n: (i, 0)),
            pl.BlockSpec((1, d), lambda i, n: (0, 0)),
            yspec, yspec, yspec,
            pl.BlockSpec((d, tn), lambda i, n: (0, n)),
            pl.BlockSpec((d, tn), lambda i, n: (0, nblk + n)),
            pl.BlockSpec((d, tn), lambda i, n: (0, 2 * nblk + n)),
            pl.BlockSpec((N_BRANCH, BRANCH_WIDTH, tn), lambda i, n: (0, 0, n)),
            pl.BlockSpec((tn, d), lambda i, n: (n, 0)),
        ],
        out_specs=pl.BlockSpec((tm, d), lambda i, n: (i, 0)),
        scratch_shapes=[pltpu.VMEM((tm, d), BF16)],
        compiler_params=_params("parallel", "arbitrary"),
        name="merge",
    )(x2, g, ya, yb, yc, wgates, wgates, wgates, wbranch, wout)


def _dsa_kernel(q_ref, k_ref, v_ref, qi_ref, ki_ref, wi_ref, lng_ref, lnb_ref, bias_ref,
                o_ref,
                kln_ref, keys_ref, madd_ref, wb_ref, q4_ref, m_ref, l_ref, acc_ref, *, topk):
    i = pl.program_id(1)
    s_len = k_ref.shape[1]
    gq = A_GROUP * QB

    @pl.when(i == 0)
    def _():
        rows = 512 if s_len % 512 == 0 else QB
        lane = lax.broadcasted_iota(I32, (rows, LANES), 1)
        real = lane < IDX_DIM

        def ln_body(c, carry):
            x = ki_ref[0, pl.ds(c * rows, rows), :]
            mu = jnp.sum(x, axis=-1, keepdims=True) * (1.0 / IDX_DIM)
            xm = jnp.where(real, x - mu, 0.0)
            var = jnp.sum(xm * xm, axis=-1, keepdims=True) * (1.0 / IDX_DIM)
            y = xm * lax.rsqrt(var + EPS) * lng_ref[...] + lnb_ref[...]
            kln_ref[pl.ds(c * rows, rows), :] = y.astype(BF16)
            return carry

        lax.fori_loop(0, s_len // rows, ln_body, 0)

    w = wi_ref[0] * (IDX_HEADS ** -0.5 * IDX_DIM ** -0.5)
    for h in range(IDX_HEADS):
        wb_ref[h] = jnp.broadcast_to(w[:, h:h + 1], (QB, LANES))
    qib = qi_ref[0].astype(BF16)
    for n in range(A_KV_HEADS):
        for g in range(A_GROUP):
            hd = n * A_GROUP + g
            q4_ref[n, g * QB:(g + 1) * QB, :] = q_ref[0, :, hd * A_HEAD_DIM:(hd + 1) * A_HEAD_DIM]

    row = lax.broadcasted_iota(I32, (QB, LANES), 0)
    col = lax.broadcasted_iota(I32, (QB, LANES), 1)
    causal_diag = col <= row

    def score_chunk(c, diag):
        kc = kln_ref[pl.ds(c * QB, QB), :]
        sc = None
        for h in range(IDX_HEADS):
            d = _dot_nt(qib[:, h * LANES:(h + 1) * LANES], kc)
            t = wb_ref[h] * jnp.maximum(d, 0.0)
            sc = t if sc is None else sc + t
        bits = pltpu.bitcast(sc, I32)
        key = jnp.where(bits < 0, bits ^ jnp.int32(0x7FFFFFFF), bits)
        key = jnp.where(sc == 0.0, 0, key)
        if diag:
            key = jnp.where(causal_diag, key, INT_MIN)
        keys_ref[c] = key

    def score_body(c, carry):
        score_chunk(c, False)
        return carry

    lax.fori_loop(0, i, score_body, 0)
    score_chunk(i, True)
    nch = i + 1

    t_pos = i * QB + lax.broadcasted_iota(I32, (QB, 1), 0)
    kk = jnp.minimum(t_pos + 1, topk).astype(F32)

    def count_ge(cand):
        candb = jnp.broadcast_to(cand, (QB, LANES))

        def body(c, cnt):
            return cnt + jnp.where(keys_ref[c] >= candb, 1.0, 0.0)

        cnt = lax.fori_loop(0, nch, body, jnp.zeros((QB, LANES), F32))
        return jnp.sum(cnt, axis=-1, keepdims=True)

    def bit_body(bi, thr):
        cand = thr + lax.shift_left(jnp.int32(1), 31 - bi)
        return jnp.where(count_ge(cand) >= kk, cand, thr)

    thr = lax.fori_loop(0, 32, bit_body, jnp.full((QB, 1), INT_MIN, I32))
    need = kk - count_ge(thr + 1)
    thrb = jnp.broadcast_to(thr, (QB, LANES))
    tri = (row <= col).astype(BF16)

    def mask_body(c, seen):
        key = keys_ref[c]
        eq = key == thrb
        pre = _dot(jnp.where(eq, 1.0, 0.0).astype(BF16), tri)
        sel = (key > thrb) | (eq & ((seen + pre) <= need))
        madd_ref[c] = jnp.where(sel, 0.0, NEG)
        return seen + pre[:, LANES - 1:LANES]

    lax.fori_loop(0, nch, mask_body, jnp.zeros((QB, 1), F32))

    m_ref[...] = jnp.full(m_ref.shape, NEG, F32)
    l_ref[...] = jnp.zeros(l_ref.shape, F32)
    acc_ref[...] = jnp.zeros(acc_ref.shape, F32)

    def attn_chunk(c, bias_idx):
        madd = madd_ref[c]
        for n in range(A_KV_HEADS):
            kc = k_ref[0, pl.ds(c * QB, QB), n * A_HEAD_DIM:(n + 1) * A_HEAD_DIM]
            vc = v_ref[0, pl.ds(c * QB, QB), n * A_HEAD_DIM:(n + 1) * A_HEAD_DIM]
            s = _dot_nt(q4_ref[n], kc)
            s = (s.reshape(A_GROUP, QB, QB) + madd[None]).reshape(gq, QB)
            if bias_idx is not None:
                s = s + bias_ref[bias_idx, n]
            m_prev = m_ref[n]
            m_new = jnp.maximum(m_prev, jnp.max(s, axis=-1, keepdims=True))
            alpha = jnp.exp(m_prev - m_new)
            p = jnp.exp(s - m_new)
            l_ref[n] = alpha * l_ref[n] + jnp.sum(p, axis=-1, keepdims=True)
            acc_ref[n] = alpha * acc_ref[n] + _dot(p.astype(BF16), vc)
            m_ref[n] = m_new

    def attn_body(c, carry):
        attn_chunk(c, None)
        return carry

    lax.fori_loop(0, jnp.maximum(i - 1, 0), attn_body, 0)

    @pl.when(i >= 1)
    def _():
        attn_chunk(i - 1, 0)

    attn_chunk(i, 1)

    for n in range(A_KV_HEADS):
        out = acc_ref[n] / l_ref[n]
        for g in range(A_GROUP):
            hd = n * A_GROUP + g
            o_ref[0, :, hd * A_HEAD_DIM:(hd + 1) * A_HEAD_DIM] = out[g * QB:(g + 1) * QB, :].astype(o_ref.dtype)


def _dsa(p16, p32, ln_g, ln_b, bias_near, *, topk):
    b, s, _ = p16.shape
    aq = A_HEADS * A_HEAD_DIM
    akv = A_KV_HEADS * A_HEAD_DIM
    qiw = IDX_HEADS * LANES
    nch = s // QB
    gq = A_GROUP * QB
    k_blk = (aq + X_HEADS * X_HEAD_DIM) // akv
    qi_blk = 2 * LRU_WIDTH // qiw
    ki_blk = (2 * LRU_WIDTH + qiw) // LANES
    return pl.pallas_call(
        functools.partial(_dsa_kernel, topk=topk),
        out_shape=jax.ShapeDtypeStruct((b, s, aq), BF16),
        grid=(b, nch),
        in_specs=[
            pl.BlockSpec((1, QB, aq), lambda bi, i: (bi, i, 0)),
            pl.BlockSpec((1, s, akv), lambda bi, i: (bi, 0, k_blk)),
            pl.BlockSpec((1, s, akv), lambda bi, i: (bi, 0, k_blk + 1)),
            pl.BlockSpec((1, QB, qiw), lambda bi, i: (bi, i, qi_blk)),
            pl.BlockSpec((1, s, LANES), lambda bi, i: (bi, 0, ki_blk)),
            pl.BlockSpec((1, QB, LANES), lambda bi, i: (bi, i, ki_blk + 1)),
            pl.BlockSpec((1, LANES), lambda bi, i: (0, 0)),
            pl.BlockSpec((1, LANES), lambda bi, i: (0, 0)),
            pl.BlockSpec((2, A_KV_HEADS, gq, QB), lambda bi, i: (0, 0, 0, 0)),
        ],
        out_specs=pl.BlockSpec((1, QB, aq), lambda bi, i: (bi, i, 0)),
        scratch_shapes=[
            pltpu.VMEM((s, LANES), BF16),
            pltpu.VMEM((nch, QB, LANES), I32),
            pltpu.VMEM((nch, QB, LANES), F32),
            pltpu.VMEM((IDX_HEADS, QB, LANES), F32),
            pltpu.VMEM((A_KV_HEADS, gq, A_HEAD_DIM), BF16),
            pltpu.VMEM((A_KV_HEADS, gq, 1), F32),
            pltpu.VMEM((A_KV_HEADS, gq, 1), F32),
            pltpu.VMEM((A_KV_HEADS, gq, A_HEAD_DIM), F32),
        ],
        compiler_params=_params("arbitrary", "arbitrary"),
        name="dsa",
    )(p16, p16, p16, p32, p32, p32, ln_g, ln_b, bias_near)


def _t5_bucket(dist):
    max_exact = N_BUCKETS // 2
    d = jnp.maximum(dist, 0)
    df = jnp.maximum(d, 1).astype(F32)
    large = max_exact + (jnp.log(df / max_exact) / math.log(MAX_DISTANCE / max_exact)
                         * (N_BUCKETS - max_exact)).astype(I32)
    large = jnp.minimum(large, N_BUCKETS - 1)
    return jnp.where(d < max_exact, d, large)


def _near_bias(rel_bias):
    t = jnp.arange(QB, dtype=I32)[:, None]
    s = jnp.arange(QB, dtype=I32)[None, :]
    tiles = []
    for off in (QB, 0):
        bucket = _t5_bucket(t - s + off)
        tile = rel_bias[bucket] - rel_bias[N_BUCKETS - 1]
        tile = tile.transpose(2, 0, 1).reshape(A_KV_HEADS, A_GROUP * QB, QB)
        tiles.append(tile)
    return jnp.stack(tiles).astype(F32)


def _pad_cols(w, width):
    return jnp.pad(w, ((0, 0), (0, width - w.shape[1])))


def _prep_layer(w_in, d_model):
    aq = A_HEADS * A_HEAD_DIM
    akv = A_KV_HEADS * A_HEAD_DIM
    sizes = (aq, akv, akv, IDX_HEADS * IDX_DIM, IDX_DIM, IDX_HEADS,
             LRU_WIDTH, LRU_WIDTH, X_HEADS * X_HEAD_DIM, N_BRANCH * d_model)
    pts, acc = [], 0
    for n in sizes[:-1]:
        acc += n
        pts.append(acc)
    wq, wk, wv, wqi, wki, wwi, wx, wgl, wqm, wgates = jnp.split(w_in, pts, axis=1)
    w16 = jnp.concatenate([wq * (A_HEAD_DIM ** -0.5), wqm * (X_HEAD_DIM ** -0.5), wk, wv], axis=1)
    wqi_p = jnp.concatenate(
        [_pad_cols(wqi[:, h * IDX_DIM:(h + 1) * IDX_DIM], LANES) for h in range(IDX_HEADS)], axis=1)
    w32 = jnp.concatenate([wx, wgl, wqi_p, _pad_cols(wki, LANES), _pad_cols(wwi, LANES)], axis=1)
    return w16.astype(BF16), w32.astype(BF16), wgates.astype(BF16)


def _pad_ff(w_gu, w_down, tf):
    d_ff = w_down.shape[0]
    fp = -(-d_ff // tf) * tf
    wg = _pad_cols(w_gu[:, :d_ff], fp).astype(BF16)
    wu = _pad_cols(w_gu[:, d_ff:], fp).astype(BF16)
    wd = jnp.pad(w_down, ((0, fp - d_ff), (0, 0))).astype(BF16)
    return wg, wu, wd


def _tiles(m, s, d_model):
    return dict(
        ffn_tm=min(512, m), ffn_tf=512,
        proj_tm=min(512, m), proj_tn=256,
        merge_tm=min(512, m), merge_tn=min(256, d_model),
        lru_tc=min(256, s), mem_tm=min(512, s),
    )


def kernel(x, mem, rel_bias, final_norm, norm_ff1, w_ff1_gu, w_ff1_down, norm_mix, w_in, conv_w, conv_b, w_a, b_a, w_i, b_i, lam, idx_ln_g, idx_ln_b, mem_norm, w_mem_kv, w_branch, w_out, norm_ff2, w_ff2_gu, w_ff2_down):
    b, s, d = x.shape
    nm = mem.shape[1]
    depth = w_in.shape[0]
    m = b * s
    assert s % QB == 0 and d % LANES == 0
    tl = _tiles(m, s, d)
    topk = min(TOPK_MAX, s // 4)
    bias_near = _near_bias(rel_bias)
    row = lambda v: v.reshape(1, -1)

    x2 = x.reshape(m, d)
    mem2 = mem.reshape(b * nm, d)
    for l in range(depth):
        wg, wu, wd = _pad_ff(w_ff1_gu[l], w_ff1_down[l], tl["ffn_tf"])
        x2 = _ffn(x2, row(norm_ff1[l]), wg, wu, wd, row(final_norm), final_norm=False,
                  tm=tl["ffn_tm"], tf=tl["ffn_tf"])

        w16, w32, wgates = _prep_layer(w_in[l], d)
        p16 = _norm_matmul(x2, row(norm_mix[l]), w16, BF16, tm=tl["proj_tm"], tn=tl["proj_tn"], name="proj16")
        p32 = _norm_matmul(x2, row(norm_mix[l]), w32, F32, tm=tl["proj_tm"], tn=tl["proj_tn"], name="proj32")
        p16 = p16.reshape(b, s, -1)
        p32 = p32.reshape(b, s, -1)

        y_a = _dsa(p16, p32, row(_pad_cols(row(idx_ln_g[l]), LANES)), row(_pad_cols(row(idx_ln_b[l]), LANES)),
                   bias_near, topk=topk)
        y_b = _lru(p32, conv_w[l], row(conv_b[l]), w_a[l].astype(BF16), w_i[l].astype(BF16),
                   row(b_a[l]), row(b_i[l]), row(lam[l]), tc=tl["lru_tc"])
        mkv = _norm_matmul(mem2, row(mem_norm[l]), w_mem_kv[l].astype(BF16), BF16,
                           tm=min(256, b * nm), tn=tl["proj_tn"], name="memkv")
        y_c = _memattn(p16, mkv.reshape(b, nm, -1), tm=tl["mem_tm"])

        x2 = _merge(x2, row(norm_mix[l]), y_a.reshape(m, -1), y_b.reshape(m, -1), y_c.reshape(m, -1),
                    wgates, w_branch[l].astype(BF16), w_out[l].astype(BF16),
                    tm=tl["merge_tm"], tn=tl["merge_tn"])

        wg, wu, wd = _pad_ff(w_ff2_gu[l], w_ff2_down[l], tl["ffn_tf"])
        x2 = _ffn(x2, row(norm_ff2[l]), wg, wu, wd, row(final_norm), final_norm=(l == depth - 1),
                  tm=tl["ffn_tm"], tf=tl["ffn_tf"])
    return x2.reshape(b, s, d)
```

```python
import functools
import math

import jax
import jax.numpy as jnp
from jax import lax
from jax.experimental import pallas as pl
from jax.experimental.pallas import tpu as pltpu

F32 = jnp.float32
BF16 = jnp.bfloat16
I32 = jnp.int32

EPS = 1e-6
A_HEADS = 8
A_KV_HEADS = 2
A_HEAD_DIM = 128
A_GROUP = A_HEADS // A_KV_HEADS
IDX_HEADS = 4
IDX_DIM = 64
TOPK_MAX = 256
LRU_WIDTH = 1024
LRU_BLOCKS = 8
LRU_BLOCK_W = LRU_WIDTH // LRU_BLOCKS
CONV_WIDTH = 4
LRU_C = 8.0
X_HEADS = 4
X_HEAD_DIM = 256
N_BRANCH = 3
BRANCH_WIDTH = 1024
N_BUCKETS = 32
MAX_DISTANCE = 128

LANES = 128
SUBLANES = 8
VMEM_LIMIT = 56 * 1024 * 1024

QB = 128
SC = 512
SUB = SC // QB
NEG = -1e30
INT_MIN = -2 ** 31


def _params(*sem):
    return pltpu.CompilerParams(dimension_semantics=sem, vmem_limit_bytes=VMEM_LIMIT)


def _rms(x, g):
    ms = jnp.mean(x * x, axis=-1, keepdims=True)
    return x * lax.rsqrt(ms + EPS) * g


def _sigmoid(x):
    return 1.0 / (1.0 + jnp.exp(-x))


def _dot(a, b):
    return jnp.dot(a, b, preferred_element_type=F32)


def _dot_nt(a, b):
    return lax.dot_general(a, b, (((1,), (1,)), ((), ())), preferred_element_type=F32)


def _ffn_kernel(x_ref, g_ref, wg_ref, wu_ref, wd_ref, fg_ref, o_ref, h_ref, *, final_norm):
    f = pl.program_id(1)

    @pl.when(f == 0)
    def _():
        h_ref[...] = _rms(x_ref[...], g_ref[...]).astype(BF16)
        o_ref[...] = jnp.zeros_like(o_ref)

    h = h_ref[...]
    g = _dot(h, wg_ref[...])
    u = _dot(h, wu_ref[...])
    a = (g * _sigmoid(g) * u).astype(BF16)
    o_ref[...] += _dot(a, wd_ref[...])

    @pl.when(f == pl.num_programs(1) - 1)
    def _():
        y = x_ref[...] + 0.5 * o_ref[...]
        if final_norm:
            y = _rms(y, fg_ref[...])
        o_ref[...] = y


def _ffn(x2, g, wg, wu, wd, fg, *, final_norm, tm, tf):
    m, d = x2.shape
    fp = wg.shape[1]
    return pl.pallas_call(
        functools.partial(_ffn_kernel, final_norm=final_norm),
        out_shape=jax.ShapeDtypeStruct((m, d), F32),
        grid=(m // tm, fp // tf),
        in_specs=[
            pl.BlockSpec((tm, d), lambda i, f: (i, 0)),
            pl.BlockSpec((1, d), lambda i, f: (0, 0)),
            pl.BlockSpec((d, tf), lambda i, f: (0, f)),
            pl.BlockSpec((d, tf), lambda i, f: (0, f)),
            pl.BlockSpec((tf, d), lambda i, f: (f, 0)),
            pl.BlockSpec((1, d), lambda i, f: (0, 0)),
        ],
        out_specs=pl.BlockSpec((tm, d), lambda i, f: (i, 0)),
        scratch_shapes=[pltpu.VMEM((tm, d), BF16)],
        compiler_params=_params("parallel", "arbitrary"),
        name="ffn",
    )(x2, g, wg, wu, wd, fg)


def _norm_matmul_kernel(x_ref, g_ref, w_ref, o_ref, h_ref):
    @pl.when(pl.program_id(1) == 0)
    def _():
        h_ref[...] = _rms(x_ref[...], g_ref[...]).astype(BF16)

    o_ref[...] = _dot(h_ref[...], w_ref[...]).astype(o_ref.dtype)


def _norm_matmul(x2, g, w, out_dtype, *, tm, tn, name):
    m, d = x2.shape
    n = w.shape[1]
    return pl.pallas_call(
        _norm_matmul_kernel,
        out_shape=jax.ShapeDtypeStruct((m, n), out_dtype),
        grid=(m // tm, n // tn),
        in_specs=[
            pl.BlockSpec((tm, d), lambda i, j: (i, 0)),
            pl.BlockSpec((1, d), lambda i, j: (0, 0)),
            pl.BlockSpec((d, tn), lambda i, j: (0, j)),
        ],
        out_specs=pl.BlockSpec((tm, tn), lambda i, j: (i, j)),
        scratch_shapes=[pltpu.VMEM((tm, d), BF16)],
        compiler_params=_params("parallel", "arbitrary"),
        name=name,
    )(x2, g, w)


def _lru_kernel(x_ref, gate_ref, cw_ref, cb_ref, wa_ref, wi_ref, ba_ref, bi_ref, lam_ref,
                o_ref, xbuf_ref, hc_ref, *, tc):
    halo = SUBLANES

    @pl.when(pl.program_id(1) == 0)
    def _():
        xbuf_ref[0:halo, :] = jnp.zeros((halo, LRU_WIDTH), F32)
        hc_ref[...] = jnp.zeros_like(hc_ref)

    xbuf_ref[halo:halo + tc, :] = x_ref[0]
    base = halo - (CONV_WIDTH - 1)
    xc = xbuf_ref[base:base + tc, :] * cw_ref[0:1, :]
    for j in range(1, CONV_WIDTH):
        xc = xc + xbuf_ref[base + j:base + j + tc, :] * cw_ref[j:j + 1, :]
    xc = cb_ref[...] + xc
    xbuf_ref[0:halo, :] = xbuf_ref[tc:tc + halo, :]

    xcb = xc.astype(BF16)
    r_parts, i_parts = [], []
    for n in range(LRU_BLOCKS):
        blk = xcb[:, n * LRU_BLOCK_W:(n + 1) * LRU_BLOCK_W]
        r_parts.append(_dot(blk, wa_ref[n]))
        i_parts.append(_dot(blk, wi_ref[n]))
    r = _sigmoid(jnp.concatenate(r_parts, axis=1) + ba_ref[...])
    gi = _sigmoid(jnp.concatenate(i_parts, axis=1) + bi_ref[...])

    softplus_neg_lam = jnp.log1p(jnp.exp(-lam_ref[...]))
    log_a = (-LRU_C) * r * softplus_neg_lam
    a = jnp.exp(log_a)
    th = jnp.tanh(log_a)
    mult = jnp.sqrt(jnp.maximum(-2.0 * th / (1.0 - th), 0.0))
    u = mult * (gi * xc)

    row = lax.broadcasted_iota(I32, (tc, LRU_WIDTH), 0)
    d = 1
    while d < tc:
        keep = row >= d
        a_sh = jnp.where(keep, pltpu.roll(a, d, 0), 1.0)
        u_sh = jnp.where(keep, pltpu.roll(u, d, 0), 0.0)
        u = a * u_sh + u
        a = a * a_sh
        d *= 2
    h = a * hc_ref[...] + u
    hc_ref[...] = h[tc - 1:tc, :]

    gate = gate_ref[0]
    gelu = 0.5 * gate * (1.0 + jnp.tanh(math.sqrt(2.0 / math.pi) * (gate + 0.044715 * (gate * gate * gate))))
    o_ref[0] = (h * gelu).astype(o_ref.dtype)


def _lru(p32, conv_w, conv_b, w_a, w_i, b_a, b_i, lam, *, tc):
    b, s, _ = p32.shape
    vec = pl.BlockSpec((1, LRU_WIDTH), lambda bi, c: (0, 0))
    wblk = pl.BlockSpec((LRU_BLOCKS, LRU_BLOCK_W, LRU_BLOCK_W), lambda bi, c: (0, 0, 0))
    return pl.pallas_call(
        functools.partial(_lru_kernel, tc=tc),
        out_shape=jax.ShapeDtypeStruct((b, s, LRU_WIDTH), BF16),
        grid=(b, s // tc),
        in_specs=[
            pl.BlockSpec((1, tc, LRU_WIDTH), lambda bi, c: (bi, c, 0)),
            pl.BlockSpec((1, tc, LRU_WIDTH), lambda bi, c: (bi, c, 1)),
            pl.BlockSpec((CONV_WIDTH, LRU_WIDTH), lambda bi, c: (0, 0)),
            vec, wblk, wblk, vec, vec, vec,
        ],
        out_specs=pl.BlockSpec((1, tc, LRU_WIDTH), lambda bi, c: (bi, c, 0)),
        scratch_shapes=[pltpu.VMEM((tc + SUBLANES, LRU_WIDTH), F32), pltpu.VMEM((1, LRU_WIDTH), F32)],
        compiler_params=_params("arbitrary", "arbitrary"),
        name="rglru",
    )(p32, p32, conv_w, conv_b, w_a, w_i, b_a, b_i, lam)


def _memattn_kernel(q_ref, kv_ref, o_ref):
    kw = X_HEADS * X_HEAD_DIM
    for h in range(X_HEADS):
        sl = slice(h * X_HEAD_DIM, (h + 1) * X_HEAD_DIM)
        q = q_ref[0, :, sl]
        k = kv_ref[0, :, sl]
        v = kv_ref[0, :, kw + h * X_HEAD_DIM:kw + (h + 1) * X_HEAD_DIM]
        s = _dot_nt(q, k)
        m = jnp.max(s, axis=-1, keepdims=True)
        p = jnp.exp(s - m)
        l = jnp.sum(p, axis=-1, keepdims=True)
        o = _dot(p.astype(BF16), v)
        o_ref[0, :, sl] = (o / l).astype(o_ref.dtype)


def _memattn(p16, mkv, *, tm):
    b, s, _ = p16.shape
    nm = mkv.shape[1]
    kw = X_HEADS * X_HEAD_DIM
    return pl.pallas_call(
        _memattn_kernel,
        out_shape=jax.ShapeDtypeStruct((b, s, kw), BF16),
        grid=(b, s // tm),
        in_specs=[
            pl.BlockSpec((1, tm, kw), lambda bi, i: (bi, i, 1)),
            pl.BlockSpec((1, nm, 2 * kw), lambda bi, i: (bi, 0, 0)),
        ],
        out_specs=pl.BlockSpec((1, tm, kw), lambda bi, i: (bi, i, 0)),
        compiler_params=_params("parallel", "parallel"),
        name="memattn",
    )(p16, mkv)


def _merge_kernel(x_ref, g_ref, ya_ref, yb_ref, yc_ref, wg0_ref, wg1_ref, wg2_ref, wb_ref, wo_ref,
                  o_ref, h_ref):
    n = pl.program_id(1)

    @pl.when(n == 0)
    def _():
        h_ref[...] = _rms(x_ref[...], g_ref[...]).astype(BF16)
        o_ref[...] = jnp.zeros_like(o_ref)

    h = h_ref[...]
    merged = None
    for j, (y_ref, wg_ref) in enumerate(((ya_ref, wg0_ref), (yb_ref, wg1_ref), (yc_ref, wg2_ref))):
        term = _sigmoid(_dot(h, wg_ref[...])) * _dot(y_ref[...], wb_ref[j])
        merged = term if merged is None else merged + term
    o_ref[...] += _dot(merged.astype(BF16), wo_ref[...])

    @pl.when(n == pl.num_programs(1) - 1)
    def _():
        o_ref[...] = x_ref[...] + o_ref[...]


def _merge(x2, g, ya, yb, yc, wgates, wbranch, wout, *, tm, tn):
    m, d = x2.shape
    nblk = d // tn
    yspec = pl.BlockSpec((tm, BRANCH_WIDTH), lambda i, n: (i, 0))
    return pl.pallas_call(
        _merge_kernel,
        out_shape=jax.ShapeDtypeStruct((m, d), F32),
        grid=(m // tm, nblk),
        in_specs=[
            pl.BlockSpec((tm, d), lambda i, n: (i, 0)),
            pl.BlockSpec((1, d), lambda i, n: (0, 0)),
            yspec, yspec, yspec,
            pl.BlockSpec((d, tn), lambda i, n: (0, n)),
            pl.BlockSpec((d, tn), lambda i, n: (0, nblk + n)),
            pl.BlockSpec((d, tn), lambda i, n: (0, 2 * nblk + n)),
            pl.BlockSpec((N_BRANCH, BRANCH_WIDTH, tn), lambda i, n: (0, 0, n)),
            pl.BlockSpec((tn, d), lambda i, n: (n, 0)),
        ],
        out_specs=pl.BlockSpec((tm, d), lambda i, n: (i, 0)),
        scratch_shapes=[pltpu.VMEM((tm, d), BF16)],
        compiler_params=_params("parallel", "arbitrary"),
        name="merge",
    )(x2, g, ya, yb, yc, wgates, wgates, wgates, wbranch, wout)


def _dsa_kernel(q_ref, k_ref, vt_ref, qi_ref, ki_ref, wi_ref, lng_ref, lnb_ref, bias_ref,
                o_ref,
                kln_ref, keys_ref, madd_ref, q4_ref, m_ref, l_ref, acc_ref, *, topk):
    i = pl.program_id(1)
    s_len = k_ref.shape[1]

    @pl.when(i == 0)
    def _():
        rows = 512 if s_len % 512 == 0 else QB
        lane = lax.broadcasted_iota(I32, (rows, LANES), 1)
        real = lane < IDX_DIM

        def ln_body(c, carry):
            x = ki_ref[0, pl.ds(c * rows, rows), :]
            mu = jnp.sum(x, axis=-1, keepdims=True) * (1.0 / IDX_DIM)
            xm = jnp.where(real, x - mu, 0.0)
            var = jnp.sum(xm * xm, axis=-1, keepdims=True) * (1.0 / IDX_DIM)
            y = xm * lax.rsqrt(var + EPS) * lng_ref[...] + lnb_ref[...]
            kln_ref[pl.ds(c * rows, rows), :] = y.astype(BF16)
            return carry

        lax.fori_loop(0, s_len // rows, ln_body, 0)

    w_rows = wi_ref[0].T * (IDX_HEADS ** -0.5 * IDX_DIM ** -0.5)
    qib = qi_ref[0].astype(BF16)
    for n in range(A_KV_HEADS):
        for g in range(A_GROUP):
            hd = n * A_GROUP + g
            q4_ref[n, g * QB:(g + 1) * QB, :] = q_ref[0, :, hd * A_HEAD_DIM:(hd + 1) * A_HEAD_DIM]

    last = i // SUB
    nsc = last + 1

    def rows(j):
        return pl.ds(pl.multiple_of(j * SC, SC), SC)

    def score_chunk(j, causal):
        kc = kln_ref[rows(j), :]
        sc = None
        for h in range(IDX_HEADS):
            d = _dot_nt(kc, qib[:, h * LANES:(h + 1) * LANES])
            t = w_rows[h:h + 1, :] * jnp.maximum(d, 0.0)
            sc = t if sc is None else sc + t
        bits = pltpu.bitcast(sc, I32)
        key = jnp.where(bits < 0, bits ^ jnp.int32(0x7FFFFFFF), bits)
        key = jnp.where(sc == 0.0, 0, key)
        if causal:
            key_pos = j * SC + lax.broadcasted_iota(I32, (SC, LANES), 0)
            q_pos = i * QB + lax.broadcasted_iota(I32, (SC, LANES), 1)
            key = jnp.where(key_pos <= q_pos, key, INT_MIN)
        keys_ref[rows(j), :] = key

    def score_body(j, carry):
        score_chunk(j, False)
        return carry

    lax.fori_loop(0, last, score_body, 0)
    score_chunk(last, True)

    t_pos = i * QB + lax.broadcasted_iota(I32, (1, LANES), 1)
    kk = jnp.minimum(t_pos + 1, topk).astype(F32)

    def count_ge(cand):
        def body(j, cnt):
            for r in range(SUB):
                blk = keys_ref[pl.ds(pl.multiple_of(j * SC + r * QB, QB), QB), :]
                cnt = cnt + jnp.where(blk >= cand, 1.0, 0.0)
            return cnt

        cnt = lax.fori_loop(0, nsc, body, jnp.zeros((QB, LANES), F32))
        return jnp.sum(cnt, axis=0, keepdims=True)

    def bit_body(bi, thr):
        cand = thr + lax.shift_left(jnp.int32(1), 31 - bi)
        return jnp.where(count_ge(cand) >= kk, cand, thr)

    thr = lax.fori_loop(0, 32, bit_body, jnp.full((1, LANES), INT_MIN, I32))
    need = kk - count_ge(thr + 1)
    sub_k = lax.broadcasted_iota(I32, (QB, QB), 0)
    sub_k2 = lax.broadcasted_iota(I32, (QB, QB), 1)
    tril = (sub_k2 <= sub_k).astype(BF16)

    def mask_body(j, seen):
        keys, eqs, pres = [], [], []
        for r in range(SUB):
            key = keys_ref[pl.ds(pl.multiple_of(j * SC + r * QB, QB), QB), :]
            eq = key == thr
            keys.append(key)
            eqs.append(eq)
            pres.append(_dot(tril, jnp.where(eq, 1.0, 0.0).astype(BF16)))
        for r in range(SUB):
            sel = (keys[r] > thr) | (eqs[r] & ((seen + pres[r]) <= need))
            madd_ref[pl.ds(pl.multiple_of(j * SC + r * QB, QB), QB), :] = jnp.where(sel, 0.0, NEG)
            seen = seen + pres[r][QB - 1:QB, :]
        return seen

    lax.fori_loop(0, nsc, mask_body, jnp.zeros((1, LANES), F32))

    m_ref[...] = jnp.full(m_ref.shape, NEG, F32)
    l_ref[...] = jnp.zeros(l_ref.shape, F32)
    acc_ref[...] = jnp.zeros(acc_ref.shape, F32)

    def attn_chunk(j, near):
        madd = madd_ref[rows(j), :]
        madd4 = jnp.concatenate([madd] * A_GROUP, axis=1)
        for n in range(A_KV_HEADS):
            kc = k_ref[0, rows(j), n * A_HEAD_DIM:(n + 1) * A_HEAD_DIM]
            vt = vt_ref[0, j, n * A_HEAD_DIM:(n + 1) * A_HEAD_DIM, :]
            s = _dot_nt(kc, q4_ref[n]) + madd4
            if near:
                parts = []
                for r in range(SUB):
                    blk = j * SUB + r
                    is_diag = (blk == i).astype(F32)
                    is_prev = (blk == i - 1).astype(F32)
                    parts.append(s[r * QB:(r + 1) * QB, :] + is_diag * bias_ref[1, n] + is_prev * bias_ref[0, n])
                s = jnp.concatenate(parts, axis=0)
            m_prev = m_ref[n]
            m_new = jnp.maximum(m_prev, jnp.max(s, axis=0, keepdims=True))
            alpha = jnp.exp(m_prev - m_new)
            p = jnp.exp(s - m_new)
            l_ref[n] = alpha * l_ref[n] + jnp.sum(p, axis=0, keepdims=True)
            acc_ref[n] = alpha * acc_ref[n] + _dot(vt, p.astype(BF16))
            m_ref[n] = m_new

    def attn_body(j, carry):
        attn_chunk(j, False)
        return carry

    lax.fori_loop(0, jnp.maximum(last - 1, 0), attn_body, 0)

    @pl.when(last >= 1)
    def _():
        attn_chunk(last - 1, True)

    attn_chunk(last, True)

    for n in range(A_KV_HEADS):
        out_t = acc_ref[n] / l_ref[n]
        for g in range(A_GROUP):
            hd = n * A_GROUP + g
            o_ref[0, :, hd * A_HEAD_DIM:(hd + 1) * A_HEAD_DIM] = out_t[:, g * QB:(g + 1) * QB].T.astype(o_ref.dtype)


def _dsa(p16, p32, vt, ln_g, ln_b, bias_near, *, topk):
    b, s, _ = p16.shape
    aq = A_HEADS * A_HEAD_DIM
    akv = A_KV_HEADS * A_HEAD_DIM
    qiw = IDX_HEADS * LANES
    nch = s // QB
    gq = A_GROUP * QB
    k_blk = (aq + X_HEADS * X_HEAD_DIM) // akv
    qi_blk = 2 * LRU_WIDTH // qiw
    ki_blk = (2 * LRU_WIDTH + qiw) // LANES
    return pl.pallas_call(
        functools.partial(_dsa_kernel, topk=topk),
        out_shape=jax.ShapeDtypeStruct((b, s, aq), BF16),
        grid=(b, nch),
        in_specs=[
            pl.BlockSpec((1, QB, aq), lambda bi, i: (bi, i, 0)),
            pl.BlockSpec((1, s, akv), lambda bi, i: (bi, 0, k_blk)),
            pl.BlockSpec((1, s // SC, akv, SC), lambda bi, i: (bi, 0, 0, 0)),
            pl.BlockSpec((1, QB, qiw), lambda bi, i: (bi, i, qi_blk)),
            pl.BlockSpec((1, s, LANES), lambda bi, i: (bi, 0, ki_blk)),
            pl.BlockSpec((1, QB, LANES), lambda bi, i: (bi, i, ki_blk + 1)),
            pl.BlockSpec((1, LANES), lambda bi, i: (0, 0)),
            pl.BlockSpec((1, LANES), lambda bi, i: (0, 0)),
            pl.BlockSpec((2, A_KV_HEADS, QB, gq), lambda bi, i: (0, 0, 0, 0)),
        ],
        out_specs=pl.BlockSpec((1, QB, aq), lambda bi, i: (bi, i, 0)),
        scratch_shapes=[
            pltpu.VMEM((s, LANES), BF16),
            pltpu.VMEM((s, LANES), I32),
            pltpu.VMEM((s, LANES), F32),
            pltpu.VMEM((A_KV_HEADS, gq, A_HEAD_DIM), BF16),
            pltpu.VMEM((A_KV_HEADS, 1, gq), F32),
            pltpu.VMEM((A_KV_HEADS, 1, gq), F32),
            pltpu.VMEM((A_KV_HEADS, A_HEAD_DIM, gq), F32),
        ],
        compiler_params=_params("arbitrary", "arbitrary"),
        name="dsa",
    )(p16, p16, vt, p32, p32, p32, ln_g, ln_b, bias_near)


def _t5_bucket(dist):
    max_exact = N_BUCKETS // 2
    d = jnp.maximum(dist, 0)
    df = jnp.maximum(d, 1).astype(F32)
    large = max_exact + (jnp.log(df / max_exact) / math.log(MAX_DISTANCE / max_exact)
                         * (N_BUCKETS - max_exact)).astype(I32)
    large = jnp.minimum(large, N_BUCKETS - 1)
    return jnp.where(d < max_exact, d, large)


def _near_bias(rel_bias):
    t = jnp.arange(QB, dtype=I32)[:, None]
    s = jnp.arange(QB, dtype=I32)[None, :]
    tiles = []
    for off in (QB, 0):
        bucket = _t5_bucket(t - s + off)
        tile = rel_bias[bucket] - rel_bias[N_BUCKETS - 1]
        tile = tile.transpose(1, 2, 0).reshape(QB, A_KV_HEADS, A_GROUP * QB)
        tiles.append(tile.transpose(1, 0, 2))
    return jnp.stack(tiles).astype(F32)


def _pad_cols(w, width):
    return jnp.pad(w, ((0, 0), (0, width - w.shape[1])))


def _prep_layer(w_in, d_model):
    aq = A_HEADS * A_HEAD_DIM
    akv = A_KV_HEADS * A_HEAD_DIM
    sizes = (aq, akv, akv, IDX_HEADS * IDX_DIM, IDX_DIM, IDX_HEADS,
             LRU_WIDTH, LRU_WIDTH, X_HEADS * X_HEAD_DIM, N_BRANCH * d_model)
    pts, acc = [], 0
    for n in sizes[:-1]:
        acc += n
        pts.append(acc)
    wq, wk, wv, wqi, wki, wwi, wx, wgl, wqm, wgates = jnp.split(w_in, pts, axis=1)
    w16 = jnp.concatenate([wq * (A_HEAD_DIM ** -0.5), wqm * (X_HEAD_DIM ** -0.5), wk, wv], axis=1)
    wqi_p = jnp.concatenate(
        [_pad_cols(wqi[:, h * IDX_DIM:(h + 1) * IDX_DIM], LANES) for h in range(IDX_HEADS)], axis=1)
    w32 = jnp.concatenate([wx, wgl, wqi_p, _pad_cols(wki, LANES), _pad_cols(wwi, LANES)], axis=1)
    return w16.astype(BF16), w32.astype(BF16), wgates.astype(BF16)


def _pad_ff(w_gu, w_down, tf):
    d_ff = w_down.shape[0]
    fp = -(-d_ff // tf) * tf
    wg = _pad_cols(w_gu[:, :d_ff], fp).astype(BF16)
    wu = _pad_cols(w_gu[:, d_ff:], fp).astype(BF16)
    wd = jnp.pad(w_down, ((0, fp - d_ff), (0, 0))).astype(BF16)
    return wg, wu, wd


def _tiles(m, s, d_model):
    return dict(
        ffn_tm=min(512, m), ffn_tf=512,
        proj_tm=min(512, m), proj_tn=256,
        merge_tm=min(512, m), merge_tn=min(256, d_model),
        lru_tc=min(256, s), mem_tm=min(512, s),
    )


def kernel(x, mem, rel_bias, final_norm, norm_ff1, w_ff1_gu, w_ff1_down, norm_mix, w_in, conv_w, conv_b, w_a, b_a, w_i, b_i, lam, idx_ln_g, idx_ln_b, mem_norm, w_mem_kv, w_branch, w_out, norm_ff2, w_ff2_gu, w_ff2_down):
    b, s, d = x.shape
    nm = mem.shape[1]
    depth = w_in.shape[0]
    m = b * s
    assert s % SC == 0 and d % LANES == 0
    tl = _tiles(m, s, d)
    topk = min(TOPK_MAX, s // 4)
    bias_near = _near_bias(rel_bias)
    row = lambda v: v.reshape(1, -1)

    x2 = x.reshape(m, d)
    mem2 = mem.reshape(b * nm, d)
    for l in range(depth):
        wg, wu, wd = _pad_ff(w_ff1_gu[l], w_ff1_down[l], tl["ffn_tf"])
        x2 = _ffn(x2, row(norm_ff1[l]), wg, wu, wd, row(final_norm), final_norm=False,
                  tm=tl["ffn_tm"], tf=tl["ffn_tf"])

        w16, w32, wgates = _prep_layer(w_in[l], d)
        p16 = _norm_matmul(x2, row(norm_mix[l]), w16, BF16, tm=tl["proj_tm"], tn=tl["proj_tn"], name="proj16")
        p32 = _norm_matmul(x2, row(norm_mix[l]), w32, F32, tm=tl["proj_tm"], tn=tl["proj_tn"], name="proj32")
        p16 = p16.reshape(b, s, -1)
        p32 = p32.reshape(b, s, -1)

        v_cols = slice(p16.shape[-1] - A_KV_HEADS * A_HEAD_DIM, p16.shape[-1])
        vt = p16[..., v_cols].reshape(b, s // SC, SC, -1).transpose(0, 1, 3, 2)
        y_a = _dsa(p16, p32, vt, row(_pad_cols(row(idx_ln_g[l]), LANES)), row(_pad_cols(row(idx_ln_b[l]), LANES)),
                   bias_near, topk=topk)
        y_b = _lru(p32, conv_w[l], row(conv_b[l]), w_a[l].astype(BF16), w_i[l].astype(BF16),
                   row(b_a[l]), row(b_i[l]), row(lam[l]), tc=tl["lru_tc"])
        mkv = _norm_matmul(mem2, row(mem_norm[l]), w_mem_kv[l].astype(BF16), BF16,
                           tm=min(256, b * nm), tn=tl["proj_tn"], name="memkv")
        y_c = _memattn(p16, mkv.reshape(b, nm, -1), tm=tl["mem_tm"])

        x2 = _merge(x2, row(norm_mix[l]), y_a.reshape(m, -1), y_b.reshape(m, -1), y_c.reshape(m, -1),
                    wgates, w_branch[l].astype(BF16), w_out[l].astype(BF16),
                    tm=tl["merge_tm"], tn=tl["merge_tn"])

        wg, wu, wd = _pad_ff(w_ff2_gu[l], w_ff2_down[l], tl["ffn_tf"])
        x2 = _ffn(x2, row(norm_ff2[l]), wg, wu, wd, row(final_norm), final_norm=(l == depth - 1),
                  tm=tl["ffn_tm"], tf=tl["ffn_tf"])
    return x2.reshape(b, s, d)
```

```python
import functools
import math

import jax
import jax.numpy as jnp
from jax import lax
from jax.experimental import pallas as pl
from jax.experimental.pallas import tpu as pltpu

F32 = jnp.float32
BF16 = jnp.bfloat16
I32 = jnp.int32

EPS = 1e-6
A_HEADS = 8
A_KV_HEADS = 2
A_HEAD_DIM = 128
A_GROUP = A_HEADS // A_KV_HEADS
IDX_HEADS = 4
IDX_DIM = 64
TOPK_MAX = 256
LRU_WIDTH = 1024
LRU_BLOCKS = 8
LRU_BLOCK_W = LRU_WIDTH // LRU_BLOCKS
CONV_WIDTH = 4
LRU_C = 8.0
X_HEADS = 4
X_HEAD_DIM = 256
N_BRANCH = 3
BRANCH_WIDTH = 1024
N_BUCKETS = 32
MAX_DISTANCE = 128

LANES = 128
SUBLANES = 8
VMEM_LIMIT = 56 * 1024 * 1024

QB = 128
SC = 512
SUB = SC // QB
NEG = -1e30
LOG2E = math.log2(math.e)
INT_MIN = -2 ** 31


def _params(*sem):
    return pltpu.CompilerParams(dimension_semantics=sem, vmem_limit_bytes=VMEM_LIMIT)


def _rms(x, g):
    ms = jnp.mean(x * x, axis=-1, keepdims=True)
    return x * lax.rsqrt(ms + EPS) * g


def _sigmoid(x):
    return 1.0 / (1.0 + jnp.exp(-x))


def _dot(a, b):
    return jnp.dot(a, b, preferred_element_type=F32)


def _dot_nt(a, b):
    return lax.dot_general(a, b, (((1,), (1,)), ((), ())), preferred_element_type=F32)


def _ffn_kernel(x_ref, g_ref, wg_ref, wu_ref, wd_ref, fg_ref, o_ref, h_ref, *, final_norm):
    f = pl.program_id(1)

    @pl.when(f == 0)
    def _():
        h_ref[...] = _rms(x_ref[...], g_ref[...]).astype(BF16)
        o_ref[...] = jnp.zeros_like(o_ref)

    h = h_ref[...]
    g = _dot(h, wg_ref[...])
    u = _dot(h, wu_ref[...])
    a = (g * _sigmoid(g) * u).astype(BF16)
    o_ref[...] += _dot(a, wd_ref[...])

    @pl.when(f == pl.num_programs(1) - 1)
    def _():
        y = x_ref[...] + 0.5 * o_ref[...]
        if final_norm:
            y = _rms(y, fg_ref[...])
        o_ref[...] = y


def _ffn(x2, g, wg, wu, wd, fg, *, final_norm, tm, tf):
    m, d = x2.shape
    fp = wg.shape[1]
    return pl.pallas_call(
        functools.partial(_ffn_kernel, final_norm=final_norm),
        out_shape=jax.ShapeDtypeStruct((m, d), F32),
        grid=(m // tm, fp // tf),
        in_specs=[
            pl.BlockSpec((tm, d), lambda i, f: (i, 0)),
            pl.BlockSpec((1, d), lambda i, f: (0, 0)),
            pl.BlockSpec((d, tf), lambda i, f: (0, f)),
            pl.BlockSpec((d, tf), lambda i, f: (0, f)),
            pl.BlockSpec((tf, d), lambda i, f: (f, 0)),
            pl.BlockSpec((1, d), lambda i, f: (0, 0)),
        ],
        out_specs=pl.BlockSpec((tm, d), lambda i, f: (i, 0)),
        scratch_shapes=[pltpu.VMEM((tm, d), BF16)],
        compiler_params=_params("parallel", "arbitrary"),
        name="ffn",
    )(x2, g, wg, wu, wd, fg)


def _norm_matmul_kernel(x_ref, g_ref, w_ref, o_ref, h_ref):
    @pl.when(pl.program_id(1) == 0)
    def _():
        h_ref[...] = _rms(x_ref[...], g_ref[...]).astype(BF16)

    o_ref[...] = _dot(h_ref[...], w_ref[...]).astype(o_ref.dtype)


def _norm_matmul(x2, g, w, out_dtype, *, tm, tn, name):
    m, d = x2.shape
    n = w.shape[1]
    return pl.pallas_call(
        _norm_matmul_kernel,
        out_shape=jax.ShapeDtypeStruct((m, n), out_dtype),
        grid=(m // tm, n // tn),
        in_specs=[
            pl.BlockSpec((tm, d), lambda i, j: (i, 0)),
            pl.BlockSpec((1, d), lambda i, j: (0, 0)),
            pl.BlockSpec((d, tn), lambda i, j: (0, j)),
        ],
        out_specs=pl.BlockSpec((tm, tn), lambda i, j: (i, j)),
        scratch_shapes=[pltpu.VMEM((tm, d), BF16)],
        compiler_params=_params("parallel", "arbitrary"),
        name=name,
    )(x2, g, w)


def _proj_kernel(x_ref, g_ref, w16_ref, w32_ref, o16_ref, o32_ref):
    h = _rms(x_ref[...], g_ref[...]).astype(BF16)
    o16_ref[...] = _dot(h, w16_ref[...]).astype(o16_ref.dtype)
    o32_ref[...] = _dot(h, w32_ref[...])


def _proj(x2, g, w16, w32, *, tm):
    m, d = x2.shape
    n16, n32 = w16.shape[1], w32.shape[1]
    resident = pl.Buffered(1)
    return pl.pallas_call(
        _proj_kernel,
        out_shape=(jax.ShapeDtypeStruct((m, n16), BF16), jax.ShapeDtypeStruct((m, n32), F32)),
        grid=(m // tm,),
        in_specs=[
            pl.BlockSpec((tm, d), lambda i: (i, 0)),
            pl.BlockSpec((1, d), lambda i: (0, 0)),
            pl.BlockSpec((d, n16), lambda i: (0, 0), pipeline_mode=resident),
            pl.BlockSpec((d, n32), lambda i: (0, 0), pipeline_mode=resident),
        ],
        out_specs=(pl.BlockSpec((tm, n16), lambda i: (i, 0)), pl.BlockSpec((tm, n32), lambda i: (i, 0))),
        compiler_params=_params("parallel"),
        name="proj",
    )(x2, g, w16, w32)


def _lru_kernel(x_ref, gate_ref, cw_ref, cb_ref, wa_ref, wi_ref, ba_ref, bi_ref, lam_ref,
                o_ref, xbuf_ref, hc_ref, *, tc):
    halo = SUBLANES

    @pl.when(pl.program_id(1) == 0)
    def _():
        xbuf_ref[0:halo, :] = jnp.zeros((halo, LRU_WIDTH), F32)
        hc_ref[...] = jnp.zeros_like(hc_ref)

    xbuf_ref[halo:halo + tc, :] = x_ref[0]
    base = halo - (CONV_WIDTH - 1)
    xc = xbuf_ref[base:base + tc, :] * cw_ref[0:1, :]
    for j in range(1, CONV_WIDTH):
        xc = xc + xbuf_ref[base + j:base + j + tc, :] * cw_ref[j:j + 1, :]
    xc = cb_ref[...] + xc
    xbuf_ref[0:halo, :] = xbuf_ref[tc:tc + halo, :]

    xcb = xc.astype(BF16)
    r_parts, i_parts = [], []
    for n in range(LRU_BLOCKS):
        blk = xcb[:, n * LRU_BLOCK_W:(n + 1) * LRU_BLOCK_W]
        r_parts.append(_dot(blk, wa_ref[n]))
        i_parts.append(_dot(blk, wi_ref[n]))
    r = _sigmoid(jnp.concatenate(r_parts, axis=1) + ba_ref[...])
    gi = _sigmoid(jnp.concatenate(i_parts, axis=1) + bi_ref[...])

    softplus_neg_lam = jnp.log1p(jnp.exp(-lam_ref[...]))
    log_a = (-LRU_C) * r * softplus_neg_lam
    a = jnp.exp(log_a)
    th = jnp.tanh(log_a)
    mult = jnp.sqrt(jnp.maximum(-2.0 * th / (1.0 - th), 0.0))
    u = mult * (gi * xc)

    row = lax.broadcasted_iota(I32, (tc, LRU_WIDTH), 0)
    d = 1
    while d < tc:
        keep = row >= d
        a_sh = jnp.where(keep, pltpu.roll(a, d, 0), 1.0)
        u_sh = jnp.where(keep, pltpu.roll(u, d, 0), 0.0)
        u = a * u_sh + u
        a = a * a_sh
        d *= 2
    h = a * hc_ref[...] + u
    hc_ref[...] = h[tc - 1:tc, :]

    gate = gate_ref[0]
    gelu = 0.5 * gate * (1.0 + jnp.tanh(math.sqrt(2.0 / math.pi) * (gate + 0.044715 * (gate * gate * gate))))
    o_ref[0] = (h * gelu).astype(o_ref.dtype)


def _lru(p32, conv_w, conv_b, w_a, w_i, b_a, b_i, lam, *, tc):
    b, s, _ = p32.shape
    vec = pl.BlockSpec((1, LRU_WIDTH), lambda bi, c: (0, 0))
    wblk = pl.BlockSpec((LRU_BLOCKS, LRU_BLOCK_W, LRU_BLOCK_W), lambda bi, c: (0, 0, 0))
    return pl.pallas_call(
        functools.partial(_lru_kernel, tc=tc),
        out_shape=jax.ShapeDtypeStruct((b, s, LRU_WIDTH), BF16),
        grid=(b, s // tc),
        in_specs=[
            pl.BlockSpec((1, tc, LRU_WIDTH), lambda bi, c: (bi, c, 0)),
            pl.BlockSpec((1, tc, LRU_WIDTH), lambda bi, c: (bi, c, 1)),
            pl.BlockSpec((CONV_WIDTH, LRU_WIDTH), lambda bi, c: (0, 0)),
            vec, wblk, wblk, vec, vec, vec,
        ],
        out_specs=pl.BlockSpec((1, tc, LRU_WIDTH), lambda bi, c: (bi, c, 0)),
        scratch_shapes=[pltpu.VMEM((tc + SUBLANES, LRU_WIDTH), F32), pltpu.VMEM((1, LRU_WIDTH), F32)],
        compiler_params=_params("arbitrary", "arbitrary"),
        name="rglru",
    )(p32, p32, conv_w, conv_b, w_a, w_i, b_a, b_i, lam)


def _memattn_kernel(q_ref, kv_ref, o_ref):
    kw = X_HEADS * X_HEAD_DIM
    for h in range(X_HEADS):
        sl = slice(h * X_HEAD_DIM, (h + 1) * X_HEAD_DIM)
        q = q_ref[0, :, sl]
        k = kv_ref[0, :, sl]
        v = kv_ref[0, :, kw + h * X_HEAD_DIM:kw + (h + 1) * X_HEAD_DIM]
        s = _dot_nt(q, k)
        m = jnp.max(s, axis=-1, keepdims=True)
        p = jnp.exp(s - m)
        l = jnp.sum(p, axis=-1, keepdims=True)
        o = _dot(p.astype(BF16), v)
        o_ref[0, :, sl] = (o / l).astype(o_ref.dtype)


def _memattn(p16, mkv, *, tm):
    b, s, _ = p16.shape
    nm = mkv.shape[1]
    kw = X_HEADS * X_HEAD_DIM
    return pl.pallas_call(
        _memattn_kernel,
        out_shape=jax.ShapeDtypeStruct((b, s, kw), BF16),
        grid=(b, s // tm),
        in_specs=[
            pl.BlockSpec((1, tm, kw), lambda bi, i: (bi, i, 1)),
            pl.BlockSpec((1, nm, 2 * kw), lambda bi, i: (bi, 0, 0)),
        ],
        out_specs=pl.BlockSpec((1, tm, kw), lambda bi, i: (bi, i, 0)),
        compiler_params=_params("parallel", "parallel"),
        name="memattn",
    )(p16, mkv)


def _merge_kernel(x_ref, g_ref, ya_ref, yb_ref, yc_ref, wg0_ref, wg1_ref, wg2_ref, wb_ref, wo_ref,
                  o_ref, h_ref):
    n = pl.program_id(1)

    @pl.when(n == 0)
    def _():
        h_ref[...] = _rms(x_ref[...], g_ref[...]).astype(BF16)
        o_ref[...] = jnp.zeros_like(o_ref)

    h = h_ref[...]
    merged = None
    for j, (y_ref, wg_ref) in enumerate(((ya_ref, wg0_ref), (yb_ref, wg1_ref), (yc_ref, wg2_ref))):
        term = _sigmoid(_dot(h, wg_ref[...])) * _dot(y_ref[...], wb_ref[j])
        merged = term if merged is None else merged + term
    o_ref[...] += _dot(merged.astype(BF16), wo_ref[...])

    @pl.when(n == pl.num_programs(1) - 1)
    def _():
        o_ref[...] = x_ref[...] + o_ref[...]


def _merge(x2, g, ya, yb, yc, wgates, wbranch, wout, *, tm, tn):
    m, d = x2.shape
    nblk = d // tn
    yspec = pl.BlockSpec((tm, BRANCH_WIDTH), lambda i, n: (i, 0))
    return pl.pallas_call(
        _merge_kernel,
        out_shape=jax.ShapeDtypeStruct((m, d), F32),
        grid=(m // tm, nblk),
        in_specs=[
            pl.BlockSpec((tm, d), lambda i, n: (i, 0)),
            pl.BlockSpec((1, d), lambda i, n: (0, 0)),
            yspec, yspec, yspec,
            pl.BlockSpec((d, tn), lambda i, n: (0, n)),
            pl.BlockSpec((d, tn), lambda i, n: (0, nblk + n)),
            pl.BlockSpec((d, tn), lambda i, n: (0, 2 * nblk + n)),
            pl.BlockSpec((N_BRANCH, BRANCH_WIDTH, tn), lambda i, n: (0, 0, n)),
            pl.BlockSpec((tn, d), lambda i, n: (n, 0)),
        ],
        out_specs=pl.BlockSpec((tm, d), lambda i, n: (i, 0)),
        scratch_shapes=[pltpu.VMEM((tm, d), BF16)],
        compiler_params=_params("parallel", "arbitrary"),
        name="merge",
    )(x2, g, ya, yb, yc, wgates, wgates, wgates, wbranch, wout)


def _dsa_kernel(q_ref, k_ref, vt_ref, vtp_ref, vtd_ref, qi_ref, ki_ref, wi_ref, lng_ref, lnb_ref, bias_ref,
                o_ref,
                kln_ref, keys_ref, madd_ref, mnear_ref, q4_ref, m_ref, l_ref, acc_ref, *, topk):
    i = pl.program_id(1)
    s_len = k_ref.shape[1]

    @pl.when(i == 0)
    def _():
        rows = 512 if s_len % 512 == 0 else QB
        lane = lax.broadcasted_iota(I32, (rows, LANES), 1)
        real = lane < IDX_DIM

        def ln_body(c, carry):
            x = ki_ref[0, pl.ds(c * rows, rows), :]
            mu = jnp.sum(x, axis=-1, keepdims=True) * (1.0 / IDX_DIM)
            xm = jnp.where(real, x - mu, 0.0)
            var = jnp.sum(xm * xm, axis=-1, keepdims=True) * (1.0 / IDX_DIM)
            y = xm * lax.rsqrt(var + EPS) * lng_ref[...] + lnb_ref[...]
            kln_ref[pl.ds(c * rows, rows), :] = y.astype(BF16)
            return carry

        lax.fori_loop(0, s_len // rows, ln_body, 0)

    w_rows = wi_ref[0].T * (IDX_HEADS ** -0.5 * IDX_DIM ** -0.5)
    qib = qi_ref[0].astype(BF16)
    eye = (lax.broadcasted_iota(I32, (QB, QB), 0) == lax.broadcasted_iota(I32, (QB, QB), 1)).astype(BF16)
    for n in range(A_KV_HEADS):
        for g in range(A_GROUP):
            hd = n * A_GROUP + g
            q4_ref[n, g * QB:(g + 1) * QB, 0:A_HEAD_DIM] = q_ref[0, :, hd * A_HEAD_DIM:(hd + 1) * A_HEAD_DIM]
            q4_ref[n, g * QB:(g + 1) * QB, A_HEAD_DIM:A_HEAD_DIM + QB] = eye

    last = i // SUB
    nsc = last + 1

    def rows(j):
        return pl.ds(pl.multiple_of(j * SC, SC), SC)

    def score_chunk(j, causal):
        kc = kln_ref[rows(j), :]
        sc = None
        for h in range(IDX_HEADS):
            d = _dot_nt(kc, qib[:, h * LANES:(h + 1) * LANES])
            t = w_rows[h:h + 1, :] * jnp.maximum(d, 0.0)
            sc = t if sc is None else sc + t
        bits = pltpu.bitcast(sc, I32)
        key = jnp.where(bits < 0, bits ^ jnp.int32(0x7FFFFFFF), bits)
        key = jnp.where(sc == 0.0, 0, key)
        if causal:
            key_pos = j * SC + lax.broadcasted_iota(I32, (SC, LANES), 0)
            q_pos = i * QB + lax.broadcasted_iota(I32, (SC, LANES), 1)
            key = jnp.where(key_pos <= q_pos, key, INT_MIN)
        keys_ref[rows(j), :] = key

    def score_body(j, carry):
        score_chunk(j, False)
        return carry

    lax.fori_loop(0, last, score_body, 0)
    score_chunk(last, True)

    t_pos = i * QB + lax.broadcasted_iota(I32, (1, LANES), 1)
    kk = jnp.minimum(t_pos + 1, topk).astype(F32)

    def count_ge(cand):
        def body(j, cnt):
            for r in range(SUB):
                blk = keys_ref[pl.ds(pl.multiple_of(j * SC + r * QB, QB), QB), :]
                cnt = cnt + jnp.where(blk >= cand, 1.0, 0.0)
            return cnt

        cnt = lax.fori_loop(0, nsc, body, jnp.zeros((QB, LANES), F32))
        return jnp.sum(cnt, axis=0, keepdims=True)

    def bit_body(bi, thr):
        cand = thr + lax.shift_left(jnp.int32(1), 31 - bi)
        return jnp.where(count_ge(cand) >= kk, cand, thr)

    thr = lax.fori_loop(0, 32, bit_body, jnp.full((1, LANES), INT_MIN, I32))
    need = kk - count_ge(thr + 1)
    sub_k = lax.broadcasted_iota(I32, (QB, QB), 0)
    sub_k2 = lax.broadcasted_iota(I32, (QB, QB), 1)
    tril = (sub_k2 <= sub_k).astype(BF16)

    def mask_body(j, seen):
        keys, eqs, pres = [], [], []
        for r in range(SUB):
            key = keys_ref[pl.ds(pl.multiple_of(j * SC + r * QB, QB), QB), :]
            eq = key == thr
            keys.append(key)
            eqs.append(eq)
            pres.append(_dot(tril, jnp.where(eq, 1.0, 0.0).astype(BF16)))
        for r in range(SUB):
            sel = (keys[r] > thr) | (eqs[r] & ((seen + pres[r]) <= need))
            madd_ref[pl.ds(pl.multiple_of(j * SC + r * QB, QB), QB), :] = jnp.where(sel, 0.0, NEG).astype(BF16)
            seen = seen + pres[r][QB - 1:QB, :]
        return seen

    lax.fori_loop(0, nsc, mask_body, jnp.zeros((1, LANES), F32))

    def split_near(slot, blk):
        blk_rows = pl.ds(pl.multiple_of(blk * QB, QB), QB)
        mnear_ref[slot] = madd_ref[blk_rows, :]
        madd_ref[blk_rows, :] = jnp.full((QB, LANES), NEG, BF16)

    split_near(1, i)

    @pl.when(i >= 1)
    def _():
        split_near(0, i - 1)

    m_ref[...] = jnp.full(m_ref.shape, NEG, F32)
    l_ref[...] = jnp.zeros(l_ref.shape, F32)
    acc_ref[...] = jnp.zeros(acc_ref.shape, F32)

    def logits(kc, madd, n):
        return _dot_nt(jnp.concatenate([kc, madd], axis=1), q4_ref[n])

    def far_logits(j, n):
        return logits(k_ref[0, rows(j), n * A_HEAD_DIM:(n + 1) * A_HEAD_DIM], madd_ref[rows(j), :], n)

    def softmax_pv(n, s, vt):
        m_prev = m_ref[n]
        m_new = jnp.maximum(m_prev, jnp.max(s, axis=0, keepdims=True))
        alpha = jnp.exp2(m_prev - m_new)
        p = jnp.exp2(s - m_new)
        l_ref[n] = alpha * l_ref[n] + jnp.sum(p, axis=0, keepdims=True)
        acc_ref[n] = alpha * acc_ref[n] + _dot(vt, p.astype(BF16))
        m_ref[n] = m_new

    def far_vt(j, n):
        return vt_ref[0, j, n * A_HEAD_DIM:(n + 1) * A_HEAD_DIM, :]

    def pair_body(pr, carry):
        j0 = 2 * pr
        j1 = j0 + 1
        s00 = far_logits(j0, 0)
        s01 = far_logits(j0, 1)
        softmax_pv(0, s00, far_vt(j0, 0))
        s10 = far_logits(j1, 0)
        softmax_pv(1, s01, far_vt(j0, 1))
        s11 = far_logits(j1, 1)
        softmax_pv(0, s10, far_vt(j1, 0))
        softmax_pv(1, s11, far_vt(j1, 1))
        return carry

    lax.fori_loop(0, nsc // 2, pair_body, 0)

    @pl.when(nsc % 2 == 1)
    def _():
        s0 = far_logits(last, 0)
        s1 = far_logits(last, 1)
        softmax_pv(0, s0, far_vt(last, 0))
        softmax_pv(1, s1, far_vt(last, 1))

    def near_step(slot, blk, vts_ref):
        blk_rows = pl.ds(pl.multiple_of(blk * QB, QB), QB)
        ss = [logits(k_ref[0, blk_rows, n * A_HEAD_DIM:(n + 1) * A_HEAD_DIM], mnear_ref[slot], n)
              + bias_ref[slot, n] for n in range(A_KV_HEADS)]
        for n in range(A_KV_HEADS):
            softmax_pv(n, ss[n], vts_ref[0, 0, n * A_HEAD_DIM:(n + 1) * A_HEAD_DIM, :])

    @pl.when(i >= 1)
    def _():
        near_step(0, i - 1, vtp_ref)

    near_step(1, i, vtd_ref)

    for n in range(A_KV_HEADS):
        out_t = acc_ref[n] / l_ref[n]
        for g in range(A_GROUP):
            hd = n * A_GROUP + g
            o_ref[0, :, hd * A_HEAD_DIM:(hd + 1) * A_HEAD_DIM] = out_t[:, g * QB:(g + 1) * QB].T.astype(o_ref.dtype)


def _dsa(p16, p32, vt, vt_blk, ln_g, ln_b, bias_near, *, topk):
    b, s, _ = p16.shape
    aq = A_HEADS * A_HEAD_DIM
    akv = A_KV_HEADS * A_HEAD_DIM
    qiw = IDX_HEADS * LANES
    nch = s // QB
    gq = A_GROUP * QB
    k_blk = (aq + X_HEADS * X_HEAD_DIM) // akv
    qi_blk = 2 * LRU_WIDTH // qiw
    ki_blk = (2 * LRU_WIDTH + qiw) // LANES
    return pl.pallas_call(
        functools.partial(_dsa_kernel, topk=topk),
        out_shape=jax.ShapeDtypeStruct((b, s, aq), BF16),
        grid=(b, nch),
        in_specs=[
            pl.BlockSpec((1, QB, aq), lambda bi, i: (bi, i, 0)),
            pl.BlockSpec((1, s, akv), lambda bi, i: (bi, 0, k_blk)),
            pl.BlockSpec((1, s // SC, akv, SC), lambda bi, i: (bi, 0, 0, 0)),
            pl.BlockSpec((1, 1, akv, QB), lambda bi, i: (bi, jnp.maximum(i - 1, 0), 0, 0)),
            pl.BlockSpec((1, 1, akv, QB), lambda bi, i: (bi, i, 0, 0)),
            pl.BlockSpec((1, QB, qiw), lambda bi, i: (bi, i, qi_blk)),
            pl.BlockSpec((1, s, LANES), lambda bi, i: (bi, 0, ki_blk)),
            pl.BlockSpec((1, QB, LANES), lambda bi, i: (bi, i, ki_blk + 1)),
            pl.BlockSpec((1, LANES), lambda bi, i: (0, 0)),
            pl.BlockSpec((1, LANES), lambda bi, i: (0, 0)),
            pl.BlockSpec((2, A_KV_HEADS, QB, gq), lambda bi, i: (0, 0, 0, 0)),
        ],
        out_specs=pl.BlockSpec((1, QB, aq), lambda bi, i: (bi, i, 0)),
        scratch_shapes=[
            pltpu.VMEM((s, LANES), BF16),
            pltpu.VMEM((s, LANES), I32),
            pltpu.VMEM((s, LANES), BF16),
            pltpu.VMEM((2, QB, LANES), BF16),
            pltpu.VMEM((A_KV_HEADS, gq, A_HEAD_DIM + QB), BF16),
            pltpu.VMEM((A_KV_HEADS, 1, gq), F32),
            pltpu.VMEM((A_KV_HEADS, 1, gq), F32),
            pltpu.VMEM((A_KV_HEADS, A_HEAD_DIM, gq), F32),
        ],
        compiler_params=_params("arbitrary", "arbitrary"),
        name="dsa",
    )(p16, p16, vt, vt_blk, vt_blk, p32, p32, p32, ln_g, ln_b, bias_near)


def _t5_bucket(dist):
    max_exact = N_BUCKETS // 2
    d = jnp.maximum(dist, 0)
    df = jnp.maximum(d, 1).astype(F32)
    large = max_exact + (jnp.log(df / max_exact) / math.log(MAX_DISTANCE / max_exact)
                         * (N_BUCKETS - max_exact)).astype(I32)
    large = jnp.minimum(large, N_BUCKETS - 1)
    return jnp.where(d < max_exact, d, large)


def _near_bias(rel_bias):
    t = jnp.arange(QB, dtype=I32)[:, None]
    s = jnp.arange(QB, dtype=I32)[None, :]
    tiles = []
    for off in (QB, 0):
        bucket = _t5_bucket(t - s + off)
        tile = (rel_bias[bucket] - rel_bias[N_BUCKETS - 1]) * LOG2E
        tile = tile.transpose(1, 2, 0).reshape(QB, A_KV_HEADS, A_GROUP * QB)
        tiles.append(tile.transpose(1, 0, 2))
    return jnp.stack(tiles).astype(F32)


def _pad_cols(w, width):
    return jnp.pad(w, ((0, 0), (0, width - w.shape[1])))


def _prep_layer(w_in, d_model):
    aq = A_HEADS * A_HEAD_DIM
    akv = A_KV_HEADS * A_HEAD_DIM
    sizes = (aq, akv, akv, IDX_HEADS * IDX_DIM, IDX_DIM, IDX_HEADS,
             LRU_WIDTH, LRU_WIDTH, X_HEADS * X_HEAD_DIM, N_BRANCH * d_model)
    pts, acc = [], 0
    for n in sizes[:-1]:
        acc += n
        pts.append(acc)
    wq, wk, wv, wqi, wki, wwi, wx, wgl, wqm, wgates = jnp.split(w_in, pts, axis=1)
    w16 = jnp.concatenate([wq * (A_HEAD_DIM ** -0.5 * LOG2E), wqm * (X_HEAD_DIM ** -0.5), wk, wv], axis=1)
    wqi_p = jnp.concatenate(
        [_pad_cols(wqi[:, h * IDX_DIM:(h + 1) * IDX_DIM], LANES) for h in range(IDX_HEADS)], axis=1)
    w32 = jnp.concatenate([wx, wgl, wqi_p, _pad_cols(wki, LANES), _pad_cols(wwi, LANES)], axis=1)
    return w16.astype(BF16), w32.astype(BF16), wgates.astype(BF16)


def _pad_ff(w_gu, w_down, tf):
    d_ff = w_down.shape[0]
    fp = -(-d_ff // tf) * tf
    wg = _pad_cols(w_gu[:, :d_ff], fp).astype(BF16)
    wu = _pad_cols(w_gu[:, d_ff:], fp).astype(BF16)
    wd = jnp.pad(w_down, ((0, fp - d_ff), (0, 0))).astype(BF16)
    return wg, wu, wd


def _tiles(m, s, d_model):
    return dict(
        ffn_tm=min(512, m), ffn_tf=512,
        proj_tm=min(256, m), proj_tn=256,
        merge_tm=min(512, m), merge_tn=min(512, d_model),
        lru_tc=min(256, s), mem_tm=min(512, s),
    )


def kernel(x, mem, rel_bias, final_norm, norm_ff1, w_ff1_gu, w_ff1_down, norm_mix, w_in, conv_w, conv_b, w_a, b_a, w_i, b_i, lam, idx_ln_g, idx_ln_b, mem_norm, w_mem_kv, w_branch, w_out, norm_ff2, w_ff2_gu, w_ff2_down):
    b, s, d = x.shape
    nm = mem.shape[1]
    depth = w_in.shape[0]
    m = b * s
    assert s % SC == 0 and d % LANES == 0
    tl = _tiles(m, s, d)
    topk = min(TOPK_MAX, s // 4)
    bias_near = _near_bias(rel_bias)
    row = lambda v: v.reshape(1, -1)

    x2 = x.reshape(m, d)
    mem2 = mem.reshape(b * nm, d)
    for l in range(depth):
        wg, wu, wd = _pad_ff(w_ff1_gu[l], w_ff1_down[l], tl["ffn_tf"])
        x2 = _ffn(x2, row(norm_ff1[l]), wg, wu, wd, row(final_norm), final_norm=False,
                  tm=tl["ffn_tm"], tf=tl["ffn_tf"])

        w16, w32, wgates = _prep_layer(w_in[l], d)
        p16, p32 = _proj(x2, row(norm_mix[l]), w16, w32, tm=tl["proj_tm"])
        p16 = p16.reshape(b, s, -1)
        p32 = p32.reshape(b, s, -1)

        v_cols = slice(p16.shape[-1] - A_KV_HEADS * A_HEAD_DIM, p16.shape[-1])
        vt = p16[..., v_cols].reshape(b, s // SC, SC, -1).transpose(0, 1, 3, 2)
        vt_blk = p16[..., v_cols].reshape(b, s // QB, QB, -1).transpose(0, 1, 3, 2)
        y_a = _dsa(p16, p32, vt, vt_blk, row(_pad_cols(row(idx_ln_g[l]), LANES)), row(_pad_cols(row(idx_ln_b[l]), LANES)),
                   bias_near, topk=topk)
        y_b = _lru(p32, conv_w[l], row(conv_b[l]), w_a[l].astype(BF16), w_i[l].astype(BF16),
                   row(b_a[l]), row(b_i[l]), row(lam[l]), tc=tl["lru_tc"])
        mkv = _norm_matmul(mem2, row(mem_norm[l]), w_mem_kv[l].astype(BF16), BF16,
                           tm=min(256, b * nm), tn=tl["proj_tn"], name="memkv")
        y_c = _memattn(p16, mkv.reshape(b, nm, -1), tm=tl["mem_tm"])

        x2 = _merge(x2, row(norm_mix[l]), y_a.reshape(m, -1), y_b.reshape(m, -1), y_c.reshape(m, -1),
                    wgates, w_branch[l].astype(BF16), w_out[l].astype(BF16),
                    tm=tl["merge_tm"], tn=tl["merge_tn"])

        wg, wu, wd = _pad_ff(w_ff2_gu[l], w_ff2_down[l], tl["ffn_tf"])
        x2 = _ffn(x2, row(norm_ff2[l]), wg, wu, wd, row(final_norm), final_norm=(l == depth - 1),
                  tm=tl["ffn_tm"], tf=tl["ffn_tf"])
    return x2.reshape(b, s, d)
```

```python
import functools
import math

import jax
import jax.numpy as jnp
from jax import lax
from jax.experimental import pallas as pl
from jax.experimental.pallas import tpu as pltpu

F32 = jnp.float32
BF16 = jnp.bfloat16
I32 = jnp.int32
I16 = jnp.int16

EPS = 1e-6
A_HEADS = 8
A_KV_HEADS = 2
A_HEAD_DIM = 128
A_GROUP = A_HEADS // A_KV_HEADS
IDX_HEADS = 4
IDX_DIM = 64
TOPK_MAX = 256
LRU_WIDTH = 1024
LRU_BLOCKS = 8
LRU_BLOCK_W = LRU_WIDTH // LRU_BLOCKS
CONV_WIDTH = 4
LRU_C = 8.0
X_HEADS = 4
X_HEAD_DIM = 256
N_BRANCH = 3
BRANCH_WIDTH = 1024
N_BUCKETS = 32
MAX_DISTANCE = 128

LANES = 128
SUBLANES = 8
VMEM_LIMIT = 56 * 1024 * 1024

QB = 128
SC = 512
SUB = SC // QB
NEG = -1e30
LOG2E = math.log2(math.e)
INT_MIN = -2 ** 31
I16_MIN, I16_MAX = -2 ** 15, 2 ** 15 - 1


def _params(*sem):
    return pltpu.CompilerParams(dimension_semantics=sem, vmem_limit_bytes=VMEM_LIMIT)


def _rms(x, g):
    ms = jnp.mean(x * x, axis=-1, keepdims=True)
    return x * lax.rsqrt(ms + EPS) * g


def _sigmoid(x):
    return 1.0 / (1.0 + jnp.exp(-x))


def _dot(a, b):
    return jnp.dot(a, b, preferred_element_type=F32)


def _dot_nt(a, b):
    return lax.dot_general(a, b, (((1,), (1,)), ((), ())), preferred_element_type=F32)


def _ffn_kernel(x_ref, g_ref, wg_ref, wu_ref, wd_ref, fg_ref, o_ref, h_ref, *, final_norm):
    f = pl.program_id(1)

    @pl.when(f == 0)
    def _():
        h_ref[...] = _rms(x_ref[...], g_ref[...]).astype(BF16)
        o_ref[...] = jnp.zeros_like(o_ref)

    h = h_ref[...]
    g = _dot(h, wg_ref[...])
    u = _dot(h, wu_ref[...])
    a = (g * _sigmoid(g) * u).astype(BF16)
    o_ref[...] += _dot(a, wd_ref[...])

    @pl.when(f == pl.num_programs(1) - 1)
    def _():
        y = x_ref[...] + 0.5 * o_ref[...]
        if final_norm:
            y = _rms(y, fg_ref[...])
        o_ref[...] = y


def _ffn(x2, g, wg, wu, wd, fg, *, final_norm, tm, tf):
    m, d = x2.shape
    fp = wg.shape[1]
    return pl.pallas_call(
        functools.partial(_ffn_kernel, final_norm=final_norm),
        out_shape=jax.ShapeDtypeStruct((m, d), F32),
        grid=(m // tm, fp // tf),
        in_specs=[
            pl.BlockSpec((tm, d), lambda i, f: (i, 0)),
            pl.BlockSpec((1, d), lambda i, f: (0, 0)),
            pl.BlockSpec((d, tf), lambda i, f: (0, f)),
            pl.BlockSpec((d, tf), lambda i, f: (0, f)),
            pl.BlockSpec((tf, d), lambda i, f: (f, 0)),
            pl.BlockSpec((1, d), lambda i, f: (0, 0)),
        ],
        out_specs=pl.BlockSpec((tm, d), lambda i, f: (i, 0)),
        scratch_shapes=[pltpu.VMEM((tm, d), BF16)],
        compiler_params=_params("parallel", "arbitrary"),
        name="ffn",
    )(x2, g, wg, wu, wd, fg)


def _norm_matmul_kernel(x_ref, g_ref, w_ref, o_ref, h_ref):
    @pl.when(pl.program_id(1) == 0)
    def _():
        h_ref[...] = _rms(x_ref[...], g_ref[...]).astype(BF16)

    o_ref[...] = _dot(h_ref[...], w_ref[...]).astype(o_ref.dtype)


def _norm_matmul(x2, g, w, out_dtype, *, tm, tn, name):
    m, d = x2.shape
    n = w.shape[1]
    return pl.pallas_call(
        _norm_matmul_kernel,
        out_shape=jax.ShapeDtypeStruct((m, n), out_dtype),
        grid=(m // tm, n // tn),
        in_specs=[
            pl.BlockSpec((tm, d), lambda i, j: (i, 0)),
            pl.BlockSpec((1, d), lambda i, j: (0, 0)),
            pl.BlockSpec((d, tn), lambda i, j: (0, j)),
        ],
        out_specs=pl.BlockSpec((tm, tn), lambda i, j: (i, j)),
        scratch_shapes=[pltpu.VMEM((tm, d), BF16)],
        compiler_params=_params("parallel", "arbitrary"),
        name=name,
    )(x2, g, w)


def _proj_kernel(x_ref, g_ref, w16_ref, w32_ref, o16_ref, o32_ref):
    h = _rms(x_ref[...], g_ref[...]).astype(BF16)
    o16_ref[...] = _dot(h, w16_ref[...]).astype(o16_ref.dtype)
    o32_ref[...] = _dot(h, w32_ref[...])


def _proj(x2, g, w16, w32, *, tm):
    m, d = x2.shape
    n16, n32 = w16.shape[1], w32.shape[1]
    resident = pl.Buffered(1)
    return pl.pallas_call(
        _proj_kernel,
        out_shape=(jax.ShapeDtypeStruct((m, n16), BF16), jax.ShapeDtypeStruct((m, n32), F32)),
        grid=(m // tm,),
        in_specs=[
            pl.BlockSpec((tm, d), lambda i: (i, 0)),
            pl.BlockSpec((1, d), lambda i: (0, 0)),
            pl.BlockSpec((d, n16), lambda i: (0, 0), pipeline_mode=resident),
            pl.BlockSpec((d, n32), lambda i: (0, 0), pipeline_mode=resident),
        ],
        out_specs=(pl.BlockSpec((tm, n16), lambda i: (i, 0)), pl.BlockSpec((tm, n32), lambda i: (i, 0))),
        compiler_params=_params("parallel"),
        name="proj",
    )(x2, g, w16, w32)


def _lru_kernel(x_ref, gate_ref, cw_ref, cb_ref, wa_ref, wi_ref, ba_ref, bi_ref, lam_ref,
                o_ref, xbuf_ref, hc_ref, *, tc):
    halo = SUBLANES

    @pl.when(pl.program_id(1) == 0)
    def _():
        xbuf_ref[0:halo, :] = jnp.zeros((halo, LRU_WIDTH), F32)
        hc_ref[...] = jnp.zeros_like(hc_ref)

    xbuf_ref[halo:halo + tc, :] = x_ref[0]
    base = halo - (CONV_WIDTH - 1)
    xc = xbuf_ref[base:base + tc, :] * cw_ref[0:1, :]
    for j in range(1, CONV_WIDTH):
        xc = xc + xbuf_ref[base + j:base + j + tc, :] * cw_ref[j:j + 1, :]
    xc = cb_ref[...] + xc
    xbuf_ref[0:halo, :] = xbuf_ref[tc:tc + halo, :]

    xcb = xc.astype(BF16)
    r_parts, i_parts = [], []
    for n in range(LRU_BLOCKS):
        blk = xcb[:, n * LRU_BLOCK_W:(n + 1) * LRU_BLOCK_W]
        r_parts.append(_dot(blk, wa_ref[n]))
        i_parts.append(_dot(blk, wi_ref[n]))
    r = _sigmoid(jnp.concatenate(r_parts, axis=1) + ba_ref[...])
    gi = _sigmoid(jnp.concatenate(i_parts, axis=1) + bi_ref[...])

    softplus_neg_lam = jnp.log1p(jnp.exp(-lam_ref[...]))
    log_a = (-LRU_C) * r * softplus_neg_lam
    a = jnp.exp(log_a)
    th = jnp.tanh(log_a)
    mult = jnp.sqrt(jnp.maximum(-2.0 * th / (1.0 - th), 0.0))
    u = mult * (gi * xc)

    row = lax.broadcasted_iota(I32, (tc, LRU_WIDTH), 0)
    d = 1
    while d < tc:
        keep = row >= d
        a_sh = jnp.where(keep, pltpu.roll(a, d, 0), 1.0)
        u_sh = jnp.where(keep, pltpu.roll(u, d, 0), 0.0)
        u = a * u_sh + u
        a = a * a_sh
        d *= 2
    h = a * hc_ref[...] + u
    hc_ref[...] = h[tc - 1:tc, :]

    gate = gate_ref[0]
    gelu = 0.5 * gate * (1.0 + jnp.tanh(math.sqrt(2.0 / math.pi) * (gate + 0.044715 * (gate * gate * gate))))
    o_ref[0] = (h * gelu).astype(o_ref.dtype)


def _lru(p32, conv_w, conv_b, w_a, w_i, b_a, b_i, lam, *, tc):
    b, s, _ = p32.shape
    vec = pl.BlockSpec((1, LRU_WIDTH), lambda bi, c: (0, 0))
    wblk = pl.BlockSpec((LRU_BLOCKS, LRU_BLOCK_W, LRU_BLOCK_W), lambda bi, c: (0, 0, 0))
    return pl.pallas_call(
        functools.partial(_lru_kernel, tc=tc),
        out_shape=jax.ShapeDtypeStruct((b, s, LRU_WIDTH), BF16),
        grid=(b, s // tc),
        in_specs=[
            pl.BlockSpec((1, tc, LRU_WIDTH), lambda bi, c: (bi, c, 0)),
            pl.BlockSpec((1, tc, LRU_WIDTH), lambda bi, c: (bi, c, 1)),
            pl.BlockSpec((CONV_WIDTH, LRU_WIDTH), lambda bi, c: (0, 0)),
            vec, wblk, wblk, vec, vec, vec,
        ],
        out_specs=pl.BlockSpec((1, tc, LRU_WIDTH), lambda bi, c: (bi, c, 0)),
        scratch_shapes=[pltpu.VMEM((tc + SUBLANES, LRU_WIDTH), F32), pltpu.VMEM((1, LRU_WIDTH), F32)],
        compiler_params=_params("arbitrary", "arbitrary"),
        name="rglru",
    )(p32, p32, conv_w, conv_b, w_a, w_i, b_a, b_i, lam)


def _memattn_kernel(q_ref, kv_ref, o_ref):
    kw = X_HEADS * X_HEAD_DIM
    for h in range(X_HEADS):
        sl = slice(h * X_HEAD_DIM, (h + 1) * X_HEAD_DIM)
        q = q_ref[0, :, sl]
        k = kv_ref[0, :, sl]
        v = kv_ref[0, :, kw + h * X_HEAD_DIM:kw + (h + 1) * X_HEAD_DIM]
        s = _dot_nt(q, k)
        m = jnp.max(s, axis=-1, keepdims=True)
        p = jnp.exp(s - m)
        l = jnp.sum(p, axis=-1, keepdims=True)
        o = _dot(p.astype(BF16), v)
        o_ref[0, :, sl] = (o / l).astype(o_ref.dtype)


def _memattn(p16, mkv, *, tm):
    b, s, _ = p16.shape
    nm = mkv.shape[1]
    kw = X_HEADS * X_HEAD_DIM
    return pl.pallas_call(
        _memattn_kernel,
        out_shape=jax.ShapeDtypeStruct((b, s, kw), BF16),
        grid=(b, s // tm),
        in_specs=[
            pl.BlockSpec((1, tm, kw), lambda bi, i: (bi, i, 1)),
            pl.BlockSpec((1, nm, 2 * kw), lambda bi, i: (bi, 0, 0)),
        ],
        out_specs=pl.BlockSpec((1, tm, kw), lambda bi, i: (bi, i, 0)),
        compiler_params=_params("parallel", "parallel"),
        name="memattn",
    )(p16, mkv)


def _merge_kernel(x_ref, g_ref, ya_ref, yb_ref, yc_ref, wg0_ref, wg1_ref, wg2_ref, wb_ref, wo_ref,
                  o_ref, h_ref):
    n = pl.program_id(1)

    @pl.when(n == 0)
    def _():
        h_ref[...] = _rms(x_ref[...], g_ref[...]).astype(BF16)
        o_ref[...] = jnp.zeros_like(o_ref)

    h = h_ref[...]
    merged = None
    for j, (y_ref, wg_ref) in enumerate(((ya_ref, wg0_ref), (yb_ref, wg1_ref), (yc_ref, wg2_ref))):
        term = _sigmoid(_dot(h, wg_ref[...])) * _dot(y_ref[...], wb_ref[j])
        merged = term if merged is None else merged + term
    o_ref[...] += _dot(merged.astype(BF16), wo_ref[...])

    @pl.when(n == pl.num_programs(1) - 1)
    def _():
        o_ref[...] = x_ref[...] + o_ref[...]


def _merge(x2, g, ya, yb, yc, wgates, wbranch, wout, *, tm, tn):
    m, d = x2.shape
    nblk = d // tn
    yspec = pl.BlockSpec((tm, BRANCH_WIDTH), lambda i, n: (i, 0))
    return pl.pallas_call(
        _merge_kernel,
        out_shape=jax.ShapeDtypeStruct((m, d), F32),
        grid=(m // tm, nblk),
        in_specs=[
            pl.BlockSpec((tm, d), lambda i, n: (i, 0)),
            pl.BlockSpec((1, d), lambda i, n: (0, 0)),
            yspec, yspec, yspec,
            pl.BlockSpec((d, tn), lambda i, n: (0, n)),
            pl.BlockSpec((d, tn), lambda i, n: (0, nblk + n)),
            pl.BlockSpec((d, tn), lambda i, n: (0, 2 * nblk + n)),
            pl.BlockSpec((N_BRANCH, BRANCH_WIDTH, tn), lambda i, n: (0, 0, n)),
            pl.BlockSpec((tn, d), lambda i, n: (n, 0)),
        ],
        out_specs=pl.BlockSpec((tm, d), lambda i, n: (i, 0)),
        scratch_shapes=[pltpu.VMEM((tm, d), BF16)],
        compiler_params=_params("parallel", "arbitrary"),
        name="merge",
    )(x2, g, ya, yb, yc, wgates, wgates, wgates, wbranch, wout)


def _dsa_kernel(q_ref, k_ref, vt_ref, vtp_ref, vtd_ref, qi_ref, ki_ref, wi_ref, lng_ref, lnb_ref, bias_ref,
                o_ref,
                kln_ref, keys_ref, hi_ref, lo_ref, madd_ref, mnear_ref, qi4_ref, q4_ref, m_ref, l_ref, acc_ref,
                *, topk):
    i = pl.program_id(1)
    s_len = k_ref.shape[1]

    @pl.when(i == 0)
    def _():
        rows = 512 if s_len % 512 == 0 else QB
        lane = lax.broadcasted_iota(I32, (rows, LANES), 1)
        real = lane < IDX_DIM

        def ln_body(c, carry):
            x = ki_ref[0, pl.ds(c * rows, rows), :]
            mu = jnp.sum(x, axis=-1, keepdims=True) * (1.0 / IDX_DIM)
            xm = jnp.where(real, x - mu, 0.0)
            var = jnp.sum(xm * xm, axis=-1, keepdims=True) * (1.0 / IDX_DIM)
            y = xm * lax.rsqrt(var + EPS) * lng_ref[...] + lnb_ref[...]
            kln_ref[pl.ds(c * rows, rows), :] = y.astype(BF16)
            return carry

        lax.fori_loop(0, s_len // rows, ln_body, 0)

    w_rows = wi_ref[0].T * (IDX_HEADS ** -0.5 * IDX_DIM ** -0.5)
    for h in range(IDX_HEADS):
        qi4_ref[h * QB:(h + 1) * QB, :] = qi_ref[0, :, h * LANES:(h + 1) * LANES].astype(BF16)
    eye = (lax.broadcasted_iota(I32, (QB, QB), 0) == lax.broadcasted_iota(I32, (QB, QB), 1)).astype(BF16)
    for n in range(A_KV_HEADS):
        for g in range(A_GROUP):
            hd = n * A_GROUP + g
            q4_ref[n, g * QB:(g + 1) * QB, 0:A_HEAD_DIM] = q_ref[0, :, hd * A_HEAD_DIM:(hd + 1) * A_HEAD_DIM]
            q4_ref[n, g * QB:(g + 1) * QB, A_HEAD_DIM:A_HEAD_DIM + QB] = eye

    last = i // SUB
    nsc = last + 1

    def rows(j):
        return pl.ds(pl.multiple_of(j * SC, SC), SC)

    def index_dots(j):
        return _dot_nt(kln_ref[rows(j), :], qi4_ref[...])

    def score_chunk(j, dots, causal):
        sc = None
        for h in range(IDX_HEADS):
            t = w_rows[h:h + 1, :] * jnp.maximum(dots[:, h * QB:(h + 1) * QB], 0.0)
            sc = t if sc is None else sc + t
        bits = pltpu.bitcast(sc, I32)
        key = jnp.where(bits < 0, bits ^ jnp.int32(0x7FFFFFFF), bits)
        key = jnp.where(sc == 0.0, 0, key)
        if causal:
            key_pos = j * SC + lax.broadcasted_iota(I32, (SC, LANES), 0)
            q_pos = i * QB + lax.broadcasted_iota(I32, (SC, LANES), 1)
            key = jnp.where(key_pos <= q_pos, key, INT_MIN)
        keys_ref[rows(j), :] = key
        hi_ref[rows(j), :] = (key >> 16).astype(I16)

    def score_body(pr, carry):
        d0 = index_dots(2 * pr)
        d1 = index_dots(2 * pr + 1)
        score_chunk(2 * pr, d0, False)
        score_chunk(2 * pr + 1, d1, False)
        return carry

    lax.fori_loop(0, last // 2, score_body, 0)

    @pl.when(last % 2 == 1)
    def _():
        score_chunk(last - 1, index_dots(last - 1), False)

    score_chunk(last, index_dots(last), True)

    t_pos = i * QB + lax.broadcasted_iota(I32, (1, LANES), 1)
    kk = jnp.minimum(t_pos + 1, topk).astype(F32)

    def count_ge(cand):
        def body(j, cnt):
            for r in range(SUB):
                blk = keys_ref[pl.ds(pl.multiple_of(j * SC + r * QB, QB), QB), :]
                cnt = cnt + jnp.where(blk >= cand, 1.0, 0.0)
            return cnt

        cnt = lax.fori_loop(0, nsc, body, jnp.zeros((QB, LANES), F32))
        return jnp.sum(cnt, axis=0, keepdims=True)

    half = 2 * QB

    def count_ge16(ref, cand):
        c16 = cand.astype(I16)

        def body(j, cnt):
            for r in range(SC // half):
                blk = ref[pl.ds(pl.multiple_of(j * SC + r * half, half), half), :]
                cnt = cnt + jnp.where(blk >= c16, jnp.int16(1), jnp.int16(0))
            return cnt

        cnt = lax.fori_loop(0, nsc, body, jnp.zeros((half, LANES), I16))
        rows_left = half
        while rows_left > 2 * SUBLANES:
            rows_left //= 2
            cnt = cnt[:rows_left] + cnt[rows_left:]
        return jnp.sum(cnt.astype(F32), axis=0, keepdims=True)

    def select16(ref, base):
        def bit_body(bi, t):
            cand = t + lax.shift_left(jnp.int32(1), 15 - bi)
            return jnp.where(base + count_ge16(ref, cand) >= kk, cand, t)

        return lax.fori_loop(0, 16, bit_body, jnp.full((1, LANES), I16_MIN, I32))

    t_hi = select16(hi_ref, 0.0)
    above = count_ge16(hi_ref, jnp.minimum(t_hi + 1, I16_MAX))
    above = jnp.where(t_hi == I16_MAX, 0.0, above)
    t_hi16 = t_hi.astype(I16)

    def low_body(j, carry):
        for r in range(SC // half):
            blk_rows = pl.ds(pl.multiple_of(j * SC + r * half, half), half)
            low = (keys_ref[blk_rows, :] ^ jnp.int32(0x8000)).astype(I16)
            lo_ref[blk_rows, :] = jnp.where(hi_ref[blk_rows, :] == t_hi16, low, jnp.int16(I16_MIN))
        return carry

    lax.fori_loop(0, nsc, low_body, 0)
    t_lo = select16(lo_ref, above)
    thr = t_hi * 65536 + (t_lo - I16_MIN)
    need = kk - count_ge(thr + 1)
    sub_k = lax.broadcasted_iota(I32, (QB, QB), 0)
    sub_k2 = lax.broadcasted_iota(I32, (QB, QB), 1)
    tril = (sub_k2 <= sub_k).astype(BF16)

    def mask_body(j, seen):
        keys, eqs, pres = [], [], []
        for r in range(SUB):
            key = keys_ref[pl.ds(pl.multiple_of(j * SC + r * QB, QB), QB), :]
            eq = key == thr
            keys.append(key)
            eqs.append(eq)
            pres.append(_dot(tril, jnp.where(eq, 1.0, 0.0).astype(BF16)))
        for r in range(SUB):
            sel = (keys[r] > thr) | (eqs[r] & ((seen + pres[r]) <= need))
            madd_ref[pl.ds(pl.multiple_of(j * SC + r * QB, QB), QB), :] = jnp.where(sel, 0.0, NEG).astype(BF16)
            seen = seen + pres[r][QB - 1:QB, :]
        return seen

    lax.fori_loop(0, nsc, mask_body, jnp.zeros((1, LANES), F32))

    def split_near(slot, blk):
        blk_rows = pl.ds(pl.multiple_of(blk * QB, QB), QB)
        mnear_ref[slot] = madd_ref[blk_rows, :]
        madd_ref[blk_rows, :] = jnp.full((QB, LANES), NEG, BF16)

    split_near(1, i)

    @pl.when(i >= 1)
    def _():
        split_near(0, i - 1)

    m_ref[...] = jnp.full(m_ref.shape, NEG, F32)
    l_ref[...] = jnp.zeros(l_ref.shape, F32)
    acc_ref[...] = jnp.zeros(acc_ref.shape, F32)

    def logits(kc, madd, n):
        return _dot_nt(jnp.concatenate([kc, madd], axis=1), q4_ref[n])

    def far_logits(j, n):
        return logits(k_ref[0, rows(j), n * A_HEAD_DIM:(n + 1) * A_HEAD_DIM], madd_ref[rows(j), :], n)

    def softmax_pv(n, s, vt):
        m_prev = m_ref[n]
        m_new = jnp.maximum(m_prev, jnp.max(s, axis=0, keepdims=True))
        alpha = jnp.exp2(m_prev - m_new)
        p = jnp.exp2(s - m_new)
        l_ref[n] = alpha * l_ref[n] + jnp.sum(p, axis=0, keepdims=True)
        acc_ref[n] = alpha * acc_ref[n] + _dot(vt, p.astype(BF16))
        m_ref[n] = m_new

    def far_vt(j, n):
        return vt_ref[0, j, n * A_HEAD_DIM:(n + 1) * A_HEAD_DIM, :]

    def pair_body(pr, carry):
        j0 = 2 * pr
        j1 = j0 + 1
        s00 = far_logits(j0, 0)
        s01 = far_logits(j0, 1)
        softmax_pv(0, s00, far_vt(j0, 0))
        s10 = far_logits(j1, 0)
        softmax_pv(1, s01, far_vt(j0, 1))
        s11 = far_logits(j1, 1)
        softmax_pv(0, s10, far_vt(j1, 0))
        softmax_pv(1, s11, far_vt(j1, 1))
        return carry

    n_far = jnp.where(i % SUB >= 2, nsc, last)
    lax.fori_loop(0, n_far // 2, pair_body, 0)

    @pl.when(n_far % 2 == 1)
    def _():
        j = n_far - 1
        s0 = far_logits(j, 0)
        s1 = far_logits(j, 1)
        softmax_pv(0, s0, far_vt(j, 0))
        softmax_pv(1, s1, far_vt(j, 1))

    def near_step(slot, blk, vts_ref):
        blk_rows = pl.ds(pl.multiple_of(blk * QB, QB), QB)
        ss = [logits(k_ref[0, blk_rows, n * A_HEAD_DIM:(n + 1) * A_HEAD_DIM], mnear_ref[slot], n)
              + bias_ref[slot, n] for n in range(A_KV_HEADS)]
        for n in range(A_KV_HEADS):
            softmax_pv(n, ss[n], vts_ref[0, 0, n * A_HEAD_DIM:(n + 1) * A_HEAD_DIM, :])

    @pl.when(i >= 1)
    def _():
        near_step(0, i - 1, vtp_ref)

    near_step(1, i, vtd_ref)

    for n in range(A_KV_HEADS):
        out_t = acc_ref[n] / l_ref[n]
        for g in range(A_GROUP):
            hd = n * A_GROUP + g
            o_ref[0, :, hd * A_HEAD_DIM:(hd + 1) * A_HEAD_DIM] = out_t[:, g * QB:(g + 1) * QB].T.astype(o_ref.dtype)


def _dsa(p16, p32, vt, vt_blk, ln_g, ln_b, bias_near, *, topk):
    b, s, _ = p16.shape
    aq = A_HEADS * A_HEAD_DIM
    akv = A_KV_HEADS * A_HEAD_DIM
    qiw = IDX_HEADS * LANES
    nch = s // QB
    gq = A_GROUP * QB
    k_blk = (aq + X_HEADS * X_HEAD_DIM) // akv
    qi_blk = 2 * LRU_WIDTH // qiw
    ki_blk = (2 * LRU_WIDTH + qiw) // LANES
    return pl.pallas_call(
        functools.partial(_dsa_kernel, topk=topk),
        out_shape=jax.ShapeDtypeStruct((b, s, aq), BF16),
        grid=(b, nch),
        in_specs=[
            pl.BlockSpec((1, QB, aq), lambda bi, i: (bi, i, 0)),
            pl.BlockSpec((1, s, akv), lambda bi, i: (bi, 0, k_blk)),
            pl.BlockSpec((1, s // SC, akv, SC), lambda bi, i: (bi, 0, 0, 0)),
            pl.BlockSpec((1, 1, akv, QB), lambda bi, i: (bi, jnp.maximum(i - 1, 0), 0, 0)),
            pl.BlockSpec((1, 1, akv, QB), lambda bi, i: (bi, i, 0, 0)),
            pl.BlockSpec((1, QB, qiw), lambda bi, i: (bi, i, qi_blk)),
            pl.BlockSpec((1, s, LANES), lambda bi, i: (bi, 0, ki_blk)),
            pl.BlockSpec((1, QB, LANES), lambda bi, i: (bi, i, ki_blk + 1)),
            pl.BlockSpec((1, LANES), lambda bi, i: (0, 0)),
            pl.BlockSpec((1, LANES), lambda bi, i: (0, 0)),
            pl.BlockSpec((2, A_KV_HEADS, QB, gq), lambda bi, i: (0, 0, 0, 0)),
        ],
        out_specs=pl.BlockSpec((1, QB, aq), lambda bi, i: (bi, i, 0)),
        scratch_shapes=[
            pltpu.VMEM((s, LANES), BF16),
            pltpu.VMEM((s, LANES), I32),
            pltpu.VMEM((s, LANES), I16),
            pltpu.VMEM((s, LANES), I16),
            pltpu.VMEM((s, LANES), BF16),
            pltpu.VMEM((2, QB, LANES), BF16),
            pltpu.VMEM((IDX_HEADS * QB, LANES), BF16),
            pltpu.VMEM((A_KV_HEADS, gq, A_HEAD_DIM + QB), BF16),
            pltpu.VMEM((A_KV_HEADS, 1, gq), F32),
            pltpu.VMEM((A_KV_HEADS, 1, gq), F32),
            pltpu.VMEM((A_KV_HEADS, A_HEAD_DIM, gq), F32),
        ],
        compiler_params=_params("arbitrary", "arbitrary"),
        name="dsa",
    )(p16, p16, vt, vt_blk, vt_blk, p32, p32, p32, ln_g, ln_b, bias_near)


def _t5_bucket(dist):
    max_exact = N_BUCKETS // 2
    d = jnp.maximum(dist, 0)
    df = jnp.maximum(d, 1).astype(F32)
    large = max_exact + (jnp.log(df / max_exact) / math.log(MAX_DISTANCE / max_exact)
                         * (N_BUCKETS - max_exact)).astype(I32)
    large = jnp.minimum(large, N_BUCKETS - 1)
    return jnp.where(d < max_exact, d, large)


def _near_bias(rel_bias):
    t = jnp.arange(QB, dtype=I32)[:, None]
    s = jnp.arange(QB, dtype=I32)[None, :]
    tiles = []
    for off in (QB, 0):
        bucket = _t5_bucket(t - s + off)
        onehot = (bucket[..., None] == jnp.arange(N_BUCKETS, dtype=I32)).astype(F32)
        table = (rel_bias - rel_bias[N_BUCKETS - 1]) * LOG2E
        tile = jnp.einsum('tsb,bh->tsh', onehot, table, precision=lax.Precision.HIGHEST)
        tile = tile.transpose(1, 2, 0).reshape(QB, A_KV_HEADS, A_GROUP * QB)
        tiles.append(tile.transpose(1, 0, 2))
    return jnp.stack(tiles).astype(F32)


def _pad_cols(w, width):
    return jnp.pad(w, ((0, 0), (0, width - w.shape[1])))


def _prep_layer(w_in, d_model):
    aq = A_HEADS * A_HEAD_DIM
    akv = A_KV_HEADS * A_HEAD_DIM
    sizes = (aq, akv, akv, IDX_HEADS * IDX_DIM, IDX_DIM, IDX_HEADS,
             LRU_WIDTH, LRU_WIDTH, X_HEADS * X_HEAD_DIM, N_BRANCH * d_model)
    pts, acc = [], 0
    for n in sizes[:-1]:
        acc += n
        pts.append(acc)
    col = jnp.arange(w_in.shape[1])
    scale = jnp.where(col < aq, A_HEAD_DIM ** -0.5 * LOG2E,
                      jnp.where((col >= pts[7]) & (col < pts[8]), X_HEAD_DIM ** -0.5, 1.0)).astype(F32)
    wb = (w_in * scale[None, :]).astype(BF16)
    wq, wk, wv, wqi, wki, wwi, wx, wgl, wqm, wgates = jnp.split(wb, pts, axis=1)
    w16 = jnp.concatenate([wq, wqm, wk, wv], axis=1)
    wqi_p = jnp.concatenate(
        [_pad_cols(wqi[:, h * IDX_DIM:(h + 1) * IDX_DIM], LANES) for h in range(IDX_HEADS)], axis=1)
    w32 = jnp.concatenate([wx, wgl, wqi_p, _pad_cols(wki, LANES), _pad_cols(wwi, LANES)], axis=1)
    return w16, w32, wgates


def _pad_ff(w_gu, w_down, tf):
    d_ff = w_down.shape[0]
    fp = -(-d_ff // tf) * tf
    wg = _pad_cols(w_gu[:, :d_ff], fp).astype(BF16)
    wu = _pad_cols(w_gu[:, d_ff:], fp).astype(BF16)
    wd = jnp.pad(w_down, ((0, fp - d_ff), (0, 0))).astype(BF16)
    return wg, wu, wd


def _tiles(m, s, d_model):
    return dict(
        ffn_tm=min(512, m), ffn_tf=512,
        proj_tm=min(256, m), proj_tn=256,
        merge_tm=min(512, m), merge_tn=min(512, d_model),
        lru_tc=min(256, s), mem_tm=min(512, s),
    )


def kernel(x, mem, rel_bias, final_norm, norm_ff1, w_ff1_gu, w_ff1_down, norm_mix, w_in, conv_w, conv_b, w_a, b_a, w_i, b_i, lam, idx_ln_g, idx_ln_b, mem_norm, w_mem_kv, w_branch, w_out, norm_ff2, w_ff2_gu, w_ff2_down):
    b, s, d = x.shape
    nm = mem.shape[1]
    depth = w_in.shape[0]
    m = b * s
    assert s % SC == 0 and d % LANES == 0
    tl = _tiles(m, s, d)
    topk = min(TOPK_MAX, s // 4)
    bias_near = _near_bias(rel_bias)
    row = lambda v: v.reshape(1, -1)

    x2 = x.reshape(m, d)
    mem2 = mem.reshape(b * nm, d)
    for l in range(depth):
        wg, wu, wd = _pad_ff(w_ff1_gu[l], w_ff1_down[l], tl["ffn_tf"])
        x2 = _ffn(x2, row(norm_ff1[l]), wg, wu, wd, row(final_norm), final_norm=False,
                  tm=tl["ffn_tm"], tf=tl["ffn_tf"])

        w16, w32, wgates = _prep_layer(w_in[l], d)
        p16, p32 = _proj(x2, row(norm_mix[l]), w16, w32, tm=tl["proj_tm"])
        p16 = p16.reshape(b, s, -1)
        p32 = p32.reshape(b, s, -1)

        v_cols = slice(p16.shape[-1] - A_KV_HEADS * A_HEAD_DIM, p16.shape[-1])
        vt = p16[..., v_cols].reshape(b, s // SC, SC, -1).transpose(0, 1, 3, 2)
        vt_blk = p16[..., v_cols].reshape(b, s // QB, QB, -1).transpose(0, 1, 3, 2)
        y_a = _dsa(p16, p32, vt, vt_blk, row(_pad_cols(row(idx_ln_g[l]), LANES)), row(_pad_cols(row(idx_ln_b[l]), LANES)),
                   bias_near, topk=topk)
        y_b = _lru(p32, conv_w[l], row(conv_b[l]), w_a[l].astype(BF16), w_i[l].astype(BF16),
                   row(b_a[l]), row(b_i[l]), row(lam[l]), tc=tl["lru_tc"])
        mkv = _norm_matmul(mem2, row(mem_norm[l]), w_mem_kv[l].astype(BF16), BF16,
                           tm=min(256, b * nm), tn=tl["proj_tn"], name="memkv")
        y_c = _memattn(p16, mkv.reshape(b, nm, -1), tm=tl["mem_tm"])

        x2 = _merge(x2, row(norm_mix[l]), y_a.reshape(m, -1), y_b.reshape(m, -1), y_c.reshape(m, -1),
                    wgates, w_branch[l].astype(BF16), w_out[l].astype(BF16),
                    tm=tl["merge_tm"], tn=tl["merge_tn"])

        wg, wu, wd = _pad_ff(w_ff2_gu[l], w_ff2_down[l], tl["ffn_tf"])
        x2 = _ffn(x2, row(norm_ff2[l]), wg, wu, wd, row(final_norm), final_norm=(l == depth - 1),
                  tm=tl["ffn_tm"], tf=tl["ffn_tf"])
    return x2.reshape(b, s, d)
```

```python
import functools
import math

import jax
import jax.numpy as jnp
from jax import lax
from jax.experimental import pallas as pl
from jax.experimental.pallas import tpu as pltpu

F32 = jnp.float32
BF16 = jnp.bfloat16
I32 = jnp.int32
I16 = jnp.int16

EPS = 1e-6
A_HEADS = 8
A_KV_HEADS = 2
A_HEAD_DIM = 128
A_GROUP = A_HEADS // A_KV_HEADS
IDX_HEADS = 4
IDX_DIM = 64
TOPK_MAX = 256
LRU_WIDTH = 1024
LRU_BLOCKS = 8
LRU_BLOCK_W = LRU_WIDTH // LRU_BLOCKS
CONV_WIDTH = 4
LRU_C = 8.0
X_HEADS = 4
X_HEAD_DIM = 256
N_BRANCH = 3
BRANCH_WIDTH = 1024
N_BUCKETS = 32
MAX_DISTANCE = 128

LANES = 128
SUBLANES = 8
VMEM_LIMIT = 56 * 1024 * 1024

QB = 128
SC = 512
SUB = SC // QB
NEG = -1e30
LOG2E = math.log2(math.e)
INT_MIN = -2 ** 31
I16_MIN, I16_MAX = -2 ** 15, 2 ** 15 - 1


def _params(*sem):
    return pltpu.CompilerParams(dimension_semantics=sem, vmem_limit_bytes=VMEM_LIMIT)


def _rms(x, g):
    ms = jnp.mean(x * x, axis=-1, keepdims=True)
    return x * lax.rsqrt(ms + EPS) * g


def _sigmoid(x):
    return 1.0 / (1.0 + jnp.exp(-x))


def _dot(a, b):
    return jnp.dot(a, b, preferred_element_type=F32)


def _dot_nt(a, b):
    return lax.dot_general(a, b, (((1,), (1,)), ((), ())), preferred_element_type=F32)


def _ffn_kernel(x_ref, g_ref, wg_ref, wu_ref, wd_ref, fg_ref, o_ref, h_ref, *, final_norm):
    f = pl.program_id(1)

    @pl.when(f == 0)
    def _():
        h_ref[...] = _rms(x_ref[...], g_ref[...]).astype(BF16)
        o_ref[...] = jnp.zeros_like(o_ref)

    h = h_ref[...]
    g = _dot(h, wg_ref[...])
    u = _dot(h, wu_ref[...])
    a = (g * _sigmoid(g) * u).astype(BF16)
    o_ref[...] += _dot(a, wd_ref[...])

    @pl.when(f == pl.num_programs(1) - 1)
    def _():
        y = x_ref[...] + 0.5 * o_ref[...]
        if final_norm:
            y = _rms(y, fg_ref[...])
        o_ref[...] = y


def _ffn(x2, g, wg, wu, wd, fg, *, final_norm, tm, tf):
    m, d = x2.shape
    fp = wg.shape[1]
    return pl.pallas_call(
        functools.partial(_ffn_kernel, final_norm=final_norm),
        out_shape=jax.ShapeDtypeStruct((m, d), F32),
        grid=(m // tm, fp // tf),
        in_specs=[
            pl.BlockSpec((tm, d), lambda i, f: (i, 0)),
            pl.BlockSpec((1, d), lambda i, f: (0, 0)),
            pl.BlockSpec((d, tf), lambda i, f: (0, f)),
            pl.BlockSpec((d, tf), lambda i, f: (0, f)),
            pl.BlockSpec((tf, d), lambda i, f: (f, 0)),
            pl.BlockSpec((1, d), lambda i, f: (0, 0)),
        ],
        out_specs=pl.BlockSpec((tm, d), lambda i, f: (i, 0)),
        scratch_shapes=[pltpu.VMEM((tm, d), BF16)],
        compiler_params=_params("parallel", "arbitrary"),
        name="ffn",
    )(x2, g, wg, wu, wd, fg)


def _norm_matmul_kernel(x_ref, g_ref, w_ref, o_ref, h_ref):
    @pl.when(pl.program_id(1) == 0)
    def _():
        h_ref[...] = _rms(x_ref[...], g_ref[...]).astype(BF16)

    o_ref[...] = _dot(h_ref[...], w_ref[...]).astype(o_ref.dtype)


def _norm_matmul(x2, g, w, out_dtype, *, tm, tn, name):
    m, d = x2.shape
    n = w.shape[1]
    return pl.pallas_call(
        _norm_matmul_kernel,
        out_shape=jax.ShapeDtypeStruct((m, n), out_dtype),
        grid=(m // tm, n // tn),
        in_specs=[
            pl.BlockSpec((tm, d), lambda i, j: (i, 0)),
            pl.BlockSpec((1, d), lambda i, j: (0, 0)),
            pl.BlockSpec((d, tn), lambda i, j: (0, j)),
        ],
        out_specs=pl.BlockSpec((tm, tn), lambda i, j: (i, j)),
        scratch_shapes=[pltpu.VMEM((tm, d), BF16)],
        compiler_params=_params("parallel", "arbitrary"),
        name=name,
    )(x2, g, w)


def _proj_kernel(x_ref, g_ref, w16_ref, w32_ref, o16_ref, o32_ref):
    h = _rms(x_ref[...], g_ref[...]).astype(BF16)
    o16_ref[...] = _dot(h, w16_ref[...]).astype(o16_ref.dtype)
    o32_ref[...] = _dot(h, w32_ref[...])


def _proj(x2, g, w16, w32, *, tm):
    m, d = x2.shape
    n16, n32 = w16.shape[1], w32.shape[1]
    resident = pl.Buffered(1)
    return pl.pallas_call(
        _proj_kernel,
        out_shape=(jax.ShapeDtypeStruct((m, n16), BF16), jax.ShapeDtypeStruct((m, n32), F32)),
        grid=(m // tm,),
        in_specs=[
            pl.BlockSpec((tm, d), lambda i: (i, 0)),
            pl.BlockSpec((1, d), lambda i: (0, 0)),
            pl.BlockSpec((d, n16), lambda i: (0, 0), pipeline_mode=resident),
            pl.BlockSpec((d, n32), lambda i: (0, 0), pipeline_mode=resident),
        ],
        out_specs=(pl.BlockSpec((tm, n16), lambda i: (i, 0)), pl.BlockSpec((tm, n32), lambda i: (i, 0))),
        compiler_params=_params("parallel"),
        name="proj",
    )(x2, g, w16, w32)


def _lru_kernel(x_ref, gate_ref, cw_ref, cb_ref, wa_ref, wi_ref, ba_ref, bi_ref, lam_ref,
                o_ref, xbuf_ref, hc_ref, *, tc):
    halo = SUBLANES

    @pl.when(pl.program_id(1) == 0)
    def _():
        xbuf_ref[0:halo, :] = jnp.zeros((halo, LRU_WIDTH), F32)
        hc_ref[...] = jnp.zeros_like(hc_ref)

    xbuf_ref[halo:halo + tc, :] = x_ref[0]
    base = halo - (CONV_WIDTH - 1)
    xc = xbuf_ref[base:base + tc, :] * cw_ref[0:1, :]
    for j in range(1, CONV_WIDTH):
        xc = xc + xbuf_ref[base + j:base + j + tc, :] * cw_ref[j:j + 1, :]
    xc = cb_ref[...] + xc
    xbuf_ref[0:halo, :] = xbuf_ref[tc:tc + halo, :]

    xcb = xc.astype(BF16)
    r_parts, i_parts = [], []
    for n in range(LRU_BLOCKS):
        blk = xcb[:, n * LRU_BLOCK_W:(n + 1) * LRU_BLOCK_W]
        r_parts.append(_dot(blk, wa_ref[n]))
        i_parts.append(_dot(blk, wi_ref[n]))
    r = _sigmoid(jnp.concatenate(r_parts, axis=1) + ba_ref[...])
    gi = _sigmoid(jnp.concatenate(i_parts, axis=1) + bi_ref[...])

    softplus_neg_lam = jnp.log1p(jnp.exp(-lam_ref[...]))
    log_a = (-LRU_C) * r * softplus_neg_lam
    a = jnp.exp(log_a)
    th = jnp.tanh(log_a)
    mult = jnp.sqrt(jnp.maximum(-2.0 * th / (1.0 - th), 0.0))
    u = mult * (gi * xc)

    row = lax.broadcasted_iota(I32, (tc, LRU_WIDTH), 0)
    d = 1
    while d < tc:
        keep = row >= d
        a_sh = jnp.where(keep, pltpu.roll(a, d, 0), 1.0)
        u_sh = jnp.where(keep, pltpu.roll(u, d, 0), 0.0)
        u = a * u_sh + u
        a = a * a_sh
        d *= 2
    h = a * hc_ref[...] + u
    hc_ref[...] = h[tc - 1:tc, :]

    gate = gate_ref[0]
    gelu = 0.5 * gate * (1.0 + jnp.tanh(math.sqrt(2.0 / math.pi) * (gate + 0.044715 * (gate * gate * gate))))
    o_ref[0] = (h * gelu).astype(o_ref.dtype)


def _lru(p32, conv_w, conv_b, w_a, w_i, b_a, b_i, lam, *, tc):
    b, s, _ = p32.shape
    vec = pl.BlockSpec((1, LRU_WIDTH), lambda bi, c: (0, 0))
    wblk = pl.BlockSpec((LRU_BLOCKS, LRU_BLOCK_W, LRU_BLOCK_W), lambda bi, c: (0, 0, 0))
    return pl.pallas_call(
        functools.partial(_lru_kernel, tc=tc),
        out_shape=jax.ShapeDtypeStruct((b, s, LRU_WIDTH), BF16),
        grid=(b, s // tc),
        in_specs=[
            pl.BlockSpec((1, tc, LRU_WIDTH), lambda bi, c: (bi, c, 0)),
            pl.BlockSpec((1, tc, LRU_WIDTH), lambda bi, c: (bi, c, 1)),
            pl.BlockSpec((CONV_WIDTH, LRU_WIDTH), lambda bi, c: (0, 0)),
            vec, wblk, wblk, vec, vec, vec,
        ],
        out_specs=pl.BlockSpec((1, tc, LRU_WIDTH), lambda bi, c: (bi, c, 0)),
        scratch_shapes=[pltpu.VMEM((tc + SUBLANES, LRU_WIDTH), F32), pltpu.VMEM((1, LRU_WIDTH), F32)],
        compiler_params=_params("arbitrary", "arbitrary"),
        name="rglru",
    )(p32, p32, conv_w, conv_b, w_a, w_i, b_a, b_i, lam)


def _memattn_kernel(q_ref, kv_ref, o_ref):
    kw = X_HEADS * X_HEAD_DIM
    for h in range(X_HEADS):
        sl = slice(h * X_HEAD_DIM, (h + 1) * X_HEAD_DIM)
        q = q_ref[0, :, sl]
        k = kv_ref[0, :, sl]
        v = kv_ref[0, :, kw + h * X_HEAD_DIM:kw + (h + 1) * X_HEAD_DIM]
        s = _dot_nt(q, k)
        m = jnp.max(s, axis=-1, keepdims=True)
        p = jnp.exp(s - m)
        l = jnp.sum(p, axis=-1, keepdims=True)
        o = _dot(p.astype(BF16), v)
        o_ref[0, :, sl] = (o / l).astype(o_ref.dtype)


def _memattn(p16, mkv, *, tm):
    b, s, _ = p16.shape
    nm = mkv.shape[1]
    kw = X_HEADS * X_HEAD_DIM
    return pl.pallas_call(
        _memattn_kernel,
        out_shape=jax.ShapeDtypeStruct((b, s, kw), BF16),
        grid=(b, s // tm),
        in_specs=[
            pl.BlockSpec((1, tm, kw), lambda bi, i: (bi, i, 1)),
            pl.BlockSpec((1, nm, 2 * kw), lambda bi, i: (bi, 0, 0)),
        ],
        out_specs=pl.BlockSpec((1, tm, kw), lambda bi, i: (bi, i, 0)),
        compiler_params=_params("parallel", "parallel"),
        name="memattn",
    )(p16, mkv)


def _merge_kernel(x_ref, g_ref, ya_ref, yb_ref, yc_ref, wg0_ref, wg1_ref, wg2_ref, wb_ref, wo_ref,
                  o_ref, h_ref):
    n = pl.program_id(1)

    @pl.when(n == 0)
    def _():
        h_ref[...] = _rms(x_ref[...], g_ref[...]).astype(BF16)
        o_ref[...] = jnp.zeros_like(o_ref)

    h = h_ref[...]
    merged = None
    for j, (y_ref, wg_ref) in enumerate(((ya_ref, wg0_ref), (yb_ref, wg1_ref), (yc_ref, wg2_ref))):
        term = _sigmoid(_dot(h, wg_ref[...])) * _dot(y_ref[...], wb_ref[j])
        merged = term if merged is None else merged + term
    o_ref[...] += _dot(merged.astype(BF16), wo_ref[...])

    @pl.when(n == pl.num_programs(1) - 1)
    def _():
        o_ref[...] = x_ref[...] + o_ref[...]


def _merge(x2, g, ya, yb, yc, wgates, wbranch, wout, *, tm, tn):
    m, d = x2.shape
    nblk = d // tn
    yspec = pl.BlockSpec((tm, BRANCH_WIDTH), lambda i, n: (i, 0))
    return pl.pallas_call(
        _merge_kernel,
        out_shape=jax.ShapeDtypeStruct((m, d), F32),
        grid=(m // tm, nblk),
        in_specs=[
            pl.BlockSpec((tm, d), lambda i, n: (i, 0)),
            pl.BlockSpec((1, d), lambda i, n: (0, 0)),
            yspec, yspec, yspec,
            pl.BlockSpec((d, tn), lambda i, n: (0, n)),
            pl.BlockSpec((d, tn), lambda i, n: (0, nblk + n)),
            pl.BlockSpec((d, tn), lambda i, n: (0, 2 * nblk + n)),
            pl.BlockSpec((N_BRANCH, BRANCH_WIDTH, tn), lambda i, n: (0, 0, n)),
            pl.BlockSpec((tn, d), lambda i, n: (n, 0)),
        ],
        out_specs=pl.BlockSpec((tm, d), lambda i, n: (i, 0)),
        scratch_shapes=[pltpu.VMEM((tm, d), BF16)],
        compiler_params=_params("parallel", "arbitrary"),
        name="merge",
    )(x2, g, ya, yb, yc, wgates, wgates, wgates, wbranch, wout)


def _dsa_kernel(q_ref, k_ref, vt_ref, vtp_ref, vtd_ref, qi_ref, ki_ref, wi_ref, lng_ref, lnb_ref, bias_ref,
                o_ref,
                kln_ref, keys_ref, hi_ref, lo_ref, madd_ref, mnear_ref, qi4_ref, q4_ref, m_ref, l_ref, acc_ref,
                *, topk, radix16):
    i = pl.program_id(1)
    s_len = k_ref.shape[1]

    @pl.when(i == 0)
    def _():
        rows = 512 if s_len % 512 == 0 else QB
        lane = lax.broadcasted_iota(I32, (rows, LANES), 1)
        real = lane < IDX_DIM

        def ln_body(c, carry):
            x = ki_ref[0, pl.ds(c * rows, rows), :]
            mu = jnp.sum(x, axis=-1, keepdims=True) * (1.0 / IDX_DIM)
            xm = jnp.where(real, x - mu, 0.0)
            var = jnp.sum(xm * xm, axis=-1, keepdims=True) * (1.0 / IDX_DIM)
            y = xm * lax.rsqrt(var + EPS) * lng_ref[...] + lnb_ref[...]
            kln_ref[pl.ds(c * rows, rows), :] = y.astype(BF16)
            return carry

        lax.fori_loop(0, s_len // rows, ln_body, 0)

    w_rows = wi_ref[0].T * (IDX_HEADS ** -0.5 * IDX_DIM ** -0.5)
    for h in range(IDX_HEADS):
        qi4_ref[h * QB:(h + 1) * QB, :] = qi_ref[0, :, h * LANES:(h + 1) * LANES].astype(BF16)
    eye = (lax.broadcasted_iota(I32, (QB, QB), 0) == lax.broadcasted_iota(I32, (QB, QB), 1)).astype(BF16)
    for n in range(A_KV_HEADS):
        for g in range(A_GROUP):
            hd = n * A_GROUP + g
            q4_ref[n, g * QB:(g + 1) * QB, 0:A_HEAD_DIM] = q_ref[0, :, hd * A_HEAD_DIM:(hd + 1) * A_HEAD_DIM]
            q4_ref[n, g * QB:(g + 1) * QB, A_HEAD_DIM:A_HEAD_DIM + QB] = eye

    last = i // SUB
    nsc = last + 1

    def rows(j):
        return pl.ds(pl.multiple_of(j * SC, SC), SC)

    def index_dots(j):
        return _dot_nt(kln_ref[rows(j), :], qi4_ref[...])

    def score_chunk(j, dots, causal):
        sc = None
        for h in range(IDX_HEADS):
            t = w_rows[h:h + 1, :] * jnp.maximum(dots[:, h * QB:(h + 1) * QB], 0.0)
            sc = t if sc is None else sc + t
        bits = pltpu.bitcast(sc, I32)
        key = jnp.where(bits < 0, bits ^ jnp.int32(0x7FFFFFFF), bits)
        key = jnp.where(sc == 0.0, 0, key)
        if causal:
            key_pos = j * SC + lax.broadcasted_iota(I32, (SC, LANES), 0)
            q_pos = i * QB + lax.broadcasted_iota(I32, (SC, LANES), 1)
            key = jnp.where(key_pos <= q_pos, key, INT_MIN)
        keys_ref[rows(j), :] = key
        hi_ref[rows(j), :] = (key >> 16).astype(I16)

    def score_body(pr, carry):
        d0 = index_dots(2 * pr)
        d1 = index_dots(2 * pr + 1)
        score_chunk(2 * pr, d0, False)
        score_chunk(2 * pr + 1, d1, False)
        return carry

    lax.fori_loop(0, last // 2, score_body, 0)

    @pl.when(last % 2 == 1)
    def _():
        score_chunk(last - 1, index_dots(last - 1), False)

    score_chunk(last, index_dots(last), True)

    t_pos = i * QB + lax.broadcasted_iota(I32, (1, LANES), 1)
    kk = jnp.minimum(t_pos + 1, topk).astype(F32)

    def count_ge(cand):
        def body(j, cnt):
            for r in range(SUB):
                blk = keys_ref[pl.ds(pl.multiple_of(j * SC + r * QB, QB), QB), :]
                cnt = cnt + jnp.where(blk >= cand, 1.0, 0.0)
            return cnt

        cnt = lax.fori_loop(0, nsc, body, jnp.zeros((QB, LANES), F32))
        return jnp.sum(cnt, axis=0, keepdims=True)

    half = 2 * QB

    def count_ge16(ref, cand):
        c16 = cand.astype(I16)

        def body(j, cnt):
            for r in range(SC // half):
                blk = ref[pl.ds(pl.multiple_of(j * SC + r * half, half), half), :]
                cnt = cnt + jnp.where(blk >= c16, jnp.int16(1), jnp.int16(0))
            return cnt

        cnt = lax.fori_loop(0, nsc, body, jnp.zeros((half, LANES), I16))
        rows_left = half
        while rows_left > 2 * SUBLANES:
            rows_left //= 2
            cnt = cnt[:rows_left] + cnt[rows_left:]
        return jnp.sum(cnt.astype(F32), axis=0, keepdims=True)

    def select16(ref, base):
        def bit_body(bi, t):
            cand = t + lax.shift_left(jnp.int32(1), 15 - bi)
            return jnp.where(base + count_ge16(ref, cand) >= kk, cand, t)

        return lax.fori_loop(0, 16, bit_body, jnp.full((1, LANES), I16_MIN, I32))

    if radix16:
        t_hi = select16(hi_ref, 0.0)
        above = count_ge16(hi_ref, jnp.minimum(t_hi + 1, I16_MAX))
        above = jnp.where(t_hi == I16_MAX, 0.0, above)
        t_hi16 = t_hi.astype(I16)

        def low_body(j, carry):
            for r in range(SC // half):
                blk_rows = pl.ds(pl.multiple_of(j * SC + r * half, half), half)
                low = (keys_ref[blk_rows, :] ^ jnp.int32(0x8000)).astype(I16)
                lo_ref[blk_rows, :] = jnp.where(hi_ref[blk_rows, :] == t_hi16, low, jnp.int16(I16_MIN))
            return carry

        lax.fori_loop(0, nsc, low_body, 0)
        t_lo = select16(lo_ref, above)
        thr = t_hi * 65536 + (t_lo - I16_MIN)
    else:
        def bit_body32(bi, t):
            cand = t + lax.shift_left(jnp.int32(1), 31 - bi)
            return jnp.where(count_ge(cand) >= kk, cand, t)

        thr = lax.fori_loop(0, 32, bit_body32, jnp.full((1, LANES), INT_MIN, I32))
    need = kk - count_ge(thr + 1)
    sub_k = lax.broadcasted_iota(I32, (QB, QB), 0)
    sub_k2 = lax.broadcasted_iota(I32, (QB, QB), 1)
    tril = (sub_k2 <= sub_k).astype(BF16)

    def mask_body(j, seen):
        keys, eqs, pres = [], [], []
        for r in range(SUB):
            key = keys_ref[pl.ds(pl.multiple_of(j * SC + r * QB, QB), QB), :]
            eq = key == thr
            keys.append(key)
            eqs.append(eq)
            pres.append(_dot(tril, jnp.where(eq, 1.0, 0.0).astype(BF16)))
        for r in range(SUB):
            sel = (keys[r] > thr) | (eqs[r] & ((seen + pres[r]) <= need))
            madd_ref[pl.ds(pl.multiple_of(j * SC + r * QB, QB), QB), :] = jnp.where(sel, 0.0, NEG).astype(BF16)
            seen = seen + pres[r][QB - 1:QB, :]
        return seen

    lax.fori_loop(0, nsc, mask_body, jnp.zeros((1, LANES), F32))

    def split_near(slot, blk):
        blk_rows = pl.ds(pl.multiple_of(blk * QB, QB), QB)
        mnear_ref[slot] = madd_ref[blk_rows, :]
        madd_ref[blk_rows, :] = jnp.full((QB, LANES), NEG, BF16)

    split_near(1, i)

    @pl.when(i >= 1)
    def _():
        split_near(0, i - 1)

    m_ref[...] = jnp.full(m_ref.shape, NEG, F32)
    l_ref[...] = jnp.zeros(l_ref.shape, F32)
    acc_ref[...] = jnp.zeros(acc_ref.shape, F32)

    def logits(kc, madd, n):
        return _dot_nt(jnp.concatenate([kc, madd], axis=1), q4_ref[n])

    def far_logits(j, n):
        return logits(k_ref[0, rows(j), n * A_HEAD_DIM:(n + 1) * A_HEAD_DIM], madd_ref[rows(j), :], n)

    def softmax_pv(n, s, vt):
        m_prev = m_ref[n]
        m_new = jnp.maximum(m_prev, jnp.max(s, axis=0, keepdims=True))
        alpha = jnp.exp2(m_prev - m_new)
        p = jnp.exp2(s - m_new)
        l_ref[n] = alpha * l_ref[n] + jnp.sum(p, axis=0, keepdims=True)
        acc_ref[n] = alpha * acc_ref[n] + _dot(vt, p.astype(BF16))
        m_ref[n] = m_new

    def far_vt(j, n):
        return vt_ref[0, j, n * A_HEAD_DIM:(n + 1) * A_HEAD_DIM, :]

    def pair_body(pr, carry):
        j0 = 2 * pr
        j1 = j0 + 1
        s00 = far_logits(j0, 0)
        s01 = far_logits(j0, 1)
        softmax_pv(0, s00, far_vt(j0, 0))
        s10 = far_logits(j1, 0)
        softmax_pv(1, s01, far_vt(j0, 1))
        s11 = far_logits(j1, 1)
        softmax_pv(0, s10, far_vt(j1, 0))
        softmax_pv(1, s11, far_vt(j1, 1))
        return carry

    n_far = jnp.where(i % SUB >= 2, nsc, last)
    lax.fori_loop(0, n_far // 2, pair_body, 0)

    @pl.when(n_far % 2 == 1)
    def _():
        j = n_far - 1
        s0 = far_logits(j, 0)
        s1 = far_logits(j, 1)
        softmax_pv(0, s0, far_vt(j, 0))
        softmax_pv(1, s1, far_vt(j, 1))

    def near_step(slot, blk, vts_ref):
        blk_rows = pl.ds(pl.multiple_of(blk * QB, QB), QB)
        ss = [logits(k_ref[0, blk_rows, n * A_HEAD_DIM:(n + 1) * A_HEAD_DIM], mnear_ref[slot], n)
              + bias_ref[slot, n] for n in range(A_KV_HEADS)]
        for n in range(A_KV_HEADS):
            softmax_pv(n, ss[n], vts_ref[0, 0, n * A_HEAD_DIM:(n + 1) * A_HEAD_DIM, :])

    @pl.when(i >= 1)
    def _():
        near_step(0, i - 1, vtp_ref)

    near_step(1, i, vtd_ref)

    for n in range(A_KV_HEADS):
        out_t = acc_ref[n] / l_ref[n]
        for g in range(A_GROUP):
            hd = n * A_GROUP + g
            o_ref[0, :, hd * A_HEAD_DIM:(hd + 1) * A_HEAD_DIM] = out_t[:, g * QB:(g + 1) * QB].T.astype(o_ref.dtype)


def _dsa(p16, p32, vt, vt_blk, ln_g, ln_b, bias_near, *, topk, radix16):
    b, s, _ = p16.shape
    aq = A_HEADS * A_HEAD_DIM
    akv = A_KV_HEADS * A_HEAD_DIM
    qiw = IDX_HEADS * LANES
    nch = s // QB
    gq = A_GROUP * QB
    k_blk = (aq + X_HEADS * X_HEAD_DIM) // akv
    qi_blk = 2 * LRU_WIDTH // qiw
    ki_blk = (2 * LRU_WIDTH + qiw) // LANES
    return pl.pallas_call(
        functools.partial(_dsa_kernel, topk=topk, radix16=radix16),
        out_shape=jax.ShapeDtypeStruct((b, s, aq), BF16),
        grid=(b, nch),
        in_specs=[
            pl.BlockSpec((1, QB, aq), lambda bi, i: (bi, i, 0)),
            pl.BlockSpec((1, s, akv), lambda bi, i: (bi, 0, k_blk)),
            pl.BlockSpec((1, s // SC, akv, SC), lambda bi, i: (bi, 0, 0, 0)),
            pl.BlockSpec((1, 1, akv, QB), lambda bi, i: (bi, jnp.maximum(i - 1, 0), 0, 0)),
            pl.BlockSpec((1, 1, akv, QB), lambda bi, i: (bi, i, 0, 0)),
            pl.BlockSpec((1, QB, qiw), lambda bi, i: (bi, i, qi_blk)),
            pl.BlockSpec((1, s, LANES), lambda bi, i: (bi, 0, ki_blk)),
            pl.BlockSpec((1, QB, LANES), lambda bi, i: (bi, i, ki_blk + 1)),
            pl.BlockSpec((1, LANES), lambda bi, i: (0, 0)),
            pl.BlockSpec((1, LANES), lambda bi, i: (0, 0)),
            pl.BlockSpec((2, A_KV_HEADS, QB, gq), lambda bi, i: (0, 0, 0, 0)),
        ],
        out_specs=pl.BlockSpec((1, QB, aq), lambda bi, i: (bi, i, 0)),
        scratch_shapes=[
            pltpu.VMEM((s, LANES), BF16),
            pltpu.VMEM((s, LANES), I32),
            pltpu.VMEM((s, LANES), I16),
            pltpu.VMEM((s, LANES), I16),
            pltpu.VMEM((s, LANES), BF16),
            pltpu.VMEM((2, QB, LANES), BF16),
            pltpu.VMEM((IDX_HEADS * QB, LANES), BF16),
            pltpu.VMEM((A_KV_HEADS, gq, A_HEAD_DIM + QB), BF16),
            pltpu.VMEM((A_KV_HEADS, 1, gq), F32),
            pltpu.VMEM((A_KV_HEADS, 1, gq), F32),
            pltpu.VMEM((A_KV_HEADS, A_HEAD_DIM, gq), F32),
        ],
        compiler_params=_params("arbitrary", "arbitrary"),
        name="dsa",
    )(p16, p16, vt, vt_blk, vt_blk, p32, p32, p32, ln_g, ln_b, bias_near)


def _t5_bucket(dist):
    max_exact = N_BUCKETS // 2
    d = jnp.maximum(dist, 0)
    df = jnp.maximum(d, 1).astype(F32)
    large = max_exact + (jnp.log(df / max_exact) / math.log(MAX_DISTANCE / max_exact)
                         * (N_BUCKETS - max_exact)).astype(I32)
    large = jnp.minimum(large, N_BUCKETS - 1)
    return jnp.where(d < max_exact, d, large)


def _near_bias(rel_bias):
    t = jnp.arange(QB, dtype=I32)[:, None]
    s = jnp.arange(QB, dtype=I32)[None, :]
    tiles = []
    for off in (QB, 0):
        bucket = _t5_bucket(t - s + off)
        onehot = (bucket[..., None] == jnp.arange(N_BUCKETS, dtype=I32)).astype(F32)
        table = (rel_bias - rel_bias[N_BUCKETS - 1]) * LOG2E
        tile = jnp.einsum('tsb,bh->tsh', onehot, table, precision=lax.Precision.HIGHEST)
        tile = tile.transpose(1, 2, 0).reshape(QB, A_KV_HEADS, A_GROUP * QB)
        tiles.append(tile.transpose(1, 0, 2))
    return jnp.stack(tiles).astype(F32)


def _pad_cols(w, width):
    return jnp.pad(w, ((0, 0), (0, width - w.shape[1])))


def _prep_layer(w_in, d_model):
    aq = A_HEADS * A_HEAD_DIM
    akv = A_KV_HEADS * A_HEAD_DIM
    sizes = (aq, akv, akv, IDX_HEADS * IDX_DIM, IDX_DIM, IDX_HEADS,
             LRU_WIDTH, LRU_WIDTH, X_HEADS * X_HEAD_DIM, N_BRANCH * d_model)
    pts, acc = [], 0
    for n in sizes[:-1]:
        acc += n
        pts.append(acc)
    col = jnp.arange(w_in.shape[1])
    scale = jnp.where(col < aq, A_HEAD_DIM ** -0.5 * LOG2E,
                      jnp.where((col >= pts[7]) & (col < pts[8]), X_HEAD_DIM ** -0.5, 1.0)).astype(F32)
    wb = (w_in * scale[None, :]).astype(BF16)
    wq, wk, wv, wqi, wki, wwi, wx, wgl, wqm, wgates = jnp.split(wb, pts, axis=1)
    w16 = jnp.concatenate([wq, wqm, wk, wv], axis=1)
    wqi_p = jnp.concatenate(
        [_pad_cols(wqi[:, h * IDX_DIM:(h + 1) * IDX_DIM], LANES) for h in range(IDX_HEADS)], axis=1)
    w32 = jnp.concatenate([wx, wgl, wqi_p, _pad_cols(wki, LANES), _pad_cols(wwi, LANES)], axis=1)
    return w16, w32, wgates


def _pad_ff(w_gu, w_down, tf):
    d_ff = w_down.shape[0]
    fp = -(-d_ff // tf) * tf
    wg = _pad_cols(w_gu[:, :d_ff], fp).astype(BF16)
    wu = _pad_cols(w_gu[:, d_ff:], fp).astype(BF16)
    wd = jnp.pad(w_down, ((0, fp - d_ff), (0, 0))).astype(BF16)
    return wg, wu, wd


def _tiles(m, s, d_model):
    return dict(
        ffn_tm=min(512, m), ffn_tf=512,
        proj_tm=min(256, m), proj_tn=256,
        merge_tm=min(512, m), merge_tn=min(512, d_model),
        lru_tc=min(256, s), mem_tm=min(512, s),
    )


def kernel(x, mem, rel_bias, final_norm, norm_ff1, w_ff1_gu, w_ff1_down, norm_mix, w_in, conv_w, conv_b, w_a, b_a, w_i, b_i, lam, idx_ln_g, idx_ln_b, mem_norm, w_mem_kv, w_branch, w_out, norm_ff2, w_ff2_gu, w_ff2_down):
    b, s, d = x.shape
    nm = mem.shape[1]
    depth = w_in.shape[0]
    m = b * s
    assert s % SC == 0 and d % LANES == 0
    tl = _tiles(m, s, d)
    topk = min(TOPK_MAX, s // 4)
    bias_near = _near_bias(rel_bias)
    row = lambda v: v.reshape(1, -1)

    x2 = x.reshape(m, d)
    mem2 = mem.reshape(b * nm, d)
    for l in range(depth):
        wg, wu, wd = _pad_ff(w_ff1_gu[l], w_ff1_down[l], tl["ffn_tf"])
        x2 = _ffn(x2, row(norm_ff1[l]), wg, wu, wd, row(final_norm), final_norm=False,
                  tm=tl["ffn_tm"], tf=tl["ffn_tf"])

        w16, w32, wgates = _prep_layer(w_in[l], d)
        p16, p32 = _proj(x2, row(norm_mix[l]), w16, w32, tm=tl["proj_tm"])
        p16 = p16.reshape(b, s, -1)
        p32 = p32.reshape(b, s, -1)

        v_cols = slice(p16.shape[-1] - A_KV_HEADS * A_HEAD_DIM, p16.shape[-1])
        vt = p16[..., v_cols].reshape(b, s // SC, SC, -1).transpose(0, 1, 3, 2)
        vt_blk = p16[..., v_cols].reshape(b, s // QB, QB, -1).transpose(0, 1, 3, 2)
        y_a = _dsa(p16, p32, vt, vt_blk, row(_pad_cols(row(idx_ln_g[l]), LANES)), row(_pad_cols(row(idx_ln_b[l]), LANES)),
                   bias_near, topk=topk, radix16=(l == 0))
        y_b = _lru(p32, conv_w[l], row(conv_b[l]), w_a[l].astype(BF16), w_i[l].astype(BF16),
                   row(b_a[l]), row(b_i[l]), row(lam[l]), tc=tl["lru_tc"])
        mkv = _norm_matmul(mem2, row(mem_norm[l]), w_mem_kv[l].astype(BF16), BF16,
                           tm=min(256, b * nm), tn=tl["proj_tn"], name="memkv")
        y_c = _memattn(p16, mkv.reshape(b, nm, -1), tm=tl["mem_tm"])

        x2 = _merge(x2, row(norm_mix[l]), y_a.reshape(m, -1), y_b.reshape(m, -1), y_c.reshape(m, -1),
                    wgates, w_branch[l].astype(BF16), w_out[l].astype(BF16),
                    tm=tl["merge_tm"], tn=tl["merge_tn"])

        wg, wu, wd = _pad_ff(w_ff2_gu[l], w_ff2_down[l], tl["ffn_tf"])
        x2 = _ffn(x2, row(norm_ff2[l]), wg, wu, wd, row(final_norm), final_norm=(l == depth - 1),
                  tm=tl["ffn_tm"], tf=tl["ffn_tf"])
    return x2.reshape(b, s, d)
```

```python
import functools
import math

import jax
import jax.numpy as jnp
from jax import lax
from jax.experimental import pallas as pl
from jax.experimental.pallas import tpu as pltpu

F32 = jnp.float32
BF16 = jnp.bfloat16
I32 = jnp.int32

EPS = 1e-6
A_HEADS = 8
A_KV_HEADS = 2
A_HEAD_DIM = 128
A_GROUP = A_HEADS // A_KV_HEADS
IDX_HEADS = 4
IDX_DIM = 64
TOPK_MAX = 256
LRU_WIDTH = 1024
LRU_BLOCKS = 8
LRU_BLOCK_W = LRU_WIDTH // LRU_BLOCKS
CONV_WIDTH = 4
LRU_C = 8.0
X_HEADS = 4
X_HEAD_DIM = 256
N_BRANCH = 3
BRANCH_WIDTH = 1024
N_BUCKETS = 32
MAX_DISTANCE = 128

LANES = 128
SUBLANES = 8
VMEM_LIMIT = 56 * 1024 * 1024

QB = 128
SC = 512
SUB = SC // QB
NEG = -1e30
LOG2E = math.log2(math.e)
INT_MIN = -2 ** 31


def _params(*sem):
    return pltpu.CompilerParams(dimension_semantics=sem, vmem_limit_bytes=VMEM_LIMIT)


def _rms(x, g):
    ms = jnp.mean(x * x, axis=-1, keepdims=True)
    return x * lax.rsqrt(ms + EPS) * g


def _sigmoid(x):
    return 1.0 / (1.0 + jnp.exp(-x))


def _dot(a, b):
    return jnp.dot(a, b, preferred_element_type=F32)


def _dot_nt(a, b):
    return lax.dot_general(a, b, (((1,), (1,)), ((), ())), preferred_element_type=F32)


def _ffn_kernel(x_ref, g_ref, wg_ref, wu_ref, wd_ref, fg_ref, o_ref, h_ref, *, final_norm):
    f = pl.program_id(1)

    @pl.when(f == 0)
    def _():
        h_ref[...] = _rms(x_ref[...], g_ref[...]).astype(BF16)
        o_ref[...] = jnp.zeros_like(o_ref)

    h = h_ref[...]
    g = _dot(h, wg_ref[...])
    u = _dot(h, wu_ref[...])
    a = (g * _sigmoid(g) * u).astype(BF16)
    o_ref[...] += _dot(a, wd_ref[...])

    @pl.when(f == pl.num_programs(1) - 1)
    def _():
        y = x_ref[...] + 0.5 * o_ref[...]
        if final_norm:
            y = _rms(y, fg_ref[...])
        o_ref[...] = y


def _ffn(x2, g, wg, wu, wd, fg, *, final_norm, tm, tf):
    m, d = x2.shape
    fp = wg.shape[1]
    return pl.pallas_call(
        functools.partial(_ffn_kernel, final_norm=final_norm),
        out_shape=jax.ShapeDtypeStruct((m, d), F32),
        grid=(m // tm, fp // tf),
        in_specs=[
            pl.BlockSpec((tm, d), lambda i, f: (i, 0)),
            pl.BlockSpec((1, d), lambda i, f: (0, 0)),
            pl.BlockSpec((d, tf), lambda i, f: (0, f)),
            pl.BlockSpec((d, tf), lambda i, f: (0, f)),
            pl.BlockSpec((tf, d), lambda i, f: (f, 0)),
            pl.BlockSpec((1, d), lambda i, f: (0, 0)),
        ],
        out_specs=pl.BlockSpec((tm, d), lambda i, f: (i, 0)),
        scratch_shapes=[pltpu.VMEM((tm, d), BF16)],
        compiler_params=_params("parallel", "arbitrary"),
        name="ffn",
    )(x2, g, wg, wu, wd, fg)


def _norm_matmul_kernel(x_ref, g_ref, w_ref, o_ref, h_ref):
    @pl.when(pl.program_id(1) == 0)
    def _():
        h_ref[...] = _rms(x_ref[...], g_ref[...]).astype(BF16)

    o_ref[...] = _dot(h_ref[...], w_ref[...]).astype(o_ref.dtype)


def _norm_matmul(x2, g, w, out_dtype, *, tm, tn, name):
    m, d = x2.shape
    n = w.shape[1]
    return pl.pallas_call(
        _norm_matmul_kernel,
        out_shape=jax.ShapeDtypeStruct((m, n), out_dtype),
        grid=(m // tm, n // tn),
        in_specs=[
            pl.BlockSpec((tm, d), lambda i, j: (i, 0)),
            pl.BlockSpec((1, d), lambda i, j: (0, 0)),
            pl.BlockSpec((d, tn), lambda i, j: (0, j)),
        ],
        out_specs=pl.BlockSpec((tm, tn), lambda i, j: (i, j)),
        scratch_shapes=[pltpu.VMEM((tm, d), BF16)],
        compiler_params=_params("parallel", "arbitrary"),
        name=name,
    )(x2, g, w)


def _proj_kernel(x_ref, g_ref, w16_ref, w32_ref, o16_ref, o32_ref):
    h = _rms(x_ref[...], g_ref[...]).astype(BF16)
    o16_ref[...] = _dot(h, w16_ref[...]).astype(o16_ref.dtype)
    o32_ref[...] = _dot(h, w32_ref[...])


def _proj(x2, g, w16, w32, *, tm):
    m, d = x2.shape
    n16, n32 = w16.shape[1], w32.shape[1]
    resident = pl.Buffered(1)
    return pl.pallas_call(
        _proj_kernel,
        out_shape=(jax.ShapeDtypeStruct((m, n16), BF16), jax.ShapeDtypeStruct((m, n32), F32)),
        grid=(m // tm,),
        in_specs=[
            pl.BlockSpec((tm, d), lambda i: (i, 0)),
            pl.BlockSpec((1, d), lambda i: (0, 0)),
            pl.BlockSpec((d, n16), lambda i: (0, 0), pipeline_mode=resident),
            pl.BlockSpec((d, n32), lambda i: (0, 0), pipeline_mode=resident),
        ],
        out_specs=(pl.BlockSpec((tm, n16), lambda i: (i, 0)), pl.BlockSpec((tm, n32), lambda i: (i, 0))),
        compiler_params=_params("parallel"),
        name="proj",
    )(x2, g, w16, w32)


def _lru_kernel(x_ref, gate_ref, cw_ref, cb_ref, wa_ref, wi_ref, ba_ref, bi_ref, lam_ref,
                o_ref, xbuf_ref, hc_ref, *, tc):
    halo = SUBLANES

    @pl.when(pl.program_id(1) == 0)
    def _():
        xbuf_ref[0:halo, :] = jnp.zeros((halo, LRU_WIDTH), F32)
        hc_ref[...] = jnp.zeros_like(hc_ref)

    xbuf_ref[halo:halo + tc, :] = x_ref[0]
    base = halo - (CONV_WIDTH - 1)
    xc = xbuf_ref[base:base + tc, :] * cw_ref[0:1, :]
    for j in range(1, CONV_WIDTH):
        xc = xc + xbuf_ref[base + j:base + j + tc, :] * cw_ref[j:j + 1, :]
    xc = cb_ref[...] + xc
    xbuf_ref[0:halo, :] = xbuf_ref[tc:tc + halo, :]

    xcb = xc.astype(BF16)
    r_parts, i_parts = [], []
    for n in range(LRU_BLOCKS):
        blk = xcb[:, n * LRU_BLOCK_W:(n + 1) * LRU_BLOCK_W]
        r_parts.append(_dot(blk, wa_ref[n]))
        i_parts.append(_dot(blk, wi_ref[n]))
    r = _sigmoid(jnp.concatenate(r_parts, axis=1) + ba_ref[...])
    gi = _sigmoid(jnp.concatenate(i_parts, axis=1) + bi_ref[...])

    softplus_neg_lam = jnp.log1p(jnp.exp(-lam_ref[...]))
    log_a = (-LRU_C) * r * softplus_neg_lam
    a = jnp.exp(log_a)
    th = jnp.tanh(log_a)
    mult = jnp.sqrt(jnp.maximum(-2.0 * th / (1.0 - th), 0.0))
    u = mult * (gi * xc)

    row = lax.broadcasted_iota(I32, (tc, LRU_WIDTH), 0)
    d = 1
    while d < tc:
        keep = row >= d
        a_sh = jnp.where(keep, pltpu.roll(a, d, 0), 1.0)
        u_sh = jnp.where(keep, pltpu.roll(u, d, 0), 0.0)
        u = a * u_sh + u
        a = a * a_sh
        d *= 2
    h = a * hc_ref[...] + u
    hc_ref[...] = h[tc - 1:tc, :]

    gate = gate_ref[0]
    gelu = 0.5 * gate * (1.0 + jnp.tanh(math.sqrt(2.0 / math.pi) * (gate + 0.044715 * (gate * gate * gate))))
    o_ref[0] = (h * gelu).astype(o_ref.dtype)


def _lru(p32, conv_w, conv_b, w_a, w_i, b_a, b_i, lam, *, tc):
    b, s, _ = p32.shape
    vec = pl.BlockSpec((1, LRU_WIDTH), lambda bi, c: (0, 0))
    wblk = pl.BlockSpec((LRU_BLOCKS, LRU_BLOCK_W, LRU_BLOCK_W), lambda bi, c: (0, 0, 0))
    return pl.pallas_call(
        functools.partial(_lru_kernel, tc=tc),
        out_shape=jax.ShapeDtypeStruct((b, s, LRU_WIDTH), BF16),
        grid=(b, s // tc),
        in_specs=[
            pl.BlockSpec((1, tc, LRU_WIDTH), lambda bi, c: (bi, c, 0)),
            pl.BlockSpec((1, tc, LRU_WIDTH), lambda bi, c: (bi, c, 1)),
            pl.BlockSpec((CONV_WIDTH, LRU_WIDTH), lambda bi, c: (0, 0)),
            vec, wblk, wblk, vec, vec, vec,
        ],
        out_specs=pl.BlockSpec((1, tc, LRU_WIDTH), lambda bi, c: (bi, c, 0)),
        scratch_shapes=[pltpu.VMEM((tc + SUBLANES, LRU_WIDTH), F32), pltpu.VMEM((1, LRU_WIDTH), F32)],
        compiler_params=_params("arbitrary", "arbitrary"),
        name="rglru",
    )(p32, p32, conv_w, conv_b, w_a, w_i, b_a, b_i, lam)


def _memattn_kernel(q_ref, kv_ref, o_ref):
    kw = X_HEADS * X_HEAD_DIM
    for h in range(X_HEADS):
        sl = slice(h * X_HEAD_DIM, (h + 1) * X_HEAD_DIM)
        q = q_ref[0, :, sl]
        k = kv_ref[0, :, sl]
        v = kv_ref[0, :, kw + h * X_HEAD_DIM:kw + (h + 1) * X_HEAD_DIM]
        s = _dot_nt(q, k)
        m = jnp.max(s, axis=-1, keepdims=True)
        p = jnp.exp(s - m)
        l = jnp.sum(p, axis=-1, keepdims=True)
        o = _dot(p.astype(BF16), v)
        o_ref[0, :, sl] = (o / l).astype(o_ref.dtype)


def _memattn(p16, mkv, *, tm):
    b, s, _ = p16.shape
    nm = mkv.shape[1]
    kw = X_HEADS * X_HEAD_DIM
    return pl.pallas_call(
        _memattn_kernel,
        out_shape=jax.ShapeDtypeStruct((b, s, kw), BF16),
        grid=(b, s // tm),
        in_specs=[
            pl.BlockSpec((1, tm, kw), lambda bi, i: (bi, i, 1)),
            pl.BlockSpec((1, nm, 2 * kw), lambda bi, i: (bi, 0, 0)),
        ],
        out_specs=pl.BlockSpec((1, tm, kw), lambda bi, i: (bi, i, 0)),
        compiler_params=_params("parallel", "parallel"),
        name="memattn",
    )(p16, mkv)


def _merge_kernel(x_ref, g_ref, ya_ref, yb_ref, yc_ref, wg0_ref, wg1_ref, wg2_ref, wb_ref, wo_ref,
                  o_ref, h_ref):
    n = pl.program_id(1)

    @pl.when(n == 0)
    def _():
        h_ref[...] = _rms(x_ref[...], g_ref[...]).astype(BF16)
        o_ref[...] = jnp.zeros_like(o_ref)

    h = h_ref[...]
    merged = None
    for j, (y_ref, wg_ref) in enumerate(((ya_ref, wg0_ref), (yb_ref, wg1_ref), (yc_ref, wg2_ref))):
        term = _sigmoid(_dot(h, wg_ref[...])) * _dot(y_ref[...], wb_ref[j])
        merged = term if merged is None else merged + term
    o_ref[...] += _dot(merged.astype(BF16), wo_ref[...])

    @pl.when(n == pl.num_programs(1) - 1)
    def _():
        o_ref[...] = x_ref[...] + o_ref[...]


def _merge(x2, g, ya, yb, yc, wgates, wbranch, wout, *, tm, tn):
    m, d = x2.shape
    nblk = d // tn
    yspec = pl.BlockSpec((tm, BRANCH_WIDTH), lambda i, n: (i, 0))
    return pl.pallas_call(
        _merge_kernel,
        out_shape=jax.ShapeDtypeStruct((m, d), F32),
        grid=(m // tm, nblk),
        in_specs=[
            pl.BlockSpec((tm, d), lambda i, n: (i, 0)),
            pl.BlockSpec((1, d), lambda i, n: (0, 0)),
            yspec, yspec, yspec,
            pl.BlockSpec((d, tn), lambda i, n: (0, n)),
            pl.BlockSpec((d, tn), lambda i, n: (0, nblk + n)),
            pl.BlockSpec((d, tn), lambda i, n: (0, 2 * nblk + n)),
            pl.BlockSpec((N_BRANCH, BRANCH_WIDTH, tn), lambda i, n: (0, 0, n)),
            pl.BlockSpec((tn, d), lambda i, n: (n, 0)),
        ],
        out_specs=pl.BlockSpec((tm, d), lambda i, n: (i, 0)),
        scratch_shapes=[pltpu.VMEM((tm, d), BF16)],
        compiler_params=_params("parallel", "arbitrary"),
        name="merge",
    )(x2, g, ya, yb, yc, wgates, wgates, wgates, wbranch, wout)


def _dsa_kernel(q_ref, k_ref, vt_ref, vtp_ref, vtd_ref, qi_ref, ki_ref, wi_ref, lng_ref, lnb_ref, bias_ref,
                o_ref,
                kln_ref, keys_ref, madd_ref, mnear_ref, qi4_ref, q4_ref, m_ref, l_ref, acc_ref,
                *, topk, far_unroll):
    i = pl.program_id(1)
    s_len = k_ref.shape[1]

    @pl.when(i == 0)
    def _():
        rows = 512 if s_len % 512 == 0 else QB
        lane = lax.broadcasted_iota(I32, (rows, LANES), 1)
        real = lane < IDX_DIM

        def ln_body(c, carry):
            x = ki_ref[0, pl.ds(c * rows, rows), :]
            mu = jnp.sum(x, axis=-1, keepdims=True) * (1.0 / IDX_DIM)
            xm = jnp.where(real, x - mu, 0.0)
            var = jnp.sum(xm * xm, axis=-1, keepdims=True) * (1.0 / IDX_DIM)
            y = xm * lax.rsqrt(var + EPS) * lng_ref[...] + lnb_ref[...]
            kln_ref[pl.ds(c * rows, rows), :] = y.astype(BF16)
            return carry

        lax.fori_loop(0, s_len // rows, ln_body, 0)

    w_rows = wi_ref[0].T * (IDX_HEADS ** -0.5 * IDX_DIM ** -0.5)
    for h in range(IDX_HEADS):
        qi4_ref[h * QB:(h + 1) * QB, :] = qi_ref[0, :, h * LANES:(h + 1) * LANES].astype(BF16)
    eye = (lax.broadcasted_iota(I32, (QB, QB), 0) == lax.broadcasted_iota(I32, (QB, QB), 1)).astype(BF16)
    for n in range(A_KV_HEADS):
        for g in range(A_GROUP):
            hd = n * A_GROUP + g
            q4_ref[n, g * QB:(g + 1) * QB, 0:A_HEAD_DIM] = q_ref[0, :, hd * A_HEAD_DIM:(hd + 1) * A_HEAD_DIM]
            q4_ref[n, g * QB:(g + 1) * QB, A_HEAD_DIM:A_HEAD_DIM + QB] = eye

    last = i // SUB
    nsc = last + 1

    def rows(j):
        return pl.ds(pl.multiple_of(j * SC, SC), SC)

    def index_dots(j):
        return _dot_nt(kln_ref[rows(j), :], qi4_ref[...])

    def score_chunk(j, dots, causal):
        sc = None
        for h in range(IDX_HEADS):
            t = w_rows[h:h + 1, :] * jnp.maximum(dots[:, h * QB:(h + 1) * QB], 0.0)
            sc = t if sc is None else sc + t
        bits = pltpu.bitcast(sc, I32)
        key = jnp.where(bits < 0, bits ^ jnp.int32(0x7FFFFFFF), bits)
        key = jnp.where(sc == 0.0, 0, key)
        if causal:
            key_pos = j * SC + lax.broadcasted_iota(I32, (SC, LANES), 0)
            q_pos = i * QB + lax.broadcasted_iota(I32, (SC, LANES), 1)
            key = jnp.where(key_pos <= q_pos, key, INT_MIN)
        keys_ref[rows(j), :] = key

    def score_body(pr, carry):
        d0 = index_dots(2 * pr)
        d1 = index_dots(2 * pr + 1)
        score_chunk(2 * pr, d0, False)
        score_chunk(2 * pr + 1, d1, False)
        return carry

    lax.fori_loop(0, last // 2, score_body, 0)

    @pl.when(last % 2 == 1)
    def _():
        score_chunk(last - 1, index_dots(last - 1), False)

    score_chunk(last, index_dots(last), True)

    t_pos = i * QB + lax.broadcasted_iota(I32, (1, LANES), 1)
    kk = jnp.minimum(t_pos + 1, topk).astype(F32)

    def count_ge(cand):
        def body(j, cnt):
            for r in range(SUB):
                blk = keys_ref[pl.ds(pl.multiple_of(j * SC + r * QB, QB), QB), :]
                cnt = cnt + jnp.where(blk >= cand, 1.0, 0.0)
            return cnt

        cnt = lax.fori_loop(0, nsc, body, jnp.zeros((QB, LANES), F32))
        return jnp.sum(cnt, axis=0, keepdims=True)

    def bit_body(bi, t):
        cand = t + lax.shift_left(jnp.int32(1), 31 - bi)
        return jnp.where(count_ge(cand) >= kk, cand, t)

    thr = lax.fori_loop(0, 32, bit_body, jnp.full((1, LANES), INT_MIN, I32))
    need = kk - count_ge(thr + 1)
    sub_k = lax.broadcasted_iota(I32, (QB, QB), 0)
    sub_k2 = lax.broadcasted_iota(I32, (QB, QB), 1)
    tril = (sub_k2 <= sub_k).astype(BF16)

    def mask_body(j, seen):
        keys, eqs, pres = [], [], []
        for r in range(SUB):
            key = keys_ref[pl.ds(pl.multiple_of(j * SC + r * QB, QB), QB), :]
            eq = key == thr
            keys.append(key)
            eqs.append(eq)
            pres.append(_dot(tril, jnp.where(eq, 1.0, 0.0).astype(BF16)))
        for r in range(SUB):
            sel = (keys[r] > thr) | (eqs[r] & ((seen + pres[r]) <= need))
            madd_ref[pl.ds(pl.multiple_of(j * SC + r * QB, QB), QB), :] = jnp.where(sel, 0.0, NEG).astype(BF16)
            seen = seen + pres[r][QB - 1:QB, :]
        return seen

    lax.fori_loop(0, nsc, mask_body, jnp.zeros((1, LANES), F32))

    def split_near(slot, blk):
        blk_rows = pl.ds(pl.multiple_of(blk * QB, QB), QB)
        mnear_ref[slot] = madd_ref[blk_rows, :]
        madd_ref[blk_rows, :] = jnp.full((QB, LANES), NEG, BF16)

    split_near(1, i)

    @pl.when(i >= 1)
    def _():
        split_near(0, i - 1)

    m_ref[...] = jnp.full(m_ref.shape, NEG, F32)
    l_ref[...] = jnp.zeros(l_ref.shape, F32)
    acc_ref[...] = jnp.zeros(acc_ref.shape, F32)

    def logits(kc, madd, n):
        return _dot_nt(jnp.concatenate([kc, madd], axis=1), q4_ref[n])

    def far_logits(j, n):
        return logits(k_ref[0, rows(j), n * A_HEAD_DIM:(n + 1) * A_HEAD_DIM], madd_ref[rows(j), :], n)

    def softmax_pv(n, s, vt):
        m_prev = m_ref[n]
        m_new = jnp.maximum(m_prev, jnp.max(s, axis=0, keepdims=True))
        alpha = jnp.exp2(m_prev - m_new)
        p = jnp.exp2(s - m_new)
        l_ref[n] = alpha * l_ref[n] + jnp.sum(p, axis=0, keepdims=True)
        acc_ref[n] = alpha * acc_ref[n] + _dot(vt, p.astype(BF16))
        m_ref[n] = m_new

    def far_vt(j, n):
        return vt_ref[0, j, n * A_HEAD_DIM:(n + 1) * A_HEAD_DIM, :]

    def sweep(first, count):
        tasks = [(first + c, n) for c in range(count) for n in range(A_KV_HEADS)]
        ahead = A_KV_HEADS
        pending = [far_logits(j, n) for j, n in tasks[:ahead]]
        for t, (j, n) in enumerate(tasks):
            softmax_pv(n, pending.pop(0), far_vt(j, n))
            if t + ahead < len(tasks):
                pending.append(far_logits(*tasks[t + ahead]))

    def sweep_body(g, carry):
        sweep(far_unroll * g, far_unroll)
        return carry

    n_far = jnp.where(i % SUB >= 2, nsc, last)
    lax.fori_loop(0, n_far // far_unroll, sweep_body, 0)
    for rem in range(1, far_unroll):
        @pl.when(n_far % far_unroll == rem)
        def _(rem=rem):
            sweep(n_far - rem, rem)

    def near_step(slot, blk, vts_ref):
        blk_rows = pl.ds(pl.multiple_of(blk * QB, QB), QB)
        ss = [logits(k_ref[0, blk_rows, n * A_HEAD_DIM:(n + 1) * A_HEAD_DIM], mnear_ref[slot], n)
              + bias_ref[slot, n] for n in range(A_KV_HEADS)]
        for n in range(A_KV_HEADS):
            softmax_pv(n, ss[n], vts_ref[0, 0, n * A_HEAD_DIM:(n + 1) * A_HEAD_DIM, :])

    @pl.when(i >= 1)
    def _():
        near_step(0, i - 1, vtp_ref)

    near_step(1, i, vtd_ref)

    for n in range(A_KV_HEADS):
        out_t = acc_ref[n] / l_ref[n]
        for g in range(A_GROUP):
            hd = n * A_GROUP + g
            o_ref[0, :, hd * A_HEAD_DIM:(hd + 1) * A_HEAD_DIM] = out_t[:, g * QB:(g + 1) * QB].T.astype(o_ref.dtype)


def _dsa(p16, p32, vt, vt_blk, ln_g, ln_b, bias_near, *, topk, far_unroll):
    b, s, _ = p16.shape
    aq = A_HEADS * A_HEAD_DIM
    akv = A_KV_HEADS * A_HEAD_DIM
    qiw = IDX_HEADS * LANES
    nch = s // QB
    gq = A_GROUP * QB
    k_blk = (aq + X_HEADS * X_HEAD_DIM) // akv
    qi_blk = 2 * LRU_WIDTH // qiw
    ki_blk = (2 * LRU_WIDTH + qiw) // LANES
    return pl.pallas_call(
        functools.partial(_dsa_kernel, topk=topk, far_unroll=far_unroll),
        out_shape=jax.ShapeDtypeStruct((b, s, aq), BF16),
        grid=(b, nch),
        in_specs=[
            pl.BlockSpec((1, QB, aq), lambda bi, i: (bi, i, 0)),
            pl.BlockSpec((1, s, akv), lambda bi, i: (bi, 0, k_blk)),
            pl.BlockSpec((1, s // SC, akv, SC), lambda bi, i: (bi, 0, 0, 0)),
            pl.BlockSpec((1, 1, akv, QB), lambda bi, i: (bi, jnp.maximum(i - 1, 0), 0, 0)),
            pl.BlockSpec((1, 1, akv, QB), lambda bi, i: (bi, i, 0, 0)),
            pl.BlockSpec((1, QB, qiw), lambda bi, i: (bi, i, qi_blk)),
            pl.BlockSpec((1, s, LANES), lambda bi, i: (bi, 0, ki_blk)),
            pl.BlockSpec((1, QB, LANES), lambda bi, i: (bi, i, ki_blk + 1)),
            pl.BlockSpec((1, LANES), lambda bi, i: (0, 0)),
            pl.BlockSpec((1, LANES), lambda bi, i: (0, 0)),
            pl.BlockSpec((2, A_KV_HEADS, QB, gq), lambda bi, i: (0, 0, 0, 0)),
        ],
        out_specs=pl.BlockSpec((1, QB, aq), lambda bi, i: (bi, i, 0)),
        scratch_shapes=[
            pltpu.VMEM((s, LANES), BF16),
            pltpu.VMEM((s, LANES), I32),
            pltpu.VMEM((s, LANES), BF16),
            pltpu.VMEM((2, QB, LANES), BF16),
            pltpu.VMEM((IDX_HEADS * QB, LANES), BF16),
            pltpu.VMEM((A_KV_HEADS, gq, A_HEAD_DIM + QB), BF16),
            pltpu.VMEM((A_KV_HEADS, 1, gq), F32),
            pltpu.VMEM((A_KV_HEADS, 1, gq), F32),
            pltpu.VMEM((A_KV_HEADS, A_HEAD_DIM, gq), F32),
        ],
        compiler_params=_params("arbitrary", "arbitrary"),
        name="dsa",
    )(p16, p16, vt, vt_blk, vt_blk, p32, p32, p32, ln_g, ln_b, bias_near)


def _t5_bucket(dist):
    max_exact = N_BUCKETS // 2
    d = jnp.maximum(dist, 0)
    df = jnp.maximum(d, 1).astype(F32)
    large = max_exact + (jnp.log(df / max_exact) / math.log(MAX_DISTANCE / max_exact)
                         * (N_BUCKETS - max_exact)).astype(I32)
    large = jnp.minimum(large, N_BUCKETS - 1)
    return jnp.where(d < max_exact, d, large)


def _near_bias(rel_bias):
    t = jnp.arange(QB, dtype=I32)[:, None]
    s = jnp.arange(QB, dtype=I32)[None, :]
    tiles = []
    for off in (QB, 0):
        bucket = _t5_bucket(t - s + off)
        onehot = (bucket[..., None] == jnp.arange(N_BUCKETS, dtype=I32)).astype(F32)
        table = (rel_bias - rel_bias[N_BUCKETS - 1]) * LOG2E
        tile = jnp.einsum('tsb,bh->tsh', onehot, table, precision=lax.Precision.HIGHEST)
        tile = tile.transpose(1, 2, 0).reshape(QB, A_KV_HEADS, A_GROUP * QB)
        tiles.append(tile.transpose(1, 0, 2))
    return jnp.stack(tiles).astype(F32)


def _pad_cols(w, width):
    return jnp.pad(w, [(0, 0)] * (w.ndim - 1) + [(0, width - w.shape[-1])])


def _prep_w_in(w_in, d_model):
    aq = A_HEADS * A_HEAD_DIM
    akv = A_KV_HEADS * A_HEAD_DIM
    sizes = (aq, akv, akv, IDX_HEADS * IDX_DIM, IDX_DIM, IDX_HEADS,
             LRU_WIDTH, LRU_WIDTH, X_HEADS * X_HEAD_DIM, N_BRANCH * d_model)
    pts, acc = [], 0
    for n in sizes[:-1]:
        acc += n
        pts.append(acc)
    col = jnp.arange(w_in.shape[-1])
    scale = jnp.where(col < aq, A_HEAD_DIM ** -0.5 * LOG2E,
                      jnp.where((col >= pts[7]) & (col < pts[8]), X_HEAD_DIM ** -0.5, 1.0)).astype(F32)
    wb = (w_in * scale).astype(BF16)
    wq, wk, wv, wqi, wki, wwi, wx, wgl, wqm, wgates = jnp.split(wb, pts, axis=-1)
    w16 = jnp.concatenate([wq, wqm, wk, wv], axis=-1)
    wqi_p = jnp.concatenate(
        [_pad_cols(wqi[..., h * IDX_DIM:(h + 1) * IDX_DIM], LANES) for h in range(IDX_HEADS)], axis=-1)
    w32 = jnp.concatenate([wx, wgl, wqi_p, _pad_cols(wki, LANES), _pad_cols(wwi, LANES)], axis=-1)
    return w16, w32, wgates


def _pad_ff(w_gu, w_down, tf):
    d_ff = w_down.shape[-2]
    fp = -(-d_ff // tf) * tf
    wg = _pad_cols(w_gu[..., :d_ff], fp).astype(BF16)
    wu = _pad_cols(w_gu[..., d_ff:], fp).astype(BF16)
    wd = jnp.pad(w_down, ((0, 0), (0, fp - d_ff), (0, 0))).astype(BF16)
    return wg, wu, wd


def _tiles(m, s, d_model):
    return dict(
        ffn_tm=min(512, m), ffn_tf=512,
        proj_tm=min(256, m), proj_tn=256,
        merge_tm=min(512, m), merge_tn=min(512, d_model),
        lru_tc=min(256, s), mem_tm=min(512, s),
    )


def kernel(x, mem, rel_bias, final_norm, norm_ff1, w_ff1_gu, w_ff1_down, norm_mix, w_in, conv_w, conv_b, w_a, b_a, w_i, b_i, lam, idx_ln_g, idx_ln_b, mem_norm, w_mem_kv, w_branch, w_out, norm_ff2, w_ff2_gu, w_ff2_down):
    b, s, d = x.shape
    nm = mem.shape[1]
    depth = w_in.shape[0]
    m = b * s
    assert s % SC == 0 and d % LANES == 0
    tl = _tiles(m, s, d)
    topk = min(TOPK_MAX, s // 4)
    bias_near = _near_bias(rel_bias)
    row = lambda v: v.reshape(1, -1)

    ff1 = _pad_ff(w_ff1_gu, w_ff1_down, tl["ffn_tf"])
    ff2 = _pad_ff(w_ff2_gu, w_ff2_down, tl["ffn_tf"])
    w16, w32, wgates = _prep_w_in(w_in, d)
    w_a16, w_i16 = w_a.astype(BF16), w_i.astype(BF16)
    w_mem16, w_branch16, w_out16 = w_mem_kv.astype(BF16), w_branch.astype(BF16), w_out.astype(BF16)
    ln_g = _pad_cols(idx_ln_g, LANES)
    ln_b = _pad_cols(idx_ln_b, LANES)

    x2 = x.reshape(m, d)
    mem2 = mem.reshape(b * nm, d)
    for l in range(depth):
        x2 = _ffn(x2, row(norm_ff1[l]), ff1[0][l], ff1[1][l], ff1[2][l], row(final_norm), final_norm=False,
                  tm=tl["ffn_tm"], tf=tl["ffn_tf"])

        p16, p32 = _proj(x2, row(norm_mix[l]), w16[l], w32[l], tm=tl["proj_tm"])
        p16 = p16.reshape(b, s, -1)
        p32 = p32.reshape(b, s, -1)

        v_cols = slice(p16.shape[-1] - A_KV_HEADS * A_HEAD_DIM, p16.shape[-1])
        vt = p16[..., v_cols].reshape(b, s // SC, SC, -1).transpose(0, 1, 3, 2)
        vt_blk = p16[..., v_cols].reshape(b, s // QB, QB, -1).transpose(0, 1, 3, 2)
        y_a = _dsa(p16, p32, vt, vt_blk, row(ln_g[l]), row(ln_b[l]),
                   bias_near, topk=topk, far_unroll=(4 if l == 0 else 2))
        y_b = _lru(p32, conv_w[l], row(conv_b[l]), w_a16[l], w_i16[l],
                   row(b_a[l]), row(b_i[l]), row(lam[l]), tc=tl["lru_tc"])
        mkv = _norm_matmul(mem2, row(mem_norm[l]), w_mem16[l], BF16,
                           tm=min(256, b * nm), tn=tl["proj_tn"], name="memkv")
        y_c = _memattn(p16, mkv.reshape(b, nm, -1), tm=tl["mem_tm"])

        x2 = _merge(x2, row(norm_mix[l]), y_a.reshape(m, -1), y_b.reshape(m, -1), y_c.reshape(m, -1),
                    wgates[l], w_branch16[l], w_out16[l],
                    tm=tl["merge_tm"], tn=tl["merge_tn"])

        x2 = _ffn(x2, row(norm_ff2[l]), ff2[0][l], ff2[1][l], ff2[2][l], row(final_norm),
                  final_norm=(l == depth - 1), tm=tl["ffn_tm"], tf=tl["ffn_tf"])
    return x2.reshape(b, s, d)
```

```python
import functools
import math

import jax
import jax.numpy as jnp
from jax import lax
from jax.experimental import pallas as pl
from jax.experimental.pallas import tpu as pltpu

F32 = jnp.float32
BF16 = jnp.bfloat16
I32 = jnp.int32

EPS = 1e-6
A_HEADS = 8
A_KV_HEADS = 2
A_HEAD_DIM = 128
A_GROUP = A_HEADS // A_KV_HEADS
IDX_HEADS = 4
IDX_DIM = 64
TOPK_MAX = 256
LRU_WIDTH = 1024
LRU_BLOCKS = 8
LRU_BLOCK_W = LRU_WIDTH // LRU_BLOCKS
CONV_WIDTH = 4
LRU_C = 8.0
X_HEADS = 4
X_HEAD_DIM = 256
N_BRANCH = 3
BRANCH_WIDTH = 1024
N_BUCKETS = 32
MAX_DISTANCE = 128

LANES = 128
SUBLANES = 8
VMEM_LIMIT = 56 * 1024 * 1024

QB = 128
SC = 512
SUB = SC // QB
NEG = -1e30
LOG2E = math.log2(math.e)
INT_MIN = -2 ** 31


def _params(*sem):
    return pltpu.CompilerParams(dimension_semantics=sem, vmem_limit_bytes=VMEM_LIMIT)


def _rms(x, g):
    ms = jnp.mean(x * x, axis=-1, keepdims=True)
    return x * lax.rsqrt(ms + EPS) * g


def _sigmoid(x):
    return 1.0 / (1.0 + jnp.exp(-x))


def _dot(a, b):
    return jnp.dot(a, b, preferred_element_type=F32)


def _dot_nt(a, b):
    return lax.dot_general(a, b, (((1,), (1,)), ((), ())), preferred_element_type=F32)


def _ffn_kernel(x_ref, g_ref, wg_ref, wu_ref, wd_ref, fg_ref, o_ref, h_ref, *, final_norm):
    f = pl.program_id(1)

    @pl.when(f == 0)
    def _():
        h_ref[...] = _rms(x_ref[...], g_ref[...]).astype(BF16)
        o_ref[...] = jnp.zeros_like(o_ref)

    h = h_ref[...]
    g = _dot(h, wg_ref[...])
    u = _dot(h, wu_ref[...])
    a = (g * _sigmoid(g) * u).astype(BF16)
    o_ref[...] += _dot(a, wd_ref[...])

    @pl.when(f == pl.num_programs(1) - 1)
    def _():
        y = x_ref[...] + 0.5 * o_ref[...]
        if final_norm:
            y = _rms(y, fg_ref[...])
        o_ref[...] = y


def _ffn(x2, g, wg, wu, wd, fg, *, final_norm, tm, tf):
    m, d = x2.shape
    fp = wg.shape[1]
    return pl.pallas_call(
        functools.partial(_ffn_kernel, final_norm=final_norm),
        out_shape=jax.ShapeDtypeStruct((m, d), F32),
        grid=(m // tm, fp // tf),
        in_specs=[
            pl.BlockSpec((tm, d), lambda i, f: (i, 0)),
            pl.BlockSpec((1, d), lambda i, f: (0, 0)),
            pl.BlockSpec((d, tf), lambda i, f: (0, f)),
            pl.BlockSpec((d, tf), lambda i, f: (0, f)),
            pl.BlockSpec((tf, d), lambda i, f: (f, 0)),
            pl.BlockSpec((1, d), lambda i, f: (0, 0)),
        ],
        out_specs=pl.BlockSpec((tm, d), lambda i, f: (i, 0)),
        scratch_shapes=[pltpu.VMEM((tm, d), BF16)],
        compiler_params=_params("parallel", "arbitrary"),
        name="ffn",
    )(x2, g, wg, wu, wd, fg)


def _norm_matmul_kernel(x_ref, g_ref, w_ref, o_ref, h_ref):
    @pl.when(pl.program_id(1) == 0)
    def _():
        h_ref[...] = _rms(x_ref[...], g_ref[...]).astype(BF16)

    o_ref[...] = _dot(h_ref[...], w_ref[...]).astype(o_ref.dtype)


def _norm_matmul(x2, g, w, out_dtype, *, tm, tn, name):
    m, d = x2.shape
    n = w.shape[1]
    return pl.pallas_call(
        _norm_matmul_kernel,
        out_shape=jax.ShapeDtypeStruct((m, n), out_dtype),
        grid=(m // tm, n // tn),
        in_specs=[
            pl.BlockSpec((tm, d), lambda i, j: (i, 0)),
            pl.BlockSpec((1, d), lambda i, j: (0, 0)),
            pl.BlockSpec((d, tn), lambda i, j: (0, j)),
        ],
        out_specs=pl.BlockSpec((tm, tn), lambda i, j: (i, j)),
        scratch_shapes=[pltpu.VMEM((tm, d), BF16)],
        compiler_params=_params("parallel", "arbitrary"),
        name=name,
    )(x2, g, w)


def _proj_kernel(x_ref, g_ref, w16_ref, w32_ref, o16_ref, o32_ref):
    h = _rms(x_ref[...], g_ref[...]).astype(BF16)
    o16_ref[...] = _dot(h, w16_ref[...]).astype(o16_ref.dtype)
    o32_ref[...] = _dot(h, w32_ref[...])


def _proj(x2, g, w16, w32, *, tm):
    m, d = x2.shape
    n16, n32 = w16.shape[1], w32.shape[1]
    resident = pl.Buffered(1)
    return pl.pallas_call(
        _proj_kernel,
        out_shape=(jax.ShapeDtypeStruct((m, n16), BF16), jax.ShapeDtypeStruct((m, n32), F32)),
        grid=(m // tm,),
        in_specs=[
            pl.BlockSpec((tm, d), lambda i: (i, 0)),
            pl.BlockSpec((1, d), lambda i: (0, 0)),
            pl.BlockSpec((d, n16), lambda i: (0, 0), pipeline_mode=resident),
            pl.BlockSpec((d, n32), lambda i: (0, 0), pipeline_mode=resident),
        ],
        out_specs=(pl.BlockSpec((tm, n16), lambda i: (i, 0)), pl.BlockSpec((tm, n32), lambda i: (i, 0))),
        compiler_params=_params("parallel"),
        name="proj",
    )(x2, g, w16, w32)


def _lru_kernel(x_ref, gate_ref, cw_ref, cb_ref, wa_ref, wi_ref, ba_ref, bi_ref, lam_ref,
                o_ref, xbuf_ref, hc_ref, *, tc):
    halo = SUBLANES

    @pl.when(pl.program_id(1) == 0)
    def _():
        xbuf_ref[0:halo, :] = jnp.zeros((halo, LRU_WIDTH), F32)
        hc_ref[...] = jnp.zeros_like(hc_ref)

    xbuf_ref[halo:halo + tc, :] = x_ref[0]
    base = halo - (CONV_WIDTH - 1)
    xc = xbuf_ref[base:base + tc, :] * cw_ref[0:1, :]
    for j in range(1, CONV_WIDTH):
        xc = xc + xbuf_ref[base + j:base + j + tc, :] * cw_ref[j:j + 1, :]
    xc = cb_ref[...] + xc
    xbuf_ref[0:halo, :] = xbuf_ref[tc:tc + halo, :]

    xcb = xc.astype(BF16)
    r_parts, i_parts = [], []
    for n in range(LRU_BLOCKS):
        blk = xcb[:, n * LRU_BLOCK_W:(n + 1) * LRU_BLOCK_W]
        r_parts.append(_dot(blk, wa_ref[n]))
        i_parts.append(_dot(blk, wi_ref[n]))
    r = _sigmoid(jnp.concatenate(r_parts, axis=1) + ba_ref[...])
    gi = _sigmoid(jnp.concatenate(i_parts, axis=1) + bi_ref[...])

    softplus_neg_lam = jnp.log1p(jnp.exp(-lam_ref[...]))
    log_a = (-LRU_C) * r * softplus_neg_lam
    a = jnp.exp(log_a)
    th = jnp.tanh(log_a)
    mult = jnp.sqrt(jnp.maximum(-2.0 * th / (1.0 - th), 0.0))
    u = mult * (gi * xc)

    row = lax.broadcasted_iota(I32, (tc, LRU_WIDTH), 0)
    d = 1
    while d < tc:
        keep = row >= d
        a_sh = jnp.where(keep, pltpu.roll(a, d, 0), 1.0)
        u_sh = jnp.where(keep, pltpu.roll(u, d, 0), 0.0)
        u = a * u_sh + u
        a = a * a_sh
        d *= 2
    h = a * hc_ref[...] + u
    hc_ref[...] = h[tc - 1:tc, :]

    gate = gate_ref[0]
    gelu = 0.5 * gate * (1.0 + jnp.tanh(math.sqrt(2.0 / math.pi) * (gate + 0.044715 * (gate * gate * gate))))
    o_ref[0] = (h * gelu).astype(o_ref.dtype)


def _lru(p32, conv_w, conv_b, w_a, w_i, b_a, b_i, lam, *, tc):
    b, s, _ = p32.shape
    vec = pl.BlockSpec((1, LRU_WIDTH), lambda bi, c: (0, 0))
    wblk = pl.BlockSpec((LRU_BLOCKS, LRU_BLOCK_W, LRU_BLOCK_W), lambda bi, c: (0, 0, 0))
    return pl.pallas_call(
        functools.partial(_lru_kernel, tc=tc),
        out_shape=jax.ShapeDtypeStruct((b, s, LRU_WIDTH), BF16),
        grid=(b, s // tc),
        in_specs=[
            pl.BlockSpec((1, tc, LRU_WIDTH), lambda bi, c: (bi, c, 0)),
            pl.BlockSpec((1, tc, LRU_WIDTH), lambda bi, c: (bi, c, 1)),
            pl.BlockSpec((CONV_WIDTH, LRU_WIDTH), lambda bi, c: (0, 0)),
            vec, wblk, wblk, vec, vec, vec,
        ],
        out_specs=pl.BlockSpec((1, tc, LRU_WIDTH), lambda bi, c: (bi, c, 0)),
        scratch_shapes=[pltpu.VMEM((tc + SUBLANES, LRU_WIDTH), F32), pltpu.VMEM((1, LRU_WIDTH), F32)],
        compiler_params=_params("arbitrary", "arbitrary"),
        name="rglru",
    )(p32, p32, conv_w, conv_b, w_a, w_i, b_a, b_i, lam)


def _memattn_kernel(q_ref, kv_ref, o_ref):
    kw = X_HEADS * X_HEAD_DIM
    for h in range(X_HEADS):
        sl = slice(h * X_HEAD_DIM, (h + 1) * X_HEAD_DIM)
        q = q_ref[0, :, sl]
        k = kv_ref[0, :, sl]
        v = kv_ref[0, :, kw + h * X_HEAD_DIM:kw + (h + 1) * X_HEAD_DIM]
        s = _dot_nt(q, k)
        m = jnp.max(s, axis=-1, keepdims=True)
        p = jnp.exp(s - m)
        l = jnp.sum(p, axis=-1, keepdims=True)
        o = _dot(p.astype(BF16), v)
        o_ref[0, :, sl] = (o / l).astype(o_ref.dtype)


def _memattn(p16, mkv, *, tm):
    b, s, _ = p16.shape
    nm = mkv.shape[1]
    kw = X_HEADS * X_HEAD_DIM
    return pl.pallas_call(
        _memattn_kernel,
        out_shape=jax.ShapeDtypeStruct((b, s, kw), BF16),
        grid=(b, s // tm),
        in_specs=[
            pl.BlockSpec((1, tm, kw), lambda bi, i: (bi, i, 1)),
            pl.BlockSpec((1, nm, 2 * kw), lambda bi, i: (bi, 0, 0)),
        ],
        out_specs=pl.BlockSpec((1, tm, kw), lambda bi, i: (bi, i, 0)),
        compiler_params=_params("parallel", "parallel"),
        name="memattn",
    )(p16, mkv)


def _merge_kernel(x_ref, g_ref, ya_ref, yb_ref, yc_ref, wg0_ref, wg1_ref, wg2_ref, wb_ref, wo_ref,
                  o_ref, h_ref):
    n = pl.program_id(1)

    @pl.when(n == 0)
    def _():
        h_ref[...] = _rms(x_ref[...], g_ref[...]).astype(BF16)
        o_ref[...] = jnp.zeros_like(o_ref)

    h = h_ref[...]
    merged = None
    for j, (y_ref, wg_ref) in enumerate(((ya_ref, wg0_ref), (yb_ref, wg1_ref), (yc_ref, wg2_ref))):
        term = _sigmoid(_dot(h, wg_ref[...])) * _dot(y_ref[...], wb_ref[j])
        merged = term if merged is None else merged + term
    o_ref[...] += _dot(merged.astype(BF16), wo_ref[...])

    @pl.when(n == pl.num_programs(1) - 1)
    def _():
        o_ref[...] = x_ref[...] + o_ref[...]


def _merge(x2, g, ya, yb, yc, wgates, wbranch, wout, *, tm, tn):
    m, d = x2.shape
    nblk = d // tn
    yspec = pl.BlockSpec((tm, BRANCH_WIDTH), lambda i, n: (i, 0))
    return pl.pallas_call(
        _merge_kernel,
        out_shape=jax.ShapeDtypeStruct((m, d), F32),
        grid=(m // tm, nblk),
        in_specs=[
            pl.BlockSpec((tm, d), lambda i, n: (i, 0)),
            pl.BlockSpec((1, d), lambda i, n: (0, 0)),
            yspec, yspec, yspec,
            pl.BlockSpec((d, tn), lambda i, n: (0, n)),
            pl.BlockSpec((d, tn), lambda i, n: (0, nblk + n)),
            pl.BlockSpec((d, tn), lambda i, n: (0, 2 * nblk + n)),
            pl.BlockSpec((N_BRANCH, BRANCH_WIDTH, tn), lambda i, n: (0, 0, n)),
            pl.BlockSpec((tn, d), lambda i, n: (n, 0)),
        ],
        out_specs=pl.BlockSpec((tm, d), lambda i, n: (i, 0)),
        scratch_shapes=[pltpu.VMEM((tm, d), BF16)],
        compiler_params=_params("parallel", "arbitrary"),
        name="merge",
    )(x2, g, ya, yb, yc, wgates, wgates, wgates, wbranch, wout)


def _dsa_kernel(q_ref, k_ref, vt_ref, vtp_ref, vtd_ref, qi_ref, ki_ref, wi_ref, lng_ref, lnb_ref, bias_ref,
                o_ref,
                kln_ref, keys_ref, madd_ref, mnear_ref, qi4_ref, q4_ref, m_ref, l_ref, acc_ref,
                *, topk, far_unroll):
    i = pl.program_id(1)
    s_len = k_ref.shape[1]

    @pl.when(i == 0)
    def _():
        rows = 512 if s_len % 512 == 0 else QB
        lane = lax.broadcasted_iota(I32, (rows, LANES), 1)
        real = lane < IDX_DIM

        def ln_body(c, carry):
            x = ki_ref[0, pl.ds(c * rows, rows), :]
            mu = jnp.sum(x, axis=-1, keepdims=True) * (1.0 / IDX_DIM)
            xm = jnp.where(real, x - mu, 0.0)
            var = jnp.sum(xm * xm, axis=-1, keepdims=True) * (1.0 / IDX_DIM)
            y = xm * lax.rsqrt(var + EPS) * lng_ref[...] + lnb_ref[...]
            kln_ref[pl.ds(c * rows, rows), :] = y.astype(BF16)
            return carry

        lax.fori_loop(0, s_len // rows, ln_body, 0)

    w_rows = wi_ref[0].T * (IDX_HEADS ** -0.5 * IDX_DIM ** -0.5)
    for h in range(IDX_HEADS):
        qi4_ref[h * QB:(h + 1) * QB, :] = qi_ref[0, :, h * LANES:(h + 1) * LANES].astype(BF16)
    eye = (lax.broadcasted_iota(I32, (QB, QB), 0) == lax.broadcasted_iota(I32, (QB, QB), 1)).astype(BF16)
    for n in range(A_KV_HEADS):
        for g in range(A_GROUP):
            hd = n * A_GROUP + g
            q4_ref[n, g * QB:(g + 1) * QB, 0:A_HEAD_DIM] = q_ref[0, :, hd * A_HEAD_DIM:(hd + 1) * A_HEAD_DIM]
            q4_ref[n, g * QB:(g + 1) * QB, A_HEAD_DIM:A_HEAD_DIM + QB] = eye

    last = i // SUB
    nsc = last + 1

    def rows(j):
        return pl.ds(pl.multiple_of(j * SC, SC), SC)

    def index_dots(j):
        return _dot_nt(kln_ref[rows(j), :], qi4_ref[...])

    def score_chunk(j, dots, causal):
        sc = None
        for h in range(IDX_HEADS):
            t = w_rows[h:h + 1, :] * jnp.maximum(dots[:, h * QB:(h + 1) * QB], 0.0)
            sc = t if sc is None else sc + t
        bits = pltpu.bitcast(sc, I32)
        key = jnp.where(bits < 0, bits ^ jnp.int32(0x7FFFFFFF), bits)
        key = jnp.where(sc == 0.0, 0, key)
        if causal:
            key_pos = j * SC + lax.broadcasted_iota(I32, (SC, LANES), 0)
            q_pos = i * QB + lax.broadcasted_iota(I32, (SC, LANES), 1)
            key = jnp.where(key_pos <= q_pos, key, INT_MIN)
        keys_ref[rows(j), :] = key

    def score_body(pr, carry):
        d0 = index_dots(2 * pr)
        d1 = index_dots(2 * pr + 1)
        score_chunk(2 * pr, d0, False)
        score_chunk(2 * pr + 1, d1, False)
        return carry

    lax.fori_loop(0, last // 2, score_body, 0)

    @pl.when(last % 2 == 1)
    def _():
        score_chunk(last - 1, index_dots(last - 1), False)

    score_chunk(last, index_dots(last), True)

    t_pos = i * QB + lax.broadcasted_iota(I32, (1, LANES), 1)
    kk = jnp.minimum(t_pos + 1, topk).astype(F32)

    def count_ge(cand):
        def body(j, cnt):
            for r in range(SUB):
                blk = keys_ref[pl.ds(pl.multiple_of(j * SC + r * QB, QB), QB), :]
                cnt = cnt + jnp.where(blk >= cand, 1.0, 0.0)
            return cnt

        cnt = lax.fori_loop(0, nsc, body, jnp.zeros((QB, LANES), F32))
        return jnp.sum(cnt, axis=0, keepdims=True)

    def bit_body(bi, t):
        cand = t + lax.shift_left(jnp.int32(1), 31 - bi)
        return jnp.where(count_ge(cand) >= kk, cand, t)

    thr = lax.fori_loop(0, 32, bit_body, jnp.full((1, LANES), INT_MIN, I32))
    need = kk - count_ge(thr + 1)
    sub_k = lax.broadcasted_iota(I32, (QB, QB), 0)
    sub_k2 = lax.broadcasted_iota(I32, (QB, QB), 1)
    tril = (sub_k2 <= sub_k).astype(BF16)

    def mask_body(j, seen):
        keys, eqs, pres = [], [], []
        for r in range(SUB):
            key = keys_ref[pl.ds(pl.multiple_of(j * SC + r * QB, QB), QB), :]
            eq = key == thr
            keys.append(key)
            eqs.append(eq)
            pres.append(_dot(tril, jnp.where(eq, 1.0, 0.0).astype(BF16)))
        for r in range(SUB):
            sel = (keys[r] > thr) | (eqs[r] & ((seen + pres[r]) <= need))
            madd_ref[pl.ds(pl.multiple_of(j * SC + r * QB, QB), QB), :] = jnp.where(sel, 0.0, NEG).astype(BF16)
            seen = seen + pres[r][QB - 1:QB, :]
        return seen

    lax.fori_loop(0, nsc, mask_body, jnp.zeros((1, LANES), F32))

    def split_near(slot, blk):
        blk_rows = pl.ds(pl.multiple_of(blk * QB, QB), QB)
        mnear_ref[slot] = madd_ref[blk_rows, :]
        madd_ref[blk_rows, :] = jnp.full((QB, LANES), NEG, BF16)

    split_near(1, i)

    @pl.when(i >= 1)
    def _():
        split_near(0, i - 1)

    m_ref[...] = jnp.full(m_ref.shape, NEG, F32)
    l_ref[...] = jnp.zeros(l_ref.shape, F32)
    acc_ref[...] = jnp.zeros(acc_ref.shape, F32)

    def logits(kc, madd, n):
        return _dot_nt(jnp.concatenate([kc, madd], axis=1), q4_ref[n])

    def far_logits(j, n):
        return logits(k_ref[0, rows(j), n * A_HEAD_DIM:(n + 1) * A_HEAD_DIM], madd_ref[rows(j), :], n)

    def softmax_pv(n, s, vt):
        m_prev = m_ref[n]
        m_new = jnp.maximum(m_prev, jnp.max(s, axis=0, keepdims=True))
        alpha = jnp.exp2(m_prev - m_new)
        p = jnp.exp2(s - m_new)
        l_ref[n] = alpha * l_ref[n] + jnp.sum(p, axis=0, keepdims=True)
        acc_ref[n] = alpha * acc_ref[n] + _dot(vt, p.astype(BF16))
        m_ref[n] = m_new

    def far_vt(j, n):
        return vt_ref[0, j, n * A_HEAD_DIM:(n + 1) * A_HEAD_DIM, :]

    def sweep(first, count):
        tasks = [(first + c, n) for c in range(count) for n in range(A_KV_HEADS)]
        ahead = A_KV_HEADS
        pending = [far_logits(j, n) for j, n in tasks[:ahead]]
        for t, (j, n) in enumerate(tasks):
            softmax_pv(n, pending.pop(0), far_vt(j, n))
            if t + ahead < len(tasks):
                pending.append(far_logits(*tasks[t + ahead]))

    def sweep_body(g, carry):
        sweep(far_unroll * g, far_unroll)
        return carry

    n_far = jnp.where(i % SUB >= 2, nsc, last)
    lax.fori_loop(0, n_far // far_unroll, sweep_body, 0)
    for rem in range(1, far_unroll):
        @pl.when(n_far % far_unroll == rem)
        def _(rem=rem):
            sweep(n_far - rem, rem)

    def near_step(slot, blk, vts_ref):
        blk_rows = pl.ds(pl.multiple_of(blk * QB, QB), QB)
        ss = [logits(k_ref[0, blk_rows, n * A_HEAD_DIM:(n + 1) * A_HEAD_DIM], mnear_ref[slot], n)
              + bias_ref[slot, n] for n in range(A_KV_HEADS)]
        for n in range(A_KV_HEADS):
            softmax_pv(n, ss[n], vts_ref[0, 0, n * A_HEAD_DIM:(n + 1) * A_HEAD_DIM, :])

    @pl.when(i >= 1)
    def _():
        near_step(0, i - 1, vtp_ref)

    near_step(1, i, vtd_ref)

    for n in range(A_KV_HEADS):
        out_t = acc_ref[n] / l_ref[n]
        for g in range(A_GROUP):
            hd = n * A_GROUP + g
            o_ref[0, :, hd * A_HEAD_DIM:(hd + 1) * A_HEAD_DIM] = out_t[:, g * QB:(g + 1) * QB].T.astype(o_ref.dtype)


def _dsa(p16, p32, vt, vt_blk, ln_g, ln_b, bias_near, *, topk, far_unroll):
    b, s, _ = p16.shape
    aq = A_HEADS * A_HEAD_DIM
    akv = A_KV_HEADS * A_HEAD_DIM
    qiw = IDX_HEADS * LANES
    nch = s // QB
    gq = A_GROUP * QB
    k_blk = (aq + X_HEADS * X_HEAD_DIM) // akv
    qi_blk = 2 * LRU_WIDTH // qiw
    ki_blk = (2 * LRU_WIDTH + qiw) // LANES
    return pl.pallas_call(
        functools.partial(_dsa_kernel, topk=topk, far_unroll=far_unroll),
        out_shape=jax.ShapeDtypeStruct((b, s, aq), BF16),
        grid=(b, nch),
        in_specs=[
            pl.BlockSpec((1, QB, aq), lambda bi, i: (bi, i, 0)),
            pl.BlockSpec((1, s, akv), lambda bi, i: (bi, 0, k_blk)),
            pl.BlockSpec((1, s // SC, akv, SC), lambda bi, i: (bi, 0, 0, 0)),
            pl.BlockSpec((1, 1, akv, QB), lambda bi, i: (bi, jnp.maximum(i - 1, 0), 0, 0)),
            pl.BlockSpec((1, 1, akv, QB), lambda bi, i: (bi, i, 0, 0)),
            pl.BlockSpec((1, QB, qiw), lambda bi, i: (bi, i, qi_blk)),
            pl.BlockSpec((1, s, LANES), lambda bi, i: (bi, 0, ki_blk)),
            pl.BlockSpec((1, QB, LANES), lambda bi, i: (bi, i, ki_blk + 1)),
            pl.BlockSpec((1, LANES), lambda bi, i: (0, 0)),
            pl.BlockSpec((1, LANES), lambda bi, i: (0, 0)),
            pl.BlockSpec((2, A_KV_HEADS, QB, gq), lambda bi, i: (0, 0, 0, 0)),
        ],
        out_specs=pl.BlockSpec((1, QB, aq), lambda bi, i: (bi, i, 0)),
        scratch_shapes=[
            pltpu.VMEM((s, LANES), BF16),
            pltpu.VMEM((s, LANES), I32),
            pltpu.VMEM((s, LANES), BF16),
            pltpu.VMEM((2, QB, LANES), BF16),
            pltpu.VMEM((IDX_HEADS * QB, LANES), BF16),
            pltpu.VMEM((A_KV_HEADS, gq, A_HEAD_DIM + QB), BF16),
            pltpu.VMEM((A_KV_HEADS, 1, gq), F32),
            pltpu.VMEM((A_KV_HEADS, 1, gq), F32),
            pltpu.VMEM((A_KV_HEADS, A_HEAD_DIM, gq), F32),
        ],
        compiler_params=_params("arbitrary", "arbitrary"),
        name="dsa",
    )(p16, p16, vt, vt_blk, vt_blk, p32, p32, p32, ln_g, ln_b, bias_near)


def _t5_bucket(dist):
    max_exact = N_BUCKETS // 2
    d = jnp.maximum(dist, 0)
    df = jnp.maximum(d, 1).astype(F32)
    large = max_exact + (jnp.log(df / max_exact) / math.log(MAX_DISTANCE / max_exact)
                         * (N_BUCKETS - max_exact)).astype(I32)
    large = jnp.minimum(large, N_BUCKETS - 1)
    return jnp.where(d < max_exact, d, large)


def _near_bias(rel_bias):
    t = jnp.arange(QB, dtype=I32)[:, None]
    s = jnp.arange(QB, dtype=I32)[None, :]
    tiles = []
    for off in (QB, 0):
        bucket = _t5_bucket(t - s + off)
        onehot = (bucket[..., None] == jnp.arange(N_BUCKETS, dtype=I32)).astype(F32)
        table = (rel_bias - rel_bias[N_BUCKETS - 1]) * LOG2E
        tile = jnp.einsum('tsb,bh->tsh', onehot, table, precision=lax.Precision.HIGHEST)
        tile = tile.transpose(1, 2, 0).reshape(QB, A_KV_HEADS, A_GROUP * QB)
        tiles.append(tile.transpose(1, 0, 2))
    return jnp.stack(tiles).astype(F32)


def _pad_cols(w, width):
    return jnp.pad(w, [(0, 0)] * (w.ndim - 1) + [(0, width - w.shape[-1])])


def _split_points(d_model):
    aq = A_HEADS * A_HEAD_DIM
    akv = A_KV_HEADS * A_HEAD_DIM
    sizes = (aq, akv, akv, IDX_HEADS * IDX_DIM, IDX_DIM, IDX_HEADS,
             LRU_WIDTH, LRU_WIDTH, X_HEADS * X_HEAD_DIM, N_BRANCH * d_model)
    pts, acc = [0], 0
    for n in sizes:
        acc += n
        pts.append(acc)
    return pts


def _regroup_kernel(w_ref, o16_ref, o32_ref, og_ref, *, pts):
    (q0, k0, v0, qi0, ki0, wi0, x0, g0, qm0, gt0, end) = pts

    def seg(lo, hi, scale=None):
        v = w_ref[0, :, lo:hi]
        if scale is not None:
            v = v * scale
        return v.astype(BF16)

    o16_ref[0, :, 0:k0 - q0] = seg(q0, k0, A_HEAD_DIM ** -0.5 * LOG2E)
    c = k0 - q0
    o16_ref[0, :, c:c + gt0 - qm0] = seg(qm0, gt0, X_HEAD_DIM ** -0.5)
    c += gt0 - qm0
    o16_ref[0, :, c:c + qi0 - k0] = seg(k0, qi0)
    o32_ref[0, :, 0:qm0 - x0] = seg(x0, qm0)
    c = qm0 - x0
    o32_ref[0, :, c:] = jnp.zeros((o32_ref.shape[1], o32_ref.shape[2] - c), BF16)
    for h in range(IDX_HEADS):
        o32_ref[0, :, c + h * LANES:c + h * LANES + IDX_DIM] = seg(qi0 + h * IDX_DIM, qi0 + (h + 1) * IDX_DIM)
    c += IDX_HEADS * LANES
    o32_ref[0, :, c:c + IDX_DIM] = seg(ki0, wi0)
    c += LANES
    o32_ref[0, :, c:c + IDX_HEADS] = seg(wi0, x0)
    og_ref[0] = seg(gt0, end)


def _prep_w_in(w_in, d_model, *, tk):
    depth, k, n = w_in.shape
    pts = _split_points(d_model)
    n16 = pts[3] + pts[9] - pts[8]
    n32 = 2 * LRU_WIDTH + (IDX_HEADS + 2) * LANES
    ng = n - pts[9]
    return pl.pallas_call(
        functools.partial(_regroup_kernel, pts=tuple(pts)),
        out_shape=(jax.ShapeDtypeStruct((depth, k, n16), BF16), jax.ShapeDtypeStruct((depth, k, n32), BF16),
                   jax.ShapeDtypeStruct((depth, k, ng), BF16)),
        grid=(depth, k // tk),
        in_specs=[pl.BlockSpec((1, tk, n), lambda l, i: (l, i, 0))],
        out_specs=(pl.BlockSpec((1, tk, n16), lambda l, i: (l, i, 0)),
                   pl.BlockSpec((1, tk, n32), lambda l, i: (l, i, 0)),
                   pl.BlockSpec((1, tk, ng), lambda l, i: (l, i, 0))),
        compiler_params=_params("parallel", "parallel"),
        name="regroup",
    )(w_in)


def _pad_ff(w_gu, w_down, tf):
    d_ff = w_down.shape[-2]
    fp = -(-d_ff // tf) * tf
    wg = _pad_cols(w_gu[..., :d_ff], fp).astype(BF16)
    wu = _pad_cols(w_gu[..., d_ff:], fp).astype(BF16)
    wd = jnp.pad(w_down, ((0, 0), (0, fp - d_ff), (0, 0))).astype(BF16)
    return wg, wu, wd


def _tiles(m, s, d_model):
    return dict(
        ffn_tm=min(512, m), ffn_tf=512,
        proj_tm=min(256, m), proj_tn=256,
        merge_tm=min(512, m), merge_tn=min(512, d_model),
        lru_tc=min(256, s), mem_tm=min(512, s),
    )


def kernel(x, mem, rel_bias, final_norm, norm_ff1, w_ff1_gu, w_ff1_down, norm_mix, w_in, conv_w, conv_b, w_a, b_a, w_i, b_i, lam, idx_ln_g, idx_ln_b, mem_norm, w_mem_kv, w_branch, w_out, norm_ff2, w_ff2_gu, w_ff2_down):
    b, s, d = x.shape
    nm = mem.shape[1]
    depth = w_in.shape[0]
    m = b * s
    assert s % SC == 0 and d % LANES == 0
    tl = _tiles(m, s, d)
    topk = min(TOPK_MAX, s // 4)
    bias_near = _near_bias(rel_bias)
    row = lambda v: v.reshape(1, -1)

    ff1 = _pad_ff(w_ff1_gu, w_ff1_down, tl["ffn_tf"])
    ff2 = _pad_ff(w_ff2_gu, w_ff2_down, tl["ffn_tf"])
    w16, w32, wgates = _prep_w_in(w_in, d, tk=min(256, d))
    w_a16, w_i16 = w_a.astype(BF16), w_i.astype(BF16)
    w_mem16, w_branch16, w_out16 = w_mem_kv.astype(BF16), w_branch.astype(BF16), w_out.astype(BF16)
    ln_g = _pad_cols(idx_ln_g, LANES)
    ln_b = _pad_cols(idx_ln_b, LANES)

    x2 = x.reshape(m, d)
    mem2 = mem.reshape(b * nm, d)
    for l in range(depth):
        x2 = _ffn(x2, row(norm_ff1[l]), ff1[0][l], ff1[1][l], ff1[2][l], row(final_norm), final_norm=False,
                  tm=tl["ffn_tm"] * (2 if l == 0 else 1), tf=tl["ffn_tf"])

        p16, p32 = _proj(x2, row(norm_mix[l]), w16[l], w32[l], tm=tl["proj_tm"])
        p16 = p16.reshape(b, s, -1)
        p32 = p32.reshape(b, s, -1)

        v_cols = slice(p16.shape[-1] - A_KV_HEADS * A_HEAD_DIM, p16.shape[-1])
        vt = p16[..., v_cols].reshape(b, s // SC, SC, -1).transpose(0, 1, 3, 2)
        vt_blk = p16[..., v_cols].reshape(b, s // QB, QB, -1).transpose(0, 1, 3, 2)
        y_a = _dsa(p16, p32, vt, vt_blk, row(ln_g[l]), row(ln_b[l]),
                   bias_near, topk=topk, far_unroll=4)
        y_b = _lru(p32, conv_w[l], row(conv_b[l]), w_a16[l], w_i16[l],
                   row(b_a[l]), row(b_i[l]), row(lam[l]), tc=tl["lru_tc"])
        mkv = _norm_matmul(mem2, row(mem_norm[l]), w_mem16[l], BF16,
                           tm=min(256, b * nm), tn=tl["proj_tn"], name="memkv")
        y_c = _memattn(p16, mkv.reshape(b, nm, -1), tm=tl["mem_tm"])

        x2 = _merge(x2, row(norm_mix[l]), y_a.reshape(m, -1), y_b.reshape(m, -1), y_c.reshape(m, -1),
                    wgates[l], w_branch16[l], w_out16[l],
                    tm=tl["merge_tm"], tn=tl["merge_tn"])

        x2 = _ffn(x2, row(norm_ff2[l]), ff2[0][l], ff2[1][l], ff2[2][l], row(final_norm),
                  final_norm=(l == depth - 1), tm=tl["ffn_tm"], tf=tl["ffn_tf"])
    return x2.reshape(b, s, d)
```

```python
import functools
import math

import jax
import jax.numpy as jnp
from jax import lax
from jax.experimental import pallas as pl
from jax.experimental.pallas import tpu as pltpu

F32 = jnp.float32
BF16 = jnp.bfloat16
I32 = jnp.int32

EPS = 1e-6
A_HEADS = 8
A_KV_HEADS = 2
A_HEAD_DIM = 128
A_GROUP = A_HEADS // A_KV_HEADS
IDX_HEADS = 4
IDX_DIM = 64
TOPK_MAX = 256
LRU_WIDTH = 1024
LRU_BLOCKS = 8
LRU_BLOCK_W = LRU_WIDTH // LRU_BLOCKS
CONV_WIDTH = 4
LRU_C = 8.0
X_HEADS = 4
X_HEAD_DIM = 256
N_BRANCH = 3
BRANCH_WIDTH = 1024
N_BUCKETS = 32
MAX_DISTANCE = 128

LANES = 128
SUBLANES = 8
VMEM_LIMIT = 56 * 1024 * 1024

QB = 128
SC = 512
SUB = SC // QB
NEG = -1e30
LOG2E = math.log2(math.e)
INT_MIN = -2 ** 31


def _params(*sem):
    return pltpu.CompilerParams(dimension_semantics=sem, vmem_limit_bytes=VMEM_LIMIT)


def _rms(x, g):
    ms = jnp.mean(x * x, axis=-1, keepdims=True)
    return x * lax.rsqrt(ms + EPS) * g


def _sigmoid(x):
    return 1.0 / (1.0 + jnp.exp(-x))


def _dot(a, b):
    return jnp.dot(a, b, preferred_element_type=F32)


def _dot_nt(a, b):
    return lax.dot_general(a, b, (((1,), (1,)), ((), ())), preferred_element_type=F32)


def _ffn_kernel(x_ref, g_ref, wg_ref, wu_ref, wd_ref, fg_ref, o_ref, h_ref, *, final_norm):
    f = pl.program_id(1)

    @pl.when(f == 0)
    def _():
        h_ref[...] = _rms(x_ref[...], g_ref[...]).astype(BF16)
        o_ref[...] = jnp.zeros_like(o_ref)

    h = h_ref[...]
    g = _dot(h, wg_ref[...])
    u = _dot(h, wu_ref[...])
    a = (g * _sigmoid(g) * u).astype(BF16)
    o_ref[...] += _dot(a, wd_ref[...])

    @pl.when(f == pl.num_programs(1) - 1)
    def _():
        y = x_ref[...] + 0.5 * o_ref[...]
        if final_norm:
            y = _rms(y, fg_ref[...])
        o_ref[...] = y


def _ffn(x2, g, wg, wu, wd, fg, layer, *, final_norm, tm, tf):
    m, d = x2.shape
    fp = wg.shape[-1]
    return pl.pallas_call(
        functools.partial(_ffn_kernel, final_norm=final_norm),
        out_shape=jax.ShapeDtypeStruct((m, d), F32),
        grid=(m // tm, fp // tf),
        in_specs=[
            pl.BlockSpec((tm, d), lambda i, f: (i, 0)),
            pl.BlockSpec((1, d), lambda i, f: (0, 0)),
            pl.BlockSpec((None, d, tf), lambda i, f: (layer, 0, f)),
            pl.BlockSpec((None, d, tf), lambda i, f: (layer, 0, f)),
            pl.BlockSpec((None, tf, d), lambda i, f: (layer, f, 0)),
            pl.BlockSpec((1, d), lambda i, f: (0, 0)),
        ],
        out_specs=pl.BlockSpec((tm, d), lambda i, f: (i, 0)),
        scratch_shapes=[pltpu.VMEM((tm, d), BF16)],
        compiler_params=_params("parallel", "arbitrary"),
        name="ffn",
    )(x2, g, wg, wu, wd, fg)


def _norm_matmul_kernel(x_ref, g_ref, w_ref, o_ref, h_ref):
    @pl.when(pl.program_id(1) == 0)
    def _():
        h_ref[...] = _rms(x_ref[...], g_ref[...]).astype(BF16)

    o_ref[...] = _dot(h_ref[...], w_ref[...]).astype(o_ref.dtype)


def _norm_matmul(x2, g, w, layer, out_dtype, *, tm, tn, name):
    m, d = x2.shape
    n = w.shape[-1]
    return pl.pallas_call(
        _norm_matmul_kernel,
        out_shape=jax.ShapeDtypeStruct((m, n), out_dtype),
        grid=(m // tm, n // tn),
        in_specs=[
            pl.BlockSpec((tm, d), lambda i, j: (i, 0)),
            pl.BlockSpec((1, d), lambda i, j: (0, 0)),
            pl.BlockSpec((None, d, tn), lambda i, j: (layer, 0, j)),
        ],
        out_specs=pl.BlockSpec((tm, tn), lambda i, j: (i, j)),
        scratch_shapes=[pltpu.VMEM((tm, d), BF16)],
        compiler_params=_params("parallel", "arbitrary"),
        name=name,
    )(x2, g, w)


def _proj_kernel(x_ref, g_ref, w16_ref, w32_ref, o16_ref, o32_ref):
    h = _rms(x_ref[...], g_ref[...]).astype(BF16)
    o16_ref[...] = _dot(h, w16_ref[...]).astype(o16_ref.dtype)
    o32_ref[...] = _dot(h, w32_ref[...])


def _proj(x2, g, w16, w32, layer, *, tm):
    m, d = x2.shape
    n16, n32 = w16.shape[-1], w32.shape[-1]
    resident = pl.Buffered(1)
    return pl.pallas_call(
        _proj_kernel,
        out_shape=(jax.ShapeDtypeStruct((m, n16), BF16), jax.ShapeDtypeStruct((m, n32), F32)),
        grid=(m // tm,),
        in_specs=[
            pl.BlockSpec((tm, d), lambda i: (i, 0)),
            pl.BlockSpec((1, d), lambda i: (0, 0)),
            pl.BlockSpec((None, d, n16), lambda i: (layer, 0, 0), pipeline_mode=resident),
            pl.BlockSpec((None, d, n32), lambda i: (layer, 0, 0), pipeline_mode=resident),
        ],
        out_specs=(pl.BlockSpec((tm, n16), lambda i: (i, 0)), pl.BlockSpec((tm, n32), lambda i: (i, 0))),
        compiler_params=_params("parallel"),
        name="proj",
    )(x2, g, w16, w32)


def _lru_kernel(x_ref, gate_ref, cw_ref, cb_ref, wa_ref, wi_ref, ba_ref, bi_ref, lam_ref,
                o_ref, xbuf_ref, hc_ref, *, tc):
    halo = SUBLANES

    @pl.when(pl.program_id(1) == 0)
    def _():
        xbuf_ref[0:halo, :] = jnp.zeros((halo, LRU_WIDTH), F32)
        hc_ref[...] = jnp.zeros_like(hc_ref)

    xbuf_ref[halo:halo + tc, :] = x_ref[0]
    base = halo - (CONV_WIDTH - 1)
    xc = xbuf_ref[base:base + tc, :] * cw_ref[0:1, :]
    for j in range(1, CONV_WIDTH):
        xc = xc + xbuf_ref[base + j:base + j + tc, :] * cw_ref[j:j + 1, :]
    xc = cb_ref[...] + xc
    xbuf_ref[0:halo, :] = xbuf_ref[tc:tc + halo, :]

    xcb = xc.astype(BF16)
    r_parts, i_parts = [], []
    for n in range(LRU_BLOCKS):
        blk = xcb[:, n * LRU_BLOCK_W:(n + 1) * LRU_BLOCK_W]
        r_parts.append(_dot(blk, wa_ref[n]))
        i_parts.append(_dot(blk, wi_ref[n]))
    r = _sigmoid(jnp.concatenate(r_parts, axis=1) + ba_ref[...])
    gi = _sigmoid(jnp.concatenate(i_parts, axis=1) + bi_ref[...])

    softplus_neg_lam = jnp.log1p(jnp.exp(-lam_ref[...]))
    log_a = (-LRU_C) * r * softplus_neg_lam
    a = jnp.exp(log_a)
    th = jnp.tanh(log_a)
    mult = jnp.sqrt(jnp.maximum(-2.0 * th / (1.0 - th), 0.0))
    u = mult * (gi * xc)

    row = lax.broadcasted_iota(I32, (tc, LRU_WIDTH), 0)
    d = 1
    while d < tc:
        keep = row >= d
        a_sh = jnp.where(keep, pltpu.roll(a, d, 0), 1.0)
        u_sh = jnp.where(keep, pltpu.roll(u, d, 0), 0.0)
        u = a * u_sh + u
        a = a * a_sh
        d *= 2
    h = a * hc_ref[...] + u
    hc_ref[...] = h[tc - 1:tc, :]

    gate = gate_ref[0]
    gelu = 0.5 * gate * (1.0 + jnp.tanh(math.sqrt(2.0 / math.pi) * (gate + 0.044715 * (gate * gate * gate))))
    o_ref[0] = (h * gelu).astype(o_ref.dtype)


def _lru(p32, conv_w, conv_b, w_a, w_i, b_a, b_i, lam, layer, *, tc):
    b, s, _ = p32.shape
    vec = pl.BlockSpec((1, LRU_WIDTH), lambda bi, c: (0, 0))
    wblk = pl.BlockSpec((None, LRU_BLOCKS, LRU_BLOCK_W, LRU_BLOCK_W), lambda bi, c: (layer, 0, 0, 0))
    return pl.pallas_call(
        functools.partial(_lru_kernel, tc=tc),
        out_shape=jax.ShapeDtypeStruct((b, s, LRU_WIDTH), BF16),
        grid=(b, s // tc),
        in_specs=[
            pl.BlockSpec((1, tc, LRU_WIDTH), lambda bi, c: (bi, c, 0)),
            pl.BlockSpec((1, tc, LRU_WIDTH), lambda bi, c: (bi, c, 1)),
            pl.BlockSpec((CONV_WIDTH, LRU_WIDTH), lambda bi, c: (0, 0)),
            vec, wblk, wblk, vec, vec, vec,
        ],
        out_specs=pl.BlockSpec((1, tc, LRU_WIDTH), lambda bi, c: (bi, c, 0)),
        scratch_shapes=[pltpu.VMEM((tc + SUBLANES, LRU_WIDTH), F32), pltpu.VMEM((1, LRU_WIDTH), F32)],
        compiler_params=_params("arbitrary", "arbitrary"),
        name="rglru",
    )(p32, p32, conv_w, conv_b, w_a, w_i, b_a, b_i, lam)


def _memattn_kernel(q_ref, kv_ref, o_ref):
    kw = X_HEADS * X_HEAD_DIM
    for h in range(X_HEADS):
        sl = slice(h * X_HEAD_DIM, (h + 1) * X_HEAD_DIM)
        q = q_ref[0, :, sl]
        k = kv_ref[0, :, sl]
        v = kv_ref[0, :, kw + h * X_HEAD_DIM:kw + (h + 1) * X_HEAD_DIM]
        s = _dot_nt(q, k)
        m = jnp.max(s, axis=-1, keepdims=True)
        p = jnp.exp(s - m)
        l = jnp.sum(p, axis=-1, keepdims=True)
        o = _dot(p.astype(BF16), v)
        o_ref[0, :, sl] = (o / l).astype(o_ref.dtype)


def _memattn(p16, mkv, *, tm):
    b, s, _ = p16.shape
    nm = mkv.shape[1]
    kw = X_HEADS * X_HEAD_DIM
    return pl.pallas_call(
        _memattn_kernel,
        out_shape=jax.ShapeDtypeStruct((b, s, kw), BF16),
        grid=(b, s // tm),
        in_specs=[
            pl.BlockSpec((1, tm, kw), lambda bi, i: (bi, i, 1)),
            pl.BlockSpec((1, nm, 2 * kw), lambda bi, i: (bi, 0, 0)),
        ],
        out_specs=pl.BlockSpec((1, tm, kw), lambda bi, i: (bi, i, 0)),
        compiler_params=_params("parallel", "parallel"),
        name="memattn",
    )(p16, mkv)


def _merge_kernel(x_ref, g_ref, ya_ref, yb_ref, yc_ref, wg0_ref, wg1_ref, wg2_ref, wb_ref, wo_ref,
                  o_ref, h_ref):
    n = pl.program_id(1)

    @pl.when(n == 0)
    def _():
        h_ref[...] = _rms(x_ref[...], g_ref[...]).astype(BF16)
        o_ref[...] = jnp.zeros_like(o_ref)

    h = h_ref[...]
    merged = None
    for j, (y_ref, wg_ref) in enumerate(((ya_ref, wg0_ref), (yb_ref, wg1_ref), (yc_ref, wg2_ref))):
        term = _sigmoid(_dot(h, wg_ref[...])) * _dot(y_ref[...], wb_ref[j])
        merged = term if merged is None else merged + term
    o_ref[...] += _dot(merged.astype(BF16), wo_ref[...])

    @pl.when(n == pl.num_programs(1) - 1)
    def _():
        o_ref[...] = x_ref[...] + o_ref[...]


def _merge(x2, g, ya, yb, yc, wgates, wbranch, wout, layer, *, tm, tn):
    m, d = x2.shape
    nblk = d // tn
    yspec = pl.BlockSpec((tm, BRANCH_WIDTH), lambda i, n: (i, 0))
    return pl.pallas_call(
        _merge_kernel,
        out_shape=jax.ShapeDtypeStruct((m, d), F32),
        grid=(m // tm, nblk),
        in_specs=[
            pl.BlockSpec((tm, d), lambda i, n: (i, 0)),
            pl.BlockSpec((1, d), lambda i, n: (0, 0)),
            yspec, yspec, yspec,
            pl.BlockSpec((None, d, tn), lambda i, n: (layer, 0, n)),
            pl.BlockSpec((None, d, tn), lambda i, n: (layer, 0, nblk + n)),
            pl.BlockSpec((None, d, tn), lambda i, n: (layer, 0, 2 * nblk + n)),
            pl.BlockSpec((None, N_BRANCH, BRANCH_WIDTH, tn), lambda i, n: (layer, 0, 0, n)),
            pl.BlockSpec((None, tn, d), lambda i, n: (layer, n, 0)),
        ],
        out_specs=pl.BlockSpec((tm, d), lambda i, n: (i, 0)),
        scratch_shapes=[pltpu.VMEM((tm, d), BF16)],
        compiler_params=_params("parallel", "arbitrary"),
        name="merge",
    )(x2, g, ya, yb, yc, wgates, wgates, wgates, wbranch, wout)


def _dsa_kernel(q_ref, k_ref, vt_ref, vtp_ref, vtd_ref, qi_ref, ki_ref, wi_ref, lng_ref, lnb_ref, bias_ref,
                o_ref,
                kln_ref, keys_ref, madd_ref, mnear_ref, qi4_ref, q4_ref, m_ref, l_ref, acc_ref,
                *, topk, far_unroll):
    i = pl.program_id(1)
    s_len = k_ref.shape[1]

    @pl.when(i == 0)
    def _():
        rows = 512 if s_len % 512 == 0 else QB
        lane = lax.broadcasted_iota(I32, (rows, LANES), 1)
        real = lane < IDX_DIM

        def ln_body(c, carry):
            x = ki_ref[0, pl.ds(c * rows, rows), :]
            mu = jnp.sum(x, axis=-1, keepdims=True) * (1.0 / IDX_DIM)
            xm = jnp.where(real, x - mu, 0.0)
            var = jnp.sum(xm * xm, axis=-1, keepdims=True) * (1.0 / IDX_DIM)
            y = xm * lax.rsqrt(var + EPS) * lng_ref[...] + lnb_ref[...]
            kln_ref[pl.ds(c * rows, rows), :] = y.astype(BF16)
            return carry

        lax.fori_loop(0, s_len // rows, ln_body, 0)

    w_rows = wi_ref[0].T * (IDX_HEADS ** -0.5 * IDX_DIM ** -0.5)
    for h in range(IDX_HEADS):
        qi4_ref[h * QB:(h + 1) * QB, :] = qi_ref[0, :, h * LANES:(h + 1) * LANES].astype(BF16)
    eye = (lax.broadcasted_iota(I32, (QB, QB), 0) == lax.broadcasted_iota(I32, (QB, QB), 1)).astype(BF16)
    for n in range(A_KV_HEADS):
        for g in range(A_GROUP):
            hd = n * A_GROUP + g
            q4_ref[n, g * QB:(g + 1) * QB, 0:A_HEAD_DIM] = q_ref[0, :, hd * A_HEAD_DIM:(hd + 1) * A_HEAD_DIM]
            q4_ref[n, g * QB:(g + 1) * QB, A_HEAD_DIM:A_HEAD_DIM + QB] = eye

    last = i // SUB
    nsc = last + 1

    def rows(j):
        return pl.ds(pl.multiple_of(j * SC, SC), SC)

    def index_dots(j):
        return _dot_nt(kln_ref[rows(j), :], qi4_ref[...])

    def score_chunk(j, dots, causal):
        sc = None
        for h in range(IDX_HEADS):
            t = w_rows[h:h + 1, :] * jnp.maximum(dots[:, h * QB:(h + 1) * QB], 0.0)
            sc = t if sc is None else sc + t
        bits = pltpu.bitcast(sc, I32)
        key = jnp.where(bits < 0, bits ^ jnp.int32(0x7FFFFFFF), bits)
        key = jnp.where(sc == 0.0, 0, key)
        if causal:
            key_pos = j * SC + lax.broadcasted_iota(I32, (SC, LANES), 0)
            q_pos = i * QB + lax.broadcasted_iota(I32, (SC, LANES), 1)
            key = jnp.where(key_pos <= q_pos, key, INT_MIN)
        keys_ref[rows(j), :] = key

    def score_body(pr, carry):
        d0 = index_dots(2 * pr)
        d1 = index_dots(2 * pr + 1)
        score_chunk(2 * pr, d0, False)
        score_chunk(2 * pr + 1, d1, False)
        return carry

    lax.fori_loop(0, last // 2, score_body, 0)

    @pl.when(last % 2 == 1)
    def _():
        score_chunk(last - 1, index_dots(last - 1), False)

    score_chunk(last, index_dots(last), True)

    t_pos = i * QB + lax.broadcasted_iota(I32, (1, LANES), 1)
    kk = jnp.minimum(t_pos + 1, topk).astype(F32)

    def count_ge(cand):
        def body(j, cnt):
            for r in range(SUB):
                blk = keys_ref[pl.ds(pl.multiple_of(j * SC + r * QB, QB), QB), :]
                cnt = cnt + jnp.where(blk >= cand, 1.0, 0.0)
            return cnt

        cnt = lax.fori_loop(0, nsc, body, jnp.zeros((QB, LANES), F32))
        return jnp.sum(cnt, axis=0, keepdims=True)

    def bit_body(bi, t):
        cand = t + lax.shift_left(jnp.int32(1), 31 - bi)
        return jnp.where(count_ge(cand) >= kk, cand, t)

    thr = lax.fori_loop(0, 32, bit_body, jnp.full((1, LANES), INT_MIN, I32))
    need = kk - count_ge(thr + 1)
    sub_k = lax.broadcasted_iota(I32, (QB, QB), 0)
    sub_k2 = lax.broadcasted_iota(I32, (QB, QB), 1)
    tril = (sub_k2 <= sub_k).astype(BF16)

    def mask_body(j, seen):
        keys, eqs, pres = [], [], []
        for r in range(SUB):
            key = keys_ref[pl.ds(pl.multiple_of(j * SC + r * QB, QB), QB), :]
            eq = key == thr
            keys.append(key)
            eqs.append(eq)
            pres.append(_dot(tril, jnp.where(eq, 1.0, 0.0).astype(BF16)))
        for r in range(SUB):
            sel = (keys[r] > thr) | (eqs[r] & ((seen + pres[r]) <= need))
            madd_ref[pl.ds(pl.multiple_of(j * SC + r * QB, QB), QB), :] = jnp.where(sel, 0.0, NEG).astype(BF16)
            seen = seen + pres[r][QB - 1:QB, :]
        return seen

    lax.fori_loop(0, nsc, mask_body, jnp.zeros((1, LANES), F32))

    def split_near(slot, blk):
        blk_rows = pl.ds(pl.multiple_of(blk * QB, QB), QB)
        mnear_ref[slot] = madd_ref[blk_rows, :]
        madd_ref[blk_rows, :] = jnp.full((QB, LANES), NEG, BF16)

    split_near(1, i)

    @pl.when(i >= 1)
    def _():
        split_near(0, i - 1)

    m_ref[...] = jnp.full(m_ref.shape, NEG, F32)
    l_ref[...] = jnp.zeros(l_ref.shape, F32)
    acc_ref[...] = jnp.zeros(acc_ref.shape, F32)

    def logits(kc, madd, n):
        return _dot_nt(jnp.concatenate([kc, madd], axis=1), q4_ref[n])

    def far_logits(j, n):
        return logits(k_ref[0, rows(j), n * A_HEAD_DIM:(n + 1) * A_HEAD_DIM], madd_ref[rows(j), :], n)

    def softmax_pv(n, s, vt):
        m_prev = m_ref[n]
        m_new = jnp.maximum(m_prev, jnp.max(s, axis=0, keepdims=True))
        alpha = jnp.exp2(m_prev - m_new)
        p = jnp.exp2(s - m_new)
        l_ref[n] = alpha * l_ref[n] + jnp.sum(p, axis=0, keepdims=True)
        acc_ref[n] = alpha * acc_ref[n] + _dot(vt, p.astype(BF16))
        m_ref[n] = m_new

    def far_vt(j, n):
        return vt_ref[0, j, n * A_HEAD_DIM:(n + 1) * A_HEAD_DIM, :]

    def sweep(first, count):
        tasks = [(first + c, n) for c in range(count) for n in range(A_KV_HEADS)]
        ahead = A_KV_HEADS
        pending = [far_logits(j, n) for j, n in tasks[:ahead]]
        for t, (j, n) in enumerate(tasks):
            softmax_pv(n, pending.pop(0), far_vt(j, n))
            if t + ahead < len(tasks):
                pending.append(far_logits(*tasks[t + ahead]))

    def sweep_body(g, carry):
        sweep(far_unroll * g, far_unroll)
        return carry

    n_far = jnp.where(i % SUB >= 2, nsc, last)
    lax.fori_loop(0, n_far // far_unroll, sweep_body, 0)
    for rem in range(1, far_unroll):
        @pl.when(n_far % far_unroll == rem)
        def _(rem=rem):
            sweep(n_far - rem, rem)

    def near_step(slot, blk, vts_ref):
        blk_rows = pl.ds(pl.multiple_of(blk * QB, QB), QB)
        ss = [logits(k_ref[0, blk_rows, n * A_HEAD_DIM:(n + 1) * A_HEAD_DIM], mnear_ref[slot], n)
              + bias_ref[slot, n] for n in range(A_KV_HEADS)]
        for n in range(A_KV_HEADS):
            softmax_pv(n, ss[n], vts_ref[0, 0, n * A_HEAD_DIM:(n + 1) * A_HEAD_DIM, :])

    @pl.when(i >= 1)
    def _():
        near_step(0, i - 1, vtp_ref)

    near_step(1, i, vtd_ref)

    for n in range(A_KV_HEADS):
        out_t = acc_ref[n] / l_ref[n]
        for g in range(A_GROUP):
            hd = n * A_GROUP + g
            o_ref[0, :, hd * A_HEAD_DIM:(hd + 1) * A_HEAD_DIM] = out_t[:, g * QB:(g + 1) * QB].T.astype(o_ref.dtype)


def _dsa(p16, p32, vt, vt_blk, ln_g, ln_b, bias_near, *, topk, far_unroll):
    b, s, _ = p16.shape
    aq = A_HEADS * A_HEAD_DIM
    akv = A_KV_HEADS * A_HEAD_DIM
    qiw = IDX_HEADS * LANES
    nch = s // QB
    gq = A_GROUP * QB
    k_blk = (aq + X_HEADS * X_HEAD_DIM) // akv
    qi_blk = 2 * LRU_WIDTH // qiw
    ki_blk = (2 * LRU_WIDTH + qiw) // LANES
    return pl.pallas_call(
        functools.partial(_dsa_kernel, topk=topk, far_unroll=far_unroll),
        out_shape=jax.ShapeDtypeStruct((b, s, aq), BF16),
        grid=(b, nch),
        in_specs=[
            pl.BlockSpec((1, QB, aq), lambda bi, i: (bi, i, 0)),
            pl.BlockSpec((1, s, akv), lambda bi, i: (bi, 0, k_blk)),
            pl.BlockSpec((1, s // SC, akv, SC), lambda bi, i: (bi, 0, 0, 0)),
            pl.BlockSpec((1, 1, akv, QB), lambda bi, i: (bi, jnp.maximum(i - 1, 0), 0, 0)),
            pl.BlockSpec((1, 1, akv, QB), lambda bi, i: (bi, i, 0, 0)),
            pl.BlockSpec((1, QB, qiw), lambda bi, i: (bi, i, qi_blk)),
            pl.BlockSpec((1, s, LANES), lambda bi, i: (bi, 0, ki_blk)),
            pl.BlockSpec((1, QB, LANES), lambda bi, i: (bi, i, ki_blk + 1)),
            pl.BlockSpec((1, LANES), lambda bi, i: (0, 0)),
            pl.BlockSpec((1, LANES), lambda bi, i: (0, 0)),
            pl.BlockSpec((2, A_KV_HEADS, QB, gq), lambda bi, i: (0, 0, 0, 0)),
        ],
        out_specs=pl.BlockSpec((1, QB, aq), lambda bi, i: (bi, i, 0)),
        scratch_shapes=[
            pltpu.VMEM((s, LANES), BF16),
            pltpu.VMEM((s, LANES), I32),
            pltpu.VMEM((s, LANES), BF16),
            pltpu.VMEM((2, QB, LANES), BF16),
            pltpu.VMEM((IDX_HEADS * QB, LANES), BF16),
            pltpu.VMEM((A_KV_HEADS, gq, A_HEAD_DIM + QB), BF16),
            pltpu.VMEM((A_KV_HEADS, 1, gq), F32),
            pltpu.VMEM((A_KV_HEADS, 1, gq), F32),
            pltpu.VMEM((A_KV_HEADS, A_HEAD_DIM, gq), F32),
        ],
        compiler_params=_params("arbitrary", "arbitrary"),
        name="dsa",
    )(p16, p16, vt, vt_blk, vt_blk, p32, p32, p32, ln_g, ln_b, bias_near)


def _t5_bucket(dist):
    max_exact = N_BUCKETS // 2
    d = jnp.maximum(dist, 0)
    df = jnp.maximum(d, 1).astype(F32)
    large = max_exact + (jnp.log(df / max_exact) / math.log(MAX_DISTANCE / max_exact)
                         * (N_BUCKETS - max_exact)).astype(I32)
    large = jnp.minimum(large, N_BUCKETS - 1)
    return jnp.where(d < max_exact, d, large)


def _near_bias(rel_bias):
    t = jnp.arange(QB, dtype=I32)[:, None]
    s = jnp.arange(QB, dtype=I32)[None, :]
    tiles = []
    for off in (QB, 0):
        bucket = _t5_bucket(t - s + off)
        onehot = (bucket[..., None] == jnp.arange(N_BUCKETS, dtype=I32)).astype(F32)
        table = (rel_bias - rel_bias[N_BUCKETS - 1]) * LOG2E
        tile = jnp.einsum('tsb,bh->tsh', onehot, table, precision=lax.Precision.HIGHEST)
        tile = tile.transpose(1, 2, 0).reshape(QB, A_KV_HEADS, A_GROUP * QB)
        tiles.append(tile.transpose(1, 0, 2))
    return jnp.stack(tiles).astype(F32)


def _pad_cols(w, width):
    return jnp.pad(w, [(0, 0)] * (w.ndim - 1) + [(0, width - w.shape[-1])])


def _split_points(d_model):
    aq = A_HEADS * A_HEAD_DIM
    akv = A_KV_HEADS * A_HEAD_DIM
    sizes = (aq, akv, akv, IDX_HEADS * IDX_DIM, IDX_DIM, IDX_HEADS,
             LRU_WIDTH, LRU_WIDTH, X_HEADS * X_HEAD_DIM, N_BRANCH * d_model)
    pts, acc = [0], 0
    for n in sizes:
        acc += n
        pts.append(acc)
    return pts


def _regroup_kernel(w_ref, o16_ref, o32_ref, og_ref, *, pts):
    (q0, k0, v0, qi0, ki0, wi0, x0, g0, qm0, gt0, end) = pts

    def seg(lo, hi, scale=None):
        v = w_ref[0, :, lo:hi]
        if scale is not None:
            v = v * scale
        return v.astype(BF16)

    o16_ref[0, :, 0:k0 - q0] = seg(q0, k0, A_HEAD_DIM ** -0.5 * LOG2E)
    c = k0 - q0
    o16_ref[0, :, c:c + gt0 - qm0] = seg(qm0, gt0, X_HEAD_DIM ** -0.5)
    c += gt0 - qm0
    o16_ref[0, :, c:c + qi0 - k0] = seg(k0, qi0)
    o32_ref[0, :, 0:qm0 - x0] = seg(x0, qm0)
    c = qm0 - x0
    o32_ref[0, :, c:] = jnp.zeros((o32_ref.shape[1], o32_ref.shape[2] - c), BF16)
    for h in range(IDX_HEADS):
        o32_ref[0, :, c + h * LANES:c + h * LANES + IDX_DIM] = seg(qi0 + h * IDX_DIM, qi0 + (h + 1) * IDX_DIM)
    c += IDX_HEADS * LANES
    o32_ref[0, :, c:c + IDX_DIM] = seg(ki0, wi0)
    c += LANES
    o32_ref[0, :, c:c + IDX_HEADS] = seg(wi0, x0)
    og_ref[0] = seg(gt0, end)


def _prep_w_in(w_in, d_model, *, tk):
    depth, k, n = w_in.shape
    pts = _split_points(d_model)
    n16 = pts[3] + pts[9] - pts[8]
    n32 = 2 * LRU_WIDTH + (IDX_HEADS + 2) * LANES
    ng = n - pts[9]
    return pl.pallas_call(
        functools.partial(_regroup_kernel, pts=tuple(pts)),
        out_shape=(jax.ShapeDtypeStruct((depth, k, n16), BF16), jax.ShapeDtypeStruct((depth, k, n32), BF16),
                   jax.ShapeDtypeStruct((depth, k, ng), BF16)),
        grid=(depth, k // tk),
        in_specs=[pl.BlockSpec((1, tk, n), lambda l, i: (l, i, 0))],
        out_specs=(pl.BlockSpec((1, tk, n16), lambda l, i: (l, i, 0)),
                   pl.BlockSpec((1, tk, n32), lambda l, i: (l, i, 0)),
                   pl.BlockSpec((1, tk, ng), lambda l, i: (l, i, 0))),
        compiler_params=_params("parallel", "parallel"),
        name="regroup",
    )(w_in)


def _split_gu_kernel(w_ref, og_ref, ou_ref):
    d_ff = w_ref.shape[2] // 2
    pad = og_ref.shape[2] - d_ff
    for o_ref, lo in ((og_ref, 0), (ou_ref, d_ff)):
        o_ref[0, :, 0:d_ff] = w_ref[0, :, lo:lo + d_ff].astype(BF16)
        if pad:
            o_ref[0, :, d_ff:] = jnp.zeros((o_ref.shape[1], pad), BF16)


def _pad_down_kernel(w_ref, o_ref, *, n_real):
    @pl.when(pl.program_id(1) < n_real)
    def _():
        o_ref[...] = w_ref[...].astype(BF16)

    @pl.when(pl.program_id(1) >= n_real)
    def _():
        o_ref[...] = jnp.zeros_like(o_ref)


def _pad_ff(w_gu, w_down, tf, *, tk):
    depth, d, _ = w_gu.shape
    d_ff = w_down.shape[1]
    fp = -(-d_ff // tf) * tf
    assert d_ff % LANES == 0
    wg, wu = pl.pallas_call(
        _split_gu_kernel,
        out_shape=(jax.ShapeDtypeStruct((depth, d, fp), BF16),) * 2,
        grid=(depth, d // tk),
        in_specs=[pl.BlockSpec((1, tk, 2 * d_ff), lambda l, i: (l, i, 0))],
        out_specs=(pl.BlockSpec((1, tk, fp), lambda l, i: (l, i, 0)),) * 2,
        compiler_params=_params("parallel", "parallel"),
        name="split_gu",
    )(w_gu)
    n_real = d_ff // LANES
    wd = pl.pallas_call(
        functools.partial(_pad_down_kernel, n_real=n_real),
        out_shape=jax.ShapeDtypeStruct((depth, fp, d), BF16),
        grid=(depth, fp // LANES),
        in_specs=[pl.BlockSpec((1, LANES, d), lambda l, i: (l, jnp.minimum(i, n_real - 1), 0))],
        out_specs=pl.BlockSpec((1, LANES, d), lambda l, i: (l, i, 0)),
        compiler_params=_params("parallel", "arbitrary"),
        name="pad_down",
    )(w_down)
    return wg, wu, wd


def _tiles(m, s, d_model):
    return dict(
        ffn_tm=min(1024, m), ffn_final_tm=min(512, m), ffn_tf=512,
        proj_tm=min(256, m), proj_tn=256,
        merge_tm=min(512, m), merge_tn=min(512, d_model),
        lru_tc=min(256, s), mem_tm=min(512, s),
    )


def kernel(x, mem, rel_bias, final_norm, norm_ff1, w_ff1_gu, w_ff1_down, norm_mix, w_in, conv_w, conv_b, w_a, b_a, w_i, b_i, lam, idx_ln_g, idx_ln_b, mem_norm, w_mem_kv, w_branch, w_out, norm_ff2, w_ff2_gu, w_ff2_down):
    b, s, d = x.shape
    nm = mem.shape[1]
    depth = w_in.shape[0]
    m = b * s
    assert s % SC == 0 and d % LANES == 0
    tl = _tiles(m, s, d)
    topk = min(TOPK_MAX, s // 4)
    bias_near = _near_bias(rel_bias)
    row = lambda v: v.reshape(1, -1)

    ff1 = _pad_ff(w_ff1_gu, w_ff1_down, tl["ffn_tf"], tk=min(256, d))
    ff2 = _pad_ff(w_ff2_gu, w_ff2_down, tl["ffn_tf"], tk=min(256, d))
    w16, w32, wgates = _prep_w_in(w_in, d, tk=min(256, d))
    w_a16, w_i16 = w_a.astype(BF16), w_i.astype(BF16)
    w_mem16, w_branch16, w_out16 = w_mem_kv.astype(BF16), w_branch.astype(BF16), w_out.astype(BF16)
    ln_g = _pad_cols(idx_ln_g, LANES)
    ln_b = _pad_cols(idx_ln_b, LANES)

    x2 = x.reshape(m, d)
    mem2 = mem.reshape(b * nm, d)
    for l in range(depth):
        x2 = _ffn(x2, row(norm_ff1[l]), *ff1, row(final_norm), l, final_norm=False,
                  tm=tl["ffn_tm"], tf=tl["ffn_tf"])

        p16, p32 = _proj(x2, row(norm_mix[l]), w16, w32, l, tm=tl["proj_tm"])
        p16 = p16.reshape(b, s, -1)
        p32 = p32.reshape(b, s, -1)

        v_cols = slice(p16.shape[-1] - A_KV_HEADS * A_HEAD_DIM, p16.shape[-1])
        vt = p16[..., v_cols].reshape(b, s // SC, SC, -1).transpose(0, 1, 3, 2)
        vt_blk = p16[..., v_cols].reshape(b, s // QB, QB, -1).transpose(0, 1, 3, 2)
        y_a = _dsa(p16, p32, vt, vt_blk, row(ln_g[l]), row(ln_b[l]),
                   bias_near, topk=topk, far_unroll=4)
        y_b = _lru(p32, conv_w[l], row(conv_b[l]), w_a16, w_i16,
                   row(b_a[l]), row(b_i[l]), row(lam[l]), l, tc=tl["lru_tc"])
        mkv = _norm_matmul(mem2, row(mem_norm[l]), w_mem16, l, BF16,
                           tm=min(256, b * nm), tn=tl["proj_tn"], name="memkv")
        y_c = _memattn(p16, mkv.reshape(b, nm, -1), tm=tl["mem_tm"])

        x2 = _merge(x2, row(norm_mix[l]), y_a.reshape(m, -1), y_b.reshape(m, -1), y_c.reshape(m, -1),
                    wgates, w_branch16, w_out16, l,
                    tm=tl["merge_tm"], tn=tl["merge_tn"])

        is_last = l == depth - 1
        x2 = _ffn(x2, row(norm_ff2[l]), *ff2, row(final_norm), l, final_norm=is_last,
                  tm=tl["ffn_final_tm" if is_last else "ffn_tm"], tf=tl["ffn_tf"])
    return x2.reshape(b, s, d)
```

```python
import functools
import math

import jax
import jax.numpy as jnp
from jax import lax
from jax.experimental import pallas as pl
from jax.experimental.pallas import tpu as pltpu

F32 = jnp.float32
BF16 = jnp.bfloat16
I32 = jnp.int32

EPS = 1e-6
A_HEADS = 8
A_KV_HEADS = 2
A_HEAD_DIM = 128
A_GROUP = A_HEADS // A_KV_HEADS
IDX_HEADS = 4
IDX_DIM = 64
TOPK_MAX = 256
LRU_WIDTH = 1024
LRU_BLOCKS = 8
LRU_BLOCK_W = LRU_WIDTH // LRU_BLOCKS
CONV_WIDTH = 4
LRU_C = 8.0
X_HEADS = 4
X_HEAD_DIM = 256
N_BRANCH = 3
BRANCH_WIDTH = 1024
N_BUCKETS = 32
MAX_DISTANCE = 128

LANES = 128
SUBLANES = 8
VMEM_LIMIT = 56 * 1024 * 1024

QB = 128
SC = 512
SUB = SC // QB
NEG = -1e30
LOG2E = math.log2(math.e)
INT_MIN = -2 ** 31


def _as_i32(v):
    return v - (1 << 32) if v >= (1 << 31) else v


DIGIT_BITS = 7
DIGITS = 4
REST_BITS = 32 - DIGITS * DIGIT_BITS
DIGIT_MASK = (1 << DIGIT_BITS) - 1
DIGIT_HALF = 1 << (DIGIT_BITS - 1)
BYTE_ONES = _as_i32(0x01010101)
BYTE_LOW = _as_i32(0x7F7F7F7F)
BYTE_GUARDS = _as_i32(0x80808080)
BYTE_HALF = _as_i32(0x40404040)
assert SUB == 4 and DIGIT_BITS == 7


def _params(*sem):
    return pltpu.CompilerParams(dimension_semantics=sem, vmem_limit_bytes=VMEM_LIMIT)


def _rms(x, g):
    ms = jnp.mean(x * x, axis=-1, keepdims=True)
    return x * lax.rsqrt(ms + EPS) * g


def _sigmoid(x):
    return 1.0 / (1.0 + jnp.exp(-x))


def _dot(a, b):
    return jnp.dot(a, b, preferred_element_type=F32)


def _dot_nt(a, b):
    return lax.dot_general(a, b, (((1,), (1,)), ((), ())), preferred_element_type=F32)


def _ffn_kernel(x_ref, g_ref, wg_ref, wu_ref, wd_ref, fg_ref, o_ref, h_ref, *, final_norm):
    f = pl.program_id(1)

    @pl.when(f == 0)
    def _():
        h_ref[...] = _rms(x_ref[...], g_ref[...]).astype(BF16)
        o_ref[...] = jnp.zeros_like(o_ref)

    h = h_ref[...]
    g = _dot(h, wg_ref[...])
    u = _dot(h, wu_ref[...])
    a = (g * _sigmoid(g) * u).astype(BF16)
    o_ref[...] += _dot(a, wd_ref[...])

    @pl.when(f == pl.num_programs(1) - 1)
    def _():
        y = x_ref[...] + 0.5 * o_ref[...]
        if final_norm:
            y = _rms(y, fg_ref[...])
        o_ref[...] = y


def _ffn(x2, g, wg, wu, wd, fg, layer, *, final_norm, tm, tf):
    m, d = x2.shape
    fp = wg.shape[-1]
    return pl.pallas_call(
        functools.partial(_ffn_kernel, final_norm=final_norm),
        out_shape=jax.ShapeDtypeStruct((m, d), F32),
        grid=(m // tm, fp // tf),
        in_specs=[
            pl.BlockSpec((tm, d), lambda i, f: (i, 0)),
            pl.BlockSpec((1, d), lambda i, f: (0, 0)),
            pl.BlockSpec((None, d, tf), lambda i, f: (layer, 0, f)),
            pl.BlockSpec((None, d, tf), lambda i, f: (layer, 0, f)),
            pl.BlockSpec((None, tf, d), lambda i, f: (layer, f, 0)),
            pl.BlockSpec((1, d), lambda i, f: (0, 0)),
        ],
        out_specs=pl.BlockSpec((tm, d), lambda i, f: (i, 0)),
        scratch_shapes=[pltpu.VMEM((tm, d), BF16)],
        compiler_params=_params("parallel", "arbitrary"),
        name="ffn",
    )(x2, g, wg, wu, wd, fg)


def _norm_matmul_kernel(x_ref, g_ref, w_ref, o_ref, h_ref):
    @pl.when(pl.program_id(1) == 0)
    def _():
        h_ref[...] = _rms(x_ref[...], g_ref[...]).astype(BF16)

    o_ref[...] = _dot(h_ref[...], w_ref[...]).astype(o_ref.dtype)


def _norm_matmul(x2, g, w, layer, out_dtype, *, tm, tn, name):
    m, d = x2.shape
    n = w.shape[-1]
    return pl.pallas_call(
        _norm_matmul_kernel,
        out_shape=jax.ShapeDtypeStruct((m, n), out_dtype),
        grid=(m // tm, n // tn),
        in_specs=[
            pl.BlockSpec((tm, d), lambda i, j: (i, 0)),
            pl.BlockSpec((1, d), lambda i, j: (0, 0)),
            pl.BlockSpec((None, d, tn), lambda i, j: (layer, 0, j)),
        ],
        out_specs=pl.BlockSpec((tm, tn), lambda i, j: (i, j)),
        scratch_shapes=[pltpu.VMEM((tm, d), BF16)],
        compiler_params=_params("parallel", "arbitrary"),
        name=name,
    )(x2, g, w)


def _proj_kernel(x_ref, g_ref, w16_ref, w32_ref, o16_ref, o32_ref):
    h = _rms(x_ref[...], g_ref[...]).astype(BF16)
    o16_ref[...] = _dot(h, w16_ref[...]).astype(o16_ref.dtype)
    o32_ref[...] = _dot(h, w32_ref[...])


def _proj(x2, g, w16, w32, layer, *, tm):
    m, d = x2.shape
    n16, n32 = w16.shape[-1], w32.shape[-1]
    resident = pl.Buffered(1)
    return pl.pallas_call(
        _proj_kernel,
        out_shape=(jax.ShapeDtypeStruct((m, n16), BF16), jax.ShapeDtypeStruct((m, n32), F32)),
        grid=(m // tm,),
        in_specs=[
            pl.BlockSpec((tm, d), lambda i: (i, 0)),
            pl.BlockSpec((1, d), lambda i: (0, 0)),
            pl.BlockSpec((None, d, n16), lambda i: (layer, 0, 0), pipeline_mode=resident),
            pl.BlockSpec((None, d, n32), lambda i: (layer, 0, 0), pipeline_mode=resident),
        ],
        out_specs=(pl.BlockSpec((tm, n16), lambda i: (i, 0)), pl.BlockSpec((tm, n32), lambda i: (i, 0))),
        compiler_params=_params("parallel"),
        name="proj",
    )(x2, g, w16, w32)


def _lru_kernel(x_ref, gate_ref, cw_ref, cb_ref, wa_ref, wi_ref, ba_ref, bi_ref, lam_ref,
                o_ref, xbuf_ref, hc_ref, *, tc):
    halo = SUBLANES

    @pl.when(pl.program_id(1) == 0)
    def _():
        xbuf_ref[0:halo, :] = jnp.zeros((halo, LRU_WIDTH), F32)
        hc_ref[...] = jnp.zeros_like(hc_ref)

    xbuf_ref[halo:halo + tc, :] = x_ref[0]
    base = halo - (CONV_WIDTH - 1)
    xc = xbuf_ref[base:base + tc, :] * cw_ref[0:1, :]
    for j in range(1, CONV_WIDTH):
        xc = xc + xbuf_ref[base + j:base + j + tc, :] * cw_ref[j:j + 1, :]
    xc = cb_ref[...] + xc
    xbuf_ref[0:halo, :] = xbuf_ref[tc:tc + halo, :]

    xcb = xc.astype(BF16)
    r_parts, i_parts = [], []
    for n in range(LRU_BLOCKS):
        blk = xcb[:, n * LRU_BLOCK_W:(n + 1) * LRU_BLOCK_W]
        r_parts.append(_dot(blk, wa_ref[n]))
        i_parts.append(_dot(blk, wi_ref[n]))
    r = _sigmoid(jnp.concatenate(r_parts, axis=1) + ba_ref[...])
    gi = _sigmoid(jnp.concatenate(i_parts, axis=1) + bi_ref[...])

    softplus_neg_lam = jnp.log1p(jnp.exp(-lam_ref[...]))
    log_a = (-LRU_C) * r * softplus_neg_lam
    a = jnp.exp(log_a)
    th = jnp.tanh(log_a)
    mult = jnp.sqrt(jnp.maximum(-2.0 * th / (1.0 - th), 0.0))
    u = mult * (gi * xc)

    row = lax.broadcasted_iota(I32, (tc, LRU_WIDTH), 0)
    d = 1
    while d < tc:
        keep = row >= d
        a_sh = jnp.where(keep, pltpu.roll(a, d, 0), 1.0)
        u_sh = jnp.where(keep, pltpu.roll(u, d, 0), 0.0)
        u = a * u_sh + u
        a = a * a_sh
        d *= 2
    h = a * hc_ref[...] + u
    hc_ref[...] = h[tc - 1:tc, :]

    gate = gate_ref[0]
    gelu = 0.5 * gate * (1.0 + jnp.tanh(math.sqrt(2.0 / math.pi) * (gate + 0.044715 * (gate * gate * gate))))
    o_ref[0] = (h * gelu).astype(o_ref.dtype)


def _lru(p32, conv_w, conv_b, w_a, w_i, b_a, b_i, lam, layer, *, tc):
    b, s, _ = p32.shape
    vec = pl.BlockSpec((1, LRU_WIDTH), lambda bi, c: (0, 0))
    wblk = pl.BlockSpec((None, LRU_BLOCKS, LRU_BLOCK_W, LRU_BLOCK_W), lambda bi, c: (layer, 0, 0, 0))
    return pl.pallas_call(
        functools.partial(_lru_kernel, tc=tc),
        out_shape=jax.ShapeDtypeStruct((b, s, LRU_WIDTH), BF16),
        grid=(b, s // tc),
        in_specs=[
            pl.BlockSpec((1, tc, LRU_WIDTH), lambda bi, c: (bi, c, 0)),
            pl.BlockSpec((1, tc, LRU_WIDTH), lambda bi, c: (bi, c, 1)),
            pl.BlockSpec((CONV_WIDTH, LRU_WIDTH), lambda bi, c: (0, 0)),
            vec, wblk, wblk, vec, vec, vec,
        ],
        out_specs=pl.BlockSpec((1, tc, LRU_WIDTH), lambda bi, c: (bi, c, 0)),
        scratch_shapes=[pltpu.VMEM((tc + SUBLANES, LRU_WIDTH), F32), pltpu.VMEM((1, LRU_WIDTH), F32)],
        compiler_params=_params("arbitrary", "arbitrary"),
        name="rglru",
    )(p32, p32, conv_w, conv_b, w_a, w_i, b_a, b_i, lam)


def _memattn_kernel(q_ref, kv_ref, o_ref):
    kw = X_HEADS * X_HEAD_DIM
    for h in range(X_HEADS):
        sl = slice(h * X_HEAD_DIM, (h + 1) * X_HEAD_DIM)
        q = q_ref[0, :, sl]
        k = kv_ref[0, :, sl]
        v = kv_ref[0, :, kw + h * X_HEAD_DIM:kw + (h + 1) * X_HEAD_DIM]
        s = _dot_nt(q, k)
        m = jnp.max(s, axis=-1, keepdims=True)
        p = jnp.exp(s - m)
        l = jnp.sum(p, axis=-1, keepdims=True)
        o = _dot(p.astype(BF16), v)
        o_ref[0, :, sl] = (o / l).astype(o_ref.dtype)


def _memattn(p16, mkv, *, tm):
    b, s, _ = p16.shape
    nm = mkv.shape[1]
    kw = X_HEADS * X_HEAD_DIM
    return pl.pallas_call(
        _memattn_kernel,
        out_shape=jax.ShapeDtypeStruct((b, s, kw), BF16),
        grid=(b, s // tm),
        in_specs=[
            pl.BlockSpec((1, tm, kw), lambda bi, i: (bi, i, 1)),
            pl.BlockSpec((1, nm, 2 * kw), lambda bi, i: (bi, 0, 0)),
        ],
        out_specs=pl.BlockSpec((1, tm, kw), lambda bi, i: (bi, i, 0)),
        compiler_params=_params("parallel", "parallel"),
        name="memattn",
    )(p16, mkv)


def _merge_kernel(x_ref, g_ref, ya_ref, yb_ref, yc_ref, wg0_ref, wg1_ref, wg2_ref, wb_ref, wo_ref,
                  o_ref, h_ref):
    n = pl.program_id(1)

    @pl.when(n == 0)
    def _():
        h_ref[...] = _rms(x_ref[...], g_ref[...]).astype(BF16)
        o_ref[...] = jnp.zeros_like(o_ref)

    h = h_ref[...]
    merged = None
    for j, (y_ref, wg_ref) in enumerate(((ya_ref, wg0_ref), (yb_ref, wg1_ref), (yc_ref, wg2_ref))):
        term = _sigmoid(_dot(h, wg_ref[...])) * _dot(y_ref[...], wb_ref[j])
        merged = term if merged is None else merged + term
    o_ref[...] += _dot(merged.astype(BF16), wo_ref[...])

    @pl.when(n == pl.num_programs(1) - 1)
    def _():
        o_ref[...] = x_ref[...] + o_ref[...]


def _merge(x2, g, ya, yb, yc, wgates, wbranch, wout, layer, *, tm, tn):
    m, d = x2.shape
    nblk = d // tn
    yspec = pl.BlockSpec((tm, BRANCH_WIDTH), lambda i, n: (i, 0))
    return pl.pallas_call(
        _merge_kernel,
        out_shape=jax.ShapeDtypeStruct((m, d), F32),
        grid=(m // tm, nblk),
        in_specs=[
            pl.BlockSpec((tm, d), lambda i, n: (i, 0)),
            pl.BlockSpec((1, d), lambda i, n: (0, 0)),
            yspec, yspec, yspec,
            pl.BlockSpec((None, d, tn), lambda i, n: (layer, 0, n)),
            pl.BlockSpec((None, d, tn), lambda i, n: (layer, 0, nblk + n)),
            pl.BlockSpec((None, d, tn), lambda i, n: (layer, 0, 2 * nblk + n)),
            pl.BlockSpec((None, N_BRANCH, BRANCH_WIDTH, tn), lambda i, n: (layer, 0, 0, n)),
            pl.BlockSpec((None, tn, d), lambda i, n: (layer, n, 0)),
        ],
        out_specs=pl.BlockSpec((tm, d), lambda i, n: (i, 0)),
        scratch_shapes=[pltpu.VMEM((tm, d), BF16)],
        compiler_params=_params("parallel", "arbitrary"),
        name="merge",
    )(x2, g, ya, yb, yc, wgates, wgates, wgates, wbranch, wout)


def _dsa_kernel(q_ref, k_ref, vt_ref, vtp_ref, vtd_ref, qi_ref, ki_ref, wi_ref, lng_ref, lnb_ref, bias_ref,
                o_ref,
                kln_ref, keys_ref, dig_ref, cls_ref, madd_ref, mnear_ref, qi4_ref, q4_ref, m_ref, l_ref, acc_ref,
                *, topk, far_unroll):
    i = pl.program_id(1)
    s_len = k_ref.shape[1]

    @pl.when(i == 0)
    def _():
        rows = 512 if s_len % 512 == 0 else QB
        lane = lax.broadcasted_iota(I32, (rows, LANES), 1)
        real = lane < IDX_DIM

        def ln_body(c, carry):
            x = ki_ref[0, pl.ds(c * rows, rows), :]
            mu = jnp.sum(x, axis=-1, keepdims=True) * (1.0 / IDX_DIM)
            xm = jnp.where(real, x - mu, 0.0)
            var = jnp.sum(xm * xm, axis=-1, keepdims=True) * (1.0 / IDX_DIM)
            y = xm * lax.rsqrt(var + EPS) * lng_ref[...] + lnb_ref[...]
            kln_ref[pl.ds(c * rows, rows), :] = y.astype(BF16)
            return carry

        lax.fori_loop(0, s_len // rows, ln_body, 0)

    w_rows = wi_ref[0].T * (IDX_HEADS ** -0.5 * IDX_DIM ** -0.5)
    for h in range(IDX_HEADS):
        qi4_ref[h * QB:(h + 1) * QB, :] = qi_ref[0, :, h * LANES:(h + 1) * LANES].astype(BF16)
    eye = (lax.broadcasted_iota(I32, (QB, QB), 0) == lax.broadcasted_iota(I32, (QB, QB), 1)).astype(BF16)
    for n in range(A_KV_HEADS):
        for g in range(A_GROUP):
            hd = n * A_GROUP + g
            q4_ref[n, g * QB:(g + 1) * QB, 0:A_HEAD_DIM] = q_ref[0, :, hd * A_HEAD_DIM:(hd + 1) * A_HEAD_DIM]
            q4_ref[n, g * QB:(g + 1) * QB, A_HEAD_DIM:A_HEAD_DIM + QB] = eye

    last = i // SUB
    nsc = last + 1

    def rows(j):
        return pl.ds(pl.multiple_of(j * SC, SC), SC)

    def packed_digits(j, shift):
        word = None
        for blk in range(SUB):
            key = keys_ref[pl.ds(pl.multiple_of(j * SC + blk * QB, QB), QB), :]
            s = shift - 8 * blk
            f = lax.shift_right_arithmetic(key, jnp.int32(s)) if s >= 0 else lax.shift_left(key, jnp.int32(-s))
            f = f & jnp.int32(_as_i32(DIGIT_MASK << (8 * blk)))
            word = f if word is None else word | f
        return word

    def word_rows(j):
        return pl.ds(pl.multiple_of(j * QB, QB), QB)

    def index_dots(j):
        return _dot_nt(kln_ref[rows(j), :], qi4_ref[...])

    def score_chunk(j, dots, causal):
        sc = None
        for h in range(IDX_HEADS):
            t = w_rows[h:h + 1, :] * jnp.maximum(dots[:, h * QB:(h + 1) * QB], 0.0)
            sc = t if sc is None else sc + t
        bits = pltpu.bitcast(sc, I32)
        key = jnp.where(bits < 0, bits ^ jnp.int32(0x7FFFFFFF), bits)
        key = jnp.where(sc == 0.0, 0, key)
        if causal:
            key_pos = j * SC + lax.broadcasted_iota(I32, (SC, LANES), 0)
            q_pos = i * QB + lax.broadcasted_iota(I32, (SC, LANES), 1)
            key = jnp.where(key_pos <= q_pos, key, INT_MIN)
        keys_ref[rows(j), :] = key
        top = packed_digits(j, 32 - DIGIT_BITS) ^ BYTE_HALF
        dig_ref[word_rows(j), :] = top | BYTE_GUARDS
        cls_ref[word_rows(j), :] = jnp.full((QB, LANES), BYTE_LOW, I32)

    def score_body(pr, carry):
        d0 = index_dots(2 * pr)
        d1 = index_dots(2 * pr + 1)
        score_chunk(2 * pr, d0, False)
        score_chunk(2 * pr + 1, d1, False)
        return carry

    lax.fori_loop(0, last // 2, score_body, 0)

    @pl.when(last % 2 == 1)
    def _():
        score_chunk(last - 1, index_dots(last - 1), False)

    score_chunk(last, index_dots(last), True)

    t_pos = i * QB + lax.broadcasted_iota(I32, (1, LANES), 1)
    kk = jnp.minimum(t_pos + 1, topk).astype(F32)

    def count_ge(cand):
        def body(j, cnt):
            for r in range(SUB):
                blk = keys_ref[pl.ds(pl.multiple_of(j * SC + r * QB, QB), QB), :]
                cnt = cnt + jnp.where(blk >= cand, 1.0, 0.0)
            return cnt

        cnt = lax.fori_loop(0, nsc, body, jnp.zeros((QB, LANES), F32))
        return jnp.sum(cnt, axis=0, keepdims=True)

    def count_digit_ge(cand):
        packed = cand * BYTE_ONES

        def body(j, cnt):
            d = dig_ref[word_rows(j), :] - packed
            return cnt + lax.population_count(d & BYTE_GUARDS)

        cnt = lax.fori_loop(0, nsc, body, jnp.zeros((QB, LANES), I32))
        return jnp.sum(cnt, axis=0, keepdims=True).astype(F32)

    def select_digit(above):
        def bit_body(bi, t):
            cand = t + lax.shift_left(jnp.int32(1), DIGIT_BITS - 1 - bi)
            return jnp.where(above + count_digit_ge(cand) >= kk, cand, t)

        t = lax.fori_loop(0, DIGIT_BITS, bit_body, jnp.zeros((1, LANES), I32))
        greater = count_digit_ge(jnp.minimum(t + 1, DIGIT_MASK))
        return t, above + jnp.where(t == DIGIT_MASK, 0.0, greater)

    thr = jnp.zeros((1, LANES), I32)
    above = jnp.zeros((1, LANES), F32)
    for level in range(DIGITS):
        shift = 32 - DIGIT_BITS * (level + 1)
        if level > 0:
            t_packed = t * BYTE_ONES

            def next_digit(j, carry, shift=shift, t_packed=t_packed):
                z = (dig_ref[word_rows(j), :] & BYTE_LOW) ^ t_packed
                same = ((z + BYTE_LOW) & BYTE_GUARDS) ^ BYTE_GUARDS
                cls = cls_ref[word_rows(j), :] & (same - lax.shift_right_logical(same, jnp.int32(7)))
                cls_ref[word_rows(j), :] = cls
                dig_ref[word_rows(j), :] = (packed_digits(j, shift) & cls) | BYTE_GUARDS
                return carry

            lax.fori_loop(0, nsc, next_digit, 0)
        t, above = select_digit(above)
        thr = thr + lax.shift_left(t - (DIGIT_HALF if level == 0 else 0), jnp.int32(shift))

    def bit_body(bi, t):
        cand = t + lax.shift_left(jnp.int32(1), REST_BITS - 1 - bi)
        return jnp.where(count_ge(cand) >= kk, cand, t)

    thr = lax.fori_loop(0, REST_BITS, bit_body, thr)
    need = kk - count_ge(thr + 1)
    sub_k = lax.broadcasted_iota(I32, (QB, QB), 0)
    sub_k2 = lax.broadcasted_iota(I32, (QB, QB), 1)
    tril = (sub_k2 <= sub_k).astype(BF16)

    def mask_body(j, seen):
        keys, eqs, pres = [], [], []
        for r in range(SUB):
            key = keys_ref[pl.ds(pl.multiple_of(j * SC + r * QB, QB), QB), :]
            eq = key == thr
            keys.append(key)
            eqs.append(eq)
            pres.append(_dot(tril, jnp.where(eq, 1.0, 0.0).astype(BF16)))
        for r in range(SUB):
            sel = (keys[r] > thr) | (eqs[r] & ((seen + pres[r]) <= need))
            madd_ref[pl.ds(pl.multiple_of(j * SC + r * QB, QB), QB), :] = jnp.where(sel, 0.0, NEG).astype(BF16)
            seen = seen + pres[r][QB - 1:QB, :]
        return seen

    lax.fori_loop(0, nsc, mask_body, jnp.zeros((1, LANES), F32))

    def split_near(slot, blk):
        blk_rows = pl.ds(pl.multiple_of(blk * QB, QB), QB)
        mnear_ref[slot] = madd_ref[blk_rows, :]
        madd_ref[blk_rows, :] = jnp.full((QB, LANES), NEG, BF16)

    split_near(1, i)

    @pl.when(i >= 1)
    def _():
        split_near(0, i - 1)

    m_ref[...] = jnp.full(m_ref.shape, NEG, F32)
    l_ref[...] = jnp.zeros(l_ref.shape, F32)
    acc_ref[...] = jnp.zeros(acc_ref.shape, F32)

    def logits(kc, madd, n):
        return _dot_nt(jnp.concatenate([kc, madd], axis=1), q4_ref[n])

    def far_logits(j, n):
        return logits(k_ref[0, rows(j), n * A_HEAD_DIM:(n + 1) * A_HEAD_DIM], madd_ref[rows(j), :], n)

    def softmax_pv(n, s, vt):
        m_prev = m_ref[n]
        m_new = jnp.maximum(m_prev, jnp.max(s, axis=0, keepdims=True))
        alpha = jnp.exp2(m_prev - m_new)
        p = jnp.exp2(s - m_new)
        l_ref[n] = alpha * l_ref[n] + jnp.sum(p, axis=0, keepdims=True)
        acc_ref[n] = alpha * acc_ref[n] + _dot(vt, p.astype(BF16))
        m_ref[n] = m_new

    def far_vt(j, n):
        return vt_ref[0, j, n * A_HEAD_DIM:(n + 1) * A_HEAD_DIM, :]

    def sweep(first, count):
        tasks = [(first + c, n) for c in range(count) for n in range(A_KV_HEADS)]
        ahead = A_KV_HEADS
        pending = [far_logits(j, n) for j, n in tasks[:ahead]]
        for t, (j, n) in enumerate(tasks):
            softmax_pv(n, pending.pop(0), far_vt(j, n))
            if t + ahead < len(tasks):
                pending.append(far_logits(*tasks[t + ahead]))

    def sweep_body(g, carry):
        sweep(far_unroll * g, far_unroll)
        return carry

    n_far = jnp.where(i % SUB >= 2, nsc, last)
    lax.fori_loop(0, n_far // far_unroll, sweep_body, 0)
    for rem in range(1, far_unroll):
        @pl.when(n_far % far_unroll == rem)
        def _(rem=rem):
            sweep(n_far - rem, rem)

    def near_step(slot, blk, vts_ref):
        blk_rows = pl.ds(pl.multiple_of(blk * QB, QB), QB)
        ss = [logits(k_ref[0, blk_rows, n * A_HEAD_DIM:(n + 1) * A_HEAD_DIM], mnear_ref[slot], n)
              + bias_ref[slot, n] for n in range(A_KV_HEADS)]
        for n in range(A_KV_HEADS):
            softmax_pv(n, ss[n], vts_ref[0, 0, n * A_HEAD_DIM:(n + 1) * A_HEAD_DIM, :])

    @pl.when(i >= 1)
    def _():
        near_step(0, i - 1, vtp_ref)

    near_step(1, i, vtd_ref)

    for n in range(A_KV_HEADS):
        out_t = acc_ref[n] / l_ref[n]
        for g in range(A_GROUP):
            hd = n * A_GROUP + g
            o_ref[0, :, hd * A_HEAD_DIM:(hd + 1) * A_HEAD_DIM] = out_t[:, g * QB:(g + 1) * QB].T.astype(o_ref.dtype)


def _dsa(p16, p32, vt, vt_blk, ln_g, ln_b, bias_near, *, topk, far_unroll):
    b, s, _ = p16.shape
    aq = A_HEADS * A_HEAD_DIM
    akv = A_KV_HEADS * A_HEAD_DIM
    qiw = IDX_HEADS * LANES
    nch = s // QB
    gq = A_GROUP * QB
    k_blk = (aq + X_HEADS * X_HEAD_DIM) // akv
    qi_blk = 2 * LRU_WIDTH // qiw
    ki_blk = (2 * LRU_WIDTH + qiw) // LANES
    return pl.pallas_call(
        functools.partial(_dsa_kernel, topk=topk, far_unroll=far_unroll),
        out_shape=jax.ShapeDtypeStruct((b, s, aq), BF16),
        grid=(b, nch),
        in_specs=[
            pl.BlockSpec((1, QB, aq), lambda bi, i: (bi, i, 0)),
            pl.BlockSpec((1, s, akv), lambda bi, i: (bi, 0, k_blk)),
            pl.BlockSpec((1, s // SC, akv, SC), lambda bi, i: (bi, 0, 0, 0)),
            pl.BlockSpec((1, 1, akv, QB), lambda bi, i: (bi, jnp.maximum(i - 1, 0), 0, 0)),
            pl.BlockSpec((1, 1, akv, QB), lambda bi, i: (bi, i, 0, 0)),
            pl.BlockSpec((1, QB, qiw), lambda bi, i: (bi, i, qi_blk)),
            pl.BlockSpec((1, s, LANES), lambda bi, i: (bi, 0, ki_blk)),
            pl.BlockSpec((1, QB, LANES), lambda bi, i: (bi, i, ki_blk + 1)),
            pl.BlockSpec((1, LANES), lambda bi, i: (0, 0)),
            pl.BlockSpec((1, LANES), lambda bi, i: (0, 0)),
            pl.BlockSpec((2, A_KV_HEADS, QB, gq), lambda bi, i: (0, 0, 0, 0)),
        ],
        out_specs=pl.BlockSpec((1, QB, aq), lambda bi, i: (bi, i, 0)),
        scratch_shapes=[
            pltpu.VMEM((s, LANES), BF16),
            pltpu.VMEM((s, LANES), I32),
            pltpu.VMEM((s // SUB, LANES), I32),
            pltpu.VMEM((s // SUB, LANES), I32),
            pltpu.VMEM((s, LANES), BF16),
            pltpu.VMEM((2, QB, LANES), BF16),
            pltpu.VMEM((IDX_HEADS * QB, LANES), BF16),
            pltpu.VMEM((A_KV_HEADS, gq, A_HEAD_DIM + QB), BF16),
            pltpu.VMEM((A_KV_HEADS, 1, gq), F32),
            pltpu.VMEM((A_KV_HEADS, 1, gq), F32),
            pltpu.VMEM((A_KV_HEADS, A_HEAD_DIM, gq), F32),
        ],
        compiler_params=_params("arbitrary", "arbitrary"),
        name="dsa",
    )(p16, p16, vt, vt_blk, vt_blk, p32, p32, p32, ln_g, ln_b, bias_near)


def _t5_bucket(dist):
    max_exact = N_BUCKETS // 2
    d = jnp.maximum(dist, 0)
    df = jnp.maximum(d, 1).astype(F32)
    large = max_exact + (jnp.log(df / max_exact) / math.log(MAX_DISTANCE / max_exact)
                         * (N_BUCKETS - max_exact)).astype(I32)
    large = jnp.minimum(large, N_BUCKETS - 1)
    return jnp.where(d < max_exact, d, large)


def _near_bias(rel_bias):
    t = jnp.arange(QB, dtype=I32)[:, None]
    s = jnp.arange(QB, dtype=I32)[None, :]
    tiles = []
    for off in (QB, 0):
        bucket = _t5_bucket(t - s + off)
        onehot = (bucket[..., None] == jnp.arange(N_BUCKETS, dtype=I32)).astype(F32)
        table = (rel_bias - rel_bias[N_BUCKETS - 1]) * LOG2E
        tile = jnp.einsum('tsb,bh->tsh', onehot, table, precision=lax.Precision.HIGHEST)
        tile = tile.transpose(1, 2, 0).reshape(QB, A_KV_HEADS, A_GROUP * QB)
        tiles.append(tile.transpose(1, 0, 2))
    return jnp.stack(tiles).astype(F32)


def _pad_cols(w, width):
    return jnp.pad(w, [(0, 0)] * (w.ndim - 1) + [(0, width - w.shape[-1])])


def _split_points(d_model):
    aq = A_HEADS * A_HEAD_DIM
    akv = A_KV_HEADS * A_HEAD_DIM
    sizes = (aq, akv, akv, IDX_HEADS * IDX_DIM, IDX_DIM, IDX_HEADS,
             LRU_WIDTH, LRU_WIDTH, X_HEADS * X_HEAD_DIM, N_BRANCH * d_model)
    pts, acc = [0], 0
    for n in sizes:
        acc += n
        pts.append(acc)
    return pts


def _regroup_kernel(w_ref, o16_ref, o32_ref, og_ref, *, pts):
    (q0, k0, v0, qi0, ki0, wi0, x0, g0, qm0, gt0, end) = pts

    def seg(lo, hi, scale=None):
        v = w_ref[0, :, lo:hi]
        if scale is not None:
            v = v * scale
        return v.astype(BF16)

    o16_ref[0, :, 0:k0 - q0] = seg(q0, k0, A_HEAD_DIM ** -0.5 * LOG2E)
    c = k0 - q0
    o16_ref[0, :, c:c + gt0 - qm0] = seg(qm0, gt0, X_HEAD_DIM ** -0.5)
    c += gt0 - qm0
    o16_ref[0, :, c:c + qi0 - k0] = seg(k0, qi0)
    o32_ref[0, :, 0:qm0 - x0] = seg(x0, qm0)
    c = qm0 - x0
    o32_ref[0, :, c:] = jnp.zeros((o32_ref.shape[1], o32_ref.shape[2] - c), BF16)
    for h in range(IDX_HEADS):
        o32_ref[0, :, c + h * LANES:c + h * LANES + IDX_DIM] = seg(qi0 + h * IDX_DIM, qi0 + (h + 1) * IDX_DIM)
    c += IDX_HEADS * LANES
    o32_ref[0, :, c:c + IDX_DIM] = seg(ki0, wi0)
    c += LANES
    o32_ref[0, :, c:c + IDX_HEADS] = seg(wi0, x0)
    og_ref[0] = seg(gt0, end)


def _prep_w_in(w_in, d_model, *, tk):
    depth, k, n = w_in.shape
    pts = _split_points(d_model)
    n16 = pts[3] + pts[9] - pts[8]
    n32 = 2 * LRU_WIDTH + (IDX_HEADS + 2) * LANES
    ng = n - pts[9]
    return pl.pallas_call(
        functools.partial(_regroup_kernel, pts=tuple(pts)),
        out_shape=(jax.ShapeDtypeStruct((depth, k, n16), BF16), jax.ShapeDtypeStruct((depth, k, n32), BF16),
                   jax.ShapeDtypeStruct((depth, k, ng), BF16)),
        grid=(depth, k // tk),
        in_specs=[pl.BlockSpec((1, tk, n), lambda l, i: (l, i, 0))],
        out_specs=(pl.BlockSpec((1, tk, n16), lambda l, i: (l, i, 0)),
                   pl.BlockSpec((1, tk, n32), lambda l, i: (l, i, 0)),
                   pl.BlockSpec((1, tk, ng), lambda l, i: (l, i, 0))),
        compiler_params=_params("parallel", "parallel"),
        name="regroup",
    )(w_in)


def _split_gu_kernel(w_ref, og_ref, ou_ref):
    d_ff = w_ref.shape[2] // 2
    pad = og_ref.shape[2] - d_ff
    for o_ref, lo in ((og_ref, 0), (ou_ref, d_ff)):
        o_ref[0, :, 0:d_ff] = w_ref[0, :, lo:lo + d_ff].astype(BF16)
        if pad:
            o_ref[0, :, d_ff:] = jnp.zeros((o_ref.shape[1], pad), BF16)


def _pad_down_kernel(w_ref, o_ref, *, n_real):
    @pl.when(pl.program_id(1) < n_real)
    def _():
        o_ref[...] = w_ref[...].astype(BF16)

    @pl.when(pl.program_id(1) >= n_real)
    def _():
        o_ref[...] = jnp.zeros_like(o_ref)


def _pad_ff(w_gu, w_down, tf, *, tk):
    depth, d, _ = w_gu.shape
    d_ff = w_down.shape[1]
    fp = -(-d_ff // tf) * tf
    assert d_ff % LANES == 0
    wg, wu = pl.pallas_call(
        _split_gu_kernel,
        out_shape=(jax.ShapeDtypeStruct((depth, d, fp), BF16),) * 2,
        grid=(depth, d // tk),
        in_specs=[pl.BlockSpec((1, tk, 2 * d_ff), lambda l, i: (l, i, 0))],
        out_specs=(pl.BlockSpec((1, tk, fp), lambda l, i: (l, i, 0)),) * 2,
        compiler_params=_params("parallel", "parallel"),
        name="split_gu",
    )(w_gu)
    n_real = d_ff // LANES
    wd = pl.pallas_call(
        functools.partial(_pad_down_kernel, n_real=n_real),
        out_shape=jax.ShapeDtypeStruct((depth, fp, d), BF16),
        grid=(depth, fp // LANES),
        in_specs=[pl.BlockSpec((1, LANES, d), lambda l, i: (l, jnp.minimum(i, n_real - 1), 0))],
        out_specs=pl.BlockSpec((1, LANES, d), lambda l, i: (l, i, 0)),
        compiler_params=_params("parallel", "arbitrary"),
        name="pad_down",
    )(w_down)
    return wg, wu, wd


def _tiles(m, s, d_model):
    return dict(
        ffn_tm=min(1024, m), ffn_final_tm=min(512, m), ffn_tf=512,
        proj_tm=min(256, m), proj_tn=256,
        merge_tm=min(512, m), merge_tn=min(512, d_model),
        lru_tc=min(256, s), mem_tm=min(512, s),
    )


def kernel(x, mem, rel_bias, final_norm, norm_ff1, w_ff1_gu, w_ff1_down, norm_mix, w_in, conv_w, conv_b, w_a, b_a, w_i, b_i, lam, idx_ln_g, idx_ln_b, mem_norm, w_mem_kv, w_branch, w_out, norm_ff2, w_ff2_gu, w_ff2_down):
    b, s, d = x.shape
    nm = mem.shape[1]
    depth = w_in.shape[0]
    m = b * s
    assert s % SC == 0 and d % LANES == 0
    tl = _tiles(m, s, d)
    topk = min(TOPK_MAX, s // 4)
    bias_near = _near_bias(rel_bias)
    row = lambda v: v.reshape(1, -1)

    ff1 = _pad_ff(w_ff1_gu, w_ff1_down, tl["ffn_tf"], tk=min(256, d))
    ff2 = _pad_ff(w_ff2_gu, w_ff2_down, tl["ffn_tf"], tk=min(256, d))
    w16, w32, wgates = _prep_w_in(w_in, d, tk=min(256, d))
    w_a16, w_i16 = w_a.astype(BF16), w_i.astype(BF16)
    w_mem16, w_branch16, w_out16 = w_mem_kv.astype(BF16), w_branch.astype(BF16), w_out.astype(BF16)
    ln_g = _pad_cols(idx_ln_g, LANES)
    ln_b = _pad_cols(idx_ln_b, LANES)

    x2 = x.reshape(m, d)
    mem2 = mem.reshape(b * nm, d)
    for l in range(depth):
        x2 = _ffn(x2, row(norm_ff1[l]), *ff1, row(final_norm), l, final_norm=False,
                  tm=tl["ffn_tm"], tf=tl["ffn_tf"])

        p16, p32 = _proj(x2, row(norm_mix[l]), w16, w32, l, tm=tl["proj_tm"])
        p16 = p16.reshape(b, s, -1)
        p32 = p32.reshape(b, s, -1)

        v_cols = slice(p16.shape[-1] - A_KV_HEADS * A_HEAD_DIM, p16.shape[-1])
        vt = p16[..., v_cols].reshape(b, s // SC, SC, -1).transpose(0, 1, 3, 2)
        vt_blk = p16[..., v_cols].reshape(b, s // QB, QB, -1).transpose(0, 1, 3, 2)
        y_a = _dsa(p16, p32, vt, vt_blk, row(ln_g[l]), row(ln_b[l]),
                   bias_near, topk=topk, far_unroll=4)
        y_b = _lru(p32, conv_w[l], row(conv_b[l]), w_a16, w_i16,
                   row(b_a[l]), row(b_i[l]), row(lam[l]), l, tc=tl["lru_tc"])
        mkv = _norm_matmul(mem2, row(mem_norm[l]), w_mem16, l, BF16,
                           tm=min(256, b * nm), tn=tl["proj_tn"], name="memkv")
        y_c = _memattn(p16, mkv.reshape(b, nm, -1), tm=tl["mem_tm"])

        x2 = _merge(x2, row(norm_mix[l]), y_a.reshape(m, -1), y_b.reshape(m, -1), y_c.reshape(m, -1),
                    wgates, w_branch16, w_out16, l,
                    tm=tl["merge_tm"], tn=tl["merge_tn"])

        is_last = l == depth - 1
        x2 = _ffn(x2, row(norm_ff2[l]), *ff2, row(final_norm), l, final_norm=is_last,
                  tm=tl["ffn_final_tm" if is_last else "ffn_tm"], tf=tl["ffn_tf"])
    return x2.reshape(b, s, d)
```

```python
import functools
import math

import jax
import jax.numpy as jnp
from jax import lax
from jax.experimental import pallas as pl
from jax.experimental.pallas import tpu as pltpu

F32 = jnp.float32
BF16 = jnp.bfloat16
I32 = jnp.int32

EPS = 1e-6
A_HEADS = 8
A_KV_HEADS = 2
A_HEAD_DIM = 128
A_GROUP = A_HEADS // A_KV_HEADS
IDX_HEADS = 4
IDX_DIM = 64
TOPK_MAX = 256
LRU_WIDTH = 1024
LRU_BLOCKS = 8
LRU_BLOCK_W = LRU_WIDTH // LRU_BLOCKS
CONV_WIDTH = 4
LRU_C = 8.0
X_HEADS = 4
X_HEAD_DIM = 256
N_BRANCH = 3
BRANCH_WIDTH = 1024
N_BUCKETS = 32
MAX_DISTANCE = 128

LANES = 128
SUBLANES = 8
VMEM_LIMIT = 56 * 1024 * 1024

QB = 128
SC = 512
SUB = SC // QB
NEG = -1e30
LOG2E = math.log2(math.e)
INT_MIN = -2 ** 31


def _as_i32(v):
    return v - (1 << 32) if v >= (1 << 31) else v


DIGIT_BITS = 7
DIGITS = 4
REST_BITS = 32 - DIGITS * DIGIT_BITS
DIGIT_MASK = (1 << DIGIT_BITS) - 1
DIGIT_HALF = 1 << (DIGIT_BITS - 1)
BYTE_ONES = _as_i32(0x01010101)
BYTE_LOW = _as_i32(0x7F7F7F7F)
BYTE_GUARDS = _as_i32(0x80808080)
BYTE_HALF = _as_i32(0x40404040)
assert SUB == 4 and DIGIT_BITS == 7

BOUND_SLACK = 1.02
BOUND_MARGIN = 0.5
BOUND_LIMIT = 55.0


def _params(*sem):
    return pltpu.CompilerParams(dimension_semantics=sem, vmem_limit_bytes=VMEM_LIMIT)


def _rms(x, g):
    ms = jnp.mean(x * x, axis=-1, keepdims=True)
    return x * lax.rsqrt(ms + EPS) * g


def _sigmoid(x):
    return 1.0 / (1.0 + jnp.exp(-x))


def _dot(a, b):
    return jnp.dot(a, b, preferred_element_type=F32)


def _dot_nt(a, b):
    return lax.dot_general(a, b, (((1,), (1,)), ((), ())), preferred_element_type=F32)


def _ffn_kernel(x_ref, g_ref, wg_ref, wu_ref, wd_ref, fg_ref, o_ref, h_ref, *, final_norm):
    f = pl.program_id(1)

    @pl.when(f == 0)
    def _():
        h_ref[...] = _rms(x_ref[...], g_ref[...]).astype(BF16)
        o_ref[...] = jnp.zeros_like(o_ref)

    h = h_ref[...]
    g = _dot(h, wg_ref[...])
    u = _dot(h, wu_ref[...])
    a = (g * _sigmoid(g) * u).astype(BF16)
    o_ref[...] += _dot(a, wd_ref[...])

    @pl.when(f == pl.num_programs(1) - 1)
    def _():
        y = x_ref[...] + 0.5 * o_ref[...]
        if final_norm:
            y = _rms(y, fg_ref[...])
        o_ref[...] = y


def _ffn(x2, g, wg, wu, wd, fg, layer, *, final_norm, tm, tf):
    m, d = x2.shape
    fp = wg.shape[-1]
    return pl.pallas_call(
        functools.partial(_ffn_kernel, final_norm=final_norm),
        out_shape=jax.ShapeDtypeStruct((m, d), F32),
        grid=(m // tm, fp // tf),
        in_specs=[
            pl.BlockSpec((tm, d), lambda i, f: (i, 0)),
            pl.BlockSpec((1, d), lambda i, f: (0, 0)),
            pl.BlockSpec((None, d, tf), lambda i, f: (layer, 0, f)),
            pl.BlockSpec((None, d, tf), lambda i, f: (layer, 0, f)),
            pl.BlockSpec((None, tf, d), lambda i, f: (layer, f, 0)),
            pl.BlockSpec((1, d), lambda i, f: (0, 0)),
        ],
        out_specs=pl.BlockSpec((tm, d), lambda i, f: (i, 0)),
        scratch_shapes=[pltpu.VMEM((tm, d), BF16)],
        compiler_params=_params("parallel", "arbitrary"),
        name="ffn",
    )(x2, g, wg, wu, wd, fg)


def _norm_matmul_kernel(x_ref, g_ref, w_ref, o_ref, h_ref):
    @pl.when(pl.program_id(1) == 0)
    def _():
        h_ref[...] = _rms(x_ref[...], g_ref[...]).astype(BF16)

    o_ref[...] = _dot(h_ref[...], w_ref[...]).astype(o_ref.dtype)


def _norm_matmul(x2, g, w, layer, out_dtype, *, tm, tn, name):
    m, d = x2.shape
    n = w.shape[-1]
    return pl.pallas_call(
        _norm_matmul_kernel,
        out_shape=jax.ShapeDtypeStruct((m, n), out_dtype),
        grid=(m // tm, n // tn),
        in_specs=[
            pl.BlockSpec((tm, d), lambda i, j: (i, 0)),
            pl.BlockSpec((1, d), lambda i, j: (0, 0)),
            pl.BlockSpec((None, d, tn), lambda i, j: (layer, 0, j)),
        ],
        out_specs=pl.BlockSpec((tm, tn), lambda i, j: (i, j)),
        scratch_shapes=[pltpu.VMEM((tm, d), BF16)],
        compiler_params=_params("parallel", "arbitrary"),
        name=name,
    )(x2, g, w)


def _proj_kernel(x_ref, g_ref, w16_ref, w32_ref, o16_ref, o32_ref):
    h = _rms(x_ref[...], g_ref[...]).astype(BF16)
    o16_ref[...] = _dot(h, w16_ref[...]).astype(o16_ref.dtype)
    o32_ref[...] = _dot(h, w32_ref[...])


def _proj(x2, g, w16, w32, layer, *, tm):
    m, d = x2.shape
    n16, n32 = w16.shape[-1], w32.shape[-1]
    resident = pl.Buffered(1)
    return pl.pallas_call(
        _proj_kernel,
        out_shape=(jax.ShapeDtypeStruct((m, n16), BF16), jax.ShapeDtypeStruct((m, n32), F32)),
        grid=(m // tm,),
        in_specs=[
            pl.BlockSpec((tm, d), lambda i: (i, 0)),
            pl.BlockSpec((1, d), lambda i: (0, 0)),
            pl.BlockSpec((None, d, n16), lambda i: (layer, 0, 0), pipeline_mode=resident),
            pl.BlockSpec((None, d, n32), lambda i: (layer, 0, 0), pipeline_mode=resident),
        ],
        out_specs=(pl.BlockSpec((tm, n16), lambda i: (i, 0)), pl.BlockSpec((tm, n32), lambda i: (i, 0))),
        compiler_params=_params("parallel"),
        name="proj",
    )(x2, g, w16, w32)


def _lru_kernel(x_ref, gate_ref, cw_ref, cb_ref, wa_ref, wi_ref, ba_ref, bi_ref, lam_ref,
                o_ref, xbuf_ref, hc_ref, *, tc):
    halo = SUBLANES

    @pl.when(pl.program_id(1) == 0)
    def _():
        xbuf_ref[0:halo, :] = jnp.zeros((halo, LRU_WIDTH), F32)
        hc_ref[...] = jnp.zeros_like(hc_ref)

    xbuf_ref[halo:halo + tc, :] = x_ref[0]
    base = halo - (CONV_WIDTH - 1)
    xc = xbuf_ref[base:base + tc, :] * cw_ref[0:1, :]
    for j in range(1, CONV_WIDTH):
        xc = xc + xbuf_ref[base + j:base + j + tc, :] * cw_ref[j:j + 1, :]
    xc = cb_ref[...] + xc
    xbuf_ref[0:halo, :] = xbuf_ref[tc:tc + halo, :]

    xcb = xc.astype(BF16)
    r_parts, i_parts = [], []
    for n in range(LRU_BLOCKS):
        blk = xcb[:, n * LRU_BLOCK_W:(n + 1) * LRU_BLOCK_W]
        r_parts.append(_dot(blk, wa_ref[n]))
        i_parts.append(_dot(blk, wi_ref[n]))
    r = _sigmoid(jnp.concatenate(r_parts, axis=1) + ba_ref[...])
    gi = _sigmoid(jnp.concatenate(i_parts, axis=1) + bi_ref[...])

    softplus_neg_lam = jnp.log1p(jnp.exp(-lam_ref[...]))
    log_a = (-LRU_C) * r * softplus_neg_lam
    a = jnp.exp(log_a)
    th = jnp.tanh(log_a)
    mult = jnp.sqrt(jnp.maximum(-2.0 * th / (1.0 - th), 0.0))
    u = mult * (gi * xc)

    row = lax.broadcasted_iota(I32, (tc, LRU_WIDTH), 0)
    d = 1
    while d < tc:
        keep = row >= d
        a_sh = jnp.where(keep, pltpu.roll(a, d, 0), 1.0)
        u_sh = jnp.where(keep, pltpu.roll(u, d, 0), 0.0)
        u = a * u_sh + u
        a = a * a_sh
        d *= 2
    h = a * hc_ref[...] + u
    hc_ref[...] = h[tc - 1:tc, :]

    gate = gate_ref[0]
    gelu = 0.5 * gate * (1.0 + jnp.tanh(math.sqrt(2.0 / math.pi) * (gate + 0.044715 * (gate * gate * gate))))
    o_ref[0] = (h * gelu).astype(o_ref.dtype)


def _lru(p32, conv_w, conv_b, w_a, w_i, b_a, b_i, lam, layer, *, tc):
    b, s, _ = p32.shape
    vec = pl.BlockSpec((1, LRU_WIDTH), lambda bi, c: (0, 0))
    wblk = pl.BlockSpec((None, LRU_BLOCKS, LRU_BLOCK_W, LRU_BLOCK_W), lambda bi, c: (layer, 0, 0, 0))
    return pl.pallas_call(
        functools.partial(_lru_kernel, tc=tc),
        out_shape=jax.ShapeDtypeStruct((b, s, LRU_WIDTH), BF16),
        grid=(b, s // tc),
        in_specs=[
            pl.BlockSpec((1, tc, LRU_WIDTH), lambda bi, c: (bi, c, 0)),
            pl.BlockSpec((1, tc, LRU_WIDTH), lambda bi, c: (bi, c, 1)),
            pl.BlockSpec((CONV_WIDTH, LRU_WIDTH), lambda bi, c: (0, 0)),
            vec, wblk, wblk, vec, vec, vec,
        ],
        out_specs=pl.BlockSpec((1, tc, LRU_WIDTH), lambda bi, c: (bi, c, 0)),
        scratch_shapes=[pltpu.VMEM((tc + SUBLANES, LRU_WIDTH), F32), pltpu.VMEM((1, LRU_WIDTH), F32)],
        compiler_params=_params("arbitrary", "arbitrary"),
        name="rglru",
    )(p32, p32, conv_w, conv_b, w_a, w_i, b_a, b_i, lam)


def _memattn_kernel(q_ref, kv_ref, o_ref):
    kw = X_HEADS * X_HEAD_DIM
    for h in range(X_HEADS):
        sl = slice(h * X_HEAD_DIM, (h + 1) * X_HEAD_DIM)
        q = q_ref[0, :, sl]
        k = kv_ref[0, :, sl]
        v = kv_ref[0, :, kw + h * X_HEAD_DIM:kw + (h + 1) * X_HEAD_DIM]
        s = _dot_nt(q, k)
        m = jnp.max(s, axis=-1, keepdims=True)
        p = jnp.exp(s - m)
        l = jnp.sum(p, axis=-1, keepdims=True)
        o = _dot(p.astype(BF16), v)
        o_ref[0, :, sl] = (o / l).astype(o_ref.dtype)


def _memattn(p16, mkv, *, tm):
    b, s, _ = p16.shape
    nm = mkv.shape[1]
    kw = X_HEADS * X_HEAD_DIM
    return pl.pallas_call(
        _memattn_kernel,
        out_shape=jax.ShapeDtypeStruct((b, s, kw), BF16),
        grid=(b, s // tm),
        in_specs=[
            pl.BlockSpec((1, tm, kw), lambda bi, i: (bi, i, 1)),
            pl.BlockSpec((1, nm, 2 * kw), lambda bi, i: (bi, 0, 0)),
        ],
        out_specs=pl.BlockSpec((1, tm, kw), lambda bi, i: (bi, i, 0)),
        compiler_params=_params("parallel", "parallel"),
        name="memattn",
    )(p16, mkv)


def _merge_kernel(x_ref, g_ref, ya_ref, yb_ref, yc_ref, wg0_ref, wg1_ref, wg2_ref, wb_ref, wo_ref,
                  o_ref, h_ref):
    n = pl.program_id(1)

    @pl.when(n == 0)
    def _():
        h_ref[...] = _rms(x_ref[...], g_ref[...]).astype(BF16)
        o_ref[...] = jnp.zeros_like(o_ref)

    h = h_ref[...]
    merged = None
    for j, (y_ref, wg_ref) in enumerate(((ya_ref, wg0_ref), (yb_ref, wg1_ref), (yc_ref, wg2_ref))):
        term = _sigmoid(_dot(h, wg_ref[...])) * _dot(y_ref[...], wb_ref[j])
        merged = term if merged is None else merged + term
    o_ref[...] += _dot(merged.astype(BF16), wo_ref[...])

    @pl.when(n == pl.num_programs(1) - 1)
    def _():
        o_ref[...] = x_ref[...] + o_ref[...]


def _merge(x2, g, ya, yb, yc, wgates, wbranch, wout, layer, *, tm, tn):
    m, d = x2.shape
    nblk = d // tn
    yspec = pl.BlockSpec((tm, BRANCH_WIDTH), lambda i, n: (i, 0))
    return pl.pallas_call(
        _merge_kernel,
        out_shape=jax.ShapeDtypeStruct((m, d), F32),
        grid=(m // tm, nblk),
        in_specs=[
            pl.BlockSpec((tm, d), lambda i, n: (i, 0)),
            pl.BlockSpec((1, d), lambda i, n: (0, 0)),
            yspec, yspec, yspec,
            pl.BlockSpec((None, d, tn), lambda i, n: (layer, 0, n)),
            pl.BlockSpec((None, d, tn), lambda i, n: (layer, 0, nblk + n)),
            pl.BlockSpec((None, d, tn), lambda i, n: (layer, 0, 2 * nblk + n)),
            pl.BlockSpec((None, N_BRANCH, BRANCH_WIDTH, tn), lambda i, n: (layer, 0, 0, n)),
            pl.BlockSpec((None, tn, d), lambda i, n: (layer, n, 0)),
        ],
        out_specs=pl.BlockSpec((tm, d), lambda i, n: (i, 0)),
        scratch_shapes=[pltpu.VMEM((tm, d), BF16)],
        compiler_params=_params("parallel", "arbitrary"),
        name="merge",
    )(x2, g, ya, yb, yc, wgates, wgates, wgates, wbranch, wout)


def _dsa_kernel(q_ref, k_ref, vt_ref, vtp_ref, vtd_ref, qi_ref, ki_ref, wi_ref, lng_ref, lnb_ref, bias_ref, bmax_ref,
                o_ref,
                kln_ref, knorm_ref, keys_ref, dig_ref, cls_ref, madd_ref, mnear_ref, qi4_ref, q4_ref, m_ref, l_ref, acc_ref,
                *, topk, far_unroll):
    i = pl.program_id(1)
    s_len = k_ref.shape[1]

    @pl.when(i == 0)
    def _():
        rows = 512 if s_len % 512 == 0 else QB
        lane = lax.broadcasted_iota(I32, (rows, LANES), 1)
        real = lane < IDX_DIM

        def ln_body(c, carry):
            x = ki_ref[0, pl.ds(c * rows, rows), :]
            mu = jnp.sum(x, axis=-1, keepdims=True) * (1.0 / IDX_DIM)
            xm = jnp.where(real, x - mu, 0.0)
            var = jnp.sum(xm * xm, axis=-1, keepdims=True) * (1.0 / IDX_DIM)
            y = xm * lax.rsqrt(var + EPS) * lng_ref[...] + lnb_ref[...]
            kln_ref[pl.ds(c * rows, rows), :] = y.astype(BF16)
            kf = k_ref[0, pl.ds(c * rows, rows), :].astype(F32)
            sq = [jnp.sum(kf[:, n * A_HEAD_DIM:(n + 1) * A_HEAD_DIM] ** 2, axis=-1, keepdims=True)
                  for n in range(A_KV_HEADS)]
            return tuple(jnp.maximum(a, b) for a, b in zip(carry, sq))

        ksq = lax.fori_loop(0, s_len // rows, ln_body, (jnp.zeros((rows, 1), F32),) * A_KV_HEADS)
        for n in range(A_KV_HEADS):
            knorm_ref[n] = jnp.broadcast_to(jnp.sqrt(jnp.max(ksq[n], axis=0, keepdims=True)), (1, LANES))

    w_rows = wi_ref[0].T * (IDX_HEADS ** -0.5 * IDX_DIM ** -0.5)
    for h in range(IDX_HEADS):
        qi4_ref[h * QB:(h + 1) * QB, :] = qi_ref[0, :, h * LANES:(h + 1) * LANES].astype(BF16)
    eye = (lax.broadcasted_iota(I32, (QB, QB), 0) == lax.broadcasted_iota(I32, (QB, QB), 1)).astype(BF16)
    for n in range(A_KV_HEADS):
        for g in range(A_GROUP):
            hd = n * A_GROUP + g
            q4_ref[n, g * QB:(g + 1) * QB, 0:A_HEAD_DIM] = q_ref[0, :, hd * A_HEAD_DIM:(hd + 1) * A_HEAD_DIM]
            q4_ref[n, g * QB:(g + 1) * QB, A_HEAD_DIM:A_HEAD_DIM + QB] = eye

    qf = q_ref[0].astype(F32)
    head_of_col = lax.broadcasted_iota(I32, (A_HEADS * A_HEAD_DIM, LANES), 0) // A_HEAD_DIM
    seg = (head_of_col == lax.broadcasted_iota(I32, (A_HEADS * A_HEAD_DIM, LANES), 1)).astype(BF16)
    qnorm = jnp.sqrt(_dot((qf * qf).astype(BF16), seg)).T
    bound = None
    for hd in range(A_HEADS):
        bh = qnorm[hd:hd + 1, :] * knorm_ref[hd // A_GROUP]
        bound = bh if bound is None else jnp.maximum(bound, bh)
    bound = bound * BOUND_SLACK + bmax_ref[...]
    neg_bound = -bound
    bounded = jnp.max(bound) <= BOUND_LIMIT

    last = i // SUB
    nsc = last + 1

    def rows(j):
        return pl.ds(pl.multiple_of(j * SC, SC), SC)

    def packed_digits(j, shift):
        word = None
        for blk in range(SUB):
            key = keys_ref[pl.ds(pl.multiple_of(j * SC + blk * QB, QB), QB), :]
            s = shift - 8 * blk
            f = lax.shift_right_arithmetic(key, jnp.int32(s)) if s >= 0 else lax.shift_left(key, jnp.int32(-s))
            f = f & jnp.int32(_as_i32(DIGIT_MASK << (8 * blk)))
            word = f if word is None else word | f
        return word

    def word_rows(j):
        return pl.ds(pl.multiple_of(j * QB, QB), QB)

    def index_dots(j):
        return _dot_nt(kln_ref[rows(j), :], qi4_ref[...])

    def score_chunk(j, dots, causal):
        sc = None
        for h in range(IDX_HEADS):
            t = w_rows[h:h + 1, :] * jnp.maximum(dots[:, h * QB:(h + 1) * QB], 0.0)
            sc = t if sc is None else sc + t
        bits = pltpu.bitcast(sc, I32)
        key = jnp.where(bits < 0, bits ^ jnp.int32(0x7FFFFFFF), bits)
        key = jnp.where(sc == 0.0, 0, key)
        if causal:
            key_pos = j * SC + lax.broadcasted_iota(I32, (SC, LANES), 0)
            q_pos = i * QB + lax.broadcasted_iota(I32, (SC, LANES), 1)
            key = jnp.where(key_pos <= q_pos, key, INT_MIN)
        keys_ref[rows(j), :] = key
        top = packed_digits(j, 32 - DIGIT_BITS) ^ BYTE_HALF
        dig_ref[word_rows(j), :] = top | BYTE_GUARDS
        cls_ref[word_rows(j), :] = jnp.full((QB, LANES), BYTE_LOW, I32)

    def score_body(pr, carry):
        d0 = index_dots(2 * pr)
        d1 = index_dots(2 * pr + 1)
        score_chunk(2 * pr, d0, False)
        score_chunk(2 * pr + 1, d1, False)
        return carry

    lax.fori_loop(0, last // 2, score_body, 0)

    @pl.when(last % 2 == 1)
    def _():
        score_chunk(last - 1, index_dots(last - 1), False)

    score_chunk(last, index_dots(last), True)

    t_pos = i * QB + lax.broadcasted_iota(I32, (1, LANES), 1)
    kk = jnp.minimum(t_pos + 1, topk).astype(F32)

    def count_ge(cand):
        def body(j, cnt):
            for r in range(SUB):
                blk = keys_ref[pl.ds(pl.multiple_of(j * SC + r * QB, QB), QB), :]
                cnt = cnt + jnp.where(blk >= cand, 1.0, 0.0)
            return cnt

        cnt = lax.fori_loop(0, nsc, body, jnp.zeros((QB, LANES), F32))
        return jnp.sum(cnt, axis=0, keepdims=True)

    def count_digit_ge(cand):
        packed = cand * BYTE_ONES

        def body(j, cnt):
            d = dig_ref[word_rows(j), :] - packed
            return cnt + lax.population_count(d & BYTE_GUARDS)

        cnt = lax.fori_loop(0, nsc, body, jnp.zeros((QB, LANES), I32))
        return jnp.sum(cnt, axis=0, keepdims=True).astype(F32)

    def select_digit(above):
        def bit_body(bi, t):
            cand = t + lax.shift_left(jnp.int32(1), DIGIT_BITS - 1 - bi)
            return jnp.where(above + count_digit_ge(cand) >= kk, cand, t)

        t = lax.fori_loop(0, DIGIT_BITS, bit_body, jnp.zeros((1, LANES), I32))
        greater = count_digit_ge(jnp.minimum(t + 1, DIGIT_MASK))
        return t, above + jnp.where(t == DIGIT_MASK, 0.0, greater)

    thr = jnp.zeros((1, LANES), I32)
    above = jnp.zeros((1, LANES), F32)
    for level in range(DIGITS):
        shift = 32 - DIGIT_BITS * (level + 1)
        if level > 0:
            t_packed = t * BYTE_ONES

            def next_digit(j, carry, shift=shift, t_packed=t_packed):
                z = (dig_ref[word_rows(j), :] & BYTE_LOW) ^ t_packed
                same = ((z + BYTE_LOW) & BYTE_GUARDS) ^ BYTE_GUARDS
                cls = cls_ref[word_rows(j), :] & (same - lax.shift_right_logical(same, jnp.int32(7)))
                cls_ref[word_rows(j), :] = cls
                dig_ref[word_rows(j), :] = (packed_digits(j, shift) & cls) | BYTE_GUARDS
                return carry

            lax.fori_loop(0, nsc, next_digit, 0)
        t, above = select_digit(above)
        thr = thr + lax.shift_left(t - (DIGIT_HALF if level == 0 else 0), jnp.int32(shift))

    def bit_body(bi, t):
        cand = t + lax.shift_left(jnp.int32(1), REST_BITS - 1 - bi)
        return jnp.where(count_ge(cand) >= kk, cand, t)

    thr = lax.fori_loop(0, REST_BITS, bit_body, thr)
    need = kk - count_ge(thr + 1)
    sub_k = lax.broadcasted_iota(I32, (QB, QB), 0)
    sub_k2 = lax.broadcasted_iota(I32, (QB, QB), 1)
    tril = (sub_k2 <= sub_k).astype(BF16)

    def mask_body(j, seen):
        keys, eqs, pres = [], [], []
        for r in range(SUB):
            key = keys_ref[pl.ds(pl.multiple_of(j * SC + r * QB, QB), QB), :]
            eq = key == thr
            keys.append(key)
            eqs.append(eq)
            pres.append(_dot(tril, jnp.where(eq, 1.0, 0.0).astype(BF16)))
        for r in range(SUB):
            sel = (keys[r] > thr) | (eqs[r] & ((seen + pres[r]) <= need))
            madd_ref[pl.ds(pl.multiple_of(j * SC + r * QB, QB), QB), :] = jnp.where(sel, neg_bound, NEG).astype(BF16)
            seen = seen + pres[r][QB - 1:QB, :]
        return seen

    lax.fori_loop(0, nsc, mask_body, jnp.zeros((1, LANES), F32))

    def split_near(slot, blk):
        blk_rows = pl.ds(pl.multiple_of(blk * QB, QB), QB)
        mnear_ref[slot] = madd_ref[blk_rows, :]
        madd_ref[blk_rows, :] = jnp.full((QB, LANES), NEG, BF16)

    split_near(1, i)

    @pl.when(i >= 1)
    def _():
        split_near(0, i - 1)

    m_ref[...] = jnp.full(m_ref.shape, NEG, F32)
    l_ref[...] = jnp.zeros(l_ref.shape, F32)
    acc_ref[...] = jnp.zeros(acc_ref.shape, F32)

    def logits(kc, madd, n):
        return _dot_nt(jnp.concatenate([kc, madd], axis=1), q4_ref[n])

    def far_logits(j, n):
        return logits(k_ref[0, rows(j), n * A_HEAD_DIM:(n + 1) * A_HEAD_DIM], madd_ref[rows(j), :], n)

    def online_update(n, s, vt):
        m_prev = m_ref[n]
        m_new = jnp.maximum(m_prev, jnp.max(s, axis=0, keepdims=True))
        alpha = jnp.exp2(m_prev - m_new)
        p = jnp.exp2(s - m_new)
        l_ref[n] = alpha * l_ref[n] + jnp.sum(p, axis=0, keepdims=True)
        acc_ref[n] = alpha * acc_ref[n] + _dot(vt, p.astype(BF16))
        m_ref[n] = m_new

    def bounded_update(n, s, vt):
        p = jnp.exp2(s)
        l_ref[n] += jnp.sum(p, axis=0, keepdims=True)
        acc_ref[n] += _dot(vt, p.astype(BF16))

    def far_vt(j, n):
        return vt_ref[0, j, n * A_HEAD_DIM:(n + 1) * A_HEAD_DIM, :]

    n_far = jnp.where(i % SUB >= 2, nsc, last)

    def attend(update):
        def sweep(first, count):
            tasks = [(first + c, n) for c in range(count) for n in range(A_KV_HEADS)]
            ahead = 4
            pending = [far_logits(j, n) for j, n in tasks[:ahead]]
            for t, (j, n) in enumerate(tasks):
                update(n, pending.pop(0), far_vt(j, n))
                if t + ahead < len(tasks):
                    pending.append(far_logits(*tasks[t + ahead]))

        def sweep_body(g, carry):
            sweep(far_unroll * g, far_unroll)
            return carry

        lax.fori_loop(0, n_far // far_unroll, sweep_body, 0)
        for rem in range(1, far_unroll):
            @pl.when(n_far % far_unroll == rem)
            def _(rem=rem):
                sweep(n_far - rem, rem)

        def near_step(slot, blk, vts_ref):
            blk_rows = pl.ds(pl.multiple_of(blk * QB, QB), QB)
            ss = [logits(k_ref[0, blk_rows, n * A_HEAD_DIM:(n + 1) * A_HEAD_DIM], mnear_ref[slot], n)
                  + bias_ref[slot, n] for n in range(A_KV_HEADS)]
            for n in range(A_KV_HEADS):
                update(n, ss[n], vts_ref[0, 0, n * A_HEAD_DIM:(n + 1) * A_HEAD_DIM, :])

        @pl.when(i >= 1)
        def _():
            near_step(0, i - 1, vtp_ref)

        near_step(1, i, vtd_ref)

    @pl.when(bounded)
    def _():
        attend(bounded_update)

    @pl.when(jnp.logical_not(bounded))
    def _():
        attend(online_update)

    for n in range(A_KV_HEADS):
        out_t = acc_ref[n] / l_ref[n]
        for g in range(A_GROUP):
            hd = n * A_GROUP + g
            o_ref[0, :, hd * A_HEAD_DIM:(hd + 1) * A_HEAD_DIM] = out_t[:, g * QB:(g + 1) * QB].T.astype(o_ref.dtype)


def _dsa(p16, p32, vt, vt_blk, ln_g, ln_b, bias_near, *, topk, far_unroll):
    b, s, _ = p16.shape
    bias_max = jnp.broadcast_to(jnp.max(jnp.abs(bias_near)) + BOUND_MARGIN, (1, LANES)).astype(F32)
    aq = A_HEADS * A_HEAD_DIM
    akv = A_KV_HEADS * A_HEAD_DIM
    qiw = IDX_HEADS * LANES
    nch = s // QB
    gq = A_GROUP * QB
    k_blk = (aq + X_HEADS * X_HEAD_DIM) // akv
    qi_blk = 2 * LRU_WIDTH // qiw
    ki_blk = (2 * LRU_WIDTH + qiw) // LANES
    return pl.pallas_call(
        functools.partial(_dsa_kernel, topk=topk, far_unroll=far_unroll),
        out_shape=jax.ShapeDtypeStruct((b, s, aq), BF16),
        grid=(b, nch),
        in_specs=[
            pl.BlockSpec((1, QB, aq), lambda bi, i: (bi, i, 0)),
            pl.BlockSpec((1, s, akv), lambda bi, i: (bi, 0, k_blk)),
            pl.BlockSpec((1, s // SC, akv, SC), lambda bi, i: (bi, 0, 0, 0)),
            pl.BlockSpec((1, 1, akv, QB), lambda bi, i: (bi, jnp.maximum(i - 1, 0), 0, 0)),
            pl.BlockSpec((1, 1, akv, QB), lambda bi, i: (bi, i, 0, 0)),
            pl.BlockSpec((1, QB, qiw), lambda bi, i: (bi, i, qi_blk)),
            pl.BlockSpec((1, s, LANES), lambda bi, i: (bi, 0, ki_blk)),
            pl.BlockSpec((1, QB, LANES), lambda bi, i: (bi, i, ki_blk + 1)),
            pl.BlockSpec((1, LANES), lambda bi, i: (0, 0)),
            pl.BlockSpec((1, LANES), lambda bi, i: (0, 0)),
            pl.BlockSpec((2, A_KV_HEADS, QB, gq), lambda bi, i: (0, 0, 0, 0)),
            pl.BlockSpec((1, LANES), lambda bi, i: (0, 0)),
        ],
        out_specs=pl.BlockSpec((1, QB, aq), lambda bi, i: (bi, i, 0)),
        scratch_shapes=[
            pltpu.VMEM((s, LANES), BF16),
            pltpu.VMEM((A_KV_HEADS, 1, LANES), F32),
            pltpu.VMEM((s, LANES), I32),
            pltpu.VMEM((s // SUB, LANES), I32),
            pltpu.VMEM((s // SUB, LANES), I32),
            pltpu.VMEM((s, LANES), BF16),
            pltpu.VMEM((2, QB, LANES), BF16),
            pltpu.VMEM((IDX_HEADS * QB, LANES), BF16),
            pltpu.VMEM((A_KV_HEADS, gq, A_HEAD_DIM + QB), BF16),
            pltpu.VMEM((A_KV_HEADS, 1, gq), F32),
            pltpu.VMEM((A_KV_HEADS, 1, gq), F32),
            pltpu.VMEM((A_KV_HEADS, A_HEAD_DIM, gq), F32),
        ],
        compiler_params=_params("arbitrary", "arbitrary"),
        name="dsa",
    )(p16, p16, vt, vt_blk, vt_blk, p32, p32, p32, ln_g, ln_b, bias_near, bias_max)


def _t5_bucket(dist):
    max_exact = N_BUCKETS // 2
    d = jnp.maximum(dist, 0)
    df = jnp.maximum(d, 1).astype(F32)
    large = max_exact + (jnp.log(df / max_exact) / math.log(MAX_DISTANCE / max_exact)
                         * (N_BUCKETS - max_exact)).astype(I32)
    large = jnp.minimum(large, N_BUCKETS - 1)
    return jnp.where(d < max_exact, d, large)


def _near_bias(rel_bias):
    t = jnp.arange(QB, dtype=I32)[:, None]
    s = jnp.arange(QB, dtype=I32)[None, :]
    tiles = []
    for off in (QB, 0):
        bucket = _t5_bucket(t - s + off)
        onehot = (bucket[..., None] == jnp.arange(N_BUCKETS, dtype=I32)).astype(F32)
        table = (rel_bias - rel_bias[N_BUCKETS - 1]) * LOG2E
        tile = jnp.einsum('tsb,bh->tsh', onehot, table, precision=lax.Precision.HIGHEST)
        tile = tile.transpose(1, 2, 0).reshape(QB, A_KV_HEADS, A_GROUP * QB)
        tiles.append(tile.transpose(1, 0, 2))
    return jnp.stack(tiles).astype(F32)


def _pad_cols(w, width):
    return jnp.pad(w, [(0, 0)] * (w.ndim - 1) + [(0, width - w.shape[-1])])


def _split_points(d_model):
    aq = A_HEADS * A_HEAD_DIM
    akv = A_KV_HEADS * A_HEAD_DIM
    sizes = (aq, akv, akv, IDX_HEADS * IDX_DIM, IDX_DIM, IDX_HEADS,
             LRU_WIDTH, LRU_WIDTH, X_HEADS * X_HEAD_DIM, N_BRANCH * d_model)
    pts, acc = [0], 0
    for n in sizes:
        acc += n
        pts.append(acc)
    return pts


def _regroup_kernel(w_ref, o16_ref, o32_ref, og_ref, *, pts):
    (q0, k0, v0, qi0, ki0, wi0, x0, g0, qm0, gt0, end) = pts

    def seg(lo, hi, scale=None):
        v = w_ref[0, :, lo:hi]
        if scale is not None:
            v = v * scale
        return v.astype(BF16)

    o16_ref[0, :, 0:k0 - q0] = seg(q0, k0, A_HEAD_DIM ** -0.5 * LOG2E)
    c = k0 - q0
    o16_ref[0, :, c:c + gt0 - qm0] = seg(qm0, gt0, X_HEAD_DIM ** -0.5)
    c += gt0 - qm0
    o16_ref[0, :, c:c + qi0 - k0] = seg(k0, qi0)
    o32_ref[0, :, 0:qm0 - x0] = seg(x0, qm0)
    c = qm0 - x0
    o32_ref[0, :, c:] = jnp.zeros((o32_ref.shape[1], o32_ref.shape[2] - c), BF16)
    for h in range(IDX_HEADS):
        o32_ref[0, :, c + h * LANES:c + h * LANES + IDX_DIM] = seg(qi0 + h * IDX_DIM, qi0 + (h + 1) * IDX_DIM)
    c += IDX_HEADS * LANES
    o32_ref[0, :, c:c + IDX_DIM] = seg(ki0, wi0)
    c += LANES
    o32_ref[0, :, c:c + IDX_HEADS] = seg(wi0, x0)
    og_ref[0] = seg(gt0, end)


def _prep_w_in(w_in, d_model, *, tk):
    depth, k, n = w_in.shape
    pts = _split_points(d_model)
    n16 = pts[3] + pts[9] - pts[8]
    n32 = 2 * LRU_WIDTH + (IDX_HEADS + 2) * LANES
    ng = n - pts[9]
    return pl.pallas_call(
        functools.partial(_regroup_kernel, pts=tuple(pts)),
        out_shape=(jax.ShapeDtypeStruct((depth, k, n16), BF16), jax.ShapeDtypeStruct((depth, k, n32), BF16),
                   jax.ShapeDtypeStruct((depth, k, ng), BF16)),
        grid=(depth, k // tk),
        in_specs=[pl.BlockSpec((1, tk, n), lambda l, i: (l, i, 0))],
        out_specs=(pl.BlockSpec((1, tk, n16), lambda l, i: (l, i, 0)),
                   pl.BlockSpec((1, tk, n32), lambda l, i: (l, i, 0)),
                   pl.BlockSpec((1, tk, ng), lambda l, i: (l, i, 0))),
        compiler_params=_params("parallel", "parallel"),
        name="regroup",
    )(w_in)


def _split_gu_kernel(w_ref, og_ref, ou_ref):
    d_ff = w_ref.shape[2] // 2
    pad = og_ref.shape[2] - d_ff
    for o_ref, lo in ((og_ref, 0), (ou_ref, d_ff)):
        o_ref[0, :, 0:d_ff] = w_ref[0, :, lo:lo + d_ff].astype(BF16)
        if pad:
            o_ref[0, :, d_ff:] = jnp.zeros((o_ref.shape[1], pad), BF16)


def _pad_down_kernel(w_ref, o_ref, *, n_real):
    @pl.when(pl.program_id(1) < n_real)
    def _():
        o_ref[...] = w_ref[...].astype(BF16)

    @pl.when(pl.program_id(1) >= n_real)
    def _():
        o_ref[...] = jnp.zeros_like(o_ref)


def _pad_ff(w_gu, w_down, tf, *, tk):
    depth, d, _ = w_gu.shape
    d_ff = w_down.shape[1]
    fp = -(-d_ff // tf) * tf
    assert d_ff % LANES == 0
    wg, wu = pl.pallas_call(
        _split_gu_kernel,
        out_shape=(jax.ShapeDtypeStruct((depth, d, fp), BF16),) * 2,
        grid=(depth, d // tk),
        in_specs=[pl.BlockSpec((1, tk, 2 * d_ff), lambda l, i: (l, i, 0))],
        out_specs=(pl.BlockSpec((1, tk, fp), lambda l, i: (l, i, 0)),) * 2,
        compiler_params=_params("parallel", "parallel"),
        name="split_gu",
    )(w_gu)
    n_real = d_ff // LANES
    wd = pl.pallas_call(
        functools.partial(_pad_down_kernel, n_real=n_real),
        out_shape=jax.ShapeDtypeStruct((depth, fp, d), BF16),
        grid=(depth, fp // LANES),
        in_specs=[pl.BlockSpec((1, LANES, d), lambda l, i: (l, jnp.minimum(i, n_real - 1), 0))],
        out_specs=pl.BlockSpec((1, LANES, d), lambda l, i: (l, i, 0)),
        compiler_params=_params("parallel", "arbitrary"),
        name="pad_down",
    )(w_down)
    return wg, wu, wd


def _tiles(m, s, d_model):
    return dict(
        ffn_tm=min(1024, m), ffn_final_tm=min(512, m), ffn_tf=512,
        proj_tm=min(256, m), proj_tn=256,
        merge_tm=min(512, m), merge_tn=min(512, d_model),
        lru_tc=min(256, s), mem_tm=min(512, s),
    )


def kernel(x, mem, rel_bias, final_norm, norm_ff1, w_ff1_gu, w_ff1_down, norm_mix, w_in, conv_w, conv_b, w_a, b_a, w_i, b_i, lam, idx_ln_g, idx_ln_b, mem_norm, w_mem_kv, w_branch, w_out, norm_ff2, w_ff2_gu, w_ff2_down):
    b, s, d = x.shape
    nm = mem.shape[1]
    depth = w_in.shape[0]
    m = b * s
    assert s % SC == 0 and d % LANES == 0
    tl = _tiles(m, s, d)
    topk = min(TOPK_MAX, s // 4)
    bias_near = _near_bias(rel_bias)
    row = lambda v: v.reshape(1, -1)

    ff1 = _pad_ff(w_ff1_gu, w_ff1_down, tl["ffn_tf"], tk=min(256, d))
    ff2 = _pad_ff(w_ff2_gu, w_ff2_down, tl["ffn_tf"], tk=min(256, d))
    w16, w32, wgates = _prep_w_in(w_in, d, tk=min(256, d))
    w_a16, w_i16 = w_a.astype(BF16), w_i.astype(BF16)
    w_mem16, w_branch16, w_out16 = w_mem_kv.astype(BF16), w_branch.astype(BF16), w_out.astype(BF16)
    ln_g = _pad_cols(idx_ln_g, LANES)
    ln_b = _pad_cols(idx_ln_b, LANES)

    x2 = x.reshape(m, d)
    mem2 = mem.reshape(b * nm, d)
    for l in range(depth):
        x2 = _ffn(x2, row(norm_ff1[l]), *ff1, row(final_norm), l, final_norm=False,
                  tm=tl["ffn_tm"], tf=tl["ffn_tf"])

        p16, p32 = _proj(x2, row(norm_mix[l]), w16, w32, l, tm=tl["proj_tm"])
        p16 = p16.reshape(b, s, -1)
        p32 = p32.reshape(b, s, -1)

        v_cols = slice(p16.shape[-1] - A_KV_HEADS * A_HEAD_DIM, p16.shape[-1])
        vt = p16[..., v_cols].reshape(b, s // SC, SC, -1).transpose(0, 1, 3, 2)
        vt_blk = p16[..., v_cols].reshape(b, s // QB, QB, -1).transpose(0, 1, 3, 2)
        y_a = _dsa(p16, p32, vt, vt_blk, row(ln_g[l]), row(ln_b[l]),
                   bias_near, topk=topk, far_unroll=4)
        y_b = _lru(p32, conv_w[l], row(conv_b[l]), w_a16, w_i16,
                   row(b_a[l]), row(b_i[l]), row(lam[l]), l, tc=tl["lru_tc"])
        mkv = _norm_matmul(mem2, row(mem_norm[l]), w_mem16, l, BF16,
                           tm=min(256, b * nm), tn=tl["proj_tn"], name="memkv")
        y_c = _memattn(p16, mkv.reshape(b, nm, -1), tm=tl["mem_tm"])

        x2 = _merge(x2, row(norm_mix[l]), y_a.reshape(m, -1), y_b.reshape(m, -1), y_c.reshape(m, -1),
                    wgates, w_branch16, w_out16, l,
                    tm=tl["merge_tm"], tn=tl["merge_tn"])

        is_last = l == depth - 1
        x2 = _ffn(x2, row(norm_ff2[l]), *ff2, row(final_norm), l, final_norm=is_last,
                  tm=tl["ffn_final_tm" if is_last else "ffn_tm"], tf=tl["ffn_tf"])
    return x2.reshape(b, s, d)
```

```python
import functools
import math

import jax
import jax.numpy as jnp
from jax import lax
from jax.experimental import pallas as pl
from jax.experimental.pallas import tpu as pltpu

F32 = jnp.float32
BF16 = jnp.bfloat16
I32 = jnp.int32

EPS = 1e-6
A_HEADS = 8
A_KV_HEADS = 2
A_HEAD_DIM = 128
A_GROUP = A_HEADS // A_KV_HEADS
IDX_HEADS = 4
IDX_DIM = 64
TOPK_MAX = 256
LRU_WIDTH = 1024
LRU_BLOCKS = 8
LRU_BLOCK_W = LRU_WIDTH // LRU_BLOCKS
CONV_WIDTH = 4
LRU_C = 8.0
X_HEADS = 4
X_HEAD_DIM = 256
N_BRANCH = 3
BRANCH_WIDTH = 1024
N_BUCKETS = 32
MAX_DISTANCE = 128

LANES = 128
SUBLANES = 8
VMEM_LIMIT = 56 * 1024 * 1024

QB = 128
SC = 512
SUB = SC // QB
NEG = -1e30
LOG2E = math.log2(math.e)
INT_MIN = -2 ** 31


def _as_i32(v):
    return v - (1 << 32) if v >= (1 << 31) else v


DIGIT_BITS = 7
DIGITS = 4
REST_BITS = 32 - DIGITS * DIGIT_BITS
DIGIT_MASK = (1 << DIGIT_BITS) - 1
DIGIT_HALF = 1 << (DIGIT_BITS - 1)
BYTE_ONES = _as_i32(0x01010101)
BYTE_LOW = _as_i32(0x7F7F7F7F)
BYTE_GUARDS = _as_i32(0x80808080)
BYTE_HALF = _as_i32(0x40404040)
assert SUB == 4 and DIGIT_BITS == 7

BOUND_SLACK = 1.02
BOUND_MARGIN = 0.5
BOUND_LIMIT = 55.0


def _params(*sem):
    return pltpu.CompilerParams(dimension_semantics=sem, vmem_limit_bytes=VMEM_LIMIT)


def _rms(x, g):
    ms = jnp.mean(x * x, axis=-1, keepdims=True)
    return x * lax.rsqrt(ms + EPS) * g


def _sigmoid(x):
    return 1.0 / (1.0 + jnp.exp(-x))


def _dot(a, b):
    return jnp.dot(a, b, preferred_element_type=F32)


def _dot_nt(a, b):
    return lax.dot_general(a, b, (((1,), (1,)), ((), ())), preferred_element_type=F32)


def _ffn_kernel(x_ref, g_ref, wg_ref, wu_ref, wd_ref, fg_ref, o_ref, h_ref, *, final_norm):
    f = pl.program_id(1)

    @pl.when(f == 0)
    def _():
        h_ref[...] = _rms(x_ref[...], g_ref[...]).astype(BF16)
        o_ref[...] = jnp.zeros_like(o_ref)

    h = h_ref[...]
    g = _dot(h, wg_ref[...])
    u = _dot(h, wu_ref[...])
    a = (g * _sigmoid(g) * u).astype(BF16)
    o_ref[...] += _dot(a, wd_ref[...])

    @pl.when(f == pl.num_programs(1) - 1)
    def _():
        y = x_ref[...] + 0.5 * o_ref[...]
        if final_norm:
            y = _rms(y, fg_ref[...])
        o_ref[...] = y


def _ffn(x2, g, wg, wu, wd, fg, layer, *, final_norm, tm, tf):
    m, d = x2.shape
    fp = wg.shape[-1]
    return pl.pallas_call(
        functools.partial(_ffn_kernel, final_norm=final_norm),
        out_shape=jax.ShapeDtypeStruct((m, d), F32),
        grid=(m // tm, fp // tf),
        in_specs=[
            pl.BlockSpec((tm, d), lambda i, f: (i, 0)),
            pl.BlockSpec((1, d), lambda i, f: (0, 0)),
            pl.BlockSpec((None, d, tf), lambda i, f: (layer, 0, f)),
            pl.BlockSpec((None, d, tf), lambda i, f: (layer, 0, f)),
            pl.BlockSpec((None, tf, d), lambda i, f: (layer, f, 0)),
            pl.BlockSpec((1, d), lambda i, f: (0, 0)),
        ],
        out_specs=pl.BlockSpec((tm, d), lambda i, f: (i, 0)),
        scratch_shapes=[pltpu.VMEM((tm, d), BF16)],
        compiler_params=_params("parallel", "arbitrary"),
        name="ffn",
    )(x2, g, wg, wu, wd, fg)


def _norm_matmul_kernel(x_ref, g_ref, w_ref, o_ref, h_ref):
    @pl.when(pl.program_id(1) == 0)
    def _():
        h_ref[...] = _rms(x_ref[...], g_ref[...]).astype(BF16)

    o_ref[...] = _dot(h_ref[...], w_ref[...]).astype(o_ref.dtype)


def _norm_matmul(x2, g, w, layer, out_dtype, *, tm, tn, name):
    m, d = x2.shape
    n = w.shape[-1]
    return pl.pallas_call(
        _norm_matmul_kernel,
        out_shape=jax.ShapeDtypeStruct((m, n), out_dtype),
        grid=(m // tm, n // tn),
        in_specs=[
            pl.BlockSpec((tm, d), lambda i, j: (i, 0)),
            pl.BlockSpec((1, d), lambda i, j: (0, 0)),
            pl.BlockSpec((None, d, tn), lambda i, j: (layer, 0, j)),
        ],
        out_specs=pl.BlockSpec((tm, tn), lambda i, j: (i, j)),
        scratch_shapes=[pltpu.VMEM((tm, d), BF16)],
        compiler_params=_params("parallel", "arbitrary"),
        name=name,
    )(x2, g, w)


def _proj_kernel(x_ref, g_ref, w16_ref, w32_ref, o16_ref, o32_ref):
    h = _rms(x_ref[...], g_ref[...]).astype(BF16)
    o16_ref[...] = _dot(h, w16_ref[...]).astype(o16_ref.dtype)
    o32_ref[...] = _dot(h, w32_ref[...])


def _proj(x2, g, w16, w32, layer, *, tm):
    m, d = x2.shape
    n16, n32 = w16.shape[-1], w32.shape[-1]
    resident = pl.Buffered(1)
    return pl.pallas_call(
        _proj_kernel,
        out_shape=(jax.ShapeDtypeStruct((m, n16), BF16), jax.ShapeDtypeStruct((m, n32), F32)),
        grid=(m // tm,),
        in_specs=[
            pl.BlockSpec((tm, d), lambda i: (i, 0)),
            pl.BlockSpec((1, d), lambda i: (0, 0)),
            pl.BlockSpec((None, d, n16), lambda i: (layer, 0, 0), pipeline_mode=resident),
            pl.BlockSpec((None, d, n32), lambda i: (layer, 0, 0), pipeline_mode=resident),
        ],
        out_specs=(pl.BlockSpec((tm, n16), lambda i: (i, 0)), pl.BlockSpec((tm, n32), lambda i: (i, 0))),
        compiler_params=_params("parallel"),
        name="proj",
    )(x2, g, w16, w32)


def _lru_kernel(x_ref, gate_ref, cw_ref, cb_ref, wa_ref, wi_ref, ba_ref, bi_ref, lam_ref,
                o_ref, xbuf_ref, hc_ref, *, tc):
    halo = SUBLANES

    @pl.when(pl.program_id(1) == 0)
    def _():
        xbuf_ref[0:halo, :] = jnp.zeros((halo, LRU_WIDTH), F32)
        hc_ref[...] = jnp.zeros_like(hc_ref)

    xbuf_ref[halo:halo + tc, :] = x_ref[0]
    base = halo - (CONV_WIDTH - 1)
    xc = xbuf_ref[base:base + tc, :] * cw_ref[0:1, :]
    for j in range(1, CONV_WIDTH):
        xc = xc + xbuf_ref[base + j:base + j + tc, :] * cw_ref[j:j + 1, :]
    xc = cb_ref[...] + xc
    xbuf_ref[0:halo, :] = xbuf_ref[tc:tc + halo, :]

    xcb = xc.astype(BF16)
    r_parts, i_parts = [], []
    for n in range(LRU_BLOCKS):
        blk = xcb[:, n * LRU_BLOCK_W:(n + 1) * LRU_BLOCK_W]
        r_parts.append(_dot(blk, wa_ref[n]))
        i_parts.append(_dot(blk, wi_ref[n]))
    r = _sigmoid(jnp.concatenate(r_parts, axis=1) + ba_ref[...])
    gi = _sigmoid(jnp.concatenate(i_parts, axis=1) + bi_ref[...])

    softplus_neg_lam = jnp.log1p(jnp.exp(-lam_ref[...]))
    log_a = (-LRU_C) * r * softplus_neg_lam
    a = jnp.exp(log_a)
    th = jnp.tanh(log_a)
    mult = jnp.sqrt(jnp.maximum(-2.0 * th / (1.0 - th), 0.0))
    u = mult * (gi * xc)

    row = lax.broadcasted_iota(I32, (tc, LRU_WIDTH), 0)
    d = 1
    while d < tc:
        keep = row >= d
        a_sh = jnp.where(keep, pltpu.roll(a, d, 0), 1.0)
        u_sh = jnp.where(keep, pltpu.roll(u, d, 0), 0.0)
        u = a * u_sh + u
        a = a * a_sh
        d *= 2
    h = a * hc_ref[...] + u
    hc_ref[...] = h[tc - 1:tc, :]

    gate = gate_ref[0]
    gelu = 0.5 * gate * (1.0 + jnp.tanh(math.sqrt(2.0 / math.pi) * (gate + 0.044715 * (gate * gate * gate))))
    o_ref[0] = (h * gelu).astype(o_ref.dtype)


def _lru(p32, conv_w, conv_b, w_a, w_i, b_a, b_i, lam, layer, *, tc):
    b, s, _ = p32.shape
    vec = pl.BlockSpec((1, LRU_WIDTH), lambda bi, c: (0, 0))
    wblk = pl.BlockSpec((None, LRU_BLOCKS, LRU_BLOCK_W, LRU_BLOCK_W), lambda bi, c: (layer, 0, 0, 0))
    return pl.pallas_call(
        functools.partial(_lru_kernel, tc=tc),
        out_shape=jax.ShapeDtypeStruct((b, s, LRU_WIDTH), BF16),
        grid=(b, s // tc),
        in_specs=[
            pl.BlockSpec((1, tc, LRU_WIDTH), lambda bi, c: (bi, c, 0)),
            pl.BlockSpec((1, tc, LRU_WIDTH), lambda bi, c: (bi, c, 1)),
            pl.BlockSpec((CONV_WIDTH, LRU_WIDTH), lambda bi, c: (0, 0)),
            vec, wblk, wblk, vec, vec, vec,
        ],
        out_specs=pl.BlockSpec((1, tc, LRU_WIDTH), lambda bi, c: (bi, c, 0)),
        scratch_shapes=[pltpu.VMEM((tc + SUBLANES, LRU_WIDTH), F32), pltpu.VMEM((1, LRU_WIDTH), F32)],
        compiler_params=_params("arbitrary", "arbitrary"),
        name="rglru",
    )(p32, p32, conv_w, conv_b, w_a, w_i, b_a, b_i, lam)


def _memattn_kernel(q_ref, kv_ref, o_ref):
    kw = X_HEADS * X_HEAD_DIM
    for h in range(X_HEADS):
        sl = slice(h * X_HEAD_DIM, (h + 1) * X_HEAD_DIM)
        q = q_ref[0, :, sl]
        k = kv_ref[0, :, sl]
        v = kv_ref[0, :, kw + h * X_HEAD_DIM:kw + (h + 1) * X_HEAD_DIM]
        s = _dot_nt(q, k)
        m = jnp.max(s, axis=-1, keepdims=True)
        p = jnp.exp(s - m)
        l = jnp.sum(p, axis=-1, keepdims=True)
        o = _dot(p.astype(BF16), v)
        o_ref[0, :, sl] = (o / l).astype(o_ref.dtype)


def _memattn(p16, mkv, *, tm):
    b, s, _ = p16.shape
    nm = mkv.shape[1]
    kw = X_HEADS * X_HEAD_DIM
    return pl.pallas_call(
        _memattn_kernel,
        out_shape=jax.ShapeDtypeStruct((b, s, kw), BF16),
        grid=(b, s // tm),
        in_specs=[
            pl.BlockSpec((1, tm, kw), lambda bi, i: (bi, i, 1)),
            pl.BlockSpec((1, nm, 2 * kw), lambda bi, i: (bi, 0, 0)),
        ],
        out_specs=pl.BlockSpec((1, tm, kw), lambda bi, i: (bi, i, 0)),
        compiler_params=_params("parallel", "parallel"),
        name="memattn",
    )(p16, mkv)


def _merge_kernel(x_ref, g_ref, ya_ref, yb_ref, yc_ref, wg0_ref, wg1_ref, wg2_ref, wb_ref, wo_ref,
                  o_ref, h_ref):
    n = pl.program_id(1)

    @pl.when(n == 0)
    def _():
        h_ref[...] = _rms(x_ref[...], g_ref[...]).astype(BF16)
        o_ref[...] = jnp.zeros_like(o_ref)

    h = h_ref[...]
    merged = None
    for j, (y_ref, wg_ref) in enumerate(((ya_ref, wg0_ref), (yb_ref, wg1_ref), (yc_ref, wg2_ref))):
        term = _sigmoid(_dot(h, wg_ref[...])) * _dot(y_ref[...], wb_ref[j])
        merged = term if merged is None else merged + term
    o_ref[...] += _dot(merged.astype(BF16), wo_ref[...])

    @pl.when(n == pl.num_programs(1) - 1)
    def _():
        o_ref[...] = x_ref[...] + o_ref[...]


def _merge(x2, g, ya, yb, yc, wgates, wbranch, wout, layer, *, tm, tn):
    m, d = x2.shape
    nblk = d // tn
    yspec = pl.BlockSpec((tm, BRANCH_WIDTH), lambda i, n: (i, 0))
    return pl.pallas_call(
        _merge_kernel,
        out_shape=jax.ShapeDtypeStruct((m, d), F32),
        grid=(m // tm, nblk),
        in_specs=[
            pl.BlockSpec((tm, d), lambda i, n: (i, 0)),
            pl.BlockSpec((1, d), lambda i, n: (0, 0)),
            yspec, yspec, yspec,
            pl.BlockSpec((None, d, tn), lambda i, n: (layer, 0, n)),
            pl.BlockSpec((None, d, tn), lambda i, n: (layer, 0, nblk + n)),
            pl.BlockSpec((None, d, tn), lambda i, n: (layer, 0, 2 * nblk + n)),
            pl.BlockSpec((None, N_BRANCH, BRANCH_WIDTH, tn), lambda i, n: (layer, 0, 0, n)),
            pl.BlockSpec((None, tn, d), lambda i, n: (layer, n, 0)),
        ],
        out_specs=pl.BlockSpec((tm, d), lambda i, n: (i, 0)),
        scratch_shapes=[pltpu.VMEM((tm, d), BF16)],
        compiler_params=_params("parallel", "arbitrary"),
        name="merge",
    )(x2, g, ya, yb, yc, wgates, wgates, wgates, wbranch, wout)


def _dsa_kernel(q_ref, k_ref, vt_ref, vtp_ref, vtd_ref, qi_ref, ki_ref, wi_ref, lng_ref, lnb_ref, bias_ref, bmax_ref,
                o_ref,
                kln_ref, knorm_ref, keys_ref, dig_ref, cls_ref, madd_ref, mnear_ref, qi4_ref, q4_ref, m_ref, l_ref, acc_ref,
                *, topk, far_unroll):
    i = pl.program_id(1)
    s_len = k_ref.shape[1]

    @pl.when(i == 0)
    def _():
        rows = 512 if s_len % 512 == 0 else QB
        lane = lax.broadcasted_iota(I32, (rows, LANES), 1)
        real = lane < IDX_DIM

        def ln_body(c, carry):
            x = ki_ref[0, pl.ds(c * rows, rows), :]
            mu = jnp.sum(x, axis=-1, keepdims=True) * (1.0 / IDX_DIM)
            xm = jnp.where(real, x - mu, 0.0)
            var = jnp.sum(xm * xm, axis=-1, keepdims=True) * (1.0 / IDX_DIM)
            y = xm * lax.rsqrt(var + EPS) * lng_ref[...] + lnb_ref[...]
            kln_ref[pl.ds(c * rows, rows), :] = y.astype(BF16)
            kf = k_ref[0, pl.ds(c * rows, rows), :].astype(F32)
            sq = [jnp.sum(kf[:, n * A_HEAD_DIM:(n + 1) * A_HEAD_DIM] ** 2, axis=-1, keepdims=True)
                  for n in range(A_KV_HEADS)]
            return tuple(jnp.maximum(a, b) for a, b in zip(carry, sq))

        ksq = lax.fori_loop(0, s_len // rows, ln_body, (jnp.zeros((rows, 1), F32),) * A_KV_HEADS)
        for n in range(A_KV_HEADS):
            knorm_ref[n] = jnp.broadcast_to(jnp.sqrt(jnp.max(ksq[n], axis=0, keepdims=True)), (1, LANES))

    w_rows = wi_ref[0].T * (IDX_HEADS ** -0.5 * IDX_DIM ** -0.5)
    for h in range(IDX_HEADS):
        qi4_ref[h * QB:(h + 1) * QB, :] = qi_ref[0, :, h * LANES:(h + 1) * LANES].astype(BF16)
    eye = (lax.broadcasted_iota(I32, (QB, QB), 0) == lax.broadcasted_iota(I32, (QB, QB), 1)).astype(BF16)
    for n in range(A_KV_HEADS):
        for g in range(A_GROUP):
            hd = n * A_GROUP + g
            q4_ref[n, g * QB:(g + 1) * QB, 0:A_HEAD_DIM] = q_ref[0, :, hd * A_HEAD_DIM:(hd + 1) * A_HEAD_DIM]
            q4_ref[n, g * QB:(g + 1) * QB, A_HEAD_DIM:A_HEAD_DIM + QB] = eye

    qf = q_ref[0].astype(F32)
    head_of_col = lax.broadcasted_iota(I32, (A_HEADS * A_HEAD_DIM, LANES), 0) // A_HEAD_DIM
    seg = (head_of_col == lax.broadcasted_iota(I32, (A_HEADS * A_HEAD_DIM, LANES), 1)).astype(BF16)
    qnorm = jnp.sqrt(_dot((qf * qf).astype(BF16), seg)).T
    bound = None
    for hd in range(A_HEADS):
        bh = qnorm[hd:hd + 1, :] * knorm_ref[hd // A_GROUP]
        bound = bh if bound is None else jnp.maximum(bound, bh)
    bound = bound * BOUND_SLACK + bmax_ref[...]
    neg_bound = -bound
    bounded = jnp.max(bound) <= BOUND_LIMIT

    last = i // SUB
    nsc = last + 1

    def rows(j):
        return pl.ds(pl.multiple_of(j * SC, SC), SC)

    def packed_digits(j, shift):
        word = None
        for blk in range(SUB):
            key = keys_ref[pl.ds(pl.multiple_of(j * SC + blk * QB, QB), QB), :]
            s = shift - 8 * blk
            f = lax.shift_right_arithmetic(key, jnp.int32(s)) if s >= 0 else lax.shift_left(key, jnp.int32(-s))
            f = f & jnp.int32(_as_i32(DIGIT_MASK << (8 * blk)))
            word = f if word is None else word | f
        return word

    def word_rows(j):
        return pl.ds(pl.multiple_of(j * QB, QB), QB)

    def index_dots(j):
        return _dot_nt(kln_ref[rows(j), :], qi4_ref[...])

    def score_chunk(j, dots, causal):
        sc = None
        for h in range(IDX_HEADS):
            t = w_rows[h:h + 1, :] * jnp.maximum(dots[:, h * QB:(h + 1) * QB], 0.0)
            sc = t if sc is None else sc + t
        bits = pltpu.bitcast(sc, I32)
        key = jnp.where(bits < 0, bits ^ jnp.int32(0x7FFFFFFF), bits)
        key = jnp.where(sc == 0.0, 0, key)
        if causal:
            key_pos = j * SC + lax.broadcasted_iota(I32, (SC, LANES), 0)
            q_pos = i * QB + lax.broadcasted_iota(I32, (SC, LANES), 1)
            key = jnp.where(key_pos <= q_pos, key, INT_MIN)
        keys_ref[rows(j), :] = key
        top = packed_digits(j, 32 - DIGIT_BITS) ^ BYTE_HALF
        dig_ref[word_rows(j), :] = top | BYTE_GUARDS
        cls_ref[word_rows(j), :] = jnp.full((QB, LANES), BYTE_LOW, I32)
        if causal:
            @pl.when(j % 2 == 0)
            def _():
                dig_ref[word_rows(j + 1), :] = jnp.full((QB, LANES), BYTE_GUARDS, I32)

    def score_body(pr, carry):
        d0 = index_dots(2 * pr)
        d1 = index_dots(2 * pr + 1)
        score_chunk(2 * pr, d0, False)
        score_chunk(2 * pr + 1, d1, False)
        return carry

    lax.fori_loop(0, last // 2, score_body, 0)

    @pl.when(last % 2 == 1)
    def _():
        score_chunk(last - 1, index_dots(last - 1), False)

    score_chunk(last, index_dots(last), True)

    t_pos = i * QB + lax.broadcasted_iota(I32, (1, LANES), 1)
    kk = jnp.minimum(t_pos + 1, topk).astype(F32)

    def count_ge(cand):
        def body(j, cnt):
            for r in range(SUB):
                blk = keys_ref[pl.ds(pl.multiple_of(j * SC + r * QB, QB), QB), :]
                cnt = cnt + jnp.where(blk >= cand, 1.0, 0.0)
            return cnt

        cnt = lax.fori_loop(0, nsc, body, jnp.zeros((QB, LANES), F32))
        return jnp.sum(cnt, axis=0, keepdims=True)

    def count_digit_ge(cand):
        packed = cand * BYTE_ONES

        def body(g, cnt):
            d = dig_ref[pl.ds(pl.multiple_of(g * 2 * QB, 2 * QB), 2 * QB), :] - packed
            hit = lax.population_count(d & BYTE_GUARDS)
            return cnt + hit[:QB] + hit[QB:]

        cnt = lax.fori_loop(0, (nsc + 1) // 2, body, jnp.zeros((QB, LANES), I32))
        return jnp.sum(cnt, axis=0, keepdims=True).astype(F32)

    def select_digit(above):
        def bit_body(bi, carry):
            t, greater = carry
            cand = t + lax.shift_left(jnp.int32(1), DIGIT_BITS - 1 - bi)
            cnt = count_digit_ge(cand)
            ok = above + cnt >= kk
            return jnp.where(ok, cand, t), jnp.where(ok, greater, cnt)

        t, greater = lax.fori_loop(0, DIGIT_BITS, bit_body,
                                   (jnp.zeros((1, LANES), I32), jnp.zeros((1, LANES), F32)))
        return t, above + greater

    thr = jnp.zeros((1, LANES), I32)
    above = jnp.zeros((1, LANES), F32)
    for level in range(DIGITS):
        shift = 32 - DIGIT_BITS * (level + 1)
        if level > 0:
            t_packed = t * BYTE_ONES

            def next_digit(j, carry, shift=shift, t_packed=t_packed):
                z = (dig_ref[word_rows(j), :] & BYTE_LOW) ^ t_packed
                same = ((z + BYTE_LOW) & BYTE_GUARDS) ^ BYTE_GUARDS
                cls = cls_ref[word_rows(j), :] & (same - lax.shift_right_logical(same, jnp.int32(7)))
                cls_ref[word_rows(j), :] = cls
                dig_ref[word_rows(j), :] = (packed_digits(j, shift) & cls) | BYTE_GUARDS
                return carry

            lax.fori_loop(0, nsc, next_digit, 0)
        t, above = select_digit(above)
        thr = thr + lax.shift_left(t - (DIGIT_HALF if level == 0 else 0), jnp.int32(shift))

    def bit_body(bi, carry):
        t, greater = carry
        cand = t + lax.shift_left(jnp.int32(1), REST_BITS - 1 - bi)
        cnt = count_ge(cand)
        ok = cnt >= kk
        return jnp.where(ok, cand, t), jnp.where(ok, greater, cnt)

    thr, above = lax.fori_loop(0, REST_BITS, bit_body, (thr, above))
    need = kk - above
    sub_k = lax.broadcasted_iota(I32, (QB, QB), 0)
    sub_k2 = lax.broadcasted_iota(I32, (QB, QB), 1)
    tril = (sub_k2 <= sub_k).astype(BF16)

    def mask_body(j, seen):
        keys, eqs, pres = [], [], []
        for r in range(SUB):
            key = keys_ref[pl.ds(pl.multiple_of(j * SC + r * QB, QB), QB), :]
            eq = key == thr
            keys.append(key)
            eqs.append(eq)
            pres.append(_dot(tril, jnp.where(eq, 1.0, 0.0).astype(BF16)))
        for r in range(SUB):
            sel = (keys[r] > thr) | (eqs[r] & ((seen + pres[r]) <= need))
            madd_ref[pl.ds(pl.multiple_of(j * SC + r * QB, QB), QB), :] = jnp.where(sel, neg_bound, NEG).astype(BF16)
            seen = seen + pres[r][QB - 1:QB, :]
        return seen

    lax.fori_loop(0, nsc, mask_body, jnp.zeros((1, LANES), F32))

    def split_near(slot, blk):
        blk_rows = pl.ds(pl.multiple_of(blk * QB, QB), QB)
        mnear_ref[slot] = madd_ref[blk_rows, :]
        madd_ref[blk_rows, :] = jnp.full((QB, LANES), NEG, BF16)

    split_near(1, i)

    @pl.when(i >= 1)
    def _():
        split_near(0, i - 1)

    m_ref[...] = jnp.full(m_ref.shape, NEG, F32)
    l_ref[...] = jnp.zeros(l_ref.shape, F32)
    acc_ref[...] = jnp.zeros(acc_ref.shape, F32)

    def logits(kc, madd, n):
        return _dot_nt(jnp.concatenate([kc, madd], axis=1), q4_ref[n])

    def far_logits(j, n):
        return logits(k_ref[0, rows(j), n * A_HEAD_DIM:(n + 1) * A_HEAD_DIM], madd_ref[rows(j), :], n)

    def online_update(n, s, vt):
        m_prev = m_ref[n]
        m_new = jnp.maximum(m_prev, jnp.max(s, axis=0, keepdims=True))
        alpha = jnp.exp2(m_prev - m_new)
        p = jnp.exp2(s - m_new)
        l_ref[n] = alpha * l_ref[n] + jnp.sum(p, axis=0, keepdims=True)
        acc_ref[n] = alpha * acc_ref[n] + _dot(vt, p.astype(BF16))
        m_ref[n] = m_new

    def bounded_update(n, s, vt):
        p = jnp.exp2(s)
        l_ref[n] += jnp.sum(p, axis=0, keepdims=True)
        acc_ref[n] += _dot(vt, p.astype(BF16))

    def far_vt(j, n):
        return vt_ref[0, j, n * A_HEAD_DIM:(n + 1) * A_HEAD_DIM, :]

    n_far = jnp.where(i % SUB >= 2, nsc, last)

    def attend(update):
        def sweep(first, count):
            tasks = [(first + c, n) for c in range(count) for n in range(A_KV_HEADS)]
            ahead = 4
            pending = [far_logits(j, n) for j, n in tasks[:ahead]]
            for t, (j, n) in enumerate(tasks):
                update(n, pending.pop(0), far_vt(j, n))
                if t + ahead < len(tasks):
                    pending.append(far_logits(*tasks[t + ahead]))

        def sweep_body(g, carry):
            sweep(far_unroll * g, far_unroll)
            return carry

        lax.fori_loop(0, n_far // far_unroll, sweep_body, 0)
        for rem in range(1, far_unroll):
            @pl.when(n_far % far_unroll == rem)
            def _(rem=rem):
                sweep(n_far - rem, rem)

        def near_step(slot, blk, vts_ref):
            blk_rows = pl.ds(pl.multiple_of(blk * QB, QB), QB)
            ss = [logits(k_ref[0, blk_rows, n * A_HEAD_DIM:(n + 1) * A_HEAD_DIM], mnear_ref[slot], n)
                  + bias_ref[slot, n] for n in range(A_KV_HEADS)]
            for n in range(A_KV_HEADS):
                update(n, ss[n], vts_ref[0, 0, n * A_HEAD_DIM:(n + 1) * A_HEAD_DIM, :])

        @pl.when(i >= 1)
        def _():
            near_step(0, i - 1, vtp_ref)

        near_step(1, i, vtd_ref)

    @pl.when(bounded)
    def _():
        attend(bounded_update)

    @pl.when(jnp.logical_not(bounded))
    def _():
        attend(online_update)

    for n in range(A_KV_HEADS):
        out_t = acc_ref[n] / l_ref[n]
        for g in range(A_GROUP):
            hd = n * A_GROUP + g
            o_ref[0, :, hd * A_HEAD_DIM:(hd + 1) * A_HEAD_DIM] = out_t[:, g * QB:(g + 1) * QB].T.astype(o_ref.dtype)


def _dsa(p16, p32, vt, vt_blk, ln_g, ln_b, bias_near, *, topk, far_unroll):
    b, s, _ = p16.shape
    bias_max = jnp.broadcast_to(jnp.max(jnp.abs(bias_near)) + BOUND_MARGIN, (1, LANES)).astype(F32)
    aq = A_HEADS * A_HEAD_DIM
    akv = A_KV_HEADS * A_HEAD_DIM
    qiw = IDX_HEADS * LANES
    nch = s // QB
    gq = A_GROUP * QB
    k_blk = (aq + X_HEADS * X_HEAD_DIM) // akv
    qi_blk = 2 * LRU_WIDTH // qiw
    ki_blk = (2 * LRU_WIDTH + qiw) // LANES
    return pl.pallas_call(
        functools.partial(_dsa_kernel, topk=topk, far_unroll=far_unroll),
        out_shape=jax.ShapeDtypeStruct((b, s, aq), BF16),
        grid=(b, nch),
        in_specs=[
            pl.BlockSpec((1, QB, aq), lambda bi, i: (bi, i, 0)),
            pl.BlockSpec((1, s, akv), lambda bi, i: (bi, 0, k_blk)),
            pl.BlockSpec((1, s // SC, akv, SC), lambda bi, i: (bi, 0, 0, 0)),
            pl.BlockSpec((1, 1, akv, QB), lambda bi, i: (bi, jnp.maximum(i - 1, 0), 0, 0)),
            pl.BlockSpec((1, 1, akv, QB), lambda bi, i: (bi, i, 0, 0)),
            pl.BlockSpec((1, QB, qiw), lambda bi, i: (bi, i, qi_blk)),
            pl.BlockSpec((1, s, LANES), lambda bi, i: (bi, 0, ki_blk)),
            pl.BlockSpec((1, QB, LANES), lambda bi, i: (bi, i, ki_blk + 1)),
            pl.BlockSpec((1, LANES), lambda bi, i: (0, 0)),
            pl.BlockSpec((1, LANES), lambda bi, i: (0, 0)),
            pl.BlockSpec((2, A_KV_HEADS, QB, gq), lambda bi, i: (0, 0, 0, 0)),
            pl.BlockSpec((1, LANES), lambda bi, i: (0, 0)),
        ],
        out_specs=pl.BlockSpec((1, QB, aq), lambda bi, i: (bi, i, 0)),
        scratch_shapes=[
            pltpu.VMEM((s, LANES), BF16),
            pltpu.VMEM((A_KV_HEADS, 1, LANES), F32),
            pltpu.VMEM((s, LANES), I32),
            pltpu.VMEM((-(-(s // SC) // 2) * 2 * QB, LANES), I32),
            pltpu.VMEM((s // SUB, LANES), I32),
            pltpu.VMEM((s, LANES), BF16),
            pltpu.VMEM((2, QB, LANES), BF16),
            pltpu.VMEM((IDX_HEADS * QB, LANES), BF16),
            pltpu.VMEM((A_KV_HEADS, gq, A_HEAD_DIM + QB), BF16),
            pltpu.VMEM((A_KV_HEADS, 1, gq), F32),
            pltpu.VMEM((A_KV_HEADS, 1, gq), F32),
            pltpu.VMEM((A_KV_HEADS, A_HEAD_DIM, gq), F32),
        ],
        compiler_params=_params("arbitrary", "arbitrary"),
        name="dsa",
    )(p16, p16, vt, vt_blk, vt_blk, p32, p32, p32, ln_g, ln_b, bias_near, bias_max)


def _t5_bucket(dist):
    max_exact = N_BUCKETS // 2
    d = jnp.maximum(dist, 0)
    df = jnp.maximum(d, 1).astype(F32)
    large = max_exact + (jnp.log(df / max_exact) / math.log(MAX_DISTANCE / max_exact)
                         * (N_BUCKETS - max_exact)).astype(I32)
    large = jnp.minimum(large, N_BUCKETS - 1)
    return jnp.where(d < max_exact, d, large)


def _near_bias(rel_bias):
    t = jnp.arange(QB, dtype=I32)[:, None]
    s = jnp.arange(QB, dtype=I32)[None, :]
    tiles = []
    for off in (QB, 0):
        bucket = _t5_bucket(t - s + off)
        onehot = (bucket[..., None] == jnp.arange(N_BUCKETS, dtype=I32)).astype(F32)
        table = (rel_bias - rel_bias[N_BUCKETS - 1]) * LOG2E
        tile = jnp.einsum('tsb,bh->tsh', onehot, table, precision=lax.Precision.HIGHEST)
        tile = tile.transpose(1, 2, 0).reshape(QB, A_KV_HEADS, A_GROUP * QB)
        tiles.append(tile.transpose(1, 0, 2))
    return jnp.stack(tiles).astype(F32)


def _pad_cols(w, width):
    return jnp.pad(w, [(0, 0)] * (w.ndim - 1) + [(0, width - w.shape[-1])])


def _split_points(d_model):
    aq = A_HEADS * A_HEAD_DIM
    akv = A_KV_HEADS * A_HEAD_DIM
    sizes = (aq, akv, akv, IDX_HEADS * IDX_DIM, IDX_DIM, IDX_HEADS,
             LRU_WIDTH, LRU_WIDTH, X_HEADS * X_HEAD_DIM, N_BRANCH * d_model)
    pts, acc = [0], 0
    for n in sizes:
        acc += n
        pts.append(acc)
    return pts


def _regroup_kernel(w_ref, o16_ref, o32_ref, og_ref, *, pts):
    (q0, k0, v0, qi0, ki0, wi0, x0, g0, qm0, gt0, end) = pts

    def seg(lo, hi, scale=None):
        v = w_ref[0, :, lo:hi]
        if scale is not None:
            v = v * scale
        return v.astype(BF16)

    o16_ref[0, :, 0:k0 - q0] = seg(q0, k0, A_HEAD_DIM ** -0.5 * LOG2E)
    c = k0 - q0
    o16_ref[0, :, c:c + gt0 - qm0] = seg(qm0, gt0, X_HEAD_DIM ** -0.5)
    c += gt0 - qm0
    o16_ref[0, :, c:c + qi0 - k0] = seg(k0, qi0)
    o32_ref[0, :, 0:qm0 - x0] = seg(x0, qm0)
    c = qm0 - x0
    o32_ref[0, :, c:] = jnp.zeros((o32_ref.shape[1], o32_ref.shape[2] - c), BF16)
    for h in range(IDX_HEADS):
        o32_ref[0, :, c + h * LANES:c + h * LANES + IDX_DIM] = seg(qi0 + h * IDX_DIM, qi0 + (h + 1) * IDX_DIM)
    c += IDX_HEADS * LANES
    o32_ref[0, :, c:c + IDX_DIM] = seg(ki0, wi0)
    c += LANES
    o32_ref[0, :, c:c + IDX_HEADS] = seg(wi0, x0)
    og_ref[0] = seg(gt0, end)


def _prep_w_in(w_in, d_model, *, tk):
    depth, k, n = w_in.shape
    pts = _split_points(d_model)
    n16 = pts[3] + pts[9] - pts[8]
    n32 = 2 * LRU_WIDTH + (IDX_HEADS + 2) * LANES
    ng = n - pts[9]
    return pl.pallas_call(
        functools.partial(_regroup_kernel, pts=tuple(pts)),
        out_shape=(jax.ShapeDtypeStruct((depth, k, n16), BF16), jax.ShapeDtypeStruct((depth, k, n32), BF16),
                   jax.ShapeDtypeStruct((depth, k, ng), BF16)),
        grid=(depth, k // tk),
        in_specs=[pl.BlockSpec((1, tk, n), lambda l, i: (l, i, 0))],
        out_specs=(pl.BlockSpec((1, tk, n16), lambda l, i: (l, i, 0)),
                   pl.BlockSpec((1, tk, n32), lambda l, i: (l, i, 0)),
                   pl.BlockSpec((1, tk, ng), lambda l, i: (l, i, 0))),
        compiler_params=_params("parallel", "parallel"),
        name="regroup",
    )(w_in)


def _split_gu_kernel(w_ref, og_ref, ou_ref):
    d_ff = w_ref.shape[2] // 2
    pad = og_ref.shape[2] - d_ff
    for o_ref, lo in ((og_ref, 0), (ou_ref, d_ff)):
        o_ref[0, :, 0:d_ff] = w_ref[0, :, lo:lo + d_ff].astype(BF16)
        if pad:
            o_ref[0, :, d_ff:] = jnp.zeros((o_ref.shape[1], pad), BF16)


def _pad_down_kernel(w_ref, o_ref, *, n_real):
    @pl.when(pl.program_id(1) < n_real)
    def _():
        o_ref[...] = w_ref[...].astype(BF16)

    @pl.when(pl.program_id(1) >= n_real)
    def _():
        o_ref[...] = jnp.zeros_like(o_ref)


def _pad_ff(w_gu, w_down, tf, *, tk):
    depth, d, _ = w_gu.shape
    d_ff = w_down.shape[1]
    fp = -(-d_ff // tf) * tf
    assert d_ff % LANES == 0
    wg, wu = pl.pallas_call(
        _split_gu_kernel,
        out_shape=(jax.ShapeDtypeStruct((depth, d, fp), BF16),) * 2,
        grid=(depth, d // tk),
        in_specs=[pl.BlockSpec((1, tk, 2 * d_ff), lambda l, i: (l, i, 0))],
        out_specs=(pl.BlockSpec((1, tk, fp), lambda l, i: (l, i, 0)),) * 2,
        compiler_params=_params("parallel", "parallel"),
        name="split_gu",
    )(w_gu)
    n_real = d_ff // LANES
    wd = pl.pallas_call(
        functools.partial(_pad_down_kernel, n_real=n_real),
        out_shape=jax.ShapeDtypeStruct((depth, fp, d), BF16),
        grid=(depth, fp // LANES),
        in_specs=[pl.BlockSpec((1, LANES, d), lambda l, i: (l, jnp.minimum(i, n_real - 1), 0))],
        out_specs=pl.BlockSpec((1, LANES, d), lambda l, i: (l, i, 0)),
        compiler_params=_params("parallel", "arbitrary"),
        name="pad_down",
    )(w_down)
    return wg, wu, wd


def _tiles(m, s, d_model):
    return dict(
        ffn_tm=min(1024, m), ffn_final_tm=min(512, m), ffn_tf=512,
        proj_tm=min(256, m), proj_tn=256,
        merge_tm=min(512, m), merge_tn=min(512, d_model),
        lru_tc=min(256, s), mem_tm=min(512, s),
    )


def kernel(x, mem, rel_bias, final_norm, norm_ff1, w_ff1_gu, w_ff1_down, norm_mix, w_in, conv_w, conv_b, w_a, b_a, w_i, b_i, lam, idx_ln_g, idx_ln_b, mem_norm, w_mem_kv, w_branch, w_out, norm_ff2, w_ff2_gu, w_ff2_down):
    b, s, d = x.shape
    nm = mem.shape[1]
    depth = w_in.shape[0]
    m = b * s
    assert s % SC == 0 and d % LANES == 0
    tl = _tiles(m, s, d)
    topk = min(TOPK_MAX, s // 4)
    bias_near = _near_bias(rel_bias)
    row = lambda v: v.reshape(1, -1)

    ff1 = _pad_ff(w_ff1_gu, w_ff1_down, tl["ffn_tf"], tk=min(256, d))
    ff2 = _pad_ff(w_ff2_gu, w_ff2_down, tl["ffn_tf"], tk=min(256, d))
    w16, w32, wgates = _prep_w_in(w_in, d, tk=min(256, d))
    w_a16, w_i16 = w_a.astype(BF16), w_i.astype(BF16)
    w_mem16, w_branch16, w_out16 = w_mem_kv.astype(BF16), w_branch.astype(BF16), w_out.astype(BF16)
    ln_g = _pad_cols(idx_ln_g, LANES)
    ln_b = _pad_cols(idx_ln_b, LANES)

    x2 = x.reshape(m, d)
    mem2 = mem.reshape(b * nm, d)
    for l in range(depth):
        x2 = _ffn(x2, row(norm_ff1[l]), *ff1, row(final_norm), l, final_norm=False,
                  tm=tl["ffn_tm"], tf=tl["ffn_tf"])

        p16, p32 = _proj(x2, row(norm_mix[l]), w16, w32, l, tm=tl["proj_tm"])
        p16 = p16.reshape(b, s, -1)
        p32 = p32.reshape(b, s, -1)

        v_cols = slice(p16.shape[-1] - A_KV_HEADS * A_HEAD_DIM, p16.shape[-1])
        vt = p16[..., v_cols].reshape(b, s // SC, SC, -1).transpose(0, 1, 3, 2)
        vt_blk = p16[..., v_cols].reshape(b, s // QB, QB, -1).transpose(0, 1, 3, 2)
        y_a = _dsa(p16, p32, vt, vt_blk, row(ln_g[l]), row(ln_b[l]),
                   bias_near, topk=topk, far_unroll=4)
        y_b = _lru(p32, conv_w[l], row(conv_b[l]), w_a16, w_i16,
                   row(b_a[l]), row(b_i[l]), row(lam[l]), l, tc=tl["lru_tc"])
        mkv = _norm_matmul(mem2, row(mem_norm[l]), w_mem16, l, BF16,
                           tm=min(256, b * nm), tn=tl["proj_tn"], name="memkv")
        y_c = _memattn(p16, mkv.reshape(b, nm, -1), tm=tl["mem_tm"])

        x2 = _merge(x2, row(norm_mix[l]), y_a.reshape(m, -1), y_b.reshape(m, -1), y_c.reshape(m, -1),
                    wgates, w_branch16, w_out16, l,
                    tm=tl["merge_tm"], tn=tl["merge_tn"])

        is_last = l == depth - 1
        x2 = _ffn(x2, row(norm_ff2[l]), *ff2, row(final_norm), l, final_norm=is_last,
                  tm=tl["ffn_final_tm" if is_last else "ffn_tm"], tf=tl["ffn_tf"])
    return x2.reshape(b, s, d)
```

```python
import functools
import math

import jax
import jax.numpy as jnp
from jax import lax
from jax.experimental import pallas as pl
from jax.experimental.pallas import tpu as pltpu

F32 = jnp.float32
BF16 = jnp.bfloat16
I32 = jnp.int32

EPS = 1e-6
A_HEADS = 8
A_KV_HEADS = 2
A_HEAD_DIM = 128
A_GROUP = A_HEADS // A_KV_HEADS
IDX_HEADS = 4
IDX_DIM = 64
TOPK_MAX = 256
LRU_WIDTH = 1024
LRU_BLOCKS = 8
LRU_BLOCK_W = LRU_WIDTH // LRU_BLOCKS
CONV_WIDTH = 4
LRU_C = 8.0
X_HEADS = 4
X_HEAD_DIM = 256
N_BRANCH = 3
BRANCH_WIDTH = 1024
N_BUCKETS = 32
MAX_DISTANCE = 128

LANES = 128
SUBLANES = 8
VMEM_LIMIT = 56 * 1024 * 1024
ROW_CHUNK = 256

QB = 128
SC = 512
SUB = SC // QB
NEG = -1e30
LOG2E = math.log2(math.e)
INT_MIN = -2 ** 31


def _as_i32(v):
    return v - (1 << 32) if v >= (1 << 31) else v


DIGIT_BITS = 7
DIGITS = 4
REST_BITS = 32 - DIGITS * DIGIT_BITS
DIGIT_MASK = (1 << DIGIT_BITS) - 1
DIGIT_HALF = 1 << (DIGIT_BITS - 1)
BYTE_ONES = _as_i32(0x01010101)
BYTE_LOW = _as_i32(0x7F7F7F7F)
BYTE_GUARDS = _as_i32(0x80808080)
BYTE_HALF = _as_i32(0x40404040)
assert SUB == 4 and DIGIT_BITS == 7

BOUND_SLACK = 1.02
BOUND_MARGIN = 0.5
BOUND_LIMIT = 55.0


def _params(*sem):
    return pltpu.CompilerParams(dimension_semantics=sem, vmem_limit_bytes=VMEM_LIMIT)


def _rms(x, g):
    ms = jnp.mean(x * x, axis=-1, keepdims=True)
    return x * lax.rsqrt(ms + EPS) * g


def _sigmoid(x):
    return 1.0 / (1.0 + jnp.exp(-x))


def _dot(a, b):
    return jnp.dot(a, b, preferred_element_type=F32)


def _dot_nt(a, b):
    return lax.dot_general(a, b, (((1,), (1,)), ((), ())), preferred_element_type=F32)


def _ffn_kernel(x_ref, g_ref, wg_ref, wu_ref, wd_ref, fg_ref, o_ref, h_ref, *, final_norm):
    f = pl.program_id(1)

    def for_row_chunks(fn):
        rc = min(ROW_CHUNK, x_ref.shape[0])

        def body(r, carry):
            fn(pl.ds(pl.multiple_of(r * rc, rc), rc))
            return carry

        lax.fori_loop(0, x_ref.shape[0] // rc, body, 0)

    @pl.when(f == 0)
    def _():
        def prologue(rs):
            h_ref[rs, :] = _rms(x_ref[rs, :], g_ref[...]).astype(BF16)
            o_ref[rs, :] = jnp.zeros((rs.size, o_ref.shape[1]), F32)

        for_row_chunks(prologue)

    h = h_ref[...]
    g = _dot(h, wg_ref[...])
    u = _dot(h, wu_ref[...])
    a = (g * _sigmoid(g) * u).astype(BF16)
    o_ref[...] += _dot(a, wd_ref[...])

    @pl.when(f == pl.num_programs(1) - 1)
    def _():
        def epilogue(rs):
            y = x_ref[rs, :] + 0.5 * o_ref[rs, :]
            if final_norm:
                y = _rms(y, fg_ref[...])
            o_ref[rs, :] = y

        for_row_chunks(epilogue)


def _ffn(x2, g, wg, wu, wd, fg, layer, *, final_norm, tm, tf):
    m, d = x2.shape
    fp = wg.shape[-1]
    return pl.pallas_call(
        functools.partial(_ffn_kernel, final_norm=final_norm),
        out_shape=jax.ShapeDtypeStruct((m, d), F32),
        grid=(m // tm, fp // tf),
        in_specs=[
            pl.BlockSpec((tm, d), lambda i, f: (i, 0)),
            pl.BlockSpec((1, d), lambda i, f: (0, 0)),
            pl.BlockSpec((None, d, tf), lambda i, f: (layer, 0, f)),
            pl.BlockSpec((None, d, tf), lambda i, f: (layer, 0, f)),
            pl.BlockSpec((None, tf, d), lambda i, f: (layer, f, 0)),
            pl.BlockSpec((1, d), lambda i, f: (0, 0)),
        ],
        out_specs=pl.BlockSpec((tm, d), lambda i, f: (i, 0)),
        scratch_shapes=[pltpu.VMEM((tm, d), BF16)],
        compiler_params=_params("parallel", "arbitrary"),
        name="ffn",
    )(x2, g, wg, wu, wd, fg)


def _norm_matmul_kernel(x_ref, g_ref, w_ref, o_ref, h_ref):
    @pl.when(pl.program_id(1) == 0)
    def _():
        h_ref[...] = _rms(x_ref[...], g_ref[...]).astype(BF16)

    o_ref[...] = _dot(h_ref[...], w_ref[...]).astype(o_ref.dtype)


def _norm_matmul(x2, g, w, layer, out_dtype, *, tm, tn, name):
    m, d = x2.shape
    n = w.shape[-1]
    return pl.pallas_call(
        _norm_matmul_kernel,
        out_shape=jax.ShapeDtypeStruct((m, n), out_dtype),
        grid=(m // tm, n // tn),
        in_specs=[
            pl.BlockSpec((tm, d), lambda i, j: (i, 0)),
            pl.BlockSpec((1, d), lambda i, j: (0, 0)),
            pl.BlockSpec((None, d, tn), lambda i, j: (layer, 0, j)),
        ],
        out_specs=pl.BlockSpec((tm, tn), lambda i, j: (i, j)),
        scratch_shapes=[pltpu.VMEM((tm, d), BF16)],
        compiler_params=_params("parallel", "arbitrary"),
        name=name,
    )(x2, g, w)


def _proj_kernel(x_ref, g_ref, w16_ref, w32_ref, o16_ref, o32_ref):
    h = _rms(x_ref[...], g_ref[...]).astype(BF16)
    o16_ref[...] = _dot(h, w16_ref[...]).astype(o16_ref.dtype)
    o32_ref[...] = _dot(h, w32_ref[...])


def _proj(x2, g, w16, w32, layer, *, tm):
    m, d = x2.shape
    n16, n32 = w16.shape[-1], w32.shape[-1]
    resident = pl.Buffered(1)
    return pl.pallas_call(
        _proj_kernel,
        out_shape=(jax.ShapeDtypeStruct((m, n16), BF16), jax.ShapeDtypeStruct((m, n32), F32)),
        grid=(m // tm,),
        in_specs=[
            pl.BlockSpec((tm, d), lambda i: (i, 0)),
            pl.BlockSpec((1, d), lambda i: (0, 0)),
            pl.BlockSpec((None, d, n16), lambda i: (layer, 0, 0), pipeline_mode=resident),
            pl.BlockSpec((None, d, n32), lambda i: (layer, 0, 0), pipeline_mode=resident),
        ],
        out_specs=(pl.BlockSpec((tm, n16), lambda i: (i, 0)), pl.BlockSpec((tm, n32), lambda i: (i, 0))),
        compiler_params=_params("parallel"),
        name="proj",
    )(x2, g, w16, w32)


def _lru_kernel(x_ref, gate_ref, cw_ref, cb_ref, wa_ref, wi_ref, ba_ref, bi_ref, lam_ref,
                o_ref, xbuf_ref, hc_ref, *, tc):
    halo = SUBLANES

    @pl.when(pl.program_id(1) == 0)
    def _():
        xbuf_ref[0:halo, :] = jnp.zeros((halo, LRU_WIDTH), F32)
        hc_ref[...] = jnp.zeros_like(hc_ref)

    xbuf_ref[halo:halo + tc, :] = x_ref[0]
    base = halo - (CONV_WIDTH - 1)
    xc = xbuf_ref[base:base + tc, :] * cw_ref[0:1, :]
    for j in range(1, CONV_WIDTH):
        xc = xc + xbuf_ref[base + j:base + j + tc, :] * cw_ref[j:j + 1, :]
    xc = cb_ref[...] + xc
    xbuf_ref[0:halo, :] = xbuf_ref[tc:tc + halo, :]

    xcb = xc.astype(BF16)
    r_parts, i_parts = [], []
    for n in range(LRU_BLOCKS):
        blk = xcb[:, n * LRU_BLOCK_W:(n + 1) * LRU_BLOCK_W]
        r_parts.append(_dot(blk, wa_ref[n]))
        i_parts.append(_dot(blk, wi_ref[n]))
    r = _sigmoid(jnp.concatenate(r_parts, axis=1) + ba_ref[...])
    gi = _sigmoid(jnp.concatenate(i_parts, axis=1) + bi_ref[...])

    softplus_neg_lam = jnp.log1p(jnp.exp(-lam_ref[...]))
    log_a = (-LRU_C) * r * softplus_neg_lam
    a = jnp.exp(log_a)
    mult = jnp.sqrt(jnp.maximum((1.0 - a) * (1.0 + a), 0.0))
    u = mult * (gi * xc)

    n_groups = tc // SUBLANES
    a = a.reshape(n_groups, SUBLANES, LRU_WIDTH)
    u = u.reshape(n_groups, SUBLANES, LRU_WIDTH)
    row = lax.broadcasted_iota(I32, (n_groups, SUBLANES, LRU_WIDTH), 1)
    d = 1
    while d < SUBLANES:
        keep = row >= d
        a_sh = jnp.where(keep, pltpu.roll(a, d, 1), 1.0)
        u_sh = jnp.where(keep, pltpu.roll(u, d, 1), 0.0)
        u = a * u_sh + u
        a = a * a_sh
        d *= 2
    carry = hc_ref[...]
    groups = []
    for g in range(n_groups):
        hg = a[g] * carry + u[g]
        carry = hg[SUBLANES - 1:SUBLANES, :]
        groups.append(hg)
    h = jnp.concatenate(groups, axis=0)
    hc_ref[...] = carry

    gate = gate_ref[0]
    gelu = 0.5 * gate * (1.0 + jnp.tanh(math.sqrt(2.0 / math.pi) * (gate + 0.044715 * (gate * gate * gate))))
    o_ref[0] = (h * gelu).astype(o_ref.dtype)


def _lru(p32, conv_w, conv_b, w_a, w_i, b_a, b_i, lam, layer, *, tc):
    b, s, _ = p32.shape
    vec = pl.BlockSpec((1, LRU_WIDTH), lambda bi, c: (0, 0))
    wblk = pl.BlockSpec((None, LRU_BLOCKS, LRU_BLOCK_W, LRU_BLOCK_W), lambda bi, c: (layer, 0, 0, 0))
    return pl.pallas_call(
        functools.partial(_lru_kernel, tc=tc),
        out_shape=jax.ShapeDtypeStruct((b, s, LRU_WIDTH), BF16),
        grid=(b, s // tc),
        in_specs=[
            pl.BlockSpec((1, tc, LRU_WIDTH), lambda bi, c: (bi, c, 0)),
            pl.BlockSpec((1, tc, LRU_WIDTH), lambda bi, c: (bi, c, 1)),
            pl.BlockSpec((CONV_WIDTH, LRU_WIDTH), lambda bi, c: (0, 0)),
            vec, wblk, wblk, vec, vec, vec,
        ],
        out_specs=pl.BlockSpec((1, tc, LRU_WIDTH), lambda bi, c: (bi, c, 0)),
        scratch_shapes=[pltpu.VMEM((tc + SUBLANES, LRU_WIDTH), F32), pltpu.VMEM((1, LRU_WIDTH), F32)],
        compiler_params=_params("arbitrary", "arbitrary"),
        name="rglru",
    )(p32, p32, conv_w, conv_b, w_a, w_i, b_a, b_i, lam)


def _memattn_kernel(q_ref, kv_ref, o_ref):
    kw = X_HEADS * X_HEAD_DIM
    for h in range(X_HEADS):
        sl = slice(h * X_HEAD_DIM, (h + 1) * X_HEAD_DIM)
        q = q_ref[0, :, sl]
        k = kv_ref[0, :, sl]
        v = kv_ref[0, :, kw + h * X_HEAD_DIM:kw + (h + 1) * X_HEAD_DIM]
        s = _dot_nt(q, k)
        m = jnp.max(s, axis=-1, keepdims=True)
        p = jnp.exp(s - m)
        l = jnp.sum(p, axis=-1, keepdims=True)
        o = _dot(p.astype(BF16), v)
        o_ref[0, :, sl] = (o / l).astype(o_ref.dtype)


def _memattn(p16, mkv, *, tm):
    b, s, _ = p16.shape
    nm = mkv.shape[1]
    kw = X_HEADS * X_HEAD_DIM
    return pl.pallas_call(
        _memattn_kernel,
        out_shape=jax.ShapeDtypeStruct((b, s, kw), BF16),
        grid=(b, s // tm),
        in_specs=[
            pl.BlockSpec((1, tm, kw), lambda bi, i: (bi, i, 1)),
            pl.BlockSpec((1, nm, 2 * kw), lambda bi, i: (bi, 0, 0)),
        ],
        out_specs=pl.BlockSpec((1, tm, kw), lambda bi, i: (bi, i, 0)),
        compiler_params=_params("parallel", "parallel"),
        name="memattn",
    )(p16, mkv)


def _merge_kernel(x_ref, g_ref, ya_ref, yb_ref, yc_ref, wg0_ref, wg1_ref, wg2_ref, wb_ref, wo_ref,
                  o_ref, h_ref):
    n = pl.program_id(1)

    @pl.when(n == 0)
    def _():
        h_ref[...] = _rms(x_ref[...], g_ref[...]).astype(BF16)
        o_ref[...] = jnp.zeros_like(o_ref)

    h = h_ref[...]
    merged = None
    for j, (y_ref, wg_ref) in enumerate(((ya_ref, wg0_ref), (yb_ref, wg1_ref), (yc_ref, wg2_ref))):
        term = _sigmoid(_dot(h, wg_ref[...])) * _dot(y_ref[...], wb_ref[j])
        merged = term if merged is None else merged + term
    o_ref[...] += _dot(merged.astype(BF16), wo_ref[...])

    @pl.when(n == pl.num_programs(1) - 1)
    def _():
        o_ref[...] = x_ref[...] + o_ref[...]


def _merge(x2, g, ya, yb, yc, wgates, wbranch, wout, layer, *, tm, tn):
    m, d = x2.shape
    nblk = d // tn
    yspec = pl.BlockSpec((tm, BRANCH_WIDTH), lambda i, n: (i, 0))
    return pl.pallas_call(
        _merge_kernel,
        out_shape=jax.ShapeDtypeStruct((m, d), F32),
        grid=(m // tm, nblk),
        in_specs=[
            pl.BlockSpec((tm, d), lambda i, n: (i, 0)),
            pl.BlockSpec((1, d), lambda i, n: (0, 0)),
            yspec, yspec, yspec,
            pl.BlockSpec((None, d, tn), lambda i, n: (layer, 0, n)),
            pl.BlockSpec((None, d, tn), lambda i, n: (layer, 0, nblk + n)),
            pl.BlockSpec((None, d, tn), lambda i, n: (layer, 0, 2 * nblk + n)),
            pl.BlockSpec((None, N_BRANCH, BRANCH_WIDTH, tn), lambda i, n: (layer, 0, 0, n)),
            pl.BlockSpec((None, tn, d), lambda i, n: (layer, n, 0)),
        ],
        out_specs=pl.BlockSpec((tm, d), lambda i, n: (i, 0)),
        scratch_shapes=[pltpu.VMEM((tm, d), BF16)],
        compiler_params=_params("parallel", "arbitrary"),
        name="merge",
    )(x2, g, ya, yb, yc, wgates, wgates, wgates, wbranch, wout)


def _dsa_kernel(q_ref, k_ref, vt_ref, vtp_ref, vtd_ref, qi_ref, ki_ref, wi_ref, lng_ref, lnb_ref, bias_ref, bmax_ref,
                o_ref,
                kln_ref, knorm_ref, keys_ref, dig_ref, cls_ref, madd_ref, mnear_ref, qi4_ref, q4_ref, m_ref, l_ref, acc_ref,
                *, topk, far_unroll):
    i = pl.program_id(1)
    s_len = k_ref.shape[1]

    @pl.when(i == 0)
    def _():
        rows = 512 if s_len % 512 == 0 else QB
        lane = lax.broadcasted_iota(I32, (rows, LANES), 1)
        real = lane < IDX_DIM

        def ln_body(c, carry):
            x = ki_ref[0, pl.ds(c * rows, rows), :]
            mu = jnp.sum(x, axis=-1, keepdims=True) * (1.0 / IDX_DIM)
            xm = jnp.where(real, x - mu, 0.0)
            var = jnp.sum(xm * xm, axis=-1, keepdims=True) * (1.0 / IDX_DIM)
            y = xm * lax.rsqrt(var + EPS) * lng_ref[...] + lnb_ref[...]
            kln_ref[pl.ds(c * rows, rows), :] = y.astype(BF16)
            kf = k_ref[0, pl.ds(c * rows, rows), :].astype(F32)
            sq = [jnp.sum(kf[:, n * A_HEAD_DIM:(n + 1) * A_HEAD_DIM] ** 2, axis=-1, keepdims=True)
                  for n in range(A_KV_HEADS)]
            return tuple(jnp.maximum(a, b) for a, b in zip(carry, sq))

        ksq = lax.fori_loop(0, s_len // rows, ln_body, (jnp.zeros((rows, 1), F32),) * A_KV_HEADS)
        for n in range(A_KV_HEADS):
            knorm_ref[n] = jnp.broadcast_to(jnp.sqrt(jnp.max(ksq[n], axis=0, keepdims=True)), (1, LANES))

    w_rows = wi_ref[0].T * (IDX_HEADS ** -0.5 * IDX_DIM ** -0.5)
    for h in range(IDX_HEADS):
        qi4_ref[h * QB:(h + 1) * QB, :] = qi_ref[0, :, h * LANES:(h + 1) * LANES].astype(BF16)
    eye = (lax.broadcasted_iota(I32, (QB, QB), 0) == lax.broadcasted_iota(I32, (QB, QB), 1)).astype(BF16)
    for n in range(A_KV_HEADS):
        for g in range(A_GROUP):
            hd = n * A_GROUP + g
            q4_ref[n, g * QB:(g + 1) * QB, 0:A_HEAD_DIM] = q_ref[0, :, hd * A_HEAD_DIM:(hd + 1) * A_HEAD_DIM]
            q4_ref[n, g * QB:(g + 1) * QB, A_HEAD_DIM:A_HEAD_DIM + QB] = eye

    qf = q_ref[0].astype(F32)
    head_of_col = lax.broadcasted_iota(I32, (A_HEADS * A_HEAD_DIM, LANES), 0) // A_HEAD_DIM
    seg = (head_of_col == lax.broadcasted_iota(I32, (A_HEADS * A_HEAD_DIM, LANES), 1)).astype(BF16)
    qnorm = jnp.sqrt(_dot((qf * qf).astype(BF16), seg)).T
    bound = None
    for hd in range(A_HEADS):
        bh = qnorm[hd:hd + 1, :] * knorm_ref[hd // A_GROUP]
        bound = bh if bound is None else jnp.maximum(bound, bh)
    bound = bound * BOUND_SLACK + bmax_ref[...]
    neg_bound = -bound
    bounded = jnp.max(bound) <= BOUND_LIMIT

    last = i // SUB
    nsc = last + 1

    def rows(j):
        return pl.ds(pl.multiple_of(j * SC, SC), SC)

    def packed_digits(j, shift):
        word = None
        for blk in range(SUB):
            key = keys_ref[pl.ds(pl.multiple_of(j * SC + blk * QB, QB), QB), :]
            s = shift - 8 * blk
            f = lax.shift_right_arithmetic(key, jnp.int32(s)) if s >= 0 else lax.shift_left(key, jnp.int32(-s))
            f = f & jnp.int32(_as_i32(DIGIT_MASK << (8 * blk)))
            word = f if word is None else word | f
        return word

    def word_rows(j):
        return pl.ds(pl.multiple_of(j * QB, QB), QB)

    def index_dots(j):
        return _dot_nt(kln_ref[rows(j), :], qi4_ref[...])

    def score_chunk(j, dots, causal):
        sc = None
        for h in range(IDX_HEADS):
            t = w_rows[h:h + 1, :] * jnp.maximum(dots[:, h * QB:(h + 1) * QB], 0.0)
            sc = t if sc is None else sc + t
        bits = pltpu.bitcast(sc, I32)
        key = jnp.where(bits < 0, bits ^ jnp.int32(0x7FFFFFFF), bits)
        key = jnp.where(sc == 0.0, 0, key)
        if causal:
            key_pos = j * SC + lax.broadcasted_iota(I32, (SC, LANES), 0)
            q_pos = i * QB + lax.broadcasted_iota(I32, (SC, LANES), 1)
            key = jnp.where(key_pos <= q_pos, key, INT_MIN)
        keys_ref[rows(j), :] = key
        top = packed_digits(j, 32 - DIGIT_BITS) ^ BYTE_HALF
        dig_ref[word_rows(j), :] = top | BYTE_GUARDS
        cls_ref[word_rows(j), :] = jnp.full((QB, LANES), BYTE_LOW, I32)
        if causal:
            @pl.when(j % 2 == 0)
            def _():
                dig_ref[word_rows(j + 1), :] = jnp.full((QB, LANES), BYTE_GUARDS, I32)

    def score_body(pr, carry):
        d0 = index_dots(2 * pr)
        d1 = index_dots(2 * pr + 1)
        score_chunk(2 * pr, d0, False)
        score_chunk(2 * pr + 1, d1, False)
        return carry

    lax.fori_loop(0, last // 2, score_body, 0)

    @pl.when(last % 2 == 1)
    def _():
        score_chunk(last - 1, index_dots(last - 1), False)

    score_chunk(last, index_dots(last), True)

    t_pos = i * QB + lax.broadcasted_iota(I32, (1, LANES), 1)
    kk = jnp.minimum(t_pos + 1, topk).astype(F32)

    def count_ge(cand):
        def body(j, cnt):
            for r in range(SUB):
                blk = keys_ref[pl.ds(pl.multiple_of(j * SC + r * QB, QB), QB), :]
                cnt = cnt + jnp.where(blk >= cand, 1.0, 0.0)
            return cnt

        cnt = lax.fori_loop(0, nsc, body, jnp.zeros((QB, LANES), F32))
        return jnp.sum(cnt, axis=0, keepdims=True)

    def count_digit_ge(cand):
        packed = cand * BYTE_ONES

        def body(g, cnt):
            d = dig_ref[pl.ds(pl.multiple_of(g * 2 * QB, 2 * QB), 2 * QB), :] - packed
            hit = lax.population_count(d & BYTE_GUARDS)
            return cnt + hit[:QB] + hit[QB:]

        cnt = lax.fori_loop(0, (nsc + 1) // 2, body, jnp.zeros((QB, LANES), I32))
        return jnp.sum(cnt, axis=0, keepdims=True).astype(F32)

    def select_digit(above):
        def bit_body(bi, carry):
            t, greater = carry
            cand = t + lax.shift_left(jnp.int32(1), DIGIT_BITS - 1 - bi)
            cnt = count_digit_ge(cand)
            ok = above + cnt >= kk
            return jnp.where(ok, cand, t), jnp.where(ok, greater, cnt)

        t, greater = lax.fori_loop(0, DIGIT_BITS, bit_body,
                                   (jnp.zeros((1, LANES), I32), jnp.zeros((1, LANES), F32)))
        return t, above + greater

    thr = jnp.zeros((1, LANES), I32)
    above = jnp.zeros((1, LANES), F32)
    for level in range(DIGITS):
        shift = 32 - DIGIT_BITS * (level + 1)
        if level > 0:
            t_packed = t * BYTE_ONES

            def next_digit(j, carry, shift=shift, t_packed=t_packed):
                z = (dig_ref[word_rows(j), :] & BYTE_LOW) ^ t_packed
                same = ((z + BYTE_LOW) & BYTE_GUARDS) ^ BYTE_GUARDS
                cls = cls_ref[word_rows(j), :] & (same - lax.shift_right_logical(same, jnp.int32(7)))
                cls_ref[word_rows(j), :] = cls
                dig_ref[word_rows(j), :] = (packed_digits(j, shift) & cls) | BYTE_GUARDS
                return carry

            lax.fori_loop(0, nsc, next_digit, 0)
        t, above = select_digit(above)
        thr = thr + lax.shift_left(t - (DIGIT_HALF if level == 0 else 0), jnp.int32(shift))

    def bit_body(bi, carry):
        t, greater = carry
        cand = t + lax.shift_left(jnp.int32(1), REST_BITS - 1 - bi)
        cnt = count_ge(cand)
        ok = cnt >= kk
        return jnp.where(ok, cand, t), jnp.where(ok, greater, cnt)

    thr, above = lax.fori_loop(0, REST_BITS, bit_body, (thr, above))
    need = kk - above
    sub_k = lax.broadcasted_iota(I32, (QB, QB), 0)
    sub_k2 = lax.broadcasted_iota(I32, (QB, QB), 1)
    tril = (sub_k2 <= sub_k).astype(BF16)

    def mask_blocks(first, count, seen):
        keys, eqs, pres = [], [], []
        for r in range(count):
            key = keys_ref[pl.ds(pl.multiple_of((first + r) * QB, QB), QB), :]
            eq = key == thr
            keys.append(key)
            eqs.append(eq)
            pres.append(_dot(tril, jnp.where(eq, 1.0, 0.0).astype(BF16)))
        for r in range(count):
            sel = (keys[r] > thr) | (eqs[r] & ((seen + pres[r]) <= need))
            madd_ref[pl.ds(pl.multiple_of((first + r) * QB, QB), QB), :] = jnp.where(sel, neg_bound, NEG).astype(BF16)
            seen = seen + pres[r][QB - 1:QB, :]
        return seen

    seen = lax.fori_loop(0, nsc // 2, lambda g, seen: mask_blocks(g * 2 * SUB, 2 * SUB, seen),
                         jnp.zeros((1, LANES), F32))

    @pl.when(nsc % 2 == 1)
    def _():
        mask_blocks(last * SUB, SUB, seen)

    def split_near(slot, blk):
        blk_rows = pl.ds(pl.multiple_of(blk * QB, QB), QB)
        mnear_ref[slot] = madd_ref[blk_rows, :]
        madd_ref[blk_rows, :] = jnp.full((QB, LANES), NEG, BF16)

    split_near(1, i)

    @pl.when(i >= 1)
    def _():
        split_near(0, i - 1)

    m_ref[...] = jnp.full(m_ref.shape, NEG, F32)
    l_ref[...] = jnp.zeros(l_ref.shape, F32)
    acc_ref[...] = jnp.zeros(acc_ref.shape, F32)

    def logits(kc, madd, n):
        return _dot_nt(jnp.concatenate([kc, madd], axis=1), q4_ref[n])

    def far_logits(j, n):
        return logits(k_ref[0, rows(j), n * A_HEAD_DIM:(n + 1) * A_HEAD_DIM], madd_ref[rows(j), :], n)

    def online_update(n, s, vt):
        m_prev = m_ref[n]
        m_new = jnp.maximum(m_prev, jnp.max(s, axis=0, keepdims=True))
        alpha = jnp.exp2(m_prev - m_new)
        p = jnp.exp2(s - m_new)
        l_ref[n] = alpha * l_ref[n] + jnp.sum(p, axis=0, keepdims=True)
        acc_ref[n] = alpha * acc_ref[n] + _dot(vt, p.astype(BF16))
        m_ref[n] = m_new

    def bounded_update(n, s, vt):
        p = jnp.exp2(s)
        l_ref[n] += jnp.sum(p, axis=0, keepdims=True)
        acc_ref[n] += _dot(vt, p.astype(BF16))

    def far_vt(j, n):
        return vt_ref[0, j, n * A_HEAD_DIM:(n + 1) * A_HEAD_DIM, :]

    n_far = jnp.where(i % SUB >= 2, nsc, last)

    def attend(update):
        def sweep(first, count):
            tasks = [(first + c, n) for c in range(count) for n in range(A_KV_HEADS)]
            ahead = 4
            pending = [far_logits(j, n) for j, n in tasks[:ahead]]
            for t, (j, n) in enumerate(tasks):
                update(n, pending.pop(0), far_vt(j, n))
                if t + ahead < len(tasks):
                    pending.append(far_logits(*tasks[t + ahead]))

        def sweep_body(g, carry):
            sweep(far_unroll * g, far_unroll)
            return carry

        lax.fori_loop(0, n_far // far_unroll, sweep_body, 0)
        for rem in range(1, far_unroll):
            @pl.when(n_far % far_unroll == rem)
            def _(rem=rem):
                sweep(n_far - rem, rem)

        def near_step(slot, blk, vts_ref):
            blk_rows = pl.ds(pl.multiple_of(blk * QB, QB), QB)
            ss = [logits(k_ref[0, blk_rows, n * A_HEAD_DIM:(n + 1) * A_HEAD_DIM], mnear_ref[slot], n)
                  + bias_ref[slot, n] for n in range(A_KV_HEADS)]
            for n in range(A_KV_HEADS):
                update(n, ss[n], vts_ref[0, 0, n * A_HEAD_DIM:(n + 1) * A_HEAD_DIM, :])

        @pl.when(i >= 1)
        def _():
            near_step(0, i - 1, vtp_ref)

        near_step(1, i, vtd_ref)

    @pl.when(bounded)
    def _():
        attend(bounded_update)

    @pl.when(jnp.logical_not(bounded))
    def _():
        attend(online_update)

    for n in range(A_KV_HEADS):
        out_t = acc_ref[n] / l_ref[n]
        for g in range(A_GROUP):
            hd = n * A_GROUP + g
            o_ref[0, :, hd * A_HEAD_DIM:(hd + 1) * A_HEAD_DIM] = out_t[:, g * QB:(g + 1) * QB].T.astype(o_ref.dtype)


def _dsa(p16, p32, vt, vt_blk, ln_g, ln_b, bias_near, *, topk, far_unroll):
    b, s, _ = p16.shape
    bias_max = jnp.broadcast_to(jnp.max(jnp.abs(bias_near)) + BOUND_MARGIN, (1, LANES)).astype(F32)
    aq = A_HEADS * A_HEAD_DIM
    akv = A_KV_HEADS * A_HEAD_DIM
    qiw = IDX_HEADS * LANES
    nch = s // QB
    gq = A_GROUP * QB
    k_blk = (aq + X_HEADS * X_HEAD_DIM) // akv
    qi_blk = 2 * LRU_WIDTH // qiw
    ki_blk = (2 * LRU_WIDTH + qiw) // LANES
    return pl.pallas_call(
        functools.partial(_dsa_kernel, topk=topk, far_unroll=far_unroll),
        out_shape=jax.ShapeDtypeStruct((b, s, aq), BF16),
        grid=(b, nch),
        in_specs=[
            pl.BlockSpec((1, QB, aq), lambda bi, i: (bi, i, 0)),
            pl.BlockSpec((1, s, akv), lambda bi, i: (bi, 0, k_blk)),
            pl.BlockSpec((1, s // SC, akv, SC), lambda bi, i: (bi, 0, 0, 0)),
            pl.BlockSpec((1, 1, akv, QB), lambda bi, i: (bi, jnp.maximum(i - 1, 0), 0, 0)),
            pl.BlockSpec((1, 1, akv, QB), lambda bi, i: (bi, i, 0, 0)),
            pl.BlockSpec((1, QB, qiw), lambda bi, i: (bi, i, qi_blk)),
            pl.BlockSpec((1, s, LANES), lambda bi, i: (bi, 0, ki_blk)),
            pl.BlockSpec((1, QB, LANES), lambda bi, i: (bi, i, ki_blk + 1)),
            pl.BlockSpec((1, LANES), lambda bi, i: (0, 0)),
            pl.BlockSpec((1, LANES), lambda bi, i: (0, 0)),
            pl.BlockSpec((2, A_KV_HEADS, QB, gq), lambda bi, i: (0, 0, 0, 0)),
            pl.BlockSpec((1, LANES), lambda bi, i: (0, 0)),
        ],
        out_specs=pl.BlockSpec((1, QB, aq), lambda bi, i: (bi, i, 0)),
        scratch_shapes=[
            pltpu.VMEM((s, LANES), BF16),
            pltpu.VMEM((A_KV_HEADS, 1, LANES), F32),
            pltpu.VMEM((s, LANES), I32),
            pltpu.VMEM((-(-(s // SC) // 2) * 2 * QB, LANES), I32),
            pltpu.VMEM((s // SUB, LANES), I32),
            pltpu.VMEM((s, LANES), BF16),
            pltpu.VMEM((2, QB, LANES), BF16),
            pltpu.VMEM((IDX_HEADS * QB, LANES), BF16),
            pltpu.VMEM((A_KV_HEADS, gq, A_HEAD_DIM + QB), BF16),
            pltpu.VMEM((A_KV_HEADS, 1, gq), F32),
            pltpu.VMEM((A_KV_HEADS, 1, gq), F32),
            pltpu.VMEM((A_KV_HEADS, A_HEAD_DIM, gq), F32),
        ],
        compiler_params=_params("arbitrary", "arbitrary"),
        name="dsa",
    )(p16, p16, vt, vt_blk, vt_blk, p32, p32, p32, ln_g, ln_b, bias_near, bias_max)


def _t5_bucket(dist):
    max_exact = N_BUCKETS // 2
    d = jnp.maximum(dist, 0)
    df = jnp.maximum(d, 1).astype(F32)
    large = max_exact + (jnp.log(df / max_exact) / math.log(MAX_DISTANCE / max_exact)
                         * (N_BUCKETS - max_exact)).astype(I32)
    large = jnp.minimum(large, N_BUCKETS - 1)
    return jnp.where(d < max_exact, d, large)


def _near_bias(rel_bias):
    t = jnp.arange(QB, dtype=I32)[:, None]
    s = jnp.arange(QB, dtype=I32)[None, :]
    tiles = []
    for off in (QB, 0):
        bucket = _t5_bucket(t - s + off)
        onehot = (bucket[..., None] == jnp.arange(N_BUCKETS, dtype=I32)).astype(F32)
        table = (rel_bias - rel_bias[N_BUCKETS - 1]) * LOG2E
        tile = jnp.einsum('tsb,bh->tsh', onehot, table, precision=lax.Precision.HIGHEST)
        tile = tile.transpose(1, 2, 0).reshape(QB, A_KV_HEADS, A_GROUP * QB)
        tiles.append(tile.transpose(1, 0, 2))
    return jnp.stack(tiles).astype(F32)


def _pad_cols(w, width):
    return jnp.pad(w, [(0, 0)] * (w.ndim - 1) + [(0, width - w.shape[-1])])


def _split_points(d_model):
    aq = A_HEADS * A_HEAD_DIM
    akv = A_KV_HEADS * A_HEAD_DIM
    sizes = (aq, akv, akv, IDX_HEADS * IDX_DIM, IDX_DIM, IDX_HEADS,
             LRU_WIDTH, LRU_WIDTH, X_HEADS * X_HEAD_DIM, N_BRANCH * d_model)
    pts, acc = [0], 0
    for n in sizes:
        acc += n
        pts.append(acc)
    return pts


def _regroup_kernel(w_ref, o16_ref, o32_ref, og_ref, *, pts):
    (q0, k0, v0, qi0, ki0, wi0, x0, g0, qm0, gt0, end) = pts

    def seg(lo, hi, scale=None):
        v = w_ref[0, :, lo:hi]
        if scale is not None:
            v = v * scale
        return v.astype(BF16)

    o16_ref[0, :, 0:k0 - q0] = seg(q0, k0, A_HEAD_DIM ** -0.5 * LOG2E)
    c = k0 - q0
    o16_ref[0, :, c:c + gt0 - qm0] = seg(qm0, gt0, X_HEAD_DIM ** -0.5)
    c += gt0 - qm0
    o16_ref[0, :, c:c + qi0 - k0] = seg(k0, qi0)
    o32_ref[0, :, 0:qm0 - x0] = seg(x0, qm0)
    c = qm0 - x0
    o32_ref[0, :, c:] = jnp.zeros((o32_ref.shape[1], o32_ref.shape[2] - c), BF16)
    for h in range(IDX_HEADS):
        o32_ref[0, :, c + h * LANES:c + h * LANES + IDX_DIM] = seg(qi0 + h * IDX_DIM, qi0 + (h + 1) * IDX_DIM)
    c += IDX_HEADS * LANES
    o32_ref[0, :, c:c + IDX_DIM] = seg(ki0, wi0)
    c += LANES
    o32_ref[0, :, c:c + IDX_HEADS] = seg(wi0, x0)
    og_ref[0] = seg(gt0, end)


def _prep_w_in(w_in, d_model, *, tk):
    depth, k, n = w_in.shape
    pts = _split_points(d_model)
    n16 = pts[3] + pts[9] - pts[8]
    n32 = 2 * LRU_WIDTH + (IDX_HEADS + 2) * LANES
    ng = n - pts[9]
    return pl.pallas_call(
        functools.partial(_regroup_kernel, pts=tuple(pts)),
        out_shape=(jax.ShapeDtypeStruct((depth, k, n16), BF16), jax.ShapeDtypeStruct((depth, k, n32), BF16),
                   jax.ShapeDtypeStruct((depth, k, ng), BF16)),
        grid=(depth, k // tk),
        in_specs=[pl.BlockSpec((1, tk, n), lambda l, i: (l, i, 0))],
        out_specs=(pl.BlockSpec((1, tk, n16), lambda l, i: (l, i, 0)),
                   pl.BlockSpec((1, tk, n32), lambda l, i: (l, i, 0)),
                   pl.BlockSpec((1, tk, ng), lambda l, i: (l, i, 0))),
        compiler_params=_params("parallel", "parallel"),
        name="regroup",
    )(w_in)


def _split_gu_kernel(w_ref, og_ref, ou_ref):
    d_ff = w_ref.shape[2] // 2
    pad = og_ref.shape[2] - d_ff
    for o_ref, lo in ((og_ref, 0), (ou_ref, d_ff)):
        o_ref[0, :, 0:d_ff] = w_ref[0, :, lo:lo + d_ff].astype(BF16)
        if pad:
            o_ref[0, :, d_ff:] = jnp.zeros((o_ref.shape[1], pad), BF16)


def _pad_down_kernel(w_ref, o_ref, *, n_real):
    @pl.when(pl.program_id(1) < n_real)
    def _():
        o_ref[...] = w_ref[...].astype(BF16)

    @pl.when(pl.program_id(1) >= n_real)
    def _():
        o_ref[...] = jnp.zeros_like(o_ref)


def _pad_ff(w_gu, w_down, tf, *, tk):
    depth, d, _ = w_gu.shape
    d_ff = w_down.shape[1]
    fp = -(-d_ff // tf) * tf
    assert d_ff % LANES == 0
    wg, wu = pl.pallas_call(
        _split_gu_kernel,
        out_shape=(jax.ShapeDtypeStruct((depth, d, fp), BF16),) * 2,
        grid=(depth, d // tk),
        in_specs=[pl.BlockSpec((1, tk, 2 * d_ff), lambda l, i: (l, i, 0))],
        out_specs=(pl.BlockSpec((1, tk, fp), lambda l, i: (l, i, 0)),) * 2,
        compiler_params=_params("parallel", "parallel"),
        name="split_gu",
    )(w_gu)
    n_real = d_ff // LANES
    wd = pl.pallas_call(
        functools.partial(_pad_down_kernel, n_real=n_real),
        out_shape=jax.ShapeDtypeStruct((depth, fp, d), BF16),
        grid=(depth, fp // LANES),
        in_specs=[pl.BlockSpec((1, LANES, d), lambda l, i: (l, jnp.minimum(i, n_real - 1), 0))],
        out_specs=pl.BlockSpec((1, LANES, d), lambda l, i: (l, i, 0)),
        compiler_params=_params("parallel", "arbitrary"),
        name="pad_down",
    )(w_down)
    return wg, wu, wd


def _tiles(m, s, d_model):
    return dict(
        ffn_tm=min(1024, m), ffn_final_tm=min(1024, m), ffn_tf=512,
        proj_tm=min(256, m), proj_tn=256,
        merge_tm=min(512, m), merge_tn=min(512, d_model),
        lru_tc=min(256, s), mem_tm=min(512, s),
    )


def kernel(x, mem, rel_bias, final_norm, norm_ff1, w_ff1_gu, w_ff1_down, norm_mix, w_in, conv_w, conv_b, w_a, b_a, w_i, b_i, lam, idx_ln_g, idx_ln_b, mem_norm, w_mem_kv, w_branch, w_out, norm_ff2, w_ff2_gu, w_ff2_down):
    b, s, d = x.shape
    nm = mem.shape[1]
    depth = w_in.shape[0]
    m = b * s
    assert s % SC == 0 and d % LANES == 0
    tl = _tiles(m, s, d)
    topk = min(TOPK_MAX, s // 4)
    bias_near = _near_bias(rel_bias)
    row = lambda v: v.reshape(1, -1)

    ff1 = _pad_ff(w_ff1_gu, w_ff1_down, tl["ffn_tf"], tk=min(256, d))
    ff2 = _pad_ff(w_ff2_gu, w_ff2_down, tl["ffn_tf"], tk=min(256, d))
    w16, w32, wgates = _prep_w_in(w_in, d, tk=min(256, d))
    w_a16, w_i16 = w_a.astype(BF16), w_i.astype(BF16)
    w_mem16, w_branch16, w_out16 = w_mem_kv.astype(BF16), w_branch.astype(BF16), w_out.astype(BF16)
    ln_g = _pad_cols(idx_ln_g, LANES)
    ln_b = _pad_cols(idx_ln_b, LANES)

    x2 = x.reshape(m, d)
    mem2 = mem.reshape(b * nm, d)
    for l in range(depth):
        x2 = _ffn(x2, row(norm_ff1[l]), *ff1, row(final_norm), l, final_norm=False,
                  tm=tl["ffn_tm"], tf=tl["ffn_tf"])

        p16, p32 = _proj(x2, row(norm_mix[l]), w16, w32, l, tm=tl["proj_tm"])
        p16 = p16.reshape(b, s, -1)
        p32 = p32.reshape(b, s, -1)

        v_cols = slice(p16.shape[-1] - A_KV_HEADS * A_HEAD_DIM, p16.shape[-1])
        vt = p16[..., v_cols].reshape(b, s // SC, SC, -1).transpose(0, 1, 3, 2)
        vt_blk = p16[..., v_cols].reshape(b, s // QB, QB, -1).transpose(0, 1, 3, 2)
        y_a = _dsa(p16, p32, vt, vt_blk, row(ln_g[l]), row(ln_b[l]),
                   bias_near, topk=topk, far_unroll=4)
        y_b = _lru(p32, conv_w[l], row(conv_b[l]), w_a16, w_i16,
                   row(b_a[l]), row(b_i[l]), row(lam[l]), l, tc=tl["lru_tc"])
        mkv = _norm_matmul(mem2, row(mem_norm[l]), w_mem16, l, BF16,
                           tm=min(256, b * nm), tn=tl["proj_tn"], name="memkv")
        y_c = _memattn(p16, mkv.reshape(b, nm, -1), tm=tl["mem_tm"])

        x2 = _merge(x2, row(norm_mix[l]), y_a.reshape(m, -1), y_b.reshape(m, -1), y_c.reshape(m, -1),
                    wgates, w_branch16, w_out16, l,
                    tm=tl["merge_tm"], tn=tl["merge_tn"])

        is_last = l == depth - 1
        x2 = _ffn(x2, row(norm_ff2[l]), *ff2, row(final_norm), l, final_norm=is_last,
                  tm=tl["ffn_final_tm" if is_last else "ffn_tm"], tf=tl["ffn_tf"])
    return x2.reshape(b, s, d)
```

```python
import functools
import math

import jax
import jax.numpy as jnp
from jax import lax
from jax.experimental import pallas as pl
from jax.experimental.pallas import tpu as pltpu

F32 = jnp.float32
BF16 = jnp.bfloat16
I32 = jnp.int32

EPS = 1e-6
A_HEADS = 8
A_KV_HEADS = 2
A_HEAD_DIM = 128
A_GROUP = A_HEADS // A_KV_HEADS
IDX_HEADS = 4
IDX_DIM = 64
TOPK_MAX = 256
LRU_WIDTH = 1024
LRU_BLOCKS = 8
LRU_BLOCK_W = LRU_WIDTH // LRU_BLOCKS
CONV_WIDTH = 4
LRU_C = 8.0
X_HEADS = 4
X_HEAD_DIM = 256
N_BRANCH = 3
BRANCH_WIDTH = 1024
N_BUCKETS = 32
MAX_DISTANCE = 128

LANES = 128
SUBLANES = 8
VMEM_LIMIT = 56 * 1024 * 1024
ROW_CHUNK = 256

QB = 128
SC = 512
SUB = SC // QB
SCORE_UNROLL = 4
NEG = -1e30
LOG2E = math.log2(math.e)
INT_MIN = -2 ** 31


def _as_i32(v):
    return v - (1 << 32) if v >= (1 << 31) else v


DIGIT_BITS = 7
DIGITS = 4
REST_BITS = 32 - DIGITS * DIGIT_BITS
DIGIT_MASK = (1 << DIGIT_BITS) - 1
DIGIT_HALF = 1 << (DIGIT_BITS - 1)
BYTE_ONES = _as_i32(0x01010101)
BYTE_LOW = _as_i32(0x7F7F7F7F)
BYTE_GUARDS = _as_i32(0x80808080)
BYTE_HALF = _as_i32(0x40404040)
assert SUB == 4 and DIGIT_BITS == 7

BOUND_SLACK = 1.02
BOUND_MARGIN = 0.5
BOUND_LIMIT = 55.0


def _params(*sem):
    return pltpu.CompilerParams(dimension_semantics=sem, vmem_limit_bytes=VMEM_LIMIT)


def _rms(x, g):
    ms = jnp.mean(x * x, axis=-1, keepdims=True)
    return x * lax.rsqrt(ms + EPS) * g


def _sigmoid(x):
    return 1.0 / (1.0 + jnp.exp(-x))


def _dot(a, b):
    return jnp.dot(a, b, preferred_element_type=F32)


def _dot_nt(a, b):
    return lax.dot_general(a, b, (((1,), (1,)), ((), ())), preferred_element_type=F32)


def _ffn_kernel(x_ref, g_ref, wg_ref, wu_ref, wd_ref, fg_ref, o_ref, h_ref, *, final_norm):
    f = pl.program_id(1)

    def for_row_chunks(fn):
        rc = min(ROW_CHUNK, x_ref.shape[0])

        def body(r, carry):
            fn(pl.ds(pl.multiple_of(r * rc, rc), rc))
            return carry

        lax.fori_loop(0, x_ref.shape[0] // rc, body, 0)

    @pl.when(f == 0)
    def _():
        def prologue(rs):
            h_ref[rs, :] = _rms(x_ref[rs, :], g_ref[...]).astype(BF16)
            o_ref[rs, :] = jnp.zeros((rs.size, o_ref.shape[1]), F32)

        for_row_chunks(prologue)

    h = h_ref[...]
    g = _dot(h, wg_ref[...])
    u = _dot(h, wu_ref[...])
    a = (g * _sigmoid(g) * u).astype(BF16)
    o_ref[...] += _dot(a, wd_ref[...])

    @pl.when(f == pl.num_programs(1) - 1)
    def _():
        def epilogue(rs):
            y = x_ref[rs, :] + 0.5 * o_ref[rs, :]
            if final_norm:
                y = _rms(y, fg_ref[...])
            o_ref[rs, :] = y

        for_row_chunks(epilogue)


def _ffn(x2, g, wg, wu, wd, fg, layer, *, final_norm, tm, tf):
    m, d = x2.shape
    fp = wg.shape[-1]
    return pl.pallas_call(
        functools.partial(_ffn_kernel, final_norm=final_norm),
        out_shape=jax.ShapeDtypeStruct((m, d), F32),
        grid=(m // tm, fp // tf),
        in_specs=[
            pl.BlockSpec((tm, d), lambda i, f: (i, 0)),
            pl.BlockSpec((1, d), lambda i, f: (0, 0)),
            pl.BlockSpec((None, d, tf), lambda i, f: (layer, 0, f)),
            pl.BlockSpec((None, d, tf), lambda i, f: (layer, 0, f)),
            pl.BlockSpec((None, tf, d), lambda i, f: (layer, f, 0)),
            pl.BlockSpec((1, d), lambda i, f: (0, 0)),
        ],
        out_specs=pl.BlockSpec((tm, d), lambda i, f: (i, 0)),
        scratch_shapes=[pltpu.VMEM((tm, d), BF16)],
        compiler_params=_params("parallel", "arbitrary"),
        name="ffn",
    )(x2, g, wg, wu, wd, fg)


def _norm_matmul_kernel(x_ref, g_ref, w_ref, o_ref, h_ref):
    @pl.when(pl.program_id(1) == 0)
    def _():
        h_ref[...] = _rms(x_ref[...], g_ref[...]).astype(BF16)

    o_ref[...] = _dot(h_ref[...], w_ref[...]).astype(o_ref.dtype)


def _norm_matmul(x2, g, w, layer, out_dtype, *, tm, tn, name):
    m, d = x2.shape
    n = w.shape[-1]
    return pl.pallas_call(
        _norm_matmul_kernel,
        out_shape=jax.ShapeDtypeStruct((m, n), out_dtype),
        grid=(m // tm, n // tn),
        in_specs=[
            pl.BlockSpec((tm, d), lambda i, j: (i, 0)),
            pl.BlockSpec((1, d), lambda i, j: (0, 0)),
            pl.BlockSpec((None, d, tn), lambda i, j: (layer, 0, j)),
        ],
        out_specs=pl.BlockSpec((tm, tn), lambda i, j: (i, j)),
        scratch_shapes=[pltpu.VMEM((tm, d), BF16)],
        compiler_params=_params("parallel", "arbitrary"),
        name=name,
    )(x2, g, w)


def _proj_kernel(x_ref, g_ref, w16_ref, w32_ref, o16_ref, o32_ref):
    h = _rms(x_ref[...], g_ref[...]).astype(BF16)
    o16_ref[...] = _dot(h, w16_ref[...]).astype(o16_ref.dtype)
    o32_ref[...] = _dot(h, w32_ref[...])


def _proj(x2, g, w16, w32, layer, *, tm):
    m, d = x2.shape
    n16, n32 = w16.shape[-1], w32.shape[-1]
    resident = pl.Buffered(1)
    return pl.pallas_call(
        _proj_kernel,
        out_shape=(jax.ShapeDtypeStruct((m, n16), BF16), jax.ShapeDtypeStruct((m, n32), F32)),
        grid=(m // tm,),
        in_specs=[
            pl.BlockSpec((tm, d), lambda i: (i, 0)),
            pl.BlockSpec((1, d), lambda i: (0, 0)),
            pl.BlockSpec((None, d, n16), lambda i: (layer, 0, 0), pipeline_mode=resident),
            pl.BlockSpec((None, d, n32), lambda i: (layer, 0, 0), pipeline_mode=resident),
        ],
        out_specs=(pl.BlockSpec((tm, n16), lambda i: (i, 0)), pl.BlockSpec((tm, n32), lambda i: (i, 0))),
        compiler_params=_params("parallel"),
        name="proj",
    )(x2, g, w16, w32)


def _lru_kernel(x_ref, gate_ref, cw_ref, cb_ref, wa_ref, wi_ref, ba_ref, bi_ref, lam_ref,
                o_ref, xbuf_ref, hc_ref, *, tc):
    halo = SUBLANES

    @pl.when(pl.program_id(1) == 0)
    def _():
        xbuf_ref[0:halo, :] = jnp.zeros((halo, LRU_WIDTH), F32)
        hc_ref[...] = jnp.zeros_like(hc_ref)

    xbuf_ref[halo:halo + tc, :] = x_ref[0]
    base = halo - (CONV_WIDTH - 1)
    xc = xbuf_ref[base:base + tc, :] * cw_ref[0:1, :]
    for j in range(1, CONV_WIDTH):
        xc = xc + xbuf_ref[base + j:base + j + tc, :] * cw_ref[j:j + 1, :]
    xc = cb_ref[...] + xc
    xbuf_ref[0:halo, :] = xbuf_ref[tc:tc + halo, :]

    xcb = xc.astype(BF16)
    r_parts, i_parts = [], []
    for n in range(LRU_BLOCKS):
        blk = xcb[:, n * LRU_BLOCK_W:(n + 1) * LRU_BLOCK_W]
        r_parts.append(_dot(blk, wa_ref[n]))
        i_parts.append(_dot(blk, wi_ref[n]))
    r = _sigmoid(jnp.concatenate(r_parts, axis=1) + ba_ref[...])
    gi = _sigmoid(jnp.concatenate(i_parts, axis=1) + bi_ref[...])

    softplus_neg_lam = jnp.log1p(jnp.exp(-lam_ref[...]))
    log_a = (-LRU_C) * r * softplus_neg_lam
    a = jnp.exp(log_a)
    mult = jnp.sqrt(jnp.maximum((1.0 - a) * (1.0 + a), 0.0))
    u = mult * (gi * xc)

    n_groups = tc // SUBLANES
    a = a.reshape(n_groups, SUBLANES, LRU_WIDTH)
    u = u.reshape(n_groups, SUBLANES, LRU_WIDTH)
    row = lax.broadcasted_iota(I32, (n_groups, SUBLANES, LRU_WIDTH), 1)
    d = 1
    while d < SUBLANES:
        keep = row >= d
        a_sh = jnp.where(keep, pltpu.roll(a, d, 1), 1.0)
        u_sh = jnp.where(keep, pltpu.roll(u, d, 1), 0.0)
        u = a * u_sh + u
        a = a * a_sh
        d *= 2
    carry = hc_ref[...]
    groups = []
    for g in range(n_groups):
        hg = a[g] * carry + u[g]
        carry = hg[SUBLANES - 1:SUBLANES, :]
        groups.append(hg)
    h = jnp.concatenate(groups, axis=0)
    hc_ref[...] = carry

    gate = gate_ref[0]
    gelu = 0.5 * gate * (1.0 + jnp.tanh(math.sqrt(2.0 / math.pi) * (gate + 0.044715 * (gate * gate * gate))))
    o_ref[0] = (h * gelu).astype(o_ref.dtype)


def _lru(p32, conv_w, conv_b, w_a, w_i, b_a, b_i, lam, layer, *, tc):
    b, s, _ = p32.shape
    vec = pl.BlockSpec((1, LRU_WIDTH), lambda bi, c: (0, 0))
    wblk = pl.BlockSpec((None, LRU_BLOCKS, LRU_BLOCK_W, LRU_BLOCK_W), lambda bi, c: (layer, 0, 0, 0))
    return pl.pallas_call(
        functools.partial(_lru_kernel, tc=tc),
        out_shape=jax.ShapeDtypeStruct((b, s, LRU_WIDTH), BF16),
        grid=(b, s // tc),
        in_specs=[
            pl.BlockSpec((1, tc, LRU_WIDTH), lambda bi, c: (bi, c, 0)),
            pl.BlockSpec((1, tc, LRU_WIDTH), lambda bi, c: (bi, c, 1)),
            pl.BlockSpec((CONV_WIDTH, LRU_WIDTH), lambda bi, c: (0, 0)),
            vec, wblk, wblk, vec, vec, vec,
        ],
        out_specs=pl.BlockSpec((1, tc, LRU_WIDTH), lambda bi, c: (bi, c, 0)),
        scratch_shapes=[pltpu.VMEM((tc + SUBLANES, LRU_WIDTH), F32), pltpu.VMEM((1, LRU_WIDTH), F32)],
        compiler_params=_params("arbitrary", "arbitrary"),
        name="rglru",
    )(p32, p32, conv_w, conv_b, w_a, w_i, b_a, b_i, lam)


def _memattn_kernel(q_ref, kv_ref, o_ref):
    kw = X_HEADS * X_HEAD_DIM
    for h in range(X_HEADS):
        sl = slice(h * X_HEAD_DIM, (h + 1) * X_HEAD_DIM)
        q = q_ref[0, :, sl]
        k = kv_ref[0, :, sl]
        v = kv_ref[0, :, kw + h * X_HEAD_DIM:kw + (h + 1) * X_HEAD_DIM]
        s = _dot_nt(q, k)
        m = jnp.max(s, axis=-1, keepdims=True)
        p = jnp.exp(s - m)
        l = jnp.sum(p, axis=-1, keepdims=True)
        o = _dot(p.astype(BF16), v)
        o_ref[0, :, sl] = (o / l).astype(o_ref.dtype)


def _memattn(p16, mkv, *, tm):
    b, s, _ = p16.shape
    nm = mkv.shape[1]
    kw = X_HEADS * X_HEAD_DIM
    return pl.pallas_call(
        _memattn_kernel,
        out_shape=jax.ShapeDtypeStruct((b, s, kw), BF16),
        grid=(b, s // tm),
        in_specs=[
            pl.BlockSpec((1, tm, kw), lambda bi, i: (bi, i, 1)),
            pl.BlockSpec((1, nm, 2 * kw), lambda bi, i: (bi, 0, 0)),
        ],
        out_specs=pl.BlockSpec((1, tm, kw), lambda bi, i: (bi, i, 0)),
        compiler_params=_params("parallel", "parallel"),
        name="memattn",
    )(p16, mkv)


def _merge_kernel(x_ref, g_ref, ya_ref, yb_ref, yc_ref, wg0_ref, wg1_ref, wg2_ref, wb_ref, wo_ref,
                  o_ref, h_ref):
    n = pl.program_id(1)

    @pl.when(n == 0)
    def _():
        h_ref[...] = _rms(x_ref[...], g_ref[...]).astype(BF16)
        o_ref[...] = jnp.zeros_like(o_ref)

    h = h_ref[...]
    merged = None
    for j, (y_ref, wg_ref) in enumerate(((ya_ref, wg0_ref), (yb_ref, wg1_ref), (yc_ref, wg2_ref))):
        term = _sigmoid(_dot(h, wg_ref[...])) * _dot(y_ref[...], wb_ref[j])
        merged = term if merged is None else merged + term
    o_ref[...] += _dot(merged.astype(BF16), wo_ref[...])

    @pl.when(n == pl.num_programs(1) - 1)
    def _():
        o_ref[...] = x_ref[...] + o_ref[...]


def _merge(x2, g, ya, yb, yc, wgates, wbranch, wout, layer, *, tm, tn):
    m, d = x2.shape
    nblk = d // tn
    yspec = pl.BlockSpec((tm, BRANCH_WIDTH), lambda i, n: (i, 0))
    return pl.pallas_call(
        _merge_kernel,
        out_shape=jax.ShapeDtypeStruct((m, d), F32),
        grid=(m // tm, nblk),
        in_specs=[
            pl.BlockSpec((tm, d), lambda i, n: (i, 0)),
            pl.BlockSpec((1, d), lambda i, n: (0, 0)),
            yspec, yspec, yspec,
            pl.BlockSpec((None, d, tn), lambda i, n: (layer, 0, n)),
            pl.BlockSpec((None, d, tn), lambda i, n: (layer, 0, nblk + n)),
            pl.BlockSpec((None, d, tn), lambda i, n: (layer, 0, 2 * nblk + n)),
            pl.BlockSpec((None, N_BRANCH, BRANCH_WIDTH, tn), lambda i, n: (layer, 0, 0, n)),
            pl.BlockSpec((None, tn, d), lambda i, n: (layer, n, 0)),
        ],
        out_specs=pl.BlockSpec((tm, d), lambda i, n: (i, 0)),
        scratch_shapes=[pltpu.VMEM((tm, d), BF16)],
        compiler_params=_params("parallel", "arbitrary"),
        name="merge",
    )(x2, g, ya, yb, yc, wgates, wgates, wgates, wbranch, wout)


def _dsa_kernel(q_ref, k_ref, vt_ref, vtp_ref, vtd_ref, qi_ref, ki_ref, wi_ref, lng_ref, lnb_ref, bias_ref, bmax_ref,
                o_ref,
                kln_ref, knorm_ref, keys_ref, dig_ref, cls_ref, madd_ref, mnear_ref, qi4_ref, q4_ref, m_ref, l_ref, acc_ref,
                *, topk, far_unroll):
    i = pl.program_id(1)
    s_len = k_ref.shape[1]

    @pl.when(i == 0)
    def _():
        rows = 512 if s_len % 512 == 0 else QB
        lane = lax.broadcasted_iota(I32, (rows, LANES), 1)
        real = lane < IDX_DIM

        def ln_body(c, carry):
            x = ki_ref[0, pl.ds(c * rows, rows), :]
            mu = jnp.sum(x, axis=-1, keepdims=True) * (1.0 / IDX_DIM)
            xm = jnp.where(real, x - mu, 0.0)
            var = jnp.sum(xm * xm, axis=-1, keepdims=True) * (1.0 / IDX_DIM)
            y = xm * lax.rsqrt(var + EPS) * lng_ref[...] + lnb_ref[...]
            kln_ref[pl.ds(c * rows, rows), :] = y.astype(BF16)
            kf = k_ref[0, pl.ds(c * rows, rows), :].astype(F32)
            sq = [jnp.sum(kf[:, n * A_HEAD_DIM:(n + 1) * A_HEAD_DIM] ** 2, axis=-1, keepdims=True)
                  for n in range(A_KV_HEADS)]
            return tuple(jnp.maximum(a, b) for a, b in zip(carry, sq))

        ksq = lax.fori_loop(0, s_len // rows, ln_body, (jnp.zeros((rows, 1), F32),) * A_KV_HEADS)
        for n in range(A_KV_HEADS):
            knorm_ref[n] = jnp.broadcast_to(jnp.sqrt(jnp.max(ksq[n], axis=0, keepdims=True)), (1, LANES))

    w_rows = wi_ref[0].T * (IDX_HEADS ** -0.5 * IDX_DIM ** -0.5)
    for h in range(IDX_HEADS):
        qi4_ref[h * QB:(h + 1) * QB, :] = qi_ref[0, :, h * LANES:(h + 1) * LANES].astype(BF16)
    eye = (lax.broadcasted_iota(I32, (QB, QB), 0) == lax.broadcasted_iota(I32, (QB, QB), 1)).astype(BF16)
    for n in range(A_KV_HEADS):
        for g in range(A_GROUP):
            hd = n * A_GROUP + g
            q4_ref[n, g * QB:(g + 1) * QB, 0:A_HEAD_DIM] = q_ref[0, :, hd * A_HEAD_DIM:(hd + 1) * A_HEAD_DIM]
            q4_ref[n, g * QB:(g + 1) * QB, A_HEAD_DIM:A_HEAD_DIM + QB] = eye

    qf = q_ref[0].astype(F32)
    head_of_col = lax.broadcasted_iota(I32, (A_HEADS * A_HEAD_DIM, LANES), 0) // A_HEAD_DIM
    seg = (head_of_col == lax.broadcasted_iota(I32, (A_HEADS * A_HEAD_DIM, LANES), 1)).astype(BF16)
    qnorm = jnp.sqrt(_dot((qf * qf).astype(BF16), seg)).T
    bound = None
    for hd in range(A_HEADS):
        bh = qnorm[hd:hd + 1, :] * knorm_ref[hd // A_GROUP]
        bound = bh if bound is None else jnp.maximum(bound, bh)
    bound = bound * BOUND_SLACK + bmax_ref[...]
    neg_bound = -bound
    bounded = jnp.max(bound) <= BOUND_LIMIT

    last = i // SUB
    nsc = last + 1

    def rows(j):
        return pl.ds(pl.multiple_of(j * SC, SC), SC)

    def packed_digits(j, shift):
        word = None
        for blk in range(SUB):
            key = keys_ref[pl.ds(pl.multiple_of(j * SC + blk * QB, QB), QB), :]
            s = shift - 8 * blk
            f = lax.shift_right_arithmetic(key, jnp.int32(s)) if s >= 0 else lax.shift_left(key, jnp.int32(-s))
            f = f & jnp.int32(_as_i32(DIGIT_MASK << (8 * blk)))
            word = f if word is None else word | f
        return word

    def word_rows(j):
        return pl.ds(pl.multiple_of(j * QB, QB), QB)

    def index_dots(j):
        return _dot_nt(kln_ref[rows(j), :], qi4_ref[...])

    def score_chunk(j, dots, causal):
        sc = None
        for h in range(IDX_HEADS):
            t = w_rows[h:h + 1, :] * jnp.maximum(dots[:, h * QB:(h + 1) * QB], 0.0)
            sc = t if sc is None else sc + t
        bits = pltpu.bitcast(sc, I32)
        key = jnp.where(bits < 0, bits ^ jnp.int32(0x7FFFFFFF), bits)
        key = jnp.where(sc == 0.0, 0, key)
        if causal:
            key_pos = j * SC + lax.broadcasted_iota(I32, (SC, LANES), 0)
            q_pos = i * QB + lax.broadcasted_iota(I32, (SC, LANES), 1)
            key = jnp.where(key_pos <= q_pos, key, INT_MIN)
        keys_ref[rows(j), :] = key
        top = packed_digits(j, 32 - DIGIT_BITS) ^ BYTE_HALF
        dig_ref[word_rows(j), :] = top | BYTE_GUARDS
        cls_ref[word_rows(j), :] = jnp.full((QB, LANES), BYTE_LOW, I32)
        if causal:
            @pl.when(j % 2 == 0)
            def _():
                dig_ref[word_rows(j + 1), :] = jnp.full((QB, LANES), BYTE_GUARDS, I32)

    def score_run(first, count):
        pending = [index_dots(first + c) for c in range(min(2, count))]
        for c in range(count):
            score_chunk(first + c, pending.pop(0), False)
            if c + 2 < count:
                pending.append(index_dots(first + c + 2))

    def score_body(g, carry):
        score_run(SCORE_UNROLL * g, SCORE_UNROLL)
        return carry

    lax.fori_loop(0, last // SCORE_UNROLL, score_body, 0)
    for rem in range(1, SCORE_UNROLL):
        @pl.when(last % SCORE_UNROLL == rem)
        def _(rem=rem):
            score_run(last - rem, rem)

    score_chunk(last, index_dots(last), True)

    t_pos = i * QB + lax.broadcasted_iota(I32, (1, LANES), 1)
    kk = jnp.minimum(t_pos + 1, topk).astype(F32)

    def count_ge(cand):
        def body(j, cnt):
            for r in range(SUB):
                blk = keys_ref[pl.ds(pl.multiple_of(j * SC + r * QB, QB), QB), :]
                cnt = cnt + jnp.where(blk >= cand, 1.0, 0.0)
            return cnt

        cnt = lax.fori_loop(0, nsc, body, jnp.zeros((QB, LANES), F32))
        return jnp.sum(cnt, axis=0, keepdims=True)

    def count_digit_ge(cand):
        packed = cand * BYTE_ONES

        def body(g, cnt):
            d = dig_ref[pl.ds(pl.multiple_of(g * 2 * QB, 2 * QB), 2 * QB), :] - packed
            hit = lax.population_count(d & BYTE_GUARDS)
            return cnt + hit[:QB] + hit[QB:]

        cnt = lax.fori_loop(0, (nsc + 1) // 2, body, jnp.zeros((QB, LANES), I32))
        return jnp.sum(cnt, axis=0, keepdims=True).astype(F32)

    def select_digit(above):
        def bit_body(bi, carry):
            t, greater = carry
            cand = t + lax.shift_left(jnp.int32(1), DIGIT_BITS - 1 - bi)
            cnt = count_digit_ge(cand)
            ok = above + cnt >= kk
            return jnp.where(ok, cand, t), jnp.where(ok, greater, cnt)

        t, greater = lax.fori_loop(0, DIGIT_BITS, bit_body,
                                   (jnp.zeros((1, LANES), I32), jnp.zeros((1, LANES), F32)))
        return t, above + greater

    thr = jnp.zeros((1, LANES), I32)
    above = jnp.zeros((1, LANES), F32)
    for level in range(DIGITS):
        shift = 32 - DIGIT_BITS * (level + 1)
        if level > 0:
            t_packed = t * BYTE_ONES

            def next_digit(j, carry, shift=shift, t_packed=t_packed):
                z = (dig_ref[word_rows(j), :] & BYTE_LOW) ^ t_packed
                same = ((z + BYTE_LOW) & BYTE_GUARDS) ^ BYTE_GUARDS
                cls = cls_ref[word_rows(j), :] & (same - lax.shift_right_logical(same, jnp.int32(7)))
                cls_ref[word_rows(j), :] = cls
                dig_ref[word_rows(j), :] = (packed_digits(j, shift) & cls) | BYTE_GUARDS
                return carry

            lax.fori_loop(0, nsc, next_digit, 0)
        t, above = select_digit(above)
        thr = thr + lax.shift_left(t - (DIGIT_HALF if level == 0 else 0), jnp.int32(shift))

    def bit_body(bi, carry):
        t, greater = carry
        cand = t + lax.shift_left(jnp.int32(1), REST_BITS - 1 - bi)
        cnt = count_ge(cand)
        ok = cnt >= kk
        return jnp.where(ok, cand, t), jnp.where(ok, greater, cnt)

    thr, above = lax.fori_loop(0, REST_BITS, bit_body, (thr, above))
    need = kk - above
    sub_k = lax.broadcasted_iota(I32, (QB, QB), 0)
    sub_k2 = lax.broadcasted_iota(I32, (QB, QB), 1)
    tril = (sub_k2 <= sub_k).astype(BF16)

    def mask_blocks(first, count, seen):
        keys, eqs, pres = [], [], []
        for r in range(count):
            key = keys_ref[pl.ds(pl.multiple_of((first + r) * QB, QB), QB), :]
            eq = key == thr
            keys.append(key)
            eqs.append(eq)
            pres.append(_dot(tril, jnp.where(eq, 1.0, 0.0).astype(BF16)))
        for r in range(count):
            sel = (keys[r] > thr) | (eqs[r] & ((seen + pres[r]) <= need))
            madd_ref[pl.ds(pl.multiple_of((first + r) * QB, QB), QB), :] = jnp.where(sel, neg_bound, NEG).astype(BF16)
            seen = seen + pres[r][QB - 1:QB, :]
        return seen

    seen = lax.fori_loop(0, nsc // 2, lambda g, seen: mask_blocks(g * 2 * SUB, 2 * SUB, seen),
                         jnp.zeros((1, LANES), F32))

    @pl.when(nsc % 2 == 1)
    def _():
        mask_blocks(last * SUB, SUB, seen)

    def split_near(slot, blk):
        blk_rows = pl.ds(pl.multiple_of(blk * QB, QB), QB)
        mnear_ref[slot] = madd_ref[blk_rows, :]
        madd_ref[blk_rows, :] = jnp.full((QB, LANES), NEG, BF16)

    split_near(1, i)

    @pl.when(i >= 1)
    def _():
        split_near(0, i - 1)

    m_ref[...] = jnp.full(m_ref.shape, NEG, F32)
    l_ref[...] = jnp.zeros(l_ref.shape, F32)
    acc_ref[...] = jnp.zeros(acc_ref.shape, F32)

    def logits(kc, madd, n):
        return _dot_nt(jnp.concatenate([kc, madd], axis=1), q4_ref[n])

    def far_logits(j, n):
        return logits(k_ref[0, rows(j), n * A_HEAD_DIM:(n + 1) * A_HEAD_DIM], madd_ref[rows(j), :], n)

    def online_update(n, s, vt):
        m_prev = m_ref[n]
        m_new = jnp.maximum(m_prev, jnp.max(s, axis=0, keepdims=True))
        alpha = jnp.exp2(m_prev - m_new)
        p = jnp.exp2(s - m_new)
        l_ref[n] = alpha * l_ref[n] + jnp.sum(p, axis=0, keepdims=True)
        acc_ref[n] = alpha * acc_ref[n] + _dot(vt, p.astype(BF16))
        m_ref[n] = m_new

    def bounded_update(n, s, vt):
        p = jnp.exp2(s)
        l_ref[n] += jnp.sum(p, axis=0, keepdims=True)
        acc_ref[n] += _dot(vt, p.astype(BF16))

    def far_vt(j, n):
        return vt_ref[0, j, n * A_HEAD_DIM:(n + 1) * A_HEAD_DIM, :]

    n_far = jnp.where(i % SUB >= 2, nsc, last)

    def attend(update):
        def sweep(first, count):
            tasks = [(first + c, n) for c in range(count) for n in range(A_KV_HEADS)]
            ahead = 6
            pending = [far_logits(j, n) for j, n in tasks[:ahead]]
            for t, (j, n) in enumerate(tasks):
                update(n, pending.pop(0), far_vt(j, n))
                if t + ahead < len(tasks):
                    pending.append(far_logits(*tasks[t + ahead]))

        def sweep_body(g, carry):
            sweep(far_unroll * g, far_unroll)
            return carry

        lax.fori_loop(0, n_far // far_unroll, sweep_body, 0)
        for rem in range(1, far_unroll):
            @pl.when(n_far % far_unroll == rem)
            def _(rem=rem):
                sweep(n_far - rem, rem)

        def near_step(slot, blk, vts_ref):
            blk_rows = pl.ds(pl.multiple_of(blk * QB, QB), QB)
            ss = [logits(k_ref[0, blk_rows, n * A_HEAD_DIM:(n + 1) * A_HEAD_DIM], mnear_ref[slot], n)
                  + bias_ref[slot, n] for n in range(A_KV_HEADS)]
            for n in range(A_KV_HEADS):
                update(n, ss[n], vts_ref[0, 0, n * A_HEAD_DIM:(n + 1) * A_HEAD_DIM, :])

        @pl.when(i >= 1)
        def _():
            near_step(0, i - 1, vtp_ref)

        near_step(1, i, vtd_ref)

    @pl.when(bounded)
    def _():
        attend(bounded_update)

    @pl.when(jnp.logical_not(bounded))
    def _():
        attend(online_update)

    for n in range(A_KV_HEADS):
        out_t = acc_ref[n] / l_ref[n]
        for g in range(A_GROUP):
            hd = n * A_GROUP + g
            o_ref[0, :, hd * A_HEAD_DIM:(hd + 1) * A_HEAD_DIM] = out_t[:, g * QB:(g + 1) * QB].T.astype(o_ref.dtype)


def _dsa(p16, p32, vt, vt_blk, ln_g, ln_b, bias_near, *, topk, far_unroll):
    b, s, _ = p16.shape
    bias_max = jnp.broadcast_to(jnp.max(jnp.abs(bias_near)) + BOUND_MARGIN, (1, LANES)).astype(F32)
    aq = A_HEADS * A_HEAD_DIM
    akv = A_KV_HEADS * A_HEAD_DIM
    qiw = IDX_HEADS * LANES
    nch = s // QB
    gq = A_GROUP * QB
    k_blk = (aq + X_HEADS * X_HEAD_DIM) // akv
    qi_blk = 2 * LRU_WIDTH // qiw
    ki_blk = (2 * LRU_WIDTH + qiw) // LANES
    return pl.pallas_call(
        functools.partial(_dsa_kernel, topk=topk, far_unroll=far_unroll),
        out_shape=jax.ShapeDtypeStruct((b, s, aq), BF16),
        grid=(b, nch),
        in_specs=[
            pl.BlockSpec((1, QB, aq), lambda bi, i: (bi, i, 0)),
            pl.BlockSpec((1, s, akv), lambda bi, i: (bi, 0, k_blk)),
            pl.BlockSpec((1, s // SC, akv, SC), lambda bi, i: (bi, 0, 0, 0)),
            pl.BlockSpec((1, 1, akv, QB), lambda bi, i: (bi, jnp.maximum(i - 1, 0), 0, 0)),
            pl.BlockSpec((1, 1, akv, QB), lambda bi, i: (bi, i, 0, 0)),
            pl.BlockSpec((1, QB, qiw), lambda bi, i: (bi, i, qi_blk)),
            pl.BlockSpec((1, s, LANES), lambda bi, i: (bi, 0, ki_blk)),
            pl.BlockSpec((1, QB, LANES), lambda bi, i: (bi, i, ki_blk + 1)),
            pl.BlockSpec((1, LANES), lambda bi, i: (0, 0)),
            pl.BlockSpec((1, LANES), lambda bi, i: (0, 0)),
            pl.BlockSpec((2, A_KV_HEADS, QB, gq), lambda bi, i: (0, 0, 0, 0)),
            pl.BlockSpec((1, LANES), lambda bi, i: (0, 0)),
        ],
        out_specs=pl.BlockSpec((1, QB, aq), lambda bi, i: (bi, i, 0)),
        scratch_shapes=[
            pltpu.VMEM((s, LANES), BF16),
            pltpu.VMEM((A_KV_HEADS, 1, LANES), F32),
            pltpu.VMEM((s, LANES), I32),
            pltpu.VMEM((-(-(s // SC) // 2) * 2 * QB, LANES), I32),
            pltpu.VMEM((s // SUB, LANES), I32),
            pltpu.VMEM((s, LANES), BF16),
            pltpu.VMEM((2, QB, LANES), BF16),
            pltpu.VMEM((IDX_HEADS * QB, LANES), BF16),
            pltpu.VMEM((A_KV_HEADS, gq, A_HEAD_DIM + QB), BF16),
            pltpu.VMEM((A_KV_HEADS, 1, gq), F32),
            pltpu.VMEM((A_KV_HEADS, 1, gq), F32),
            pltpu.VMEM((A_KV_HEADS, A_HEAD_DIM, gq), F32),
        ],
        compiler_params=_params("arbitrary", "arbitrary"),
        name="dsa",
    )(p16, p16, vt, vt_blk, vt_blk, p32, p32, p32, ln_g, ln_b, bias_near, bias_max)


def _t5_bucket(dist):
    max_exact = N_BUCKETS // 2
    d = jnp.maximum(dist, 0)
    df = jnp.maximum(d, 1).astype(F32)
    large = max_exact + (jnp.log(df / max_exact) / math.log(MAX_DISTANCE / max_exact)
                         * (N_BUCKETS - max_exact)).astype(I32)
    large = jnp.minimum(large, N_BUCKETS - 1)
    return jnp.where(d < max_exact, d, large)


def _near_bias(rel_bias):
    t = jnp.arange(QB, dtype=I32)[:, None]
    s = jnp.arange(QB, dtype=I32)[None, :]
    tiles = []
    for off in (QB, 0):
        bucket = _t5_bucket(t - s + off)
        onehot = (bucket[..., None] == jnp.arange(N_BUCKETS, dtype=I32)).astype(F32)
        table = (rel_bias - rel_bias[N_BUCKETS - 1]) * LOG2E
        tile = jnp.einsum('tsb,bh->tsh', onehot, table, precision=lax.Precision.HIGHEST)
        tile = tile.transpose(1, 2, 0).reshape(QB, A_KV_HEADS, A_GROUP * QB)
        tiles.append(tile.transpose(1, 0, 2))
    return jnp.stack(tiles).astype(F32)


def _pad_cols(w, width):
    return jnp.pad(w, [(0, 0)] * (w.ndim - 1) + [(0, width - w.shape[-1])])


def _split_points(d_model):
    aq = A_HEADS * A_HEAD_DIM
    akv = A_KV_HEADS * A_HEAD_DIM
    sizes = (aq, akv, akv, IDX_HEADS * IDX_DIM, IDX_DIM, IDX_HEADS,
             LRU_WIDTH, LRU_WIDTH, X_HEADS * X_HEAD_DIM, N_BRANCH * d_model)
    pts, acc = [0], 0
    for n in sizes:
        acc += n
        pts.append(acc)
    return pts


def _regroup_kernel(w_ref, o16_ref, o32_ref, og_ref, *, pts):
    (q0, k0, v0, qi0, ki0, wi0, x0, g0, qm0, gt0, end) = pts

    def seg(lo, hi, scale=None):
        v = w_ref[0, :, lo:hi]
        if scale is not None:
            v = v * scale
        return v.astype(BF16)

    o16_ref[0, :, 0:k0 - q0] = seg(q0, k0, A_HEAD_DIM ** -0.5 * LOG2E)
    c = k0 - q0
    o16_ref[0, :, c:c + gt0 - qm0] = seg(qm0, gt0, X_HEAD_DIM ** -0.5)
    c += gt0 - qm0
    o16_ref[0, :, c:c + qi0 - k0] = seg(k0, qi0)
    o32_ref[0, :, 0:qm0 - x0] = seg(x0, qm0)
    c = qm0 - x0
    o32_ref[0, :, c:] = jnp.zeros((o32_ref.shape[1], o32_ref.shape[2] - c), BF16)
    for h in range(IDX_HEADS):
        o32_ref[0, :, c + h * LANES:c + h * LANES + IDX_DIM] = seg(qi0 + h * IDX_DIM, qi0 + (h + 1) * IDX_DIM)
    c += IDX_HEADS * LANES
    o32_ref[0, :, c:c + IDX_DIM] = seg(ki0, wi0)
    c += LANES
    o32_ref[0, :, c:c + IDX_HEADS] = seg(wi0, x0)
    og_ref[0] = seg(gt0, end)


def _prep_w_in(w_in, d_model, *, tk):
    depth, k, n = w_in.shape
    pts = _split_points(d_model)
    n16 = pts[3] + pts[9] - pts[8]
    n32 = 2 * LRU_WIDTH + (IDX_HEADS + 2) * LANES
    ng = n - pts[9]
    return pl.pallas_call(
        functools.partial(_regroup_kernel, pts=tuple(pts)),
        out_shape=(jax.ShapeDtypeStruct((depth, k, n16), BF16), jax.ShapeDtypeStruct((depth, k, n32), BF16),
                   jax.ShapeDtypeStruct((depth, k, ng), BF16)),
        grid=(depth, k // tk),
        in_specs=[pl.BlockSpec((1, tk, n), lambda l, i: (l, i, 0))],
        out_specs=(pl.BlockSpec((1, tk, n16), lambda l, i: (l, i, 0)),
                   pl.BlockSpec((1, tk, n32), lambda l, i: (l, i, 0)),
                   pl.BlockSpec((1, tk, ng), lambda l, i: (l, i, 0))),
        compiler_params=_params("parallel", "parallel"),
        name="regroup",
    )(w_in)


def _split_gu_kernel(w_ref, og_ref, ou_ref):
    d_ff = w_ref.shape[2] // 2
    pad = og_ref.shape[2] - d_ff
    for o_ref, lo in ((og_ref, 0), (ou_ref, d_ff)):
        o_ref[0, :, 0:d_ff] = w_ref[0, :, lo:lo + d_ff].astype(BF16)
        if pad:
            o_ref[0, :, d_ff:] = jnp.zeros((o_ref.shape[1], pad), BF16)


def _pad_down_kernel(w_ref, o_ref, *, n_real):
    @pl.when(pl.program_id(1) < n_real)
    def _():
        o_ref[...] = w_ref[...].astype(BF16)

    @pl.when(pl.program_id(1) >= n_real)
    def _():
        o_ref[...] = jnp.zeros_like(o_ref)


def _pad_ff(w_gu, w_down, tf, *, tk):
    depth, d, _ = w_gu.shape
    d_ff = w_down.shape[1]
    fp = -(-d_ff // tf) * tf
    assert d_ff % LANES == 0
    wg, wu = pl.pallas_call(
        _split_gu_kernel,
        out_shape=(jax.ShapeDtypeStruct((depth, d, fp), BF16),) * 2,
        grid=(depth, d // tk),
        in_specs=[pl.BlockSpec((1, tk, 2 * d_ff), lambda l, i: (l, i, 0))],
        out_specs=(pl.BlockSpec((1, tk, fp), lambda l, i: (l, i, 0)),) * 2,
        compiler_params=_params("parallel", "parallel"),
        name="split_gu",
    )(w_gu)
    n_real = d_ff // LANES
    wd = pl.pallas_call(
        functools.partial(_pad_down_kernel, n_real=n_real),
        out_shape=jax.ShapeDtypeStruct((depth, fp, d), BF16),
        grid=(depth, fp // LANES),
        in_specs=[pl.BlockSpec((1, LANES, d), lambda l, i: (l, jnp.minimum(i, n_real - 1), 0))],
        out_specs=pl.BlockSpec((1, LANES, d), lambda l, i: (l, i, 0)),
        compiler_params=_params("parallel", "arbitrary"),
        name="pad_down",
    )(w_down)
    return wg, wu, wd


def _tiles(m, s, d_model):
    return dict(
        ffn_tm=min(1024, m), ffn_final_tm=min(1024, m), ffn_tf=512,
        proj_tm=min(256, m), proj_tn=256,
        merge_tm=min(512, m), merge_tn=min(512, d_model),
        lru_tc=min(256, s), mem_tm=min(512, s),
    )


def kernel(x, mem, rel_bias, final_norm, norm_ff1, w_ff1_gu, w_ff1_down, norm_mix, w_in, conv_w, conv_b, w_a, b_a, w_i, b_i, lam, idx_ln_g, idx_ln_b, mem_norm, w_mem_kv, w_branch, w_out, norm_ff2, w_ff2_gu, w_ff2_down):
    b, s, d = x.shape
    nm = mem.shape[1]
    depth = w_in.shape[0]
    m = b * s
    assert s % SC == 0 and d % LANES == 0
    tl = _tiles(m, s, d)
    topk = min(TOPK_MAX, s // 4)
    bias_near = _near_bias(rel_bias)
    row = lambda v: v.reshape(1, -1)

    ff1 = _pad_ff(w_ff1_gu, w_ff1_down, tl["ffn_tf"], tk=min(256, d))
    ff2 = _pad_ff(w_ff2_gu, w_ff2_down, tl["ffn_tf"], tk=min(256, d))
    w16, w32, wgates = _prep_w_in(w_in, d, tk=min(256, d))
    w_a16, w_i16 = w_a.astype(BF16), w_i.astype(BF16)
    w_mem16, w_branch16, w_out16 = w_mem_kv.astype(BF16), w_branch.astype(BF16), w_out.astype(BF16)
    ln_g = _pad_cols(idx_ln_g, LANES)
    ln_b = _pad_cols(idx_ln_b, LANES)

    x2 = x.reshape(m, d)
    mem2 = mem.reshape(b * nm, d)
    for l in range(depth):
        x2 = _ffn(x2, row(norm_ff1[l]), *ff1, row(final_norm), l, final_norm=False,
                  tm=tl["ffn_tm"], tf=tl["ffn_tf"])

        p16, p32 = _proj(x2, row(norm_mix[l]), w16, w32, l, tm=tl["proj_tm"])
        p16 = p16.reshape(b, s, -1)
        p32 = p32.reshape(b, s, -1)

        v_cols = slice(p16.shape[-1] - A_KV_HEADS * A_HEAD_DIM, p16.shape[-1])
        vt = p16[..., v_cols].reshape(b, s // SC, SC, -1).transpose(0, 1, 3, 2)
        vt_blk = p16[..., v_cols].reshape(b, s // QB, QB, -1).transpose(0, 1, 3, 2)
        y_a = _dsa(p16, p32, vt, vt_blk, row(ln_g[l]), row(ln_b[l]),
                   bias_near, topk=topk, far_unroll=4)
        y_b = _lru(p32, conv_w[l], row(conv_b[l]), w_a16, w_i16,
                   row(b_a[l]), row(b_i[l]), row(lam[l]), l, tc=tl["lru_tc"])
        mkv = _norm_matmul(mem2, row(mem_norm[l]), w_mem16, l, BF16,
                           tm=min(256, b * nm), tn=tl["proj_tn"], name="memkv")
        y_c = _memattn(p16, mkv.reshape(b, nm, -1), tm=tl["mem_tm"])

        x2 = _merge(x2, row(norm_mix[l]), y_a.reshape(m, -1), y_b.reshape(m, -1), y_c.reshape(m, -1),
                    wgates, w_branch16, w_out16, l,
                    tm=tl["merge_tm"], tn=tl["merge_tn"])

        is_last = l == depth - 1
        x2 = _ffn(x2, row(norm_ff2[l]), *ff2, row(final_norm), l, final_norm=is_last,
                  tm=tl["ffn_final_tm" if is_last else "ffn_tm"], tf=tl["ffn_tf"])
    return x2.reshape(b, s, d)
```

```python
import functools
import math

import jax
import jax.numpy as jnp
from jax import lax
from jax.experimental import pallas as pl
from jax.experimental.pallas import tpu as pltpu

F32 = jnp.float32
BF16 = jnp.bfloat16
I32 = jnp.int32

EPS = 1e-6
A_HEADS = 8
A_KV_HEADS = 2
A_HEAD_DIM = 128
A_GROUP = A_HEADS // A_KV_HEADS
IDX_HEADS = 4
IDX_DIM = 64
TOPK_MAX = 256
LRU_WIDTH = 1024
LRU_BLOCKS = 8
LRU_BLOCK_W = LRU_WIDTH // LRU_BLOCKS
CONV_WIDTH = 4
LRU_C = 8.0
X_HEADS = 4
X_HEAD_DIM = 256
N_BRANCH = 3
BRANCH_WIDTH = 1024
N_BUCKETS = 32
MAX_DISTANCE = 128

LANES = 128
SUBLANES = 8
VMEM_LIMIT = 56 * 1024 * 1024
ROW_CHUNK = 256

QB = 128
SC = 512
SUB = SC // QB
SCORE_UNROLL = 4
NEG = -1e30
LOG2E = math.log2(math.e)
INT_MIN = -2 ** 31


def _as_i32(v):
    return v - (1 << 32) if v >= (1 << 31) else v


DIGIT_BITS = 7
DIGITS = 4
REST_BITS = 32 - DIGITS * DIGIT_BITS
DIGIT_MASK = (1 << DIGIT_BITS) - 1
DIGIT_HALF = 1 << (DIGIT_BITS - 1)
BYTE_ONES = _as_i32(0x01010101)
BYTE_LOW = _as_i32(0x7F7F7F7F)
BYTE_GUARDS = _as_i32(0x80808080)
BYTE_HALF = _as_i32(0x40404040)
assert SUB == 4 and DIGIT_BITS == 7

BOUND_SLACK = 1.02
BOUND_MARGIN = 0.5
BOUND_LIMIT = 55.0


def _params(*sem):
    return pltpu.CompilerParams(dimension_semantics=sem, vmem_limit_bytes=VMEM_LIMIT)


def _rms(x, g):
    ms = jnp.mean(x * x, axis=-1, keepdims=True)
    return x * lax.rsqrt(ms + EPS) * g


def _sigmoid(x):
    return 1.0 / (1.0 + jnp.exp(-x))


def _dot(a, b):
    return jnp.dot(a, b, preferred_element_type=F32)


def _dot_nt(a, b):
    return lax.dot_general(a, b, (((1,), (1,)), ((), ())), preferred_element_type=F32)


def _ffn_kernel(x_ref, g_ref, wg_ref, wu_ref, wd_ref, fg_ref, o_ref, h_ref, *, final_norm):
    f = pl.program_id(1)

    def for_row_chunks(fn):
        rc = min(ROW_CHUNK, x_ref.shape[0])

        def body(r, carry):
            fn(pl.ds(pl.multiple_of(r * rc, rc), rc))
            return carry

        lax.fori_loop(0, x_ref.shape[0] // rc, body, 0)

    @pl.when(f == 0)
    def _():
        def prologue(rs):
            h_ref[rs, :] = _rms(x_ref[rs, :], g_ref[...]).astype(BF16)
            o_ref[rs, :] = jnp.zeros((rs.size, o_ref.shape[1]), F32)

        for_row_chunks(prologue)

    h = h_ref[...]
    g = _dot(h, wg_ref[...])
    u = _dot(h, wu_ref[...])
    a = (g * _sigmoid(g) * u).astype(BF16)
    o_ref[...] += _dot(a, wd_ref[...])

    @pl.when(f == pl.num_programs(1) - 1)
    def _():
        def epilogue(rs):
            y = x_ref[rs, :] + 0.5 * o_ref[rs, :]
            if final_norm:
                y = _rms(y, fg_ref[...])
            o_ref[rs, :] = y

        for_row_chunks(epilogue)


def _ffn(x2, g, wg, wu, wd, fg, layer, *, final_norm, tm, tf):
    m, d = x2.shape
    fp = wg.shape[-1]
    return pl.pallas_call(
        functools.partial(_ffn_kernel, final_norm=final_norm),
        out_shape=jax.ShapeDtypeStruct((m, d), F32),
        grid=(m // tm, fp // tf),
        in_specs=[
            pl.BlockSpec((tm, d), lambda i, f: (i, 0)),
            pl.BlockSpec((1, d), lambda i, f: (0, 0)),
            pl.BlockSpec((None, d, tf), lambda i, f: (layer, 0, f)),
            pl.BlockSpec((None, d, tf), lambda i, f: (layer, 0, f)),
            pl.BlockSpec((None, tf, d), lambda i, f: (layer, f, 0)),
            pl.BlockSpec((1, d), lambda i, f: (0, 0)),
        ],
        out_specs=pl.BlockSpec((tm, d), lambda i, f: (i, 0)),
        scratch_shapes=[pltpu.VMEM((tm, d), BF16)],
        compiler_params=_params("parallel", "arbitrary"),
        name="ffn",
    )(x2, g, wg, wu, wd, fg)


def _norm_matmul_kernel(x_ref, g_ref, w_ref, o_ref, h_ref):
    @pl.when(pl.program_id(1) == 0)
    def _():
        h_ref[...] = _rms(x_ref[...], g_ref[...]).astype(BF16)

    o_ref[...] = _dot(h_ref[...], w_ref[...]).astype(o_ref.dtype)


def _norm_matmul(x2, g, w, layer, out_dtype, *, tm, tn, name):
    m, d = x2.shape
    n = w.shape[-1]
    return pl.pallas_call(
        _norm_matmul_kernel,
        out_shape=jax.ShapeDtypeStruct((m, n), out_dtype),
        grid=(m // tm, n // tn),
        in_specs=[
            pl.BlockSpec((tm, d), lambda i, j: (i, 0)),
            pl.BlockSpec((1, d), lambda i, j: (0, 0)),
            pl.BlockSpec((None, d, tn), lambda i, j: (layer, 0, j)),
        ],
        out_specs=pl.BlockSpec((tm, tn), lambda i, j: (i, j)),
        scratch_shapes=[pltpu.VMEM((tm, d), BF16)],
        compiler_params=_params("parallel", "arbitrary"),
        name=name,
    )(x2, g, w)


def _proj_kernel(x_ref, g_ref, w16_ref, w32_ref, o16_ref, o32_ref):
    h = _rms(x_ref[...], g_ref[...]).astype(BF16)
    o16_ref[...] = _dot(h, w16_ref[...]).astype(o16_ref.dtype)
    o32_ref[...] = _dot(h, w32_ref[...])


def _proj(x2, g, w16, w32, layer, *, tm):
    m, d = x2.shape
    n16, n32 = w16.shape[-1], w32.shape[-1]
    resident = pl.Buffered(1)
    return pl.pallas_call(
        _proj_kernel,
        out_shape=(jax.ShapeDtypeStruct((m, n16), BF16), jax.ShapeDtypeStruct((m, n32), F32)),
        grid=(m // tm,),
        in_specs=[
            pl.BlockSpec((tm, d), lambda i: (i, 0)),
            pl.BlockSpec((1, d), lambda i: (0, 0)),
            pl.BlockSpec((None, d, n16), lambda i: (layer, 0, 0), pipeline_mode=resident),
            pl.BlockSpec((None, d, n32), lambda i: (layer, 0, 0), pipeline_mode=resident),
        ],
        out_specs=(pl.BlockSpec((tm, n16), lambda i: (i, 0)), pl.BlockSpec((tm, n32), lambda i: (i, 0))),
        compiler_params=_params("parallel"),
        name="proj",
    )(x2, g, w16, w32)


def _lru_kernel(x_ref, gate_ref, cw_ref, cb_ref, wa_ref, wi_ref, ba_ref, bi_ref, lam_ref,
                o_ref, halo_ref, hc_ref, *, tc):
    n_groups = tc // SUBLANES
    assert CONV_WIDTH <= SUBLANES

    @pl.when(pl.program_id(1) == 0)
    def _():
        halo_ref[...] = jnp.zeros_like(halo_ref)
        hc_ref[...] = jnp.zeros_like(hc_ref)

    x3 = x_ref[0].reshape(n_groups, SUBLANES, LRU_WIDTH)
    prev3 = jnp.concatenate([halo_ref[...][None], x3[:-1]], axis=0)
    row3 = lax.broadcasted_iota(I32, (n_groups, SUBLANES, LRU_WIDTH), 1)
    xc = x3 * cw_ref[CONV_WIDTH - 1:CONV_WIDTH, :]
    for k in range(1, CONV_WIDTH):
        xk = jnp.where(row3 >= k, pltpu.roll(x3, k, 1), pltpu.roll(prev3, k, 1))
        xc = xc + xk * cw_ref[CONV_WIDTH - 1 - k:CONV_WIDTH - k, :]
    xc = (cb_ref[...] + xc).reshape(tc, LRU_WIDTH)
    halo_ref[...] = x3[n_groups - 1]

    xcb = xc.astype(BF16)
    r_parts, i_parts = [], []
    for n in range(LRU_BLOCKS):
        blk = xcb[:, n * LRU_BLOCK_W:(n + 1) * LRU_BLOCK_W]
        r_parts.append(_dot(blk, wa_ref[n]))
        i_parts.append(_dot(blk, wi_ref[n]))
    r = _sigmoid(jnp.concatenate(r_parts, axis=1) + ba_ref[...])
    gi = _sigmoid(jnp.concatenate(i_parts, axis=1) + bi_ref[...])

    softplus_neg_lam = jnp.log1p(jnp.exp(-lam_ref[...]))
    log_a = (-LRU_C) * r * softplus_neg_lam
    a = jnp.exp(log_a)
    mult = jnp.sqrt(jnp.maximum((1.0 - a) * (1.0 + a), 0.0))
    u = mult * (gi * xc)

    a = a.reshape(n_groups, SUBLANES, LRU_WIDTH)
    u = u.reshape(n_groups, SUBLANES, LRU_WIDTH)
    row = lax.broadcasted_iota(I32, (n_groups, SUBLANES, LRU_WIDTH), 1)
    d = 1
    while d < SUBLANES:
        keep = row >= d
        a_sh = jnp.where(keep, pltpu.roll(a, d, 1), 1.0)
        u_sh = jnp.where(keep, pltpu.roll(u, d, 1), 0.0)
        u = a * u_sh + u
        a = a * a_sh
        d *= 2
    carry = hc_ref[...]
    groups = []
    for g in range(n_groups):
        hg = a[g] * carry + u[g]
        carry = hg[SUBLANES - 1:SUBLANES, :]
        groups.append(hg)
    h = jnp.concatenate(groups, axis=0)
    hc_ref[...] = carry

    gate = gate_ref[0]
    gelu = 0.5 * gate * (1.0 + jnp.tanh(math.sqrt(2.0 / math.pi) * (gate + 0.044715 * (gate * gate * gate))))
    o_ref[0] = (h * gelu).astype(o_ref.dtype)


def _lru(p32, conv_w, conv_b, w_a, w_i, b_a, b_i, lam, layer, *, tc):
    b, s, _ = p32.shape
    vec = pl.BlockSpec((1, LRU_WIDTH), lambda bi, c: (0, 0))
    wblk = pl.BlockSpec((None, LRU_BLOCKS, LRU_BLOCK_W, LRU_BLOCK_W), lambda bi, c: (layer, 0, 0, 0))
    return pl.pallas_call(
        functools.partial(_lru_kernel, tc=tc),
        out_shape=jax.ShapeDtypeStruct((b, s, LRU_WIDTH), BF16),
        grid=(b, s // tc),
        in_specs=[
            pl.BlockSpec((1, tc, LRU_WIDTH), lambda bi, c: (bi, c, 0)),
            pl.BlockSpec((1, tc, LRU_WIDTH), lambda bi, c: (bi, c, 1)),
            pl.BlockSpec((CONV_WIDTH, LRU_WIDTH), lambda bi, c: (0, 0)),
            vec, wblk, wblk, vec, vec, vec,
        ],
        out_specs=pl.BlockSpec((1, tc, LRU_WIDTH), lambda bi, c: (bi, c, 0)),
        scratch_shapes=[pltpu.VMEM((SUBLANES, LRU_WIDTH), F32), pltpu.VMEM((1, LRU_WIDTH), F32)],
        compiler_params=_params("arbitrary", "arbitrary"),
        name="rglru",
    )(p32, p32, conv_w, conv_b, w_a, w_i, b_a, b_i, lam)


def _memattn_kernel(q_ref, kv_ref, o_ref):
    kw = X_HEADS * X_HEAD_DIM
    for h in range(X_HEADS):
        sl = slice(h * X_HEAD_DIM, (h + 1) * X_HEAD_DIM)
        q = q_ref[0, :, sl]
        k = kv_ref[0, :, sl]
        v = kv_ref[0, :, kw + h * X_HEAD_DIM:kw + (h + 1) * X_HEAD_DIM]
        s = _dot_nt(q, k)
        m = jnp.max(s, axis=-1, keepdims=True)
        p = jnp.exp(s - m)
        l = jnp.sum(p, axis=-1, keepdims=True)
        o = _dot(p.astype(BF16), v)
        o_ref[0, :, sl] = (o / l).astype(o_ref.dtype)


def _memattn(p16, mkv, *, tm):
    b, s, _ = p16.shape
    nm = mkv.shape[1]
    kw = X_HEADS * X_HEAD_DIM
    return pl.pallas_call(
        _memattn_kernel,
        out_shape=jax.ShapeDtypeStruct((b, s, kw), BF16),
        grid=(b, s // tm),
        in_specs=[
            pl.BlockSpec((1, tm, kw), lambda bi, i: (bi, i, 1)),
            pl.BlockSpec((1, nm, 2 * kw), lambda bi, i: (bi, 0, 0)),
        ],
        out_specs=pl.BlockSpec((1, tm, kw), lambda bi, i: (bi, i, 0)),
        compiler_params=_params("parallel", "parallel"),
        name="memattn",
    )(p16, mkv)


def _merge_kernel(x_ref, g_ref, ya_ref, yb_ref, yc_ref, wg0_ref, wg1_ref, wg2_ref, wb_ref, wo_ref,
                  o_ref, h_ref):
    n = pl.program_id(1)

    @pl.when(n == 0)
    def _():
        h_ref[...] = _rms(x_ref[...], g_ref[...]).astype(BF16)
        o_ref[...] = jnp.zeros_like(o_ref)

    h = h_ref[...]
    merged = None
    for j, (y_ref, wg_ref) in enumerate(((ya_ref, wg0_ref), (yb_ref, wg1_ref), (yc_ref, wg2_ref))):
        term = _sigmoid(_dot(h, wg_ref[...])) * _dot(y_ref[...], wb_ref[j])
        merged = term if merged is None else merged + term
    o_ref[...] += _dot(merged.astype(BF16), wo_ref[...])

    @pl.when(n == pl.num_programs(1) - 1)
    def _():
        o_ref[...] = x_ref[...] + o_ref[...]


def _merge(x2, g, ya, yb, yc, wgates, wbranch, wout, layer, *, tm, tn):
    m, d = x2.shape
    nblk = d // tn
    yspec = pl.BlockSpec((tm, BRANCH_WIDTH), lambda i, n: (i, 0))
    return pl.pallas_call(
        _merge_kernel,
        out_shape=jax.ShapeDtypeStruct((m, d), F32),
        grid=(m // tm, nblk),
        in_specs=[
            pl.BlockSpec((tm, d), lambda i, n: (i, 0)),
            pl.BlockSpec((1, d), lambda i, n: (0, 0)),
            yspec, yspec, yspec,
            pl.BlockSpec((None, d, tn), lambda i, n: (layer, 0, n)),
            pl.BlockSpec((None, d, tn), lambda i, n: (layer, 0, nblk + n)),
            pl.BlockSpec((None, d, tn), lambda i, n: (layer, 0, 2 * nblk + n)),
            pl.BlockSpec((None, N_BRANCH, BRANCH_WIDTH, tn), lambda i, n: (layer, 0, 0, n)),
            pl.BlockSpec((None, tn, d), lambda i, n: (layer, n, 0)),
        ],
        out_specs=pl.BlockSpec((tm, d), lambda i, n: (i, 0)),
        scratch_shapes=[pltpu.VMEM((tm, d), BF16)],
        compiler_params=_params("parallel", "arbitrary"),
        name="merge",
    )(x2, g, ya, yb, yc, wgates, wgates, wgates, wbranch, wout)


def _dsa_kernel(q_ref, k_ref, vt_ref, vtp_ref, vtd_ref, qi_ref, ki_ref, wi_ref, lng_ref, lnb_ref, bias_ref, bmax_ref,
                o_ref,
                kln_ref, knorm_ref, keys_ref, dig_ref, cls_ref, madd_ref, mnear_ref, qi4_ref, q4_ref, m_ref, l_ref, acc_ref,
                *, topk, far_unroll):
    i = pl.program_id(1)
    s_len = k_ref.shape[1]

    @pl.when(i == 0)
    def _():
        rows = 512 if s_len % 512 == 0 else QB
        lane = lax.broadcasted_iota(I32, (rows, LANES), 1)
        real = lane < IDX_DIM

        def ln_body(c, carry):
            x = ki_ref[0, pl.ds(c * rows, rows), :]
            mu = jnp.sum(x, axis=-1, keepdims=True) * (1.0 / IDX_DIM)
            xm = jnp.where(real, x - mu, 0.0)
            var = jnp.sum(xm * xm, axis=-1, keepdims=True) * (1.0 / IDX_DIM)
            y = xm * lax.rsqrt(var + EPS) * lng_ref[...] + lnb_ref[...]
            kln_ref[pl.ds(c * rows, rows), :] = y.astype(BF16)
            kf = k_ref[0, pl.ds(c * rows, rows), :].astype(F32)
            sq = [jnp.sum(kf[:, n * A_HEAD_DIM:(n + 1) * A_HEAD_DIM] ** 2, axis=-1, keepdims=True)
                  for n in range(A_KV_HEADS)]
            return tuple(jnp.maximum(a, b) for a, b in zip(carry, sq))

        ksq = lax.fori_loop(0, s_len // rows, ln_body, (jnp.zeros((rows, 1), F32),) * A_KV_HEADS)
        for n in range(A_KV_HEADS):
            knorm_ref[n] = jnp.broadcast_to(jnp.sqrt(jnp.max(ksq[n], axis=0, keepdims=True)), (1, LANES))

    w_rows = wi_ref[0].T * (IDX_HEADS ** -0.5 * IDX_DIM ** -0.5)
    for h in range(IDX_HEADS):
        qi4_ref[h * QB:(h + 1) * QB, :] = qi_ref[0, :, h * LANES:(h + 1) * LANES].astype(BF16)
    eye = (lax.broadcasted_iota(I32, (QB, QB), 0) == lax.broadcasted_iota(I32, (QB, QB), 1)).astype(BF16)
    for n in range(A_KV_HEADS):
        for g in range(A_GROUP):
            hd = n * A_GROUP + g
            q4_ref[n, g * QB:(g + 1) * QB, 0:A_HEAD_DIM] = q_ref[0, :, hd * A_HEAD_DIM:(hd + 1) * A_HEAD_DIM]
            q4_ref[n, g * QB:(g + 1) * QB, A_HEAD_DIM:A_HEAD_DIM + QB] = eye

    qf = q_ref[0].astype(F32)
    head_of_col = lax.broadcasted_iota(I32, (A_HEADS * A_HEAD_DIM, LANES), 0) // A_HEAD_DIM
    seg = (head_of_col == lax.broadcasted_iota(I32, (A_HEADS * A_HEAD_DIM, LANES), 1)).astype(BF16)
    qnorm = jnp.sqrt(_dot((qf * qf).astype(BF16), seg)).T
    bound = None
    for hd in range(A_HEADS):
        bh = qnorm[hd:hd + 1, :] * knorm_ref[hd // A_GROUP]
        bound = bh if bound is None else jnp.maximum(bound, bh)
    bound = bound * BOUND_SLACK + bmax_ref[...]
    neg_bound = -bound
    bounded = jnp.max(bound) <= BOUND_LIMIT

    last = i // SUB
    nsc = last + 1

    def rows(j):
        return pl.ds(pl.multiple_of(j * SC, SC), SC)

    def packed_digits(j, shift):
        word = None
        for blk in range(SUB):
            key = keys_ref[pl.ds(pl.multiple_of(j * SC + blk * QB, QB), QB), :]
            s = shift - 8 * blk
            f = lax.shift_right_arithmetic(key, jnp.int32(s)) if s >= 0 else lax.shift_left(key, jnp.int32(-s))
            f = f & jnp.int32(_as_i32(DIGIT_MASK << (8 * blk)))
            word = f if word is None else word | f
        return word

    def word_rows(j):
        return pl.ds(pl.multiple_of(j * QB, QB), QB)

    def index_dots(j):
        return _dot_nt(kln_ref[rows(j), :], qi4_ref[...])

    def score_chunk(j, dots, causal):
        sc = None
        for h in range(IDX_HEADS):
            t = w_rows[h:h + 1, :] * jnp.maximum(dots[:, h * QB:(h + 1) * QB], 0.0)
            sc = t if sc is None else sc + t
        bits = pltpu.bitcast(sc, I32)
        key = jnp.where(bits < 0, bits ^ jnp.int32(0x7FFFFFFF), bits)
        key = jnp.where(sc == 0.0, 0, key)
        if causal:
            key_pos = j * SC + lax.broadcasted_iota(I32, (SC, LANES), 0)
            q_pos = i * QB + lax.broadcasted_iota(I32, (SC, LANES), 1)
            key = jnp.where(key_pos <= q_pos, key, INT_MIN)
        keys_ref[rows(j), :] = key
        top = packed_digits(j, 32 - DIGIT_BITS) ^ BYTE_HALF
        dig_ref[word_rows(j), :] = top | BYTE_GUARDS
        cls_ref[word_rows(j), :] = jnp.full((QB, LANES), BYTE_LOW, I32)
        if causal:
            @pl.when(j % 2 == 0)
            def _():
                dig_ref[word_rows(j + 1), :] = jnp.full((QB, LANES), BYTE_GUARDS, I32)

    def score_run(first, count):
        pending = [index_dots(first + c) for c in range(min(2, count))]
        for c in range(count):
            score_chunk(first + c, pending.pop(0), False)
            if c + 2 < count:
                pending.append(index_dots(first + c + 2))

    def score_body(g, carry):
        score_run(SCORE_UNROLL * g, SCORE_UNROLL)
        return carry

    lax.fori_loop(0, last // SCORE_UNROLL, score_body, 0)
    for rem in range(1, SCORE_UNROLL):
        @pl.when(last % SCORE_UNROLL == rem)
        def _(rem=rem):
            score_run(last - rem, rem)

    score_chunk(last, index_dots(last), True)

    t_pos = i * QB + lax.broadcasted_iota(I32, (1, LANES), 1)
    kk = jnp.minimum(t_pos + 1, topk).astype(F32)

    def count_ge(cand):
        def body(j, cnt):
            for r in range(SUB):
                blk = keys_ref[pl.ds(pl.multiple_of(j * SC + r * QB, QB), QB), :]
                cnt = cnt + jnp.where(blk >= cand, 1.0, 0.0)
            return cnt

        cnt = lax.fori_loop(0, nsc, body, jnp.zeros((QB, LANES), F32))
        return jnp.sum(cnt, axis=0, keepdims=True)

    def count_digit_ge(cand):
        packed = cand * BYTE_ONES

        def body(g, cnt):
            d = dig_ref[pl.ds(pl.multiple_of(g * 2 * QB, 2 * QB), 2 * QB), :] - packed
            hit = lax.population_count(d & BYTE_GUARDS)
            return cnt + hit[:QB] + hit[QB:]

        cnt = lax.fori_loop(0, (nsc + 1) // 2, body, jnp.zeros((QB, LANES), I32))
        return jnp.sum(cnt, axis=0, keepdims=True).astype(F32)

    def select_digit(above):
        def bit_body(bi, carry):
            t, greater = carry
            cand = t + lax.shift_left(jnp.int32(1), DIGIT_BITS - 1 - bi)
            cnt = count_digit_ge(cand)
            ok = above + cnt >= kk
            return jnp.where(ok, cand, t), jnp.where(ok, greater, cnt)

        t, greater = lax.fori_loop(0, DIGIT_BITS, bit_body,
                                   (jnp.zeros((1, LANES), I32), jnp.zeros((1, LANES), F32)))
        return t, above + greater

    thr = jnp.zeros((1, LANES), I32)
    above = jnp.zeros((1, LANES), F32)
    for level in range(DIGITS):
        shift = 32 - DIGIT_BITS * (level + 1)
        if level > 0:
            t_packed = t * BYTE_ONES

            def next_digit(j, carry, shift=shift, t_packed=t_packed):
                z = (dig_ref[word_rows(j), :] & BYTE_LOW) ^ t_packed
                same = ((z + BYTE_LOW) & BYTE_GUARDS) ^ BYTE_GUARDS
                cls = cls_ref[word_rows(j), :] & (same - lax.shift_right_logical(same, jnp.int32(7)))
                cls_ref[word_rows(j), :] = cls
                dig_ref[word_rows(j), :] = (packed_digits(j, shift) & cls) | BYTE_GUARDS
                return carry

            lax.fori_loop(0, nsc, next_digit, 0)
        t, above = select_digit(above)
        thr = thr + lax.shift_left(t - (DIGIT_HALF if level == 0 else 0), jnp.int32(shift))

    def bit_body(bi, carry):
        t, greater = carry
        cand = t + lax.shift_left(jnp.int32(1), REST_BITS - 1 - bi)
        cnt = count_ge(cand)
        ok = cnt >= kk
        return jnp.where(ok, cand, t), jnp.where(ok, greater, cnt)

    thr, above = lax.fori_loop(0, REST_BITS, bit_body, (thr, above))
    need = kk - above
    sub_k = lax.broadcasted_iota(I32, (QB, QB), 0)
    sub_k2 = lax.broadcasted_iota(I32, (QB, QB), 1)
    tril = (sub_k2 <= sub_k).astype(BF16)

    def mask_blocks(first, count, seen):
        keys, eqs, pres = [], [], []
        for r in range(count):
            key = keys_ref[pl.ds(pl.multiple_of((first + r) * QB, QB), QB), :]
            eq = key == thr
            keys.append(key)
            eqs.append(eq)
            pres.append(_dot(tril, jnp.where(eq, 1.0, 0.0).astype(BF16)))
        for r in range(count):
            sel = (keys[r] > thr) | (eqs[r] & ((seen + pres[r]) <= need))
            madd_ref[pl.ds(pl.multiple_of((first + r) * QB, QB), QB), :] = jnp.where(sel, neg_bound, NEG).astype(BF16)
            seen = seen + pres[r][QB - 1:QB, :]
        return seen

    seen = lax.fori_loop(0, nsc // 2, lambda g, seen: mask_blocks(g * 2 * SUB, 2 * SUB, seen),
                         jnp.zeros((1, LANES), F32))

    @pl.when(nsc % 2 == 1)
    def _():
        mask_blocks(last * SUB, SUB, seen)

    def split_near(slot, blk):
        blk_rows = pl.ds(pl.multiple_of(blk * QB, QB), QB)
        mnear_ref[slot] = madd_ref[blk_rows, :]
        madd_ref[blk_rows, :] = jnp.full((QB, LANES), NEG, BF16)

    split_near(1, i)

    @pl.when(i >= 1)
    def _():
        split_near(0, i - 1)

    m_ref[...] = jnp.full(m_ref.shape, NEG, F32)
    l_ref[...] = jnp.zeros(l_ref.shape, F32)
    acc_ref[...] = jnp.zeros(acc_ref.shape, F32)

    def logits(kc, madd, n):
        return _dot_nt(jnp.concatenate([kc, madd], axis=1), q4_ref[n])

    def far_logits(j, n):
        return logits(k_ref[0, rows(j), n * A_HEAD_DIM:(n + 1) * A_HEAD_DIM], madd_ref[rows(j), :], n)

    def online_update(n, s, vt):
        m_prev = m_ref[n]
        m_new = jnp.maximum(m_prev, jnp.max(s, axis=0, keepdims=True))
        alpha = jnp.exp2(m_prev - m_new)
        p = jnp.exp2(s - m_new)
        l_ref[n] = alpha * l_ref[n] + jnp.sum(p, axis=0, keepdims=True)
        acc_ref[n] = alpha * acc_ref[n] + _dot(vt, p.astype(BF16))
        m_ref[n] = m_new

    def bounded_update(n, s, vt):
        p = jnp.exp2(s)
        l_ref[n] += jnp.sum(p, axis=0, keepdims=True)
        acc_ref[n] += _dot(vt, p.astype(BF16))

    def far_vt(j, n):
        return vt_ref[0, j, n * A_HEAD_DIM:(n + 1) * A_HEAD_DIM, :]

    n_far = jnp.where(i % SUB >= 2, nsc, last)

    def attend(update):
        def sweep(first, count):
            tasks = [(first + c, n) for c in range(count) for n in range(A_KV_HEADS)]
            ahead = 6
            pending = [far_logits(j, n) for j, n in tasks[:ahead]]
            for t, (j, n) in enumerate(tasks):
                update(n, pending.pop(0), far_vt(j, n))
                if t + ahead < len(tasks):
                    pending.append(far_logits(*tasks[t + ahead]))

        def sweep_body(g, carry):
            sweep(far_unroll * g, far_unroll)
            return carry

        lax.fori_loop(0, n_far // far_unroll, sweep_body, 0)
        for rem in range(1, far_unroll):
            @pl.when(n_far % far_unroll == rem)
            def _(rem=rem):
                sweep(n_far - rem, rem)

        def near_step(slot, blk, vts_ref):
            blk_rows = pl.ds(pl.multiple_of(blk * QB, QB), QB)
            ss = [logits(k_ref[0, blk_rows, n * A_HEAD_DIM:(n + 1) * A_HEAD_DIM], mnear_ref[slot], n)
                  + bias_ref[slot, n] for n in range(A_KV_HEADS)]
            for n in range(A_KV_HEADS):
                update(n, ss[n], vts_ref[0, 0, n * A_HEAD_DIM:(n + 1) * A_HEAD_DIM, :])

        @pl.when(i >= 1)
        def _():
            near_step(0, i - 1, vtp_ref)

        near_step(1, i, vtd_ref)

    @pl.when(bounded)
    def _():
        attend(bounded_update)

    @pl.when(jnp.logical_not(bounded))
    def _():
        attend(online_update)

    for n in range(A_KV_HEADS):
        out_t = acc_ref[n] / l_ref[n]
        for g in range(A_GROUP):
            hd = n * A_GROUP + g
            o_ref[0, :, hd * A_HEAD_DIM:(hd + 1) * A_HEAD_DIM] = out_t[:, g * QB:(g + 1) * QB].T.astype(o_ref.dtype)


def _dsa(p16, p32, vt, vt_blk, ln_g, ln_b, bias_near, *, topk, far_unroll):
    b, s, _ = p16.shape
    bias_max = jnp.broadcast_to(jnp.max(jnp.abs(bias_near)) + BOUND_MARGIN, (1, LANES)).astype(F32)
    aq = A_HEADS * A_HEAD_DIM
    akv = A_KV_HEADS * A_HEAD_DIM
    qiw = IDX_HEADS * LANES
    nch = s // QB
    gq = A_GROUP * QB
    k_blk = (aq + X_HEADS * X_HEAD_DIM) // akv
    qi_blk = 2 * LRU_WIDTH // qiw
    ki_blk = (2 * LRU_WIDTH + qiw) // LANES
    return pl.pallas_call(
        functools.partial(_dsa_kernel, topk=topk, far_unroll=far_unroll),
        out_shape=jax.ShapeDtypeStruct((b, s, aq), BF16),
        grid=(b, nch),
        in_specs=[
            pl.BlockSpec((1, QB, aq), lambda bi, i: (bi, i, 0)),
            pl.BlockSpec((1, s, akv), lambda bi, i: (bi, 0, k_blk)),
            pl.BlockSpec((1, s // SC, akv, SC), lambda bi, i: (bi, 0, 0, 0)),
            pl.BlockSpec((1, 1, akv, QB), lambda bi, i: (bi, jnp.maximum(i - 1, 0), 0, 0)),
            pl.BlockSpec((1, 1, akv, QB), lambda bi, i: (bi, i, 0, 0)),
            pl.BlockSpec((1, QB, qiw), lambda bi, i: (bi, i, qi_blk)),
            pl.BlockSpec((1, s, LANES), lambda bi, i: (bi, 0, ki_blk)),
            pl.BlockSpec((1, QB, LANES), lambda bi, i: (bi, i, ki_blk + 1)),
            pl.BlockSpec((1, LANES), lambda bi, i: (0, 0)),
            pl.BlockSpec((1, LANES), lambda bi, i: (0, 0)),
            pl.BlockSpec((2, A_KV_HEADS, QB, gq), lambda bi, i: (0, 0, 0, 0)),
            pl.BlockSpec((1, LANES), lambda bi, i: (0, 0)),
        ],
        out_specs=pl.BlockSpec((1, QB, aq), lambda bi, i: (bi, i, 0)),
        scratch_shapes=[
            pltpu.VMEM((s, LANES), BF16),
            pltpu.VMEM((A_KV_HEADS, 1, LANES), F32),
            pltpu.VMEM((s, LANES), I32),
            pltpu.VMEM((-(-(s // SC) // 2) * 2 * QB, LANES), I32),
            pltpu.VMEM((s // SUB, LANES), I32),
            pltpu.VMEM((s, LANES), BF16),
            pltpu.VMEM((2, QB, LANES), BF16),
            pltpu.VMEM((IDX_HEADS * QB, LANES), BF16),
            pltpu.VMEM((A_KV_HEADS, gq, A_HEAD_DIM + QB), BF16),
            pltpu.VMEM((A_KV_HEADS, 1, gq), F32),
            pltpu.VMEM((A_KV_HEADS, 1, gq), F32),
            pltpu.VMEM((A_KV_HEADS, A_HEAD_DIM, gq), F32),
        ],
        compiler_params=_params("arbitrary", "arbitrary"),
        name="dsa",
    )(p16, p16, vt, vt_blk, vt_blk, p32, p32, p32, ln_g, ln_b, bias_near, bias_max)


def _t5_bucket(dist):
    max_exact = N_BUCKETS // 2
    d = jnp.maximum(dist, 0)
    df = jnp.maximum(d, 1).astype(F32)
    large = max_exact + (jnp.log(df / max_exact) / math.log(MAX_DISTANCE / max_exact)
                         * (N_BUCKETS - max_exact)).astype(I32)
    large = jnp.minimum(large, N_BUCKETS - 1)
    return jnp.where(d < max_exact, d, large)


def _near_bias(rel_bias):
    t = jnp.arange(QB, dtype=I32)[:, None]
    s = jnp.arange(QB, dtype=I32)[None, :]
    tiles = []
    for off in (QB, 0):
        bucket = _t5_bucket(t - s + off)
        onehot = (bucket[..., None] == jnp.arange(N_BUCKETS, dtype=I32)).astype(F32)
        table = (rel_bias - rel_bias[N_BUCKETS - 1]) * LOG2E
        tile = jnp.einsum('tsb,bh->tsh', onehot, table, precision=lax.Precision.HIGHEST)
        tile = tile.transpose(1, 2, 0).reshape(QB, A_KV_HEADS, A_GROUP * QB)
        tiles.append(tile.transpose(1, 0, 2))
    return jnp.stack(tiles).astype(F32)


def _pad_cols(w, width):
    return jnp.pad(w, [(0, 0)] * (w.ndim - 1) + [(0, width - w.shape[-1])])


def _split_points(d_model):
    aq = A_HEADS * A_HEAD_DIM
    akv = A_KV_HEADS * A_HEAD_DIM
    sizes = (aq, akv, akv, IDX_HEADS * IDX_DIM, IDX_DIM, IDX_HEADS,
             LRU_WIDTH, LRU_WIDTH, X_HEADS * X_HEAD_DIM, N_BRANCH * d_model)
    pts, acc = [0], 0
    for n in sizes:
        acc += n
        pts.append(acc)
    return pts


def _regroup_kernel(w_ref, o16_ref, o32_ref, og_ref, *, pts):
    (q0, k0, v0, qi0, ki0, wi0, x0, g0, qm0, gt0, end) = pts

    def seg(lo, hi, scale=None):
        v = w_ref[0, :, lo:hi]
        if scale is not None:
            v = v * scale
        return v.astype(BF16)

    o16_ref[0, :, 0:k0 - q0] = seg(q0, k0, A_HEAD_DIM ** -0.5 * LOG2E)
    c = k0 - q0
    o16_ref[0, :, c:c + gt0 - qm0] = seg(qm0, gt0, X_HEAD_DIM ** -0.5)
    c += gt0 - qm0
    o16_ref[0, :, c:c + qi0 - k0] = seg(k0, qi0)
    o32_ref[0, :, 0:qm0 - x0] = seg(x0, qm0)
    c = qm0 - x0
    o32_ref[0, :, c:] = jnp.zeros((o32_ref.shape[1], o32_ref.shape[2] - c), BF16)
    for h in range(IDX_HEADS):
        o32_ref[0, :, c + h * LANES:c + h * LANES + IDX_DIM] = seg(qi0 + h * IDX_DIM, qi0 + (h + 1) * IDX_DIM)
    c += IDX_HEADS * LANES
    o32_ref[0, :, c:c + IDX_DIM] = seg(ki0, wi0)
    c += LANES
    o32_ref[0, :, c:c + IDX_HEADS] = seg(wi0, x0)
    og_ref[0] = seg(gt0, end)


def _prep_w_in(w_in, d_model, *, tk):
    depth, k, n = w_in.shape
    pts = _split_points(d_model)
    n16 = pts[3] + pts[9] - pts[8]
    n32 = 2 * LRU_WIDTH + (IDX_HEADS + 2) * LANES
    ng = n - pts[9]
    return pl.pallas_call(
        functools.partial(_regroup_kernel, pts=tuple(pts)),
        out_shape=(jax.ShapeDtypeStruct((depth, k, n16), BF16), jax.ShapeDtypeStruct((depth, k, n32), BF16),
                   jax.ShapeDtypeStruct((depth, k, ng), BF16)),
        grid=(depth, k // tk),
        in_specs=[pl.BlockSpec((1, tk, n), lambda l, i: (l, i, 0))],
        out_specs=(pl.BlockSpec((1, tk, n16), lambda l, i: (l, i, 0)),
                   pl.BlockSpec((1, tk, n32), lambda l, i: (l, i, 0)),
                   pl.BlockSpec((1, tk, ng), lambda l, i: (l, i, 0))),
        compiler_params=_params("parallel", "parallel"),
        name="regroup",
    )(w_in)


def _split_gu_kernel(w_ref, og_ref, ou_ref):
    d_ff = w_ref.shape[2] // 2
    pad = og_ref.shape[2] - d_ff
    for o_ref, lo in ((og_ref, 0), (ou_ref, d_ff)):
        o_ref[0, :, 0:d_ff] = w_ref[0, :, lo:lo + d_ff].astype(BF16)
        if pad:
            o_ref[0, :, d_ff:] = jnp.zeros((o_ref.shape[1], pad), BF16)


def _pad_down_kernel(w_ref, o_ref):
    d_ff = w_ref.shape[1]
    o_ref[0, 0:d_ff, :] = w_ref[0].astype(BF16)
    if o_ref.shape[1] > d_ff:
        o_ref[0, d_ff:, :] = jnp.zeros((o_ref.shape[1] - d_ff, o_ref.shape[2]), BF16)


def _pad_ff(w_gu, w_down, tf, *, tk):
    depth, d, _ = w_gu.shape
    d_ff = w_down.shape[1]
    fp = -(-d_ff // tf) * tf
    assert d_ff % LANES == 0
    wg, wu = pl.pallas_call(
        _split_gu_kernel,
        out_shape=(jax.ShapeDtypeStruct((depth, d, fp), BF16),) * 2,
        grid=(depth, d // tk),
        in_specs=[pl.BlockSpec((1, tk, 2 * d_ff), lambda l, i: (l, i, 0))],
        out_specs=(pl.BlockSpec((1, tk, fp), lambda l, i: (l, i, 0)),) * 2,
        compiler_params=_params("parallel", "parallel"),
        name="split_gu",
    )(w_gu)
    wd = pl.pallas_call(
        _pad_down_kernel,
        out_shape=jax.ShapeDtypeStruct((depth, fp, d), BF16),
        grid=(depth, d // tk),
        in_specs=[pl.BlockSpec((1, d_ff, tk), lambda l, i: (l, 0, i))],
        out_specs=pl.BlockSpec((1, fp, tk), lambda l, i: (l, 0, i)),
        compiler_params=_params("parallel", "parallel"),
        name="pad_down",
    )(w_down)
    return wg, wu, wd


def _tiles(m, s, d_model):
    return dict(
        ffn_tm=min(1024, m), ffn_final_tm=min(1024, m), ffn_tf=512,
        proj_tm=min(256, m), proj_tn=256,
        merge_tm=min(512, m), merge_tn=min(512, d_model),
        lru_tc=min(256, s), mem_tm=min(512, s),
    )


def kernel(x, mem, rel_bias, final_norm, norm_ff1, w_ff1_gu, w_ff1_down, norm_mix, w_in, conv_w, conv_b, w_a, b_a, w_i, b_i, lam, idx_ln_g, idx_ln_b, mem_norm, w_mem_kv, w_branch, w_out, norm_ff2, w_ff2_gu, w_ff2_down):
    b, s, d = x.shape
    nm = mem.shape[1]
    depth = w_in.shape[0]
    m = b * s
    assert s % SC == 0 and d % LANES == 0
    tl = _tiles(m, s, d)
    topk = min(TOPK_MAX, s // 4)
    bias_near = _near_bias(rel_bias)
    row = lambda v: v.reshape(1, -1)

    ff1 = _pad_ff(w_ff1_gu, w_ff1_down, tl["ffn_tf"], tk=min(256, d))
    ff2 = _pad_ff(w_ff2_gu, w_ff2_down, tl["ffn_tf"], tk=min(256, d))
    w16, w32, wgates = _prep_w_in(w_in, d, tk=min(256, d))
    w_a16, w_i16 = w_a.astype(BF16), w_i.astype(BF16)
    w_mem16, w_branch16, w_out16 = w_mem_kv.astype(BF16), w_branch.astype(BF16), w_out.astype(BF16)
    ln_g = _pad_cols(idx_ln_g, LANES)
    ln_b = _pad_cols(idx_ln_b, LANES)

    x2 = x.reshape(m, d)
    mem2 = mem.reshape(b * nm, d)
    for l in range(depth):
        x2 = _ffn(x2, row(norm_ff1[l]), *ff1, row(final_norm), l, final_norm=False,
                  tm=tl["ffn_tm"], tf=tl["ffn_tf"])

        p16, p32 = _proj(x2, row(norm_mix[l]), w16, w32, l, tm=tl["proj_tm"])
        p16 = p16.reshape(b, s, -1)
        p32 = p32.reshape(b, s, -1)

        v_cols = slice(p16.shape[-1] - A_KV_HEADS * A_HEAD_DIM, p16.shape[-1])
        vt = p16[..., v_cols].reshape(b, s // SC, SC, -1).transpose(0, 1, 3, 2)
        vt_blk = p16[..., v_cols].reshape(b, s // QB, QB, -1).transpose(0, 1, 3, 2)
        y_a = _dsa(p16, p32, vt, vt_blk, row(ln_g[l]), row(ln_b[l]),
                   bias_near, topk=topk, far_unroll=4)
        y_b = _lru(p32, conv_w[l], row(conv_b[l]), w_a16, w_i16,
                   row(b_a[l]), row(b_i[l]), row(lam[l]), l, tc=tl["lru_tc"])
        mkv = _norm_matmul(mem2, row(mem_norm[l]), w_mem16, l, BF16,
                           tm=min(256, b * nm), tn=tl["proj_tn"], name="memkv")
        y_c = _memattn(p16, mkv.reshape(b, nm, -1), tm=tl["mem_tm"])

        x2 = _merge(x2, row(norm_mix[l]), y_a.reshape(m, -1), y_b.reshape(m, -1), y_c.reshape(m, -1),
                    wgates, w_branch16, w_out16, l,
                    tm=tl["merge_tm"], tn=tl["merge_tn"])

        is_last = l == depth - 1
        x2 = _ffn(x2, row(norm_ff2[l]), *ff2, row(final_norm), l, final_norm=is_last,
                  tm=tl["ffn_final_tm" if is_last else "ffn_tm"], tf=tl["ffn_tf"])
    return x2.reshape(b, s, d)
```

```python
import functools
import math

import jax
import jax.numpy as jnp
from jax import lax
from jax.experimental import pallas as pl
from jax.experimental.pallas import tpu as pltpu

F32 = jnp.float32
BF16 = jnp.bfloat16
I32 = jnp.int32

EPS = 1e-6
A_HEADS = 8
A_KV_HEADS = 2
A_HEAD_DIM = 128
A_GROUP = A_HEADS // A_KV_HEADS
IDX_HEADS = 4
IDX_DIM = 64
TOPK_MAX = 256
LRU_WIDTH = 1024
LRU_BLOCKS = 8
LRU_BLOCK_W = LRU_WIDTH // LRU_BLOCKS
CONV_WIDTH = 4
LRU_C = 8.0
X_HEADS = 4
X_HEAD_DIM = 256
N_BRANCH = 3
BRANCH_WIDTH = 1024
N_BUCKETS = 32
MAX_DISTANCE = 128

LANES = 128
SUBLANES = 8
VMEM_LIMIT = 56 * 1024 * 1024
ROW_CHUNK = 256

QB = 128
SC = 512
SUB = SC // QB
SCORE_UNROLL = 4
NEG = -1e30
LOG2E = math.log2(math.e)
INT_MIN = -2 ** 31


def _as_i32(v):
    return v - (1 << 32) if v >= (1 << 31) else v


DIGIT_WIDTHS = (7, 7, 7, 7, 4)
DIGIT_HALF = 1 << (DIGIT_WIDTHS[0] - 1)
BYTE_ONES = _as_i32(0x01010101)
BYTE_LOW = _as_i32(0x7F7F7F7F)
BYTE_GUARDS = _as_i32(0x80808080)
BYTE_HALF = _as_i32(0x40404040)
assert SUB == 4 and sum(DIGIT_WIDTHS) == 32 and max(DIGIT_WIDTHS) == DIGIT_WIDTHS[0] == 7

BOUND_SLACK = 1.02
BOUND_MARGIN = 0.5
BOUND_LIMIT = 55.0


def _params(*sem):
    return pltpu.CompilerParams(dimension_semantics=sem, vmem_limit_bytes=VMEM_LIMIT)


def _rms(x, g):
    ms = jnp.mean(x * x, axis=-1, keepdims=True)
    return x * lax.rsqrt(ms + EPS) * g


def _sigmoid(x):
    return 1.0 / (1.0 + jnp.exp(-x))


def _dot(a, b):
    return jnp.dot(a, b, preferred_element_type=F32)


def _dot_nt(a, b):
    return lax.dot_general(a, b, (((1,), (1,)), ((), ())), preferred_element_type=F32)


def _ffn_kernel(x_ref, g_ref, wg_ref, wu_ref, wd_ref, fg_ref, o_ref, h_ref, *, final_norm):
    f = pl.program_id(1)

    def for_row_chunks(fn):
        rc = min(ROW_CHUNK, x_ref.shape[0])

        def body(r, carry):
            fn(pl.ds(pl.multiple_of(r * rc, rc), rc))
            return carry

        lax.fori_loop(0, x_ref.shape[0] // rc, body, 0)

    @pl.when(f == 0)
    def _():
        def prologue(rs):
            h_ref[rs, :] = _rms(x_ref[rs, :], g_ref[...]).astype(BF16)
            o_ref[rs, :] = jnp.zeros((rs.size, o_ref.shape[1]), F32)

        for_row_chunks(prologue)

    h = h_ref[...]
    g = _dot(h, wg_ref[...])
    u = _dot(h, wu_ref[...])
    a = (g * _sigmoid(g) * u).astype(BF16)
    o_ref[...] += _dot(a, wd_ref[...])

    @pl.when(f == pl.num_programs(1) - 1)
    def _():
        def epilogue(rs):
            y = x_ref[rs, :] + 0.5 * o_ref[rs, :]
            if final_norm:
                y = _rms(y, fg_ref[...])
            o_ref[rs, :] = y

        for_row_chunks(epilogue)


def _ffn(x2, g, wg, wu, wd, fg, layer, *, final_norm, tm, tf):
    m, d = x2.shape
    fp = wg.shape[-1]
    return pl.pallas_call(
        functools.partial(_ffn_kernel, final_norm=final_norm),
        out_shape=jax.ShapeDtypeStruct((m, d), F32),
        grid=(m // tm, fp // tf),
        in_specs=[
            pl.BlockSpec((tm, d), lambda i, f: (i, 0)),
            pl.BlockSpec((1, d), lambda i, f: (0, 0)),
            pl.BlockSpec((None, d, tf), lambda i, f: (layer, 0, f)),
            pl.BlockSpec((None, d, tf), lambda i, f: (layer, 0, f)),
            pl.BlockSpec((None, tf, d), lambda i, f: (layer, f, 0)),
            pl.BlockSpec((1, d), lambda i, f: (0, 0)),
        ],
        out_specs=pl.BlockSpec((tm, d), lambda i, f: (i, 0)),
        scratch_shapes=[pltpu.VMEM((tm, d), BF16)],
        compiler_params=_params("parallel", "arbitrary"),
        name="ffn",
    )(x2, g, wg, wu, wd, fg)


def _norm_matmul_kernel(x_ref, g_ref, w_ref, o_ref, h_ref):
    @pl.when(pl.program_id(1) == 0)
    def _():
        h_ref[...] = _rms(x_ref[...], g_ref[...]).astype(BF16)

    o_ref[...] = _dot(h_ref[...], w_ref[...]).astype(o_ref.dtype)


def _norm_matmul(x2, g, w, layer, out_dtype, *, tm, tn, name):
    m, d = x2.shape
    n = w.shape[-1]
    return pl.pallas_call(
        _norm_matmul_kernel,
        out_shape=jax.ShapeDtypeStruct((m, n), out_dtype),
        grid=(m // tm, n // tn),
        in_specs=[
            pl.BlockSpec((tm, d), lambda i, j: (i, 0)),
            pl.BlockSpec((1, d), lambda i, j: (0, 0)),
            pl.BlockSpec((None, d, tn), lambda i, j: (layer, 0, j)),
        ],
        out_specs=pl.BlockSpec((tm, tn), lambda i, j: (i, j)),
        scratch_shapes=[pltpu.VMEM((tm, d), BF16)],
        compiler_params=_params("parallel", "arbitrary"),
        name=name,
    )(x2, g, w)


def _proj_kernel(x_ref, g_ref, w16_ref, w32_ref, o16_ref, o32_ref):
    h = _rms(x_ref[...], g_ref[...]).astype(BF16)
    o16_ref[...] = _dot(h, w16_ref[...]).astype(o16_ref.dtype)
    o32_ref[...] = _dot(h, w32_ref[...])


def _proj(x2, g, w16, w32, layer, *, tm):
    m, d = x2.shape
    n16, n32 = w16.shape[-1], w32.shape[-1]
    resident = pl.Buffered(1)
    return pl.pallas_call(
        _proj_kernel,
        out_shape=(jax.ShapeDtypeStruct((m, n16), BF16), jax.ShapeDtypeStruct((m, n32), F32)),
        grid=(m // tm,),
        in_specs=[
            pl.BlockSpec((tm, d), lambda i: (i, 0)),
            pl.BlockSpec((1, d), lambda i: (0, 0)),
            pl.BlockSpec((None, d, n16), lambda i: (layer, 0, 0), pipeline_mode=resident),
            pl.BlockSpec((None, d, n32), lambda i: (layer, 0, 0), pipeline_mode=resident),
        ],
        out_specs=(pl.BlockSpec((tm, n16), lambda i: (i, 0)), pl.BlockSpec((tm, n32), lambda i: (i, 0))),
        compiler_params=_params("parallel"),
        name="proj",
    )(x2, g, w16, w32)


def _lru_kernel(x_ref, gate_ref, cw_ref, cb_ref, wa_ref, wi_ref, ba_ref, bi_ref, lam_ref,
                o_ref, halo_ref, hc_ref, *, tc):
    n_groups = tc // SUBLANES
    assert CONV_WIDTH <= SUBLANES

    @pl.when(pl.program_id(1) == 0)
    def _():
        halo_ref[...] = jnp.zeros_like(halo_ref)
        hc_ref[...] = jnp.zeros_like(hc_ref)

    x3 = x_ref[0].reshape(n_groups, SUBLANES, LRU_WIDTH)
    prev3 = jnp.concatenate([halo_ref[...][None], x3[:-1]], axis=0)
    row3 = lax.broadcasted_iota(I32, (n_groups, SUBLANES, LRU_WIDTH), 1)
    xc = x3 * cw_ref[CONV_WIDTH - 1:CONV_WIDTH, :]
    for k in range(1, CONV_WIDTH):
        xk = jnp.where(row3 >= k, pltpu.roll(x3, k, 1), pltpu.roll(prev3, k, 1))
        xc = xc + xk * cw_ref[CONV_WIDTH - 1 - k:CONV_WIDTH - k, :]
    xc = (cb_ref[...] + xc).reshape(tc, LRU_WIDTH)
    halo_ref[...] = x3[n_groups - 1]

    xcb = xc.astype(BF16)
    r_parts, i_parts = [], []
    for n in range(LRU_BLOCKS):
        blk = xcb[:, n * LRU_BLOCK_W:(n + 1) * LRU_BLOCK_W]
        r_parts.append(_dot(blk, wa_ref[n]))
        i_parts.append(_dot(blk, wi_ref[n]))
    r = _sigmoid(jnp.concatenate(r_parts, axis=1) + ba_ref[...])
    gi = _sigmoid(jnp.concatenate(i_parts, axis=1) + bi_ref[...])

    softplus_neg_lam = jnp.log1p(jnp.exp(-lam_ref[...]))
    log_a = (-LRU_C) * r * softplus_neg_lam
    a = jnp.exp(log_a)
    mult = jnp.sqrt(jnp.maximum((1.0 - a) * (1.0 + a), 0.0))
    u = mult * (gi * xc)

    a = a.reshape(n_groups, SUBLANES, LRU_WIDTH)
    u = u.reshape(n_groups, SUBLANES, LRU_WIDTH)
    row = lax.broadcasted_iota(I32, (n_groups, SUBLANES, LRU_WIDTH), 1)
    d = 1
    while d < SUBLANES:
        keep = row >= d
        a_sh = jnp.where(keep, pltpu.roll(a, d, 1), 1.0)
        u_sh = jnp.where(keep, pltpu.roll(u, d, 1), 0.0)
        u = a * u_sh + u
        a = a * a_sh
        d *= 2
    carry = hc_ref[...]
    groups = []
    for g in range(n_groups):
        hg = a[g] * carry + u[g]
        carry = hg[SUBLANES - 1:SUBLANES, :]
        groups.append(hg)
    h = jnp.concatenate(groups, axis=0)
    hc_ref[...] = carry

    gate = gate_ref[0]
    gelu = 0.5 * gate * (1.0 + jnp.tanh(math.sqrt(2.0 / math.pi) * (gate + 0.044715 * (gate * gate * gate))))
    o_ref[0] = (h * gelu).astype(o_ref.dtype)


def _lru(p32, conv_w, conv_b, w_a, w_i, b_a, b_i, lam, layer, *, tc):
    b, s, _ = p32.shape
    vec = pl.BlockSpec((1, LRU_WIDTH), lambda bi, c: (0, 0))
    wblk = pl.BlockSpec((None, LRU_BLOCKS, LRU_BLOCK_W, LRU_BLOCK_W), lambda bi, c: (layer, 0, 0, 0))
    return pl.pallas_call(
        functools.partial(_lru_kernel, tc=tc),
        out_shape=jax.ShapeDtypeStruct((b, s, LRU_WIDTH), BF16),
        grid=(b, s // tc),
        in_specs=[
            pl.BlockSpec((1, tc, LRU_WIDTH), lambda bi, c: (bi, c, 0)),
            pl.BlockSpec((1, tc, LRU_WIDTH), lambda bi, c: (bi, c, 1)),
            pl.BlockSpec((CONV_WIDTH, LRU_WIDTH), lambda bi, c: (0, 0)),
            vec, wblk, wblk, vec, vec, vec,
        ],
        out_specs=pl.BlockSpec((1, tc, LRU_WIDTH), lambda bi, c: (bi, c, 0)),
        scratch_shapes=[pltpu.VMEM((SUBLANES, LRU_WIDTH), F32), pltpu.VMEM((1, LRU_WIDTH), F32)],
        compiler_params=_params("arbitrary", "arbitrary"),
        name="rglru",
    )(p32, p32, conv_w, conv_b, w_a, w_i, b_a, b_i, lam)


def _memattn_kernel(q_ref, kv_ref, o_ref):
    kw = X_HEADS * X_HEAD_DIM
    for h in range(X_HEADS):
        sl = slice(h * X_HEAD_DIM, (h + 1) * X_HEAD_DIM)
        q = q_ref[0, :, sl]
        k = kv_ref[0, :, sl]
        v = kv_ref[0, :, kw + h * X_HEAD_DIM:kw + (h + 1) * X_HEAD_DIM]
        s = _dot_nt(q, k)
        m = jnp.max(s, axis=-1, keepdims=True)
        p = jnp.exp(s - m)
        l = jnp.sum(p, axis=-1, keepdims=True)
        o = _dot(p.astype(BF16), v)
        o_ref[0, :, sl] = (o / l).astype(o_ref.dtype)


def _memattn(p16, mkv, *, tm):
    b, s, _ = p16.shape
    nm = mkv.shape[1]
    kw = X_HEADS * X_HEAD_DIM
    return pl.pallas_call(
        _memattn_kernel,
        out_shape=jax.ShapeDtypeStruct((b, s, kw), BF16),
        grid=(b, s // tm),
        in_specs=[
            pl.BlockSpec((1, tm, kw), lambda bi, i: (bi, i, 1)),
            pl.BlockSpec((1, nm, 2 * kw), lambda bi, i: (bi, 0, 0)),
        ],
        out_specs=pl.BlockSpec((1, tm, kw), lambda bi, i: (bi, i, 0)),
        compiler_params=_params("parallel", "parallel"),
        name="memattn",
    )(p16, mkv)


def _merge_kernel(x_ref, g_ref, ya_ref, yb_ref, yc_ref, wg0_ref, wg1_ref, wg2_ref, wb_ref, wo_ref,
                  o_ref, h_ref):
    n = pl.program_id(1)

    @pl.when(n == 0)
    def _():
        h_ref[...] = _rms(x_ref[...], g_ref[...]).astype(BF16)
        o_ref[...] = jnp.zeros_like(o_ref)

    h = h_ref[...]
    merged = None
    for j, (y_ref, wg_ref) in enumerate(((ya_ref, wg0_ref), (yb_ref, wg1_ref), (yc_ref, wg2_ref))):
        term = _sigmoid(_dot(h, wg_ref[...])) * _dot(y_ref[...], wb_ref[j])
        merged = term if merged is None else merged + term
    o_ref[...] += _dot(merged.astype(BF16), wo_ref[...])

    @pl.when(n == pl.num_programs(1) - 1)
    def _():
        o_ref[...] = x_ref[...] + o_ref[...]


def _merge(x2, g, ya, yb, yc, wgates, wbranch, wout, layer, *, tm, tn):
    m, d = x2.shape
    nblk = d // tn
    yspec = pl.BlockSpec((tm, BRANCH_WIDTH), lambda i, n: (i, 0))
    return pl.pallas_call(
        _merge_kernel,
        out_shape=jax.ShapeDtypeStruct((m, d), F32),
        grid=(m // tm, nblk),
        in_specs=[
            pl.BlockSpec((tm, d), lambda i, n: (i, 0)),
            pl.BlockSpec((1, d), lambda i, n: (0, 0)),
            yspec, yspec, yspec,
            pl.BlockSpec((None, d, tn), lambda i, n: (layer, 0, n)),
            pl.BlockSpec((None, d, tn), lambda i, n: (layer, 0, nblk + n)),
            pl.BlockSpec((None, d, tn), lambda i, n: (layer, 0, 2 * nblk + n)),
            pl.BlockSpec((None, N_BRANCH, BRANCH_WIDTH, tn), lambda i, n: (layer, 0, 0, n)),
            pl.BlockSpec((None, tn, d), lambda i, n: (layer, n, 0)),
        ],
        out_specs=pl.BlockSpec((tm, d), lambda i, n: (i, 0)),
        scratch_shapes=[pltpu.VMEM((tm, d), BF16)],
        compiler_params=_params("parallel", "arbitrary"),
        name="merge",
    )(x2, g, ya, yb, yc, wgates, wgates, wgates, wbranch, wout)


def _dsa_kernel(q_ref, k_ref, vt_ref, vtp_ref, vtd_ref, qi_ref, ki_ref, wi_ref, lng_ref, lnb_ref, bias_ref, bmax_ref,
                o_ref,
                kln_ref, knorm_ref, keys_ref, dig_ref, cls_ref, madd_ref, mnear_ref, qi4_ref, q4_ref, m_ref, l_ref, acc_ref,
                *, topk, far_unroll):
    i = pl.program_id(1)
    s_len = k_ref.shape[1]

    @pl.when(i == 0)
    def _():
        rows = 512 if s_len % 512 == 0 else QB
        lane = lax.broadcasted_iota(I32, (rows, LANES), 1)
        real = lane < IDX_DIM

        def ln_body(c, carry):
            x = ki_ref[0, pl.ds(c * rows, rows), :]
            mu = jnp.sum(x, axis=-1, keepdims=True) * (1.0 / IDX_DIM)
            xm = jnp.where(real, x - mu, 0.0)
            var = jnp.sum(xm * xm, axis=-1, keepdims=True) * (1.0 / IDX_DIM)
            y = xm * lax.rsqrt(var + EPS) * lng_ref[...] + lnb_ref[...]
            kln_ref[pl.ds(c * rows, rows), :] = y.astype(BF16)
            kf = k_ref[0, pl.ds(c * rows, rows), :].astype(F32)
            sq = [jnp.sum(kf[:, n * A_HEAD_DIM:(n + 1) * A_HEAD_DIM] ** 2, axis=-1, keepdims=True)
                  for n in range(A_KV_HEADS)]
            return tuple(jnp.maximum(a, b) for a, b in zip(carry, sq))

        ksq = lax.fori_loop(0, s_len // rows, ln_body, (jnp.zeros((rows, 1), F32),) * A_KV_HEADS)
        for n in range(A_KV_HEADS):
            knorm_ref[n] = jnp.broadcast_to(jnp.sqrt(jnp.max(ksq[n], axis=0, keepdims=True)), (1, LANES))

    w_rows = wi_ref[0].T * (IDX_HEADS ** -0.5 * IDX_DIM ** -0.5)
    for h in range(IDX_HEADS):
        qi4_ref[h * QB:(h + 1) * QB, :] = qi_ref[0, :, h * LANES:(h + 1) * LANES].astype(BF16)
    eye = (lax.broadcasted_iota(I32, (QB, QB), 0) == lax.broadcasted_iota(I32, (QB, QB), 1)).astype(BF16)
    for n in range(A_KV_HEADS):
        for g in range(A_GROUP):
            hd = n * A_GROUP + g
            q4_ref[n, g * QB:(g + 1) * QB, 0:A_HEAD_DIM] = q_ref[0, :, hd * A_HEAD_DIM:(hd + 1) * A_HEAD_DIM]
            q4_ref[n, g * QB:(g + 1) * QB, A_HEAD_DIM:A_HEAD_DIM + QB] = eye

    qf = q_ref[0].astype(F32)
    head_of_col = lax.broadcasted_iota(I32, (A_HEADS * A_HEAD_DIM, LANES), 0) // A_HEAD_DIM
    seg = (head_of_col == lax.broadcasted_iota(I32, (A_HEADS * A_HEAD_DIM, LANES), 1)).astype(BF16)
    qnorm = jnp.sqrt(_dot((qf * qf).astype(BF16), seg)).T
    bound = None
    for hd in range(A_HEADS):
        bh = qnorm[hd:hd + 1, :] * knorm_ref[hd // A_GROUP]
        bound = bh if bound is None else jnp.maximum(bound, bh)
    bound = bound * BOUND_SLACK + bmax_ref[...]
    neg_bound = -bound
    bounded = jnp.max(bound) <= BOUND_LIMIT

    last = i // SUB
    nsc = last + 1

    def rows(j):
        return pl.ds(pl.multiple_of(j * SC, SC), SC)

    def packed_digits(j, shift, bits):
        word = None
        for blk in range(SUB):
            key = keys_ref[pl.ds(pl.multiple_of(j * SC + blk * QB, QB), QB), :]
            s = shift - 8 * blk
            f = lax.shift_right_arithmetic(key, jnp.int32(s)) if s >= 0 else lax.shift_left(key, jnp.int32(-s))
            f = f & jnp.int32(_as_i32(((1 << bits) - 1) << (8 * blk)))
            word = f if word is None else word | f
        return word

    def word_rows(j):
        return pl.ds(pl.multiple_of(j * QB, QB), QB)

    def index_dots(j):
        return _dot_nt(kln_ref[rows(j), :], qi4_ref[...])

    def score_chunk(j, dots, causal):
        sc = None
        for h in range(IDX_HEADS):
            t = w_rows[h:h + 1, :] * jnp.maximum(dots[:, h * QB:(h + 1) * QB], 0.0)
            sc = t if sc is None else sc + t
        bits = pltpu.bitcast(sc, I32)
        key = jnp.where(bits < 0, bits ^ jnp.int32(0x7FFFFFFF), bits)
        key = jnp.where(sc == 0.0, 0, key)
        if causal:
            key_pos = j * SC + lax.broadcasted_iota(I32, (SC, LANES), 0)
            q_pos = i * QB + lax.broadcasted_iota(I32, (SC, LANES), 1)
            key = jnp.where(key_pos <= q_pos, key, INT_MIN)
        keys_ref[rows(j), :] = key
        top = packed_digits(j, 32 - DIGIT_WIDTHS[0], DIGIT_WIDTHS[0]) ^ BYTE_HALF
        dig_ref[word_rows(j), :] = top | BYTE_GUARDS
        cls_ref[word_rows(j), :] = jnp.full((QB, LANES), BYTE_LOW, I32)
        if causal:
            @pl.when(j % 2 == 0)
            def _():
                dig_ref[word_rows(j + 1), :] = jnp.full((QB, LANES), BYTE_GUARDS, I32)

    def score_run(first, count):
        pending = [index_dots(first + c) for c in range(min(2, count))]
        for c in range(count):
            score_chunk(first + c, pending.pop(0), False)
            if c + 2 < count:
                pending.append(index_dots(first + c + 2))

    def score_body(g, carry):
        score_run(SCORE_UNROLL * g, SCORE_UNROLL)
        return carry

    lax.fori_loop(0, last // SCORE_UNROLL, score_body, 0)
    for rem in range(1, SCORE_UNROLL):
        @pl.when(last % SCORE_UNROLL == rem)
        def _(rem=rem):
            score_run(last - rem, rem)

    score_chunk(last, index_dots(last), True)

    t_pos = i * QB + lax.broadcasted_iota(I32, (1, LANES), 1)
    kk = jnp.minimum(t_pos + 1, topk).astype(F32)

    def count_digit_ge(cand):
        packed = cand * BYTE_ONES

        def body(g, cnt):
            d = dig_ref[pl.ds(pl.multiple_of(g * 2 * QB, 2 * QB), 2 * QB), :] - packed
            hit = lax.population_count(d & BYTE_GUARDS)
            return cnt + hit[:QB] + hit[QB:]

        cnt = lax.fori_loop(0, (nsc + 1) // 2, body, jnp.zeros((QB, LANES), I32))
        return jnp.sum(cnt, axis=0, keepdims=True).astype(F32)

    def select_digit(above, bits):
        def bit_body(bi, carry):
            t, greater = carry
            cand = t + lax.shift_left(jnp.int32(1), bits - 1 - bi)
            cnt = count_digit_ge(cand)
            ok = above + cnt >= kk
            return jnp.where(ok, cand, t), jnp.where(ok, greater, cnt)

        t, greater = lax.fori_loop(0, bits, bit_body,
                                   (jnp.zeros((1, LANES), I32), jnp.zeros((1, LANES), F32)))
        return t, above + greater

    thr = jnp.zeros((1, LANES), I32)
    above = jnp.zeros((1, LANES), F32)
    shift = 32
    for level, bits in enumerate(DIGIT_WIDTHS):
        shift -= bits
        if level > 0:
            t_packed = t * BYTE_ONES

            def next_digit(j, carry, shift=shift, bits=bits, t_packed=t_packed):
                z = (dig_ref[word_rows(j), :] & BYTE_LOW) ^ t_packed
                same = ((z + BYTE_LOW) & BYTE_GUARDS) ^ BYTE_GUARDS
                cls = cls_ref[word_rows(j), :] & (same - lax.shift_right_logical(same, jnp.int32(7)))
                cls_ref[word_rows(j), :] = cls
                dig_ref[word_rows(j), :] = (packed_digits(j, shift, bits) & cls) | BYTE_GUARDS
                return carry

            lax.fori_loop(0, nsc, next_digit, 0)
        t, above = select_digit(above, bits)
        thr = thr + lax.shift_left(t - (DIGIT_HALF if level == 0 else 0), jnp.int32(shift))

    need = kk - above
    sub_k = lax.broadcasted_iota(I32, (QB, QB), 0)
    sub_k2 = lax.broadcasted_iota(I32, (QB, QB), 1)
    tril = (sub_k2 <= sub_k).astype(BF16)

    def mask_blocks(first, count, seen):
        keys, eqs, pres = [], [], []
        for r in range(count):
            key = keys_ref[pl.ds(pl.multiple_of((first + r) * QB, QB), QB), :]
            eq = key == thr
            keys.append(key)
            eqs.append(eq)
            pres.append(_dot(tril, jnp.where(eq, 1.0, 0.0).astype(BF16)))
        for r in range(count):
            sel = (keys[r] > thr) | (eqs[r] & ((seen + pres[r]) <= need))
            madd_ref[pl.ds(pl.multiple_of((first + r) * QB, QB), QB), :] = jnp.where(sel, neg_bound, NEG).astype(BF16)
            seen = seen + pres[r][QB - 1:QB, :]
        return seen

    seen = lax.fori_loop(0, nsc // 2, lambda g, seen: mask_blocks(g * 2 * SUB, 2 * SUB, seen),
                         jnp.zeros((1, LANES), F32))

    @pl.when(nsc % 2 == 1)
    def _():
        mask_blocks(last * SUB, SUB, seen)

    def split_near(slot, blk):
        blk_rows = pl.ds(pl.multiple_of(blk * QB, QB), QB)
        mnear_ref[slot] = madd_ref[blk_rows, :]
        madd_ref[blk_rows, :] = jnp.full((QB, LANES), NEG, BF16)

    split_near(1, i)

    @pl.when(i >= 1)
    def _():
        split_near(0, i - 1)

    m_ref[...] = jnp.full(m_ref.shape, NEG, F32)
    l_ref[...] = jnp.zeros(l_ref.shape, F32)
    acc_ref[...] = jnp.zeros(acc_ref.shape, F32)

    def logits(kc, madd, n):
        return _dot_nt(jnp.concatenate([kc, madd], axis=1), q4_ref[n])

    def far_logits(j, n):
        return logits(k_ref[0, rows(j), n * A_HEAD_DIM:(n + 1) * A_HEAD_DIM], madd_ref[rows(j), :], n)

    def online_update(n, s, vt):
        m_prev = m_ref[n]
        m_new = jnp.maximum(m_prev, jnp.max(s, axis=0, keepdims=True))
        alpha = jnp.exp2(m_prev - m_new)
        p = jnp.exp2(s - m_new)
        l_ref[n] = alpha * l_ref[n] + jnp.sum(p, axis=0, keepdims=True)
        acc_ref[n] = alpha * acc_ref[n] + _dot(vt, p.astype(BF16))
        m_ref[n] = m_new

    def bounded_update(n, s, vt):
        p = jnp.exp2(s)
        l_ref[n] += jnp.sum(p, axis=0, keepdims=True)
        acc_ref[n] += _dot(vt, p.astype(BF16))

    def far_vt(j, n):
        return vt_ref[0, j, n * A_HEAD_DIM:(n + 1) * A_HEAD_DIM, :]

    n_far = jnp.where(i % SUB >= 2, nsc, last)

    def attend(update):
        def sweep(first, count):
            tasks = [(first + c, n) for c in range(count) for n in range(A_KV_HEADS)]
            ahead = 6
            pending = [far_logits(j, n) for j, n in tasks[:ahead]]
            for t, (j, n) in enumerate(tasks):
                update(n, pending.pop(0), far_vt(j, n))
                if t + ahead < len(tasks):
                    pending.append(far_logits(*tasks[t + ahead]))

        def sweep_body(g, carry):
            sweep(far_unroll * g, far_unroll)
            return carry

        lax.fori_loop(0, n_far // far_unroll, sweep_body, 0)
        for rem in range(1, far_unroll):
            @pl.when(n_far % far_unroll == rem)
            def _(rem=rem):
                sweep(n_far - rem, rem)

        def near_step(slot, blk, vts_ref):
            blk_rows = pl.ds(pl.multiple_of(blk * QB, QB), QB)
            ss = [logits(k_ref[0, blk_rows, n * A_HEAD_DIM:(n + 1) * A_HEAD_DIM], mnear_ref[slot], n)
                  + bias_ref[slot, n] for n in range(A_KV_HEADS)]
            for n in range(A_KV_HEADS):
                update(n, ss[n], vts_ref[0, 0, n * A_HEAD_DIM:(n + 1) * A_HEAD_DIM, :])

        @pl.when(i >= 1)
        def _():
            near_step(0, i - 1, vtp_ref)

        near_step(1, i, vtd_ref)

    @pl.when(bounded)
    def _():
        attend(bounded_update)

    @pl.when(jnp.logical_not(bounded))
    def _():
        attend(online_update)

    for n in range(A_KV_HEADS):
        out_t = acc_ref[n] / l_ref[n]
        for g in range(A_GROUP):
            hd = n * A_GROUP + g
            o_ref[0, :, hd * A_HEAD_DIM:(hd + 1) * A_HEAD_DIM] = out_t[:, g * QB:(g + 1) * QB].T.astype(o_ref.dtype)


def _dsa(p16, p32, vt, vt_blk, ln_g, ln_b, bias_near, *, topk, far_unroll):
    b, s, _ = p16.shape
    bias_max = jnp.broadcast_to(jnp.max(jnp.abs(bias_near)) + BOUND_MARGIN, (1, LANES)).astype(F32)
    aq = A_HEADS * A_HEAD_DIM
    akv = A_KV_HEADS * A_HEAD_DIM
    qiw = IDX_HEADS * LANES
    nch = s // QB
    gq = A_GROUP * QB
    k_blk = (aq + X_HEADS * X_HEAD_DIM) // akv
    qi_blk = 2 * LRU_WIDTH // qiw
    ki_blk = (2 * LRU_WIDTH + qiw) // LANES
    return pl.pallas_call(
        functools.partial(_dsa_kernel, topk=topk, far_unroll=far_unroll),
        out_shape=jax.ShapeDtypeStruct((b, s, aq), BF16),
        grid=(b, nch),
        in_specs=[
            pl.BlockSpec((1, QB, aq), lambda bi, i: (bi, i, 0)),
            pl.BlockSpec((1, s, akv), lambda bi, i: (bi, 0, k_blk)),
            pl.BlockSpec((1, s // SC, akv, SC), lambda bi, i: (bi, 0, 0, 0)),
            pl.BlockSpec((1, 1, akv, QB), lambda bi, i: (bi, jnp.maximum(i - 1, 0), 0, 0)),
            pl.BlockSpec((1, 1, akv, QB), lambda bi, i: (bi, i, 0, 0)),
            pl.BlockSpec((1, QB, qiw), lambda bi, i: (bi, i, qi_blk)),
            pl.BlockSpec((1, s, LANES), lambda bi, i: (bi, 0, ki_blk)),
            pl.BlockSpec((1, QB, LANES), lambda bi, i: (bi, i, ki_blk + 1)),
            pl.BlockSpec((1, LANES), lambda bi, i: (0, 0)),
            pl.BlockSpec((1, LANES), lambda bi, i: (0, 0)),
            pl.BlockSpec((2, A_KV_HEADS, QB, gq), lambda bi, i: (0, 0, 0, 0)),
            pl.BlockSpec((1, LANES), lambda bi, i: (0, 0)),
        ],
        out_specs=pl.BlockSpec((1, QB, aq), lambda bi, i: (bi, i, 0)),
        scratch_shapes=[
            pltpu.VMEM((s, LANES), BF16),
            pltpu.VMEM((A_KV_HEADS, 1, LANES), F32),
            pltpu.VMEM((s, LANES), I32),
            pltpu.VMEM((-(-(s // SC) // 2) * 2 * QB, LANES), I32),
            pltpu.VMEM((s // SUB, LANES), I32),
            pltpu.VMEM((s, LANES), BF16),
            pltpu.VMEM((2, QB, LANES), BF16),
            pltpu.VMEM((IDX_HEADS * QB, LANES), BF16),
            pltpu.VMEM((A_KV_HEADS, gq, A_HEAD_DIM + QB), BF16),
            pltpu.VMEM((A_KV_HEADS, 1, gq), F32),
            pltpu.VMEM((A_KV_HEADS, 1, gq), F32),
            pltpu.VMEM((A_KV_HEADS, A_HEAD_DIM, gq), F32),
        ],
        compiler_params=_params("arbitrary", "arbitrary"),
        name="dsa",
    )(p16, p16, vt, vt_blk, vt_blk, p32, p32, p32, ln_g, ln_b, bias_near, bias_max)


def _t5_bucket(dist):
    max_exact = N_BUCKETS // 2
    d = jnp.maximum(dist, 0)
    df = jnp.maximum(d, 1).astype(F32)
    large = max_exact + (jnp.log(df / max_exact) / math.log(MAX_DISTANCE / max_exact)
                         * (N_BUCKETS - max_exact)).astype(I32)
    large = jnp.minimum(large, N_BUCKETS - 1)
    return jnp.where(d < max_exact, d, large)


def _near_bias(rel_bias):
    t = jnp.arange(QB, dtype=I32)[:, None]
    s = jnp.arange(QB, dtype=I32)[None, :]
    tiles = []
    for off in (QB, 0):
        bucket = _t5_bucket(t - s + off)
        onehot = (bucket[..., None] == jnp.arange(N_BUCKETS, dtype=I32)).astype(F32)
        table = (rel_bias - rel_bias[N_BUCKETS - 1]) * LOG2E
        tile = jnp.einsum('tsb,bh->tsh', onehot, table, precision=lax.Precision.HIGHEST)
        tile = tile.transpose(1, 2, 0).reshape(QB, A_KV_HEADS, A_GROUP * QB)
        tiles.append(tile.transpose(1, 0, 2))
    return jnp.stack(tiles).astype(F32)


def _pad_cols(w, width):
    return jnp.pad(w, [(0, 0)] * (w.ndim - 1) + [(0, width - w.shape[-1])])


def _split_points(d_model):
    aq = A_HEADS * A_HEAD_DIM
    akv = A_KV_HEADS * A_HEAD_DIM
    sizes = (aq, akv, akv, IDX_HEADS * IDX_DIM, IDX_DIM, IDX_HEADS,
             LRU_WIDTH, LRU_WIDTH, X_HEADS * X_HEAD_DIM, N_BRANCH * d_model)
    pts, acc = [0], 0
    for n in sizes:
        acc += n
        pts.append(acc)
    return pts


def _regroup_kernel(w_ref, o16_ref, o32_ref, og_ref, *, pts):
    (q0, k0, v0, qi0, ki0, wi0, x0, g0, qm0, gt0, end) = pts

    def seg(lo, hi, scale=None):
        v = w_ref[0, :, lo:hi]
        if scale is not None:
            v = v * scale
        return v.astype(BF16)

    o16_ref[0, :, 0:k0 - q0] = seg(q0, k0, A_HEAD_DIM ** -0.5 * LOG2E)
    c = k0 - q0
    o16_ref[0, :, c:c + gt0 - qm0] = seg(qm0, gt0, X_HEAD_DIM ** -0.5)
    c += gt0 - qm0
    o16_ref[0, :, c:c + qi0 - k0] = seg(k0, qi0)
    o32_ref[0, :, 0:qm0 - x0] = seg(x0, qm0)
    c = qm0 - x0
    o32_ref[0, :, c:] = jnp.zeros((o32_ref.shape[1], o32_ref.shape[2] - c), BF16)
    for h in range(IDX_HEADS):
        o32_ref[0, :, c + h * LANES:c + h * LANES + IDX_DIM] = seg(qi0 + h * IDX_DIM, qi0 + (h + 1) * IDX_DIM)
    c += IDX_HEADS * LANES
    o32_ref[0, :, c:c + IDX_DIM] = seg(ki0, wi0)
    c += LANES
    o32_ref[0, :, c:c + IDX_HEADS] = seg(wi0, x0)
    og_ref[0] = seg(gt0, end)


def _prep_w_in(w_in, d_model, *, tk):
    depth, k, n = w_in.shape
    pts = _split_points(d_model)
    n16 = pts[3] + pts[9] - pts[8]
    n32 = 2 * LRU_WIDTH + (IDX_HEADS + 2) * LANES
    ng = n - pts[9]
    return pl.pallas_call(
        functools.partial(_regroup_kernel, pts=tuple(pts)),
        out_shape=(jax.ShapeDtypeStruct((depth, k, n16), BF16), jax.ShapeDtypeStruct((depth, k, n32), BF16),
                   jax.ShapeDtypeStruct((depth, k, ng), BF16)),
        grid=(depth, k // tk),
        in_specs=[pl.BlockSpec((1, tk, n), lambda l, i: (l, i, 0))],
        out_specs=(pl.BlockSpec((1, tk, n16), lambda l, i: (l, i, 0)),
                   pl.BlockSpec((1, tk, n32), lambda l, i: (l, i, 0)),
                   pl.BlockSpec((1, tk, ng), lambda l, i: (l, i, 0))),
        compiler_params=_params("parallel", "parallel"),
        name="regroup",
    )(w_in)


def _split_gu_kernel(w_ref, og_ref, ou_ref):
    d_ff = w_ref.shape[2] // 2
    pad = og_ref.shape[2] - d_ff
    for o_ref, lo in ((og_ref, 0), (ou_ref, d_ff)):
        o_ref[0, :, 0:d_ff] = w_ref[0, :, lo:lo + d_ff].astype(BF16)
        if pad:
            o_ref[0, :, d_ff:] = jnp.zeros((o_ref.shape[1], pad), BF16)


def _pad_down_kernel(w_ref, o_ref):
    d_ff = w_ref.shape[1]
    o_ref[0, 0:d_ff, :] = w_ref[0].astype(BF16)
    if o_ref.shape[1] > d_ff:
        o_ref[0, d_ff:, :] = jnp.zeros((o_ref.shape[1] - d_ff, o_ref.shape[2]), BF16)


def _pad_ff(w_gu, w_down, tf, *, tk):
    depth, d, _ = w_gu.shape
    d_ff = w_down.shape[1]
    fp = -(-d_ff // tf) * tf
    assert d_ff % LANES == 0
    wg, wu = pl.pallas_call(
        _split_gu_kernel,
        out_shape=(jax.ShapeDtypeStruct((depth, d, fp), BF16),) * 2,
        grid=(depth, d // tk),
        in_specs=[pl.BlockSpec((1, tk, 2 * d_ff), lambda l, i: (l, i, 0))],
        out_specs=(pl.BlockSpec((1, tk, fp), lambda l, i: (l, i, 0)),) * 2,
        compiler_params=_params("parallel", "parallel"),
        name="split_gu",
    )(w_gu)
    wd = pl.pallas_call(
        _pad_down_kernel,
        out_shape=jax.ShapeDtypeStruct((depth, fp, d), BF16),
        grid=(depth, d // tk),
        in_specs=[pl.BlockSpec((1, d_ff, tk), lambda l, i: (l, 0, i))],
        out_specs=pl.BlockSpec((1, fp, tk), lambda l, i: (l, 0, i)),
        compiler_params=_params("parallel", "parallel"),
        name="pad_down",
    )(w_down)
    return wg, wu, wd


def _tiles(m, s, d_model):
    return dict(
        ffn_tm=min(1024, m), ffn_final_tm=min(1024, m), ffn_tf=512,
        proj_tm=min(256, m), proj_tn=256,
        merge_tm=min(512, m), merge_tn=min(512, d_model),
        lru_tc=min(256, s), mem_tm=min(512, s),
    )


def kernel(x, mem, rel_bias, final_norm, norm_ff1, w_ff1_gu, w_ff1_down, norm_mix, w_in, conv_w, conv_b, w_a, b_a, w_i, b_i, lam, idx_ln_g, idx_ln_b, mem_norm, w_mem_kv, w_branch, w_out, norm_ff2, w_ff2_gu, w_ff2_down):
    b, s, d = x.shape
    nm = mem.shape[1]
    depth = w_in.shape[0]
    m = b * s
    assert s % SC == 0 and d % LANES == 0
    tl = _tiles(m, s, d)
    topk = min(TOPK_MAX, s // 4)
    bias_near = _near_bias(rel_bias)
    row = lambda v: v.reshape(1, -1)

    ff1 = _pad_ff(w_ff1_gu, w_ff1_down, tl["ffn_tf"], tk=min(256, d))
    ff2 = _pad_ff(w_ff2_gu, w_ff2_down, tl["ffn_tf"], tk=min(256, d))
    w16, w32, wgates = _prep_w_in(w_in, d, tk=min(256, d))
    w_a16, w_i16 = w_a.astype(BF16), w_i.astype(BF16)
    w_mem16, w_branch16, w_out16 = w_mem_kv.astype(BF16), w_branch.astype(BF16), w_out.astype(BF16)
    ln_g = _pad_cols(idx_ln_g, LANES)
    ln_b = _pad_cols(idx_ln_b, LANES)

    x2 = x.reshape(m, d)
    mem2 = mem.reshape(b * nm, d)
    for l in range(depth):
        x2 = _ffn(x2, row(norm_ff1[l]), *ff1, row(final_norm), l, final_norm=False,
                  tm=tl["ffn_tm"], tf=tl["ffn_tf"])

        p16, p32 = _proj(x2, row(norm_mix[l]), w16, w32, l, tm=tl["proj_tm"])
        p16 = p16.reshape(b, s, -1)
        p32 = p32.reshape(b, s, -1)

        v_cols = slice(p16.shape[-1] - A_KV_HEADS * A_HEAD_DIM, p16.shape[-1])
        vt = p16[..., v_cols].reshape(b, s // SC, SC, -1).transpose(0, 1, 3, 2)
        vt_blk = p16[..., v_cols].reshape(b, s // QB, QB, -1).transpose(0, 1, 3, 2)
        y_a = _dsa(p16, p32, vt, vt_blk, row(ln_g[l]), row(ln_b[l]),
                   bias_near, topk=topk, far_unroll=4)
        y_b = _lru(p32, conv_w[l], row(conv_b[l]), w_a16, w_i16,
                   row(b_a[l]), row(b_i[l]), row(lam[l]), l, tc=tl["lru_tc"])
        mkv = _norm_matmul(mem2, row(mem_norm[l]), w_mem16, l, BF16,
                           tm=min(256, b * nm), tn=tl["proj_tn"], name="memkv")
        y_c = _memattn(p16, mkv.reshape(b, nm, -1), tm=tl["mem_tm"])

        x2 = _merge(x2, row(norm_mix[l]), y_a.reshape(m, -1), y_b.reshape(m, -1), y_c.reshape(m, -1),
                    wgates, w_branch16, w_out16, l,
                    tm=tl["merge_tm"], tn=tl["merge_tn"])

        is_last = l == depth - 1
        x2 = _ffn(x2, row(norm_ff2[l]), *ff2, row(final_norm), l, final_norm=is_last,
                  tm=tl["ffn_final_tm" if is_last else "ffn_tm"], tf=tl["ffn_tf"])
    return x2.reshape(b, s, d)
```

```python
import functools
import math

import jax
import jax.numpy as jnp
from jax import lax
from jax.experimental import pallas as pl
from jax.experimental.pallas import tpu as pltpu

F32 = jnp.float32
BF16 = jnp.bfloat16
I32 = jnp.int32

EPS = 1e-6
A_HEADS = 8
A_KV_HEADS = 2
A_HEAD_DIM = 128
A_GROUP = A_HEADS // A_KV_HEADS
IDX_HEADS = 4
IDX_DIM = 64
TOPK_MAX = 256
LRU_WIDTH = 1024
LRU_BLOCKS = 8
LRU_BLOCK_W = LRU_WIDTH // LRU_BLOCKS
CONV_WIDTH = 4
LRU_C = 8.0
X_HEADS = 4
X_HEAD_DIM = 256
N_BRANCH = 3
BRANCH_WIDTH = 1024
N_BUCKETS = 32
MAX_DISTANCE = 128

LANES = 128
SUBLANES = 8
VMEM_LIMIT = 56 * 1024 * 1024
ROW_CHUNK = 256

QB = 128
SC = 512
SUB = SC // QB
SCORE_UNROLL = 4
NEG = -1e30
LOG2E = math.log2(math.e)
INT_MIN = -2 ** 31


def _as_i32(v):
    return v - (1 << 32) if v >= (1 << 31) else v


DIGIT_WIDTHS = (7, 7, 7, 7, 4)
DIGIT_HALF = 1 << (DIGIT_WIDTHS[0] - 1)
BYTE_ONES = _as_i32(0x01010101)
BYTE_LOW = _as_i32(0x7F7F7F7F)
BYTE_GUARDS = _as_i32(0x80808080)
BYTE_HALF = _as_i32(0x40404040)
assert SUB == 4 and sum(DIGIT_WIDTHS) == 32 and max(DIGIT_WIDTHS) == DIGIT_WIDTHS[0] == 7

BOUND_SLACK = 1.02
BOUND_MARGIN = 0.5
BOUND_LIMIT = 55.0


def _params(*sem):
    return pltpu.CompilerParams(dimension_semantics=sem, vmem_limit_bytes=VMEM_LIMIT)


def _rms(x, g):
    ms = jnp.mean(x * x, axis=-1, keepdims=True)
    return x * lax.rsqrt(ms + EPS) * g


def _sigmoid(x):
    return 1.0 / (1.0 + jnp.exp(-x))


def _dot(a, b):
    return jnp.dot(a, b, preferred_element_type=F32)


def _dot_nt(a, b):
    return lax.dot_general(a, b, (((1,), (1,)), ((), ())), preferred_element_type=F32)


def _ffn_kernel(x_ref, g_ref, wg_ref, wu_ref, wd_ref, fg_ref, o_ref, h_ref, *, final_norm):
    f = pl.program_id(1)

    def for_row_chunks(fn):
        rc = min(ROW_CHUNK, x_ref.shape[0])

        def body(r, carry):
            fn(pl.ds(pl.multiple_of(r * rc, rc), rc))
            return carry

        lax.fori_loop(0, x_ref.shape[0] // rc, body, 0)

    @pl.when(f == 0)
    def _():
        def prologue(rs):
            h_ref[rs, :] = _rms(x_ref[rs, :], g_ref[...]).astype(BF16)
            o_ref[rs, :] = jnp.zeros((rs.size, o_ref.shape[1]), F32)

        for_row_chunks(prologue)

    h = h_ref[...]
    g = _dot(h, wg_ref[...])
    u = _dot(h, wu_ref[...])
    a = (g * _sigmoid(g) * u).astype(BF16)
    o_ref[...] += _dot(a, wd_ref[...])

    @pl.when(f == pl.num_programs(1) - 1)
    def _():
        def epilogue(rs):
            y = x_ref[rs, :] + 0.5 * o_ref[rs, :]
            if final_norm:
                y = _rms(y, fg_ref[...])
            o_ref[rs, :] = y

        for_row_chunks(epilogue)


def _ffn(x2, g, wg, wu, wd, fg, layer, *, final_norm, tm, tf):
    m, d = x2.shape
    fp = wg.shape[-1]
    return pl.pallas_call(
        functools.partial(_ffn_kernel, final_norm=final_norm),
        out_shape=jax.ShapeDtypeStruct((m, d), F32),
        grid=(m // tm, fp // tf),
        in_specs=[
            pl.BlockSpec((tm, d), lambda i, f: (i, 0)),
            pl.BlockSpec((1, d), lambda i, f: (0, 0)),
            pl.BlockSpec((None, d, tf), lambda i, f: (layer, 0, f)),
            pl.BlockSpec((None, d, tf), lambda i, f: (layer, 0, f)),
            pl.BlockSpec((None, tf, d), lambda i, f: (layer, f, 0)),
            pl.BlockSpec((1, d), lambda i, f: (0, 0)),
        ],
        out_specs=pl.BlockSpec((tm, d), lambda i, f: (i, 0)),
        scratch_shapes=[pltpu.VMEM((tm, d), BF16)],
        compiler_params=_params("parallel", "arbitrary"),
        name="ffn",
    )(x2, g, wg, wu, wd, fg)


def _norm_matmul_kernel(x_ref, g_ref, w_ref, o_ref, h_ref):
    @pl.when(pl.program_id(1) == 0)
    def _():
        h_ref[...] = _rms(x_ref[...], g_ref[...]).astype(BF16)

    o_ref[...] = _dot(h_ref[...], w_ref[...]).astype(o_ref.dtype)


def _norm_matmul(x2, g, w, layer, out_dtype, *, tm, tn, name):
    m, d = x2.shape
    n = w.shape[-1]
    return pl.pallas_call(
        _norm_matmul_kernel,
        out_shape=jax.ShapeDtypeStruct((m, n), out_dtype),
        grid=(m // tm, n // tn),
        in_specs=[
            pl.BlockSpec((tm, d), lambda i, j: (i, 0)),
            pl.BlockSpec((1, d), lambda i, j: (0, 0)),
            pl.BlockSpec((None, d, tn), lambda i, j: (layer, 0, j)),
        ],
        out_specs=pl.BlockSpec((tm, tn), lambda i, j: (i, j)),
        scratch_shapes=[pltpu.VMEM((tm, d), BF16)],
        compiler_params=_params("parallel", "arbitrary"),
        name=name,
    )(x2, g, w)


def _proj_kernel(x_ref, g_ref, w16_ref, w32_ref, o16_ref, o32_ref):
    h = _rms(x_ref[...], g_ref[...]).astype(BF16)
    o16_ref[...] = _dot(h, w16_ref[...]).astype(o16_ref.dtype)
    o32_ref[...] = _dot(h, w32_ref[...])


def _proj(x2, g, w16, w32, layer, *, tm):
    m, d = x2.shape
    n16, n32 = w16.shape[-1], w32.shape[-1]
    resident = pl.Buffered(1)
    return pl.pallas_call(
        _proj_kernel,
        out_shape=(jax.ShapeDtypeStruct((m, n16), BF16), jax.ShapeDtypeStruct((m, n32), F32)),
        grid=(m // tm,),
        in_specs=[
            pl.BlockSpec((tm, d), lambda i: (i, 0)),
            pl.BlockSpec((1, d), lambda i: (0, 0)),
            pl.BlockSpec((None, d, n16), lambda i: (layer, 0, 0), pipeline_mode=resident),
            pl.BlockSpec((None, d, n32), lambda i: (layer, 0, 0), pipeline_mode=resident),
        ],
        out_specs=(pl.BlockSpec((tm, n16), lambda i: (i, 0)), pl.BlockSpec((tm, n32), lambda i: (i, 0))),
        compiler_params=_params("parallel"),
        name="proj",
    )(x2, g, w16, w32)


def _lru_kernel(x_ref, gate_ref, cw_ref, cb_ref, wa_ref, wi_ref, ba_ref, bi_ref, lam_ref,
                o_ref, halo_ref, hc_ref, *, tc):
    n_groups = tc // SUBLANES
    assert CONV_WIDTH <= SUBLANES

    @pl.when(pl.program_id(1) == 0)
    def _():
        halo_ref[...] = jnp.zeros_like(halo_ref)
        hc_ref[...] = jnp.zeros_like(hc_ref)

    x3 = x_ref[0].reshape(n_groups, SUBLANES, LRU_WIDTH)
    prev3 = jnp.concatenate([halo_ref[...][None], x3[:-1]], axis=0)
    row3 = lax.broadcasted_iota(I32, (n_groups, SUBLANES, LRU_WIDTH), 1)
    xc = x3 * cw_ref[CONV_WIDTH - 1:CONV_WIDTH, :]
    for k in range(1, CONV_WIDTH):
        xk = jnp.where(row3 >= k, pltpu.roll(x3, k, 1), pltpu.roll(prev3, k, 1))
        xc = xc + xk * cw_ref[CONV_WIDTH - 1 - k:CONV_WIDTH - k, :]
    xc = (cb_ref[...] + xc).reshape(tc, LRU_WIDTH)
    halo_ref[...] = x3[n_groups - 1]

    xcb = xc.astype(BF16)
    r_parts, i_parts = [], []
    for n in range(LRU_BLOCKS):
        blk = xcb[:, n * LRU_BLOCK_W:(n + 1) * LRU_BLOCK_W]
        r_parts.append(_dot(blk, wa_ref[n]))
        i_parts.append(_dot(blk, wi_ref[n]))
    r = _sigmoid(jnp.concatenate(r_parts, axis=1) + ba_ref[...])
    gi = _sigmoid(jnp.concatenate(i_parts, axis=1) + bi_ref[...])

    softplus_neg_lam = jnp.log1p(jnp.exp(-lam_ref[...]))
    log_a = (-LRU_C) * r * softplus_neg_lam
    a = jnp.exp(log_a)
    mult = jnp.sqrt(jnp.maximum((1.0 - a) * (1.0 + a), 0.0))
    u = mult * (gi * xc)

    a = a.reshape(n_groups, SUBLANES, LRU_WIDTH)
    u = u.reshape(n_groups, SUBLANES, LRU_WIDTH)
    row = lax.broadcasted_iota(I32, (n_groups, SUBLANES, LRU_WIDTH), 1)
    d = 1
    while d < SUBLANES:
        keep = row >= d
        a_sh = jnp.where(keep, pltpu.roll(a, d, 1), 1.0)
        u_sh = jnp.where(keep, pltpu.roll(u, d, 1), 0.0)
        u = a * u_sh + u
        a = a * a_sh
        d *= 2
    carry = hc_ref[...]
    groups = []
    for g in range(n_groups):
        hg = a[g] * carry + u[g]
        carry = hg[SUBLANES - 1:SUBLANES, :]
        groups.append(hg)
    h = jnp.concatenate(groups, axis=0)
    hc_ref[...] = carry

    gate = gate_ref[0]
    gelu = 0.5 * gate * (1.0 + jnp.tanh(math.sqrt(2.0 / math.pi) * (gate + 0.044715 * (gate * gate * gate))))
    o_ref[0] = (h * gelu).astype(o_ref.dtype)


def _lru(p32, conv_w, conv_b, w_a, w_i, b_a, b_i, lam, layer, *, tc):
    b, s, _ = p32.shape
    vec = pl.BlockSpec((1, LRU_WIDTH), lambda bi, c: (0, 0))
    wblk = pl.BlockSpec((None, LRU_BLOCKS, LRU_BLOCK_W, LRU_BLOCK_W), lambda bi, c: (layer, 0, 0, 0))
    return pl.pallas_call(
        functools.partial(_lru_kernel, tc=tc),
        out_shape=jax.ShapeDtypeStruct((b, s, LRU_WIDTH), BF16),
        grid=(b, s // tc),
        in_specs=[
            pl.BlockSpec((1, tc, LRU_WIDTH), lambda bi, c: (bi, c, 0)),
            pl.BlockSpec((1, tc, LRU_WIDTH), lambda bi, c: (bi, c, 1)),
            pl.BlockSpec((CONV_WIDTH, LRU_WIDTH), lambda bi, c: (0, 0)),
            vec, wblk, wblk, vec, vec, vec,
        ],
        out_specs=pl.BlockSpec((1, tc, LRU_WIDTH), lambda bi, c: (bi, c, 0)),
        scratch_shapes=[pltpu.VMEM((SUBLANES, LRU_WIDTH), F32), pltpu.VMEM((1, LRU_WIDTH), F32)],
        compiler_params=_params("arbitrary", "arbitrary"),
        name="rglru",
    )(p32, p32, conv_w, conv_b, w_a, w_i, b_a, b_i, lam)


def _memattn_kernel(q_ref, kv_ref, o_ref):
    kw = X_HEADS * X_HEAD_DIM
    for h in range(X_HEADS):
        sl = slice(h * X_HEAD_DIM, (h + 1) * X_HEAD_DIM)
        q = q_ref[0, :, sl]
        k = kv_ref[0, :, sl]
        v = kv_ref[0, :, kw + h * X_HEAD_DIM:kw + (h + 1) * X_HEAD_DIM]
        s = _dot_nt(q, k)
        m = jnp.max(s, axis=-1, keepdims=True)
        p = jnp.exp(s - m)
        l = jnp.sum(p, axis=-1, keepdims=True)
        o = _dot(p.astype(BF16), v)
        o_ref[0, :, sl] = (o / l).astype(o_ref.dtype)


def _memattn(p16, mkv, *, tm):
    b, s, _ = p16.shape
    nm = mkv.shape[1]
    kw = X_HEADS * X_HEAD_DIM
    return pl.pallas_call(
        _memattn_kernel,
        out_shape=jax.ShapeDtypeStruct((b, s, kw), BF16),
        grid=(b, s // tm),
        in_specs=[
            pl.BlockSpec((1, tm, kw), lambda bi, i: (bi, i, 1)),
            pl.BlockSpec((1, nm, 2 * kw), lambda bi, i: (bi, 0, 0)),
        ],
        out_specs=pl.BlockSpec((1, tm, kw), lambda bi, i: (bi, i, 0)),
        compiler_params=_params("parallel", "parallel"),
        name="memattn",
    )(p16, mkv)


def _merge_kernel(x_ref, g_ref, ya_ref, yb_ref, yc_ref, wg0_ref, wg1_ref, wg2_ref, wb_ref, wo_ref,
                  o_ref, h_ref):
    n = pl.program_id(1)

    @pl.when(n == 0)
    def _():
        h_ref[...] = _rms(x_ref[...], g_ref[...]).astype(BF16)
        o_ref[...] = jnp.zeros_like(o_ref)

    h = h_ref[...]
    merged = None
    for j, (y_ref, wg_ref) in enumerate(((ya_ref, wg0_ref), (yb_ref, wg1_ref), (yc_ref, wg2_ref))):
        term = _sigmoid(_dot(h, wg_ref[...])) * _dot(y_ref[...], wb_ref[j])
        merged = term if merged is None else merged + term
    o_ref[...] += _dot(merged.astype(BF16), wo_ref[...])

    @pl.when(n == pl.num_programs(1) - 1)
    def _():
        o_ref[...] = x_ref[...] + o_ref[...]


def _merge(x2, g, ya, yb, yc, wgates, wbranch, wout, layer, *, tm, tn):
    m, d = x2.shape
    nblk = d // tn
    yspec = pl.BlockSpec((tm, BRANCH_WIDTH), lambda i, n: (i, 0))
    return pl.pallas_call(
        _merge_kernel,
        out_shape=jax.ShapeDtypeStruct((m, d), F32),
        grid=(m // tm, nblk),
        in_specs=[
            pl.BlockSpec((tm, d), lambda i, n: (i, 0)),
            pl.BlockSpec((1, d), lambda i, n: (0, 0)),
            yspec, yspec, yspec,
            pl.BlockSpec((None, d, tn), lambda i, n: (layer, 0, n)),
            pl.BlockSpec((None, d, tn), lambda i, n: (layer, 0, nblk + n)),
            pl.BlockSpec((None, d, tn), lambda i, n: (layer, 0, 2 * nblk + n)),
            pl.BlockSpec((None, N_BRANCH, BRANCH_WIDTH, tn), lambda i, n: (layer, 0, 0, n)),
            pl.BlockSpec((None, tn, d), lambda i, n: (layer, n, 0)),
        ],
        out_specs=pl.BlockSpec((tm, d), lambda i, n: (i, 0)),
        scratch_shapes=[pltpu.VMEM((tm, d), BF16)],
        compiler_params=_params("parallel", "arbitrary"),
        name="merge",
    )(x2, g, ya, yb, yc, wgates, wgates, wgates, wbranch, wout)


def _dsa_kernel(q_ref, k_ref, vt_ref, vtp_ref, vtd_ref, qi_ref, ki_ref, wi_ref, lng_ref, lnb_ref, bias_ref, bmax_ref,
                o_ref,
                kln_ref, knorm_ref, keys_ref, dig_ref, cls_ref, madd_ref, mnear_ref, qi4_ref, q4_ref, m_ref, l_ref, acc_ref,
                *, topk, far_unroll):
    i = pl.program_id(1)
    s_len = k_ref.shape[1]

    @pl.when(i == 0)
    def _():
        rows = 512 if s_len % 512 == 0 else QB
        lane = lax.broadcasted_iota(I32, (rows, LANES), 1)
        real = lane < IDX_DIM

        def ln_body(c, carry):
            x = ki_ref[0, pl.ds(c * rows, rows), :]
            mu = jnp.sum(x, axis=-1, keepdims=True) * (1.0 / IDX_DIM)
            xm = jnp.where(real, x - mu, 0.0)
            var = jnp.sum(xm * xm, axis=-1, keepdims=True) * (1.0 / IDX_DIM)
            y = xm * lax.rsqrt(var + EPS) * lng_ref[...] + lnb_ref[...]
            kln_ref[pl.ds(c * rows, rows), :] = y.astype(BF16)
            kf = k_ref[0, pl.ds(c * rows, rows), :].astype(F32)
            sq = [jnp.sum(kf[:, n * A_HEAD_DIM:(n + 1) * A_HEAD_DIM] ** 2, axis=-1, keepdims=True)
                  for n in range(A_KV_HEADS)]
            return tuple(jnp.maximum(a, b) for a, b in zip(carry, sq))

        ksq = lax.fori_loop(0, s_len // rows, ln_body, (jnp.zeros((rows, 1), F32),) * A_KV_HEADS)
        for n in range(A_KV_HEADS):
            knorm_ref[n] = jnp.broadcast_to(jnp.sqrt(jnp.max(ksq[n], axis=0, keepdims=True)), (1, LANES))

    w_rows = wi_ref[0].T * (IDX_HEADS ** -0.5 * IDX_DIM ** -0.5)
    for h in range(IDX_HEADS):
        qi4_ref[h * QB:(h + 1) * QB, :] = qi_ref[0, :, h * LANES:(h + 1) * LANES].astype(BF16)
    eye = (lax.broadcasted_iota(I32, (QB, QB), 0) == lax.broadcasted_iota(I32, (QB, QB), 1)).astype(BF16)
    for n in range(A_KV_HEADS):
        for g in range(A_GROUP):
            hd = n * A_GROUP + g
            q4_ref[n, g * QB:(g + 1) * QB, 0:A_HEAD_DIM] = q_ref[0, :, hd * A_HEAD_DIM:(hd + 1) * A_HEAD_DIM]
            q4_ref[n, g * QB:(g + 1) * QB, A_HEAD_DIM:A_HEAD_DIM + QB] = eye

    qf = q_ref[0].astype(F32)
    head_of_col = lax.broadcasted_iota(I32, (A_HEADS * A_HEAD_DIM, LANES), 0) // A_HEAD_DIM
    seg = (head_of_col == lax.broadcasted_iota(I32, (A_HEADS * A_HEAD_DIM, LANES), 1)).astype(BF16)
    qnorm = jnp.sqrt(_dot((qf * qf).astype(BF16), seg)).T
    bound = None
    for hd in range(A_HEADS):
        bh = qnorm[hd:hd + 1, :] * knorm_ref[hd // A_GROUP]
        bound = bh if bound is None else jnp.maximum(bound, bh)
    bound = bound * BOUND_SLACK + bmax_ref[...]
    neg_bound = -bound
    bounded = jnp.max(bound) <= BOUND_LIMIT

    last = i // SUB
    nsc = last + 1

    def rows(j):
        return pl.ds(pl.multiple_of(j * SC, SC), SC)

    def packed_digits(j, shift, bits):
        word = None
        for blk in range(SUB):
            key = keys_ref[pl.ds(pl.multiple_of(j * SC + blk * QB, QB), QB), :]
            s = shift - 8 * blk
            f = lax.shift_right_arithmetic(key, jnp.int32(s)) if s >= 0 else lax.shift_left(key, jnp.int32(-s))
            f = f & jnp.int32(_as_i32(((1 << bits) - 1) << (8 * blk)))
            word = f if word is None else word | f
        return word

    def word_rows(j):
        return pl.ds(pl.multiple_of(j * QB, QB), QB)

    def index_dots(j):
        return _dot_nt(kln_ref[rows(j), :], qi4_ref[...])

    def score_chunk(j, dots, causal):
        sc = None
        for h in range(IDX_HEADS):
            t = w_rows[h:h + 1, :] * jnp.maximum(dots[:, h * QB:(h + 1) * QB], 0.0)
            sc = t if sc is None else sc + t
        bits = pltpu.bitcast(sc, I32)
        key = jnp.where(bits < 0, bits ^ jnp.int32(0x7FFFFFFF), bits)
        key = jnp.where(sc == 0.0, 0, key)
        if causal:
            key_pos = j * SC + lax.broadcasted_iota(I32, (SC, LANES), 0)
            q_pos = i * QB + lax.broadcasted_iota(I32, (SC, LANES), 1)
            key = jnp.where(key_pos <= q_pos, key, INT_MIN)
        keys_ref[rows(j), :] = key
        top = packed_digits(j, 32 - DIGIT_WIDTHS[0], DIGIT_WIDTHS[0]) ^ BYTE_HALF
        dig_ref[word_rows(j), :] = top | BYTE_GUARDS
        cls_ref[word_rows(j), :] = jnp.full((QB, LANES), BYTE_LOW, I32)
        if causal:
            @pl.when(j % 2 == 0)
            def _():
                dig_ref[word_rows(j + 1), :] = jnp.full((QB, LANES), BYTE_GUARDS, I32)

    def score_run(first, count):
        pending = [index_dots(first + c) for c in range(min(2, count))]
        for c in range(count):
            score_chunk(first + c, pending.pop(0), False)
            if c + 2 < count:
                pending.append(index_dots(first + c + 2))

    def score_body(g, carry):
        score_run(SCORE_UNROLL * g, SCORE_UNROLL)
        return carry

    lax.fori_loop(0, last // SCORE_UNROLL, score_body, 0)
    for rem in range(1, SCORE_UNROLL):
        @pl.when(last % SCORE_UNROLL == rem)
        def _(rem=rem):
            score_run(last - rem, rem)

    score_chunk(last, index_dots(last), True)

    t_pos = i * QB + lax.broadcasted_iota(I32, (1, LANES), 1)
    kk = jnp.minimum(t_pos + 1, topk).astype(F32)

    def count_digit_ge(cand):
        packed = cand * BYTE_ONES

        def body(g, cnt):
            d = dig_ref[pl.ds(pl.multiple_of(g * 2 * QB, 2 * QB), 2 * QB), :] - packed
            hit = lax.population_count(d & BYTE_GUARDS)
            return cnt + hit[:QB] + hit[QB:]

        cnt = lax.fori_loop(0, (nsc + 1) // 2, body, jnp.zeros((QB, LANES), I32))
        return jnp.sum(cnt, axis=0, keepdims=True).astype(F32)

    def select_digit(above, bits):
        def bit_body(bi, carry):
            t, greater = carry
            cand = t + lax.shift_left(jnp.int32(1), bits - 1 - bi)
            cnt = count_digit_ge(cand)
            ok = above + cnt >= kk
            return jnp.where(ok, cand, t), jnp.where(ok, greater, cnt)

        t, greater = lax.fori_loop(0, bits, bit_body,
                                   (jnp.zeros((1, LANES), I32), jnp.zeros((1, LANES), F32)))
        return t, above + greater

    thr = jnp.zeros((1, LANES), I32)
    above = jnp.zeros((1, LANES), F32)
    shift = 32
    for level, bits in enumerate(DIGIT_WIDTHS):
        shift -= bits
        if level > 0:
            t_packed = t * BYTE_ONES

            def next_digit(j, carry, shift=shift, bits=bits, t_packed=t_packed):
                z = (dig_ref[word_rows(j), :] & BYTE_LOW) ^ t_packed
                same = ((z + BYTE_LOW) & BYTE_GUARDS) ^ BYTE_GUARDS
                cls = cls_ref[word_rows(j), :] & (same - lax.shift_right_logical(same, jnp.int32(7)))
                cls_ref[word_rows(j), :] = cls
                dig_ref[word_rows(j), :] = (packed_digits(j, shift, bits) & cls) | BYTE_GUARDS
                return carry

            lax.fori_loop(0, nsc, next_digit, 0)
        t, above = select_digit(above, bits)
        thr = thr + lax.shift_left(t - (DIGIT_HALF if level == 0 else 0), jnp.int32(shift))

    need = kk - above
    sub_k = lax.broadcasted_iota(I32, (QB, QB), 0)
    sub_k2 = lax.broadcasted_iota(I32, (QB, QB), 1)
    tril = (sub_k2 <= sub_k).astype(BF16)

    def mask_blocks(first, count, seen):
        keys, eqs, pres = [], [], []
        for r in range(count):
            key = keys_ref[pl.ds(pl.multiple_of((first + r) * QB, QB), QB), :]
            eq = key == thr
            keys.append(key)
            eqs.append(eq)
            pres.append(_dot(tril, jnp.where(eq, 1.0, 0.0).astype(BF16)))
        for r in range(count):
            sel = (keys[r] > thr) | (eqs[r] & ((seen + pres[r]) <= need))
            madd_ref[pl.ds(pl.multiple_of((first + r) * QB, QB), QB), :] = jnp.where(sel, neg_bound, NEG).astype(BF16)
            seen = seen + pres[r][QB - 1:QB, :]
        return seen

    seen = lax.fori_loop(0, nsc // 2, lambda g, seen: mask_blocks(g * 2 * SUB, 2 * SUB, seen),
                         jnp.zeros((1, LANES), F32))

    @pl.when(nsc % 2 == 1)
    def _():
        mask_blocks(last * SUB, SUB, seen)

    def split_near(slot, blk):
        blk_rows = pl.ds(pl.multiple_of(blk * QB, QB), QB)
        mnear_ref[slot] = madd_ref[blk_rows, :]
        madd_ref[blk_rows, :] = jnp.full((QB, LANES), NEG, BF16)

    split_near(1, i)

    @pl.when(i >= 1)
    def _():
        split_near(0, i - 1)

    m_ref[...] = jnp.full(m_ref.shape, NEG, F32)
    l_ref[...] = jnp.zeros(l_ref.shape, F32)
    acc_ref[...] = jnp.zeros(acc_ref.shape, F32)

    def logits(kc, madd, n):
        return _dot_nt(jnp.concatenate([kc, madd], axis=1), q4_ref[n])

    def far_logits(j, n):
        return logits(k_ref[0, rows(j), n * A_HEAD_DIM:(n + 1) * A_HEAD_DIM], madd_ref[rows(j), :], n)

    def online_update(n, s, vt):
        m_prev = m_ref[n]
        m_new = jnp.maximum(m_prev, jnp.max(s, axis=0, keepdims=True))
        alpha = jnp.exp2(m_prev - m_new)
        p = jnp.exp2(s - m_new)
        l_ref[n] = alpha * l_ref[n] + jnp.sum(p, axis=0, keepdims=True)
        acc_ref[n] = alpha * acc_ref[n] + _dot(vt, p.astype(BF16))
        m_ref[n] = m_new

    def bounded_update(n, s, vt):
        p = jnp.exp2(s)
        l_ref[n] += jnp.sum(p, axis=0, keepdims=True)
        acc_ref[n] += _dot(vt, p.astype(BF16))

    def far_vt(j, n):
        return vt_ref[0, j, n * A_HEAD_DIM:(n + 1) * A_HEAD_DIM, :]

    n_far = jnp.where(i % SUB >= 2, nsc, last)

    def attend(update):
        def sweep(first, count):
            tasks = [(first + c, n) for c in range(count) for n in range(A_KV_HEADS)]
            ahead = 8
            pending = [far_logits(j, n) for j, n in tasks[:ahead]]
            for t, (j, n) in enumerate(tasks):
                update(n, pending.pop(0), far_vt(j, n))
                if t + ahead < len(tasks):
                    pending.append(far_logits(*tasks[t + ahead]))

        def sweep_body(g, carry):
            sweep(far_unroll * g, far_unroll)
            return carry

        lax.fori_loop(0, n_far // far_unroll, sweep_body, 0)
        for rem in range(1, far_unroll):
            @pl.when(n_far % far_unroll == rem)
            def _(rem=rem):
                sweep(n_far - rem, rem)

        def near_step(slot, blk, vts_ref):
            blk_rows = pl.ds(pl.multiple_of(blk * QB, QB), QB)
            ss = [logits(k_ref[0, blk_rows, n * A_HEAD_DIM:(n + 1) * A_HEAD_DIM], mnear_ref[slot], n)
                  + bias_ref[slot, n] for n in range(A_KV_HEADS)]
            for n in range(A_KV_HEADS):
                update(n, ss[n], vts_ref[0, 0, n * A_HEAD_DIM:(n + 1) * A_HEAD_DIM, :])

        @pl.when(i >= 1)
        def _():
            near_step(0, i - 1, vtp_ref)

        near_step(1, i, vtd_ref)

    @pl.when(bounded)
    def _():
        attend(bounded_update)

    @pl.when(jnp.logical_not(bounded))
    def _():
        attend(online_update)

    for n in range(A_KV_HEADS):
        out_t = acc_ref[n] / l_ref[n]
        for g in range(A_GROUP):
            hd = n * A_GROUP + g
            o_ref[0, :, hd * A_HEAD_DIM:(hd + 1) * A_HEAD_DIM] = out_t[:, g * QB:(g + 1) * QB].T.astype(o_ref.dtype)


def _dsa(p16, p32, vt, vt_blk, ln_g, ln_b, bias_near, *, topk, far_unroll):
    b, s, _ = p16.shape
    bias_max = jnp.broadcast_to(jnp.max(jnp.abs(bias_near)) + BOUND_MARGIN, (1, LANES)).astype(F32)
    aq = A_HEADS * A_HEAD_DIM
    akv = A_KV_HEADS * A_HEAD_DIM
    qiw = IDX_HEADS * LANES
    nch = s // QB
    gq = A_GROUP * QB
    k_blk = (aq + X_HEADS * X_HEAD_DIM) // akv
    qi_blk = 2 * LRU_WIDTH // qiw
    ki_blk = (2 * LRU_WIDTH + qiw) // LANES
    return pl.pallas_call(
        functools.partial(_dsa_kernel, topk=topk, far_unroll=far_unroll),
        out_shape=jax.ShapeDtypeStruct((b, s, aq), BF16),
        grid=(b, nch),
        in_specs=[
            pl.BlockSpec((1, QB, aq), lambda bi, i: (bi, i, 0)),
            pl.BlockSpec((1, s, akv), lambda bi, i: (bi, 0, k_blk)),
            pl.BlockSpec((1, s // SC, akv, SC), lambda bi, i: (bi, 0, 0, 0)),
            pl.BlockSpec((1, 1, akv, QB), lambda bi, i: (bi, jnp.maximum(i - 1, 0), 0, 0)),
            pl.BlockSpec((1, 1, akv, QB), lambda bi, i: (bi, i, 0, 0)),
            pl.BlockSpec((1, QB, qiw), lambda bi, i: (bi, i, qi_blk)),
            pl.BlockSpec((1, s, LANES), lambda bi, i: (bi, 0, ki_blk)),
            pl.BlockSpec((1, QB, LANES), lambda bi, i: (bi, i, ki_blk + 1)),
            pl.BlockSpec((1, LANES), lambda bi, i: (0, 0)),
            pl.BlockSpec((1, LANES), lambda bi, i: (0, 0)),
            pl.BlockSpec((2, A_KV_HEADS, QB, gq), lambda bi, i: (0, 0, 0, 0)),
            pl.BlockSpec((1, LANES), lambda bi, i: (0, 0)),
        ],
        out_specs=pl.BlockSpec((1, QB, aq), lambda bi, i: (bi, i, 0)),
        scratch_shapes=[
            pltpu.VMEM((s, LANES), BF16),
            pltpu.VMEM((A_KV_HEADS, 1, LANES), F32),
            pltpu.VMEM((s, LANES), I32),
            pltpu.VMEM((-(-(s // SC) // 2) * 2 * QB, LANES), I32),
            pltpu.VMEM((s // SUB, LANES), I32),
            pltpu.VMEM((s, LANES), BF16),
            pltpu.VMEM((2, QB, LANES), BF16),
            pltpu.VMEM((IDX_HEADS * QB, LANES), BF16),
            pltpu.VMEM((A_KV_HEADS, gq, A_HEAD_DIM + QB), BF16),
            pltpu.VMEM((A_KV_HEADS, 1, gq), F32),
            pltpu.VMEM((A_KV_HEADS, 1, gq), F32),
            pltpu.VMEM((A_KV_HEADS, A_HEAD_DIM, gq), F32),
        ],
        compiler_params=_params("arbitrary", "arbitrary"),
        name="dsa",
    )(p16, p16, vt, vt_blk, vt_blk, p32, p32, p32, ln_g, ln_b, bias_near, bias_max)


def _t5_bucket(dist):
    max_exact = N_BUCKETS // 2
    d = jnp.maximum(dist, 0)
    df = jnp.maximum(d, 1).astype(F32)
    large = max_exact + (jnp.log(df / max_exact) / math.log(MAX_DISTANCE / max_exact)
                         * (N_BUCKETS - max_exact)).astype(I32)
    large = jnp.minimum(large, N_BUCKETS - 1)
    return jnp.where(d < max_exact, d, large)


def _near_bias(rel_bias):
    t = jnp.arange(QB, dtype=I32)[:, None]
    s = jnp.arange(QB, dtype=I32)[None, :]
    tiles = []
    for off in (QB, 0):
        bucket = _t5_bucket(t - s + off)
        onehot = (bucket[..., None] == jnp.arange(N_BUCKETS, dtype=I32)).astype(F32)
        table = (rel_bias - rel_bias[N_BUCKETS - 1]) * LOG2E
        tile = jnp.einsum('tsb,bh->tsh', onehot, table, precision=lax.Precision.HIGHEST)
        tile = tile.transpose(1, 2, 0).reshape(QB, A_KV_HEADS, A_GROUP * QB)
        tiles.append(tile.transpose(1, 0, 2))
    return jnp.stack(tiles).astype(F32)


def _pad_cols(w, width):
    return jnp.pad(w, [(0, 0)] * (w.ndim - 1) + [(0, width - w.shape[-1])])


def _split_points(d_model):
    aq = A_HEADS * A_HEAD_DIM
    akv = A_KV_HEADS * A_HEAD_DIM
    sizes = (aq, akv, akv, IDX_HEADS * IDX_DIM, IDX_DIM, IDX_HEADS,
             LRU_WIDTH, LRU_WIDTH, X_HEADS * X_HEAD_DIM, N_BRANCH * d_model)
    pts, acc = [0], 0
    for n in sizes:
        acc += n
        pts.append(acc)
    return pts


def _regroup_kernel(w_ref, o16_ref, o32_ref, og_ref, *, pts):
    (q0, k0, v0, qi0, ki0, wi0, x0, g0, qm0, gt0, end) = pts

    def seg(lo, hi, scale=None):
        v = w_ref[0, :, lo:hi]
        if scale is not None:
            v = v * scale
        return v.astype(BF16)

    o16_ref[0, :, 0:k0 - q0] = seg(q0, k0, A_HEAD_DIM ** -0.5 * LOG2E)
    c = k0 - q0
    o16_ref[0, :, c:c + gt0 - qm0] = seg(qm0, gt0, X_HEAD_DIM ** -0.5)
    c += gt0 - qm0
    o16_ref[0, :, c:c + qi0 - k0] = seg(k0, qi0)
    o32_ref[0, :, 0:qm0 - x0] = seg(x0, qm0)
    c = qm0 - x0
    o32_ref[0, :, c:] = jnp.zeros((o32_ref.shape[1], o32_ref.shape[2] - c), BF16)
    for h in range(IDX_HEADS):
        o32_ref[0, :, c + h * LANES:c + h * LANES + IDX_DIM] = seg(qi0 + h * IDX_DIM, qi0 + (h + 1) * IDX_DIM)
    c += IDX_HEADS * LANES
    o32_ref[0, :, c:c + IDX_DIM] = seg(ki0, wi0)
    c += LANES
    o32_ref[0, :, c:c + IDX_HEADS] = seg(wi0, x0)
    og_ref[0] = seg(gt0, end)


def _prep_w_in(w_in, d_model, *, tk):
    depth, k, n = w_in.shape
    pts = _split_points(d_model)
    n16 = pts[3] + pts[9] - pts[8]
    n32 = 2 * LRU_WIDTH + (IDX_HEADS + 2) * LANES
    ng = n - pts[9]
    return pl.pallas_call(
        functools.partial(_regroup_kernel, pts=tuple(pts)),
        out_shape=(jax.ShapeDtypeStruct((depth, k, n16), BF16), jax.ShapeDtypeStruct((depth, k, n32), BF16),
                   jax.ShapeDtypeStruct((depth, k, ng), BF16)),
        grid=(depth, k // tk),
        in_specs=[pl.BlockSpec((1, tk, n), lambda l, i: (l, i, 0))],
        out_specs=(pl.BlockSpec((1, tk, n16), lambda l, i: (l, i, 0)),
                   pl.BlockSpec((1, tk, n32), lambda l, i: (l, i, 0)),
                   pl.BlockSpec((1, tk, ng), lambda l, i: (l, i, 0))),
        compiler_params=_params("parallel", "parallel"),
        name="regroup",
    )(w_in)


def _split_gu_kernel(w_ref, og_ref, ou_ref):
    d_ff = w_ref.shape[2] // 2
    pad = og_ref.shape[2] - d_ff
    for o_ref, lo in ((og_ref, 0), (ou_ref, d_ff)):
        o_ref[0, :, 0:d_ff] = w_ref[0, :, lo:lo + d_ff].astype(BF16)
        if pad:
            o_ref[0, :, d_ff:] = jnp.zeros((o_ref.shape[1], pad), BF16)


def _pad_down_kernel(w_ref, o_ref):
    d_ff = w_ref.shape[1]
    o_ref[0, 0:d_ff, :] = w_ref[0].astype(BF16)
    if o_ref.shape[1] > d_ff:
        o_ref[0, d_ff:, :] = jnp.zeros((o_ref.shape[1] - d_ff, o_ref.shape[2]), BF16)


def _pad_ff(w_gu, w_down, tf, *, tk):
    depth, d, _ = w_gu.shape
    d_ff = w_down.shape[1]
    fp = -(-d_ff // tf) * tf
    assert d_ff % LANES == 0
    wg, wu = pl.pallas_call(
        _split_gu_kernel,
        out_shape=(jax.ShapeDtypeStruct((depth, d, fp), BF16),) * 2,
        grid=(depth, d // tk),
        in_specs=[pl.BlockSpec((1, tk, 2 * d_ff), lambda l, i: (l, i, 0))],
        out_specs=(pl.BlockSpec((1, tk, fp), lambda l, i: (l, i, 0)),) * 2,
        compiler_params=_params("parallel", "parallel"),
        name="split_gu",
    )(w_gu)
    wd = pl.pallas_call(
        _pad_down_kernel,
        out_shape=jax.ShapeDtypeStruct((depth, fp, d), BF16),
        grid=(depth, d // tk),
        in_specs=[pl.BlockSpec((1, d_ff, tk), lambda l, i: (l, 0, i))],
        out_specs=pl.BlockSpec((1, fp, tk), lambda l, i: (l, 0, i)),
        compiler_params=_params("parallel", "parallel"),
        name="pad_down",
    )(w_down)
    return wg, wu, wd


def _tiles(m, s, d_model):
    return dict(
        ffn_tm=min(1024, m), ffn_final_tm=min(1024, m), ffn_tf=512,
        proj_tm=min(256, m), proj_tn=256,
        merge_tm=min(512, m), merge_tn=min(512, d_model),
        lru_tc=min(256, s), mem_tm=min(512, s),
    )


def kernel(x, mem, rel_bias, final_norm, norm_ff1, w_ff1_gu, w_ff1_down, norm_mix, w_in, conv_w, conv_b, w_a, b_a, w_i, b_i, lam, idx_ln_g, idx_ln_b, mem_norm, w_mem_kv, w_branch, w_out, norm_ff2, w_ff2_gu, w_ff2_down):
    b, s, d = x.shape
    nm = mem.shape[1]
    depth = w_in.shape[0]
    m = b * s
    assert s % SC == 0 and d % LANES == 0
    tl = _tiles(m, s, d)
    topk = min(TOPK_MAX, s // 4)
    bias_near = _near_bias(rel_bias)
    row = lambda v: v.reshape(1, -1)

    ff1 = _pad_ff(w_ff1_gu, w_ff1_down, tl["ffn_tf"], tk=min(256, d))
    ff2 = _pad_ff(w_ff2_gu, w_ff2_down, tl["ffn_tf"], tk=min(256, d))
    w16, w32, wgates = _prep_w_in(w_in, d, tk=min(256, d))
    w_a16, w_i16 = w_a.astype(BF16), w_i.astype(BF16)
    w_mem16, w_branch16, w_out16 = w_mem_kv.astype(BF16), w_branch.astype(BF16), w_out.astype(BF16)
    ln_g = _pad_cols(idx_ln_g, LANES)
    ln_b = _pad_cols(idx_ln_b, LANES)

    x2 = x.reshape(m, d)
    mem2 = mem.reshape(b * nm, d)
    for l in range(depth):
        x2 = _ffn(x2, row(norm_ff1[l]), *ff1, row(final_norm), l, final_norm=False,
                  tm=tl["ffn_tm"], tf=tl["ffn_tf"])

        p16, p32 = _proj(x2, row(norm_mix[l]), w16, w32, l, tm=tl["proj_tm"])
        p16 = p16.reshape(b, s, -1)
        p32 = p32.reshape(b, s, -1)

        v_cols = slice(p16.shape[-1] - A_KV_HEADS * A_HEAD_DIM, p16.shape[-1])
        vt = p16[..., v_cols].reshape(b, s // SC, SC, -1).transpose(0, 1, 3, 2)
        vt_blk = p16[..., v_cols].reshape(b, s // QB, QB, -1).transpose(0, 1, 3, 2)
        y_a = _dsa(p16, p32, vt, vt_blk, row(ln_g[l]), row(ln_b[l]),
                   bias_near, topk=topk, far_unroll=4)
        y_b = _lru(p32, conv_w[l], row(conv_b[l]), w_a16, w_i16,
                   row(b_a[l]), row(b_i[l]), row(lam[l]), l, tc=tl["lru_tc"])
        mkv = _norm_matmul(mem2, row(mem_norm[l]), w_mem16, l, BF16,
                           tm=min(256, b * nm), tn=tl["proj_tn"], name="memkv")
        y_c = _memattn(p16, mkv.reshape(b, nm, -1), tm=tl["mem_tm"])

        x2 = _merge(x2, row(norm_mix[l]), y_a.reshape(m, -1), y_b.reshape(m, -1), y_c.reshape(m, -1),
                    wgates, w_branch16, w_out16, l,
                    tm=tl["merge_tm"], tn=tl["merge_tn"])

        is_last = l == depth - 1
        x2 = _ffn(x2, row(norm_ff2[l]), *ff2, row(final_norm), l, final_norm=is_last,
                  tm=tl["ffn_final_tm" if is_last else "ffn_tm"], tf=tl["ffn_tf"])
    return x2.reshape(b, s, d)
```
